```python
import functools
import jax, jax.numpy as jnp
from jax import lax
import numpy as np

D_MODEL = 2048
BATCH = 2
SEQ = 8192
DEPTH = 1
DEC_BATCH = 16
DEC_SEQ = 32
PAST_LEN = 2048

CHUNK = 64
Q_BLOCK = 128
HEAD_DIM = 128
A_HEADS = 8
A_KV_HEADS = 2
A_GROUP = A_HEADS // A_KV_HEADS
IDX_HEADS = 8
IDX_DIM = 64
TOPK_KEYS = 256
B_HEADS = 8
N_EXPERTS = 64
N_ACTIVE = 8
N_GROUPS = 8
N_ACTIVE_GROUPS = 4
EXPERT_DIM = 512
SHARED_DIM = 512
ROUTED_SCALE = 2.5
EPS = 1e-6

A_WIDTH = A_HEADS * HEAD_DIM
A_KV_WIDTH = A_KV_HEADS * HEAD_DIM
IDX_Q_WIDTH = IDX_HEADS * IDX_DIM
B_WIDTH = B_HEADS * HEAD_DIM
IN_SPLITS = (A_WIDTH, A_KV_WIDTH, A_KV_WIDTH, IDX_Q_WIDTH, IDX_DIM, IDX_HEADS, B_WIDTH, B_WIDTH, B_WIDTH, D_MODEL, D_MODEL)
IN_WIDTH = sum(IN_SPLITS)

kernel_name = 'chunk_causal_dsa_stickbreaking_moe_step'


def rms_norm(x, gain):
    x32 = x.astype(jnp.float32)
    y = x32 * lax.rsqrt(jnp.mean(x32 * x32, axis=-1, keepdims=True) + EPS)
    return (y * gain.astype(jnp.float32)).astype(x.dtype)


def alibi_slopes(n_heads):
    return jnp.exp2(-8.0 * jnp.arange(1, n_heads + 1, dtype=jnp.float32) / n_heads)


def dsa_attend(qa, qi, wi, k_all, v_all, ki_all, q_pos, n_sel):
    b, nq = qa.shape[:2]
    n_keys = k_all.shape[1]
    q_chunk = q_pos // CHUNK
    admissible = (jnp.arange(n_keys) // CHUNK)[None, :] <= q_chunk[:, None]
    idx_logits = jnp.einsum('bqhe,ble->bqhl', qi, ki_all).astype(jnp.float32) * (IDX_DIM ** -0.5)
    head_w = wi.astype(jnp.float32) * (IDX_HEADS ** -0.5)
    score = jnp.einsum('bqhl,bqh->bql', jax.nn.relu(idx_logits), head_w)
    score = jnp.where(admissible[None], score, -jnp.inf)
    _, sel = lax.top_k(score, n_sel)
    take_rows = jax.vmap(lambda rows, idx: rows[idx])
    k_sel = take_rows(k_all, sel)
    v_sel = take_rows(v_all, sel)
    valid = (sel // CHUNK) <= q_chunk[None, :, None]
    qg = qa.reshape(b, nq, A_KV_HEADS, A_GROUP, HEAD_DIM)
    logits = jnp.einsum('bqgrd,bqkgd->bqgrk', qg, k_sel).astype(jnp.float32) * (HEAD_DIM ** -0.5)
    dist = jnp.abs(q_pos[None, :, None] - sel).astype(jnp.float32)
    slopes = alibi_slopes(A_HEADS).reshape(A_KV_HEADS, A_GROUP)
    logits = logits - slopes[None, None, :, :, None] * dist[:, :, None, None, :]
    logits = jnp.where(valid[:, :, None, None, :], logits, -jnp.inf)
    probs = jax.nn.softmax(logits, axis=-1).astype(v_sel.dtype)
    out = jnp.einsum('bqgrk,bqkgd->bqgrd', probs, v_sel)
    return out.reshape(b, nq, A_WIDTH)


def stick_breaking(qb, k_all, v_all, q_pos):
    b, nq = qb.shape[:2]
    n_keys = k_all.shape[1]
    strict = jnp.arange(n_keys)[None, :] < q_pos[:, None]
    z = jnp.einsum('bqhd,blhd->bhql', qb, k_all).astype(jnp.float32) * (HEAD_DIM ** -0.5)
    log_stay = jnp.where(strict, jax.nn.log_sigmoid(-z), 0.0)
    after = lax.cumsum(log_stay, axis=3, reverse=True) - log_stay
    weights = jnp.where(strict, jnp.exp(jax.nn.log_sigmoid(z) + after), 0.0)
    out = jnp.einsum('bhql,blhd->bqhd', weights.astype(v_all.dtype), v_all)
    return out.reshape(b, nq, B_WIDTH)


def mixers_prompt(qa, ka, va, qi, ki, wi, qb, kb, vb):
    b, n = qa.shape[:2]
    nb = n // Q_BLOCK
    n_sel = min(TOPK_KEYS, n // 4)
    blocks = lambda a: jnp.moveaxis(a.reshape(b, nb, Q_BLOCK, *a.shape[2:]), 1, 0)
    unblock = lambda a: jnp.moveaxis(a, 0, 1).reshape(b, n, a.shape[-1])
    q_pos = jnp.arange(n).reshape(nb, Q_BLOCK)
    ya = lax.map(lambda s: dsa_attend(s[0], s[1], s[2], ka, va, ki, s[3], n_sel),
                 (blocks(qa), blocks(qi), blocks(wi), q_pos))
    yb = lax.map(lambda s: stick_breaking(s[0], kb, vb, s[1]), (blocks(qb), q_pos))
    return unblock(ya), unblock(yb)


def mixers_sample(cka, cva, cki, ckb, cvb, qa, ka, va, qi, ki, wi, qb, kb, vb):
    past = cka.shape[1]
    n = qa.shape[1]
    q_pos = past + jnp.arange(n)
    n_sel = min(TOPK_KEYS, (past + n) // 4)
    cat = lambda cached, new: jnp.concatenate([cached, new], axis=1)
    ya = dsa_attend(qa, qi, wi, cat(cka, ka), cat(cva, va), cat(cki, ki), q_pos, n_sel)
    yb = stick_breaking(qb, cat(ckb, kb), cat(cvb, vb), q_pos)
    return ya, yb


def moe_ffn(h, w_router, b_router, w_gate_e, w_up_e, w_down_e, w_gate_s, w_up_s, w_down_s):
    b, n, d = h.shape
    t = h.reshape(b * n, d)
    affinity = jax.nn.sigmoid((t @ w_router).astype(jnp.float32))
    biased = affinity + b_router.astype(jnp.float32)
    grouped = biased.reshape(-1, N_GROUPS, N_EXPERTS // N_GROUPS)
    group_score = jnp.sum(lax.top_k(grouped, 2)[0], axis=-1)
    _, top_groups = lax.top_k(group_score, N_ACTIVE_GROUPS)
    group_keep = jnp.any(top_groups[:, :, None] == jnp.arange(N_GROUPS)[None, None, :], axis=1)
    expert_keep = jnp.repeat(group_keep, N_EXPERTS // N_GROUPS, axis=1)
    _, top_e = lax.top_k(jnp.where(expert_keep, biased, -jnp.inf), N_ACTIVE)
    sel_aff = jnp.take_along_axis(affinity, top_e, axis=1)
    wts = sel_aff / jnp.sum(sel_aff, axis=-1, keepdims=True) * ROUTED_SCALE
    gates = jnp.sum(jax.nn.one_hot(top_e, N_EXPERTS, dtype=jnp.float32) * wts[..., None], axis=1).astype(t.dtype)
    y = (jax.nn.silu(t @ w_gate_s) * (t @ w_up_s)) @ w_down_s
    for e in range(N_EXPERTS):
        hid = jax.nn.silu(t @ w_gate_e[e]) * (t @ w_up_e[e])
        y = y + (gates[:, e:e + 1] * hid) @ w_down_e[e]
    return y.reshape(b, n, d)


def trunk_layer(x, c, mix, norm_mix, norm_ffn, w_ada, b_ada, w_in, q_norm_a, k_norm_a,
                w_branch_a, w_branch_b, w_out, w_router, b_router,
                w_gate_e, w_up_e, w_down_e, w_gate_s, w_up_s, w_down_s):
    b, n, _ = x.shape
    mod = (jax.nn.silu(c) @ w_ada + b_ada)[:, None, :]
    shift1, scale1, gate1, shift2, scale2, gate2 = jnp.split(mod, 6, axis=-1)
    h = rms_norm(x, norm_mix) * (1.0 + scale1) + shift1
    split_points = [int(p) for p in np.cumsum(IN_SPLITS)[:-1]]
    qa, ka, va, qi, ki, wi, qb, kb, vb, ga, gb = jnp.split(h @ w_in, split_points, axis=-1)
    qa = rms_norm(qa.reshape(b, n, A_HEADS, HEAD_DIM), q_norm_a)
    ka = rms_norm(ka.reshape(b, n, A_KV_HEADS, HEAD_DIM), k_norm_a)
    va = va.reshape(b, n, A_KV_HEADS, HEAD_DIM)
    qi = qi.reshape(b, n, IDX_HEADS, IDX_DIM)
    qb = qb.reshape(b, n, B_HEADS, HEAD_DIM)
    kb = kb.reshape(b, n, B_HEADS, HEAD_DIM)
    vb = vb.reshape(b, n, B_HEADS, HEAD_DIM)
    ya, yb = mix(qa, ka, va, qi, ki, wi, qb, kb, vb)
    merged = jax.nn.sigmoid(ga) * (ya @ w_branch_a) + jax.nn.sigmoid(gb) * (yb @ w_branch_b)
    x = x + gate1 * (merged @ w_out)
    h = rms_norm(x, norm_ffn) * (1.0 + scale2) + shift2
    x = x + gate2 * moe_ffn(h, w_router, b_router, w_gate_e, w_up_e, w_down_e, w_gate_s, w_up_s, w_down_s)
    return x, (ka, va, ki, kb, vb)


def setup_inputs(seed: int = 0) -> dict:
    key = jax.random.key(seed)
    keys = iter(jax.random.split(key, 32))

    def normal(shape, scale):
        return jax.random.normal(next(keys), shape, jnp.float32) * scale

    d = D_MODEL
    return {
        'x_prompt': normal((BATCH, SEQ, d), 1.0),
        'x_sample': normal((DEC_BATCH, DEC_SEQ, d), 1.0),
        'cache_a_k': normal((DEPTH, DEC_BATCH, PAST_LEN, A_KV_HEADS, HEAD_DIM), 1.0),
        'cache_a_v': normal((DEPTH, DEC_BATCH, PAST_LEN, A_KV_HEADS, HEAD_DIM), 1.0),
        'cache_a_idx_k': normal((DEPTH, DEC_BATCH, PAST_LEN, IDX_DIM), 1.0),
        'cache_b_k': normal((DEPTH, DEC_BATCH, PAST_LEN, B_HEADS, HEAD_DIM), 1.0),
        'cache_b_v': normal((DEPTH, DEC_BATCH, PAST_LEN, B_HEADS, HEAD_DIM), 1.0),
        'c_prompt': normal((BATCH, d), 1.0),
        'c_sample': normal((DEC_BATCH, d), 1.0),
        'norm_mix': 1.0 + normal((DEPTH, d), 0.02),
        'norm_ffn': 1.0 + normal((DEPTH, d), 0.02),
        'w_ada': normal((DEPTH, d, 6 * d), 0.5 * d ** -0.5),
        'b_ada': normal((DEPTH, 6 * d), 0.01),
        'w_in': normal((DEPTH, d, IN_WIDTH), d ** -0.5),
        'q_norm_a': 1.0 + normal((DEPTH, HEAD_DIM), 0.02),
        'k_norm_a': 1.0 + normal((DEPTH, HEAD_DIM), 0.02),
        'w_branch_a': normal((DEPTH, A_WIDTH, d), A_WIDTH ** -0.5),
        'w_branch_b': normal((DEPTH, B_WIDTH, d), B_WIDTH ** -0.5),
        'w_out': normal((DEPTH, d, d), d ** -0.5),
        'w_router': normal((DEPTH, d, N_EXPERTS), d ** -0.5),
        'b_router': normal((DEPTH, N_EXPERTS), 0.01),
        'w_gate_e': normal((DEPTH, N_EXPERTS, d, EXPERT_DIM), d ** -0.5),
        'w_up_e': normal((DEPTH, N_EXPERTS, d, EXPERT_DIM), d ** -0.5),
        'w_down_e': normal((DEPTH, N_EXPERTS, EXPERT_DIM, d), EXPERT_DIM ** -0.5),
        'w_gate_s': normal((DEPTH, d, SHARED_DIM), d ** -0.5),
        'w_up_s': normal((DEPTH, d, SHARED_DIM), d ** -0.5),
        'w_down_s': normal((DEPTH, SHARED_DIM, d), SHARED_DIM ** -0.5),
    }


def reference(x_prompt, x_sample, cache_a_k, cache_a_v, cache_a_idx_k, cache_b_k, cache_b_v,
              c_prompt, c_sample, norm_mix, norm_ffn, w_ada, b_ada, w_in, q_norm_a, k_norm_a,
              w_branch_a, w_branch_b, w_out, w_router, b_router,
              w_gate_e, w_up_e, w_down_e, w_gate_s, w_up_s, w_down_s):
    weights = (norm_mix, norm_ffn, w_ada, b_ada, w_in, q_norm_a, k_norm_a,
               w_branch_a, w_branch_b, w_out, w_router, b_router,
               w_gate_e, w_up_e, w_down_e, w_gate_s, w_up_s, w_down_s)
    caches = (cache_a_k, cache_a_v, cache_a_idx_k, cache_b_k, cache_b_v)
    xp, xs = x_prompt, x_sample
    prompt_rows, sample_rows = [], []
    for layer in range(DEPTH):
        lw = [w[layer] for w in weights]
        xp, rows_p = trunk_layer(xp, c_prompt, mixers_prompt, *lw)
        mix_s = functools.partial(mixers_sample, *[cc[layer] for cc in caches])
        xs, rows_s = trunk_layer(xs, c_sample, mix_s, *lw)
        prompt_rows.append(rows_p)
        sample_rows.append(rows_s)
    pa_k, pa_v, pa_i, pb_k, pb_v = [jnp.stack(r) for r in zip(*prompt_rows)]
    sa_k, sa_v, sa_i, sb_k, sb_v = [jnp.stack(r) for r in zip(*sample_rows)]
    return (xp, xs, pa_k, pa_v, pa_i, pb_k, pb_v, sa_k, sa_v, sa_i, sb_k, sb_v)
```

```python
import functools

import jax
import jax.numpy as jnp
from jax import lax
from jax.experimental import pallas as pl
from jax.experimental.pallas import tpu as pltpu

CHUNK = 64
HEAD_DIM = 128
A_HEADS = 8
A_KV_HEADS = 2
A_GROUP = A_HEADS // A_KV_HEADS
IDX_HEADS = 8
IDX_DIM = 64
TOPK_KEYS = 256
B_HEADS = 8
N_EXPERTS = 64
N_ACTIVE = 8
N_GROUPS = 8
N_ACTIVE_GROUPS = 4
GROUP_SIZE = N_EXPERTS // N_GROUPS
ROUTED_SCALE = 2.5
EPS = 1e-6

A_WIDTH = A_HEADS * HEAD_DIM
A_KV_WIDTH = A_KV_HEADS * HEAD_DIM
IDX_Q_WIDTH = IDX_HEADS * IDX_DIM
B_WIDTH = B_HEADS * HEAD_DIM

LANES = 128
MOD_GROUP = 32
VMEM_LIMIT = 56 * 1024 * 1024
NEG_BIG = -1e30
INT_MIN = -(2 ** 31)

F32 = jnp.float32
BF16 = jnp.bfloat16
NT_DIMS = (((1,), (1,)), ((), ()))


def _params(*sem):
    return pltpu.CompilerParams(dimension_semantics=sem, vmem_limit_bytes=VMEM_LIMIT)


def _silu(x):
    return x * jax.nn.sigmoid(x)


def _rms(x, gain):
    return x * lax.rsqrt(jnp.mean(x * x, axis=-1, keepdims=True) + EPS) * gain


def _modulate(y, scale_g, shift_g):
    tm, d = y.shape
    y3 = y.reshape(tm // MOD_GROUP, MOD_GROUP, d)
    out = y3 * (1.0 + scale_g[:, None, :]) + shift_g[:, None, :]
    return out.reshape(tm, d)


def _gate_rows(y, gate_g):
    tm, d = y.shape
    return (y.reshape(tm // MOD_GROUP, MOD_GROUP, d) * gate_g[:, None, :]).reshape(tm, d)


def _row_tile(t, pref):
    tm = min(pref, t)
    while t % tm:
        tm //= 2
    return tm


def _ada_kernel(c_ref, w_ref, b_ref, o_ref):
    s = _silu(c_ref[...]).astype(BF16)
    o_ref[...] = jnp.dot(s, w_ref[...].astype(BF16), preferred_element_type=F32) + b_ref[...]


def ada_mod(c, w_ada, b_ada):
    r, d = c.shape
    n = w_ada.shape[1]
    tn = _row_tile(n, 1024)
    return pl.pallas_call(
        _ada_kernel,
        grid=(n // tn,),
        in_specs=[pl.BlockSpec((r, d), lambda j: (0, 0)),
                  pl.BlockSpec((d, tn), lambda j: (0, j)),
                  pl.BlockSpec((1, tn), lambda j: (0, j))],
        out_specs=pl.BlockSpec((r, tn), lambda j: (0, j)),
        out_shape=jax.ShapeDtypeStruct((r, n), F32),
        compiler_params=_params("arbitrary"),
        name="ada_mod",
    )(c, w_ada, b_ada.reshape(1, n))


def _in_a_kernel(x_ref, nw_ref, sc_ref, sh_ref, w_ref, qn_ref, kn_ref,
                 h_ref, qa_ref, ka_ref, va_ref, kab_ref, vab_ref, qi_ref, ki_ref, kib_ref, wi_ref):
    h = _modulate(_rms(x_ref[...], nw_ref[...]), sc_ref[...], sh_ref[...])
    hb = h.astype(BF16)
    h_ref[...] = hb
    off = 0
    for hd in range(A_HEADS):
        q = jnp.dot(hb, w_ref[:, off:off + HEAD_DIM], preferred_element_type=F32)
        qa_ref[:, hd * HEAD_DIM:(hd + 1) * HEAD_DIM] = _rms(q, qn_ref[...]).astype(BF16)
        off += HEAD_DIM
    for hd in range(A_KV_HEADS):
        k = jnp.dot(hb, w_ref[:, off:off + HEAD_DIM], preferred_element_type=F32)
        k = _rms(k, kn_ref[...])
        ka_ref[:, hd * HEAD_DIM:(hd + 1) * HEAD_DIM] = k
        kab_ref[:, hd * HEAD_DIM:(hd + 1) * HEAD_DIM] = k.astype(BF16)
        off += HEAD_DIM
    v = jnp.dot(hb, w_ref[:, off:off + A_KV_WIDTH], preferred_element_type=F32)
    va_ref[...] = v
    vab_ref[...] = v.astype(BF16)
    off += A_KV_WIDTH
    qi_ref[...] = jnp.dot(hb, w_ref[:, off:off + IDX_Q_WIDTH], preferred_element_type=F32).astype(BF16)
    off += IDX_Q_WIDTH
    kw = jnp.dot(hb, w_ref[:, off:off + LANES], preferred_element_type=F32)
    ki = kw[:, :IDX_DIM]
    ki_ref[...] = ki
    kib_ref[...] = ki.astype(BF16)
    wi_ref[...] = kw[:, IDX_DIM:IDX_DIM + IDX_HEADS]


def in_proj_a(x, norm_w, scale_g, shift_g, w_a, q_norm, k_norm, tm):
    t, d = x.shape
    wa = w_a.shape[1]
    gm = tm // MOD_GROUP
    row = lambda w: pl.BlockSpec((tm, w), lambda i: (i, 0))
    const = lambda a, b: pl.BlockSpec((a, b), lambda i: (0, 0))
    outs = [(d, BF16), (A_WIDTH, BF16), (A_KV_WIDTH, F32), (A_KV_WIDTH, F32), (A_KV_WIDTH, BF16),
            (A_KV_WIDTH, BF16), (IDX_Q_WIDTH, BF16), (IDX_DIM, F32), (IDX_DIM, BF16), (IDX_HEADS, F32)]
    return pl.pallas_call(
        _in_a_kernel,
        grid=(t // tm,),
        in_specs=[row(d), const(1, d), pl.BlockSpec((gm, d), lambda i: (i, 0)),
                  pl.BlockSpec((gm, d), lambda i: (i, 0)), const(d, wa),
                  const(1, HEAD_DIM), const(1, HEAD_DIM)],
        out_specs=[row(w) for w, _ in outs],
        out_shape=[jax.ShapeDtypeStruct((t, w), dt) for w, dt in outs],
        compiler_params=_params("arbitrary"),
        name="in_proj_a",
    )(x, norm_w, scale_g, shift_g, w_a, q_norm, k_norm)


def _in_b_kernel(h_ref, w_ref, qb_ref, kb_ref, vb_ref, kbb_ref, vbb_ref):
    hb = h_ref[...]
    qb_ref[...] = jnp.dot(hb, w_ref[:, :B_WIDTH], preferred_element_type=F32).astype(BF16)
    k = jnp.dot(hb, w_ref[:, B_WIDTH:2 * B_WIDTH], preferred_element_type=F32)
    kb_ref[...] = k
    kbb_ref[...] = k.astype(BF16)
    v = jnp.dot(hb, w_ref[:, 2 * B_WIDTH:], preferred_element_type=F32)
    vb_ref[...] = v
    vbb_ref[...] = v.astype(BF16)


def in_proj_b(h, w_b, tm):
    t, d = h.shape
    row = lambda w: pl.BlockSpec((tm, w), lambda i: (i, 0))
    outs = [BF16, F32, F32, BF16, BF16]
    return pl.pallas_call(
        _in_b_kernel,
        grid=(t // tm,),
        in_specs=[row(d), pl.BlockSpec((d, 3 * B_WIDTH), lambda i: (0, 0))],
        out_specs=[row(B_WIDTH) for _ in outs],
        out_shape=[jax.ShapeDtypeStruct((t, B_WIDTH), dt) for dt in outs],
        compiler_params=_params("arbitrary"),
        name="in_proj_b",
    )(h, w_b)


def _dsa_kernel(qa_ref, qi_ref, wi_ref, k_ref, v_ref, ki_ref, o_ref,
                key_ref, m_ref, l_ref, acc_ref, p_ref, al_ref,
                *, q_pos0, n_keys, n_sel, tq, tc):
    i = pl.program_id(1)
    q0 = q_pos0 + i * tq
    kmax = jnp.minimum(((q0 + tq - 1) // CHUNK + 1) * CHUNK, n_keys)
    nch = (kmax + tc - 1) // tc
    qpos = q0 + lax.broadcasted_iota(jnp.int32, (tq, 1), 0)
    qchunk = qpos // CHUNK

    head_w = wi_ref[...] * (IDX_HEADS ** -0.5)
    qi_heads = [qi_ref[:, h * IDX_DIM:(h + 1) * IDX_DIM] for h in range(IDX_HEADS)]

    def score_body(c, carry):
        start = pl.multiple_of(c * tc, tc)
        kic = ki_ref[pl.ds(start, tc), :]
        s = jnp.zeros((tq, tc), F32)
        for h in range(IDX_HEADS):
            lg = lax.dot_general(qi_heads[h], kic, NT_DIMS, preferred_element_type=F32) * (IDX_DIM ** -0.5)
            s = s + jnp.maximum(lg, 0.0) * head_w[:, h:h + 1]
        kpos = start + lax.broadcasted_iota(jnp.int32, (1, tc), 1)
        adm = jnp.logical_and(kpos // CHUNK <= qchunk, kpos < n_keys)
        bits = lax.bitcast_convert_type(s, jnp.int32)
        key = bits ^ ((bits >> 31) & 0x7FFFFFFF)
        key_ref[c] = jnp.where(adm, key, INT_MIN)
        return carry

    lax.fori_loop(0, nch, score_body, 0)

    def count_ge(cand):
        def body(c, acc):
            ge = jnp.where(key_ref[c] >= cand, 1.0, 0.0)
            part = ge[:, :LANES]
            for j in range(1, tc // LANES):
                part = part + ge[:, j * LANES:(j + 1) * LANES]
            return acc + part
        acc = lax.fori_loop(0, nch, body, jnp.zeros((tq, LANES), F32))
        return jnp.sum(acc, axis=1, keepdims=True)

    thr = jnp.where(count_ge(jnp.zeros((tq, 1), jnp.int32)) >= n_sel, 0, INT_MIN).astype(jnp.int32)

    def bit_body(b, thr):
        cand = thr | jnp.left_shift(jnp.int32(1), 30 - b)
        return jnp.where(count_ge(cand) >= n_sel, cand, thr)

    thr = lax.fori_loop(0, 31, bit_body, thr)

    scale = HEAD_DIM ** -0.5
    for g in range(A_KV_HEADS):
        q4 = jnp.concatenate(
            [qa_ref[:, (g * A_GROUP + r) * HEAD_DIM:(g * A_GROUP + r + 1) * HEAD_DIM] for r in range(A_GROUP)],
            axis=0)
        m_ref[...] = jnp.full(m_ref.shape, NEG_BIG, F32)
        l_ref[...] = jnp.zeros(l_ref.shape, F32)
        acc_ref[...] = jnp.zeros(acc_ref.shape, F32)

        def att_body(c, carry):
            start = pl.multiple_of(c * tc, tc)
            kc = k_ref[pl.ds(start, tc), g * HEAD_DIM:(g + 1) * HEAD_DIM]
            vc = v_ref[pl.ds(start, tc), g * HEAD_DIM:(g + 1) * HEAD_DIM]
            lg = lax.dot_general(q4, kc, NT_DIMS, preferred_element_type=F32) * scale
            key = key_ref[c]
            sel = jnp.logical_and(key >= thr, key != INT_MIN)
            kpos = start + lax.broadcasted_iota(jnp.int32, (1, tc), 1)
            dist = jnp.abs(qpos - kpos).astype(F32)
            for r in range(A_GROUP):
                slope = 2.0 ** (-8.0 * (g * A_GROUP + r + 1) / A_HEADS)
                rows = slice(r * tq, (r + 1) * tq)
                t = jnp.where(sel, lg[rows] - slope * dist, NEG_BIG)
                m_old = m_ref[rows]
                m_new = jnp.maximum(m_old, jnp.max(t, axis=1, keepdims=True))
                p = jnp.exp(t - m_new)
                alpha = jnp.exp(m_old - m_new)
                l_ref[rows] = alpha * l_ref[rows] + jnp.sum(p, axis=1, keepdims=True)
                m_ref[rows] = m_new
                al_ref[rows] = alpha
                p_ref[rows] = p.astype(BF16)
            pv = jnp.dot(p_ref[...], vc, preferred_element_type=F32)
            acc_ref[...] = al_ref[...] * acc_ref[...] + pv
            return carry

        lax.fori_loop(0, nch, att_body, 0)
        for r in range(A_GROUP):
            rows = slice(r * tq, (r + 1) * tq)
            hd = g * A_GROUP + r
            o_ref[:, hd * HEAD_DIM:(hd + 1) * HEAD_DIM] = (acc_ref[rows] / l_ref[rows]).astype(BF16)


def dsa_attention(qa, qi, wi, k, v, ki, *, q_pos0, n_keys, tq, tc):
    b, n, _ = qa.shape
    l_pad = k.shape[1]
    n_sel = min(TOPK_KEYS, n_keys // 4)
    qspec = lambda w: pl.BlockSpec((None, tq, w), lambda bi, i: (bi, i, 0))
    kspec = lambda w: pl.BlockSpec((None, l_pad, w), lambda bi, i: (bi, 0, 0))
    kern = functools.partial(_dsa_kernel, q_pos0=q_pos0, n_keys=n_keys, n_sel=n_sel, tq=tq, tc=tc)
    return pl.pallas_call(
        kern,
        grid=(b, n // tq),
        in_specs=[qspec(A_WIDTH), qspec(IDX_Q_WIDTH), qspec(IDX_HEADS),
                  kspec(A_KV_WIDTH), kspec(A_KV_WIDTH), kspec(IDX_DIM)],
        out_specs=qspec(A_WIDTH),
        out_shape=jax.ShapeDtypeStruct((b, n, A_WIDTH), BF16),
        scratch_shapes=[pltpu.VMEM((l_pad // tc, tq, tc), jnp.int32),
                        pltpu.VMEM((A_GROUP * tq, 1), F32),
                        pltpu.VMEM((A_GROUP * tq, 1), F32),
                        pltpu.VMEM((A_GROUP * tq, HEAD_DIM), F32),
                        pltpu.VMEM((A_GROUP * tq, tc), BF16),
                        pltpu.VMEM((A_GROUP * tq, 1), F32)],
        compiler_params=_params("arbitrary", "arbitrary"),
        name="dsa",
    )(qa, qi, wi, k, v, ki)


def _stick_kernel(q_ref, k_ref, v_ref, o_ref, *, q_pos0, tq, tc):
    i = pl.program_id(2)
    t0 = q_pos0 + i * tq
    tpos = t0 + lax.broadcasted_iota(jnp.int32, (tq, 1), 0)
    c_first = (t0 + tq - 2) // tc
    q = q_ref[...]
    upper = (lax.broadcasted_iota(jnp.int32, (tc, tc), 0)
             > lax.broadcasted_iota(jnp.int32, (tc, tc), 1)).astype(BF16)
    scale = HEAD_DIM ** -0.5

    def body(n, carry):
        tail, acc = carry
        c = c_first - n
        start = pl.multiple_of(c * tc, tc)
        kc = k_ref[pl.ds(start, tc), :]
        vc = v_ref[pl.ds(start, tc), :]
        z = lax.dot_general(q, kc, NT_DIMS, preferred_element_type=F32) * scale
        spos = start + lax.broadcasted_iota(jnp.int32, (1, tc), 1)
        strict = spos < tpos
        softplus = jnp.log(1.0 + jnp.exp(-jnp.abs(z)))
        log_go = jnp.minimum(z, 0.0) - softplus
        log_stay = jnp.where(strict, jnp.minimum(-z, 0.0) - softplus, 0.0)
        hi = log_stay.astype(BF16)
        lo = (log_stay - hi.astype(F32)).astype(BF16)
        after = (jnp.dot(hi, upper, preferred_element_type=F32)
                 + jnp.dot(lo, upper, preferred_element_type=F32) + tail)
        w = jnp.where(strict, jnp.exp(log_go + after), 0.0)
        acc = acc + jnp.dot(w.astype(BF16), vc, preferred_element_type=F32)
        tail = tail + jnp.sum(log_stay, axis=1, keepdims=True)
        return tail, acc

    _, acc = lax.fori_loop(0, c_first + 1, body,
                           (jnp.zeros((tq, 1), F32), jnp.zeros((tq, HEAD_DIM), F32)))
    o_ref[...] = acc.astype(BF16)


def stick_attention(qb, k, v, *, q_pos0, tq, tc):
    b, n, _ = qb.shape
    l_pad = k.shape[1]
    qspec = pl.BlockSpec((None, tq, HEAD_DIM), lambda bi, h, i: (bi, i, h))
    kspec = pl.BlockSpec((None, l_pad, HEAD_DIM), lambda bi, h, i: (bi, 0, h))
    kern = functools.partial(_stick_kernel, q_pos0=q_pos0, tq=tq, tc=tc)
    return pl.pallas_call(
        kern,
        grid=(b, B_HEADS, n // tq),
        in_specs=[qspec, kspec, kspec],
        out_specs=qspec,
        out_shape=jax.ShapeDtypeStruct((b, n, B_WIDTH), BF16),
        compiler_params=_params("arbitrary", "arbitrary", "arbitrary"),
        name="stick",
    )(qb, k, v)


def _merge_kernel(h_ref, ya_ref, yb_ref, wga_ref, wgb_ref, wa_ref, wb_ref, o_ref):
    hb = h_ref[...]
    ga = jax.nn.sigmoid(jnp.dot(hb, wga_ref[...], preferred_element_type=F32))
    gb = jax.nn.sigmoid(jnp.dot(hb, wgb_ref[...], preferred_element_type=F32))
    pa = jnp.dot(ya_ref[...], wa_ref[...], preferred_element_type=F32)
    pb = jnp.dot(yb_ref[...], wb_ref[...], preferred_element_type=F32)
    o_ref[...] = (ga * pa + gb * pb).astype(BF16)


def merge_branches(h, ya, yb, w_ga, w_gb, w_a, w_b, tm, tn):
    t, d = h.shape
    row = lambda w: pl.BlockSpec((tm, w), lambda i, j: (i, 0))
    col = lambda k: pl.BlockSpec((k, tn), lambda i, j: (0, j))
    return pl.pallas_call(
        _merge_kernel,
        grid=(t // tm, d // tn),
        in_specs=[row(d), row(A_WIDTH), row(B_WIDTH), col(d), col(d), col(A_WIDTH), col(B_WIDTH)],
        out_specs=pl.BlockSpec((tm, tn), lambda i, j: (i, j)),
        out_shape=jax.ShapeDtypeStruct((t, d), BF16),
        compiler_params=_params("arbitrary", "arbitrary"),
        name="merge",
    )(h, ya, yb, w_ga, w_gb, w_a, w_b)


def _pack_bf16_pair(lo, hi):
    lo_bits = lax.bitcast_convert_type(lo.astype(BF16).astype(F32), jnp.uint32)
    hi_bits = lax.bitcast_convert_type(hi.astype(BF16).astype(F32), jnp.uint32)
    return (lo_bits >> 16) | (hi_bits & jnp.uint32(0xFFFF0000))


def _unpack_bf16_pair(u):
    lo = lax.bitcast_convert_type(u << 16, F32).astype(BF16)
    hi = lax.bitcast_convert_type(u & jnp.uint32(0xFFFF0000), F32).astype(BF16)
    return lo, hi


def _first_max(vals, idx, axes, n):
    m = vals
    for ax in axes:
        m = jnp.max(m, axis=ax, keepdims=True)
    first = jnp.where(vals == m, idx, n)
    for ax in axes:
        first = jnp.min(first, axis=ax, keepdims=True)
    return m, first


def _out_kernel(mg_ref, x_ref, g1_ref, sc_ref, sh_ref, nw_ref, wo_ref, wr_ref, br_ref,
                x1_ref, hu_ref, gate_ref, tope_ref):
    tm, d = x_ref.shape
    proj = jnp.dot(mg_ref[...], wo_ref[...], preferred_element_type=F32)
    x1 = x_ref[...] + _gate_rows(proj, g1_ref[...])
    x1_ref[...] = x1
    h2 = _modulate(_rms(x1, nw_ref[...]), sc_ref[...], sh_ref[...])
    hu_ref[...] = _pack_bf16_pair(h2[:, :d // 2], h2[:, d // 2:])

    logits = lax.dot_general(wr_ref[...], h2.astype(BF16), NT_DIMS, preferred_element_type=F32)
    aff = jax.nn.sigmoid(logits)
    shape3 = (N_GROUPS, GROUP_SIZE, tm)
    aff3 = aff.reshape(shape3)
    biased = (aff + br_ref[...]).reshape(shape3)
    io_in = lax.broadcasted_iota(jnp.int32, shape3, 1)
    m1, i1 = _first_max(biased, io_in, (1,), GROUP_SIZE)
    m2 = jnp.max(jnp.where(io_in == i1, -jnp.inf, biased), axis=1, keepdims=True)
    gscore = m1 + m2
    gio = lax.broadcasted_iota(jnp.int32, gscore.shape, 0)
    keep = jnp.zeros(gscore.shape, jnp.bool_)
    for _ in range(N_ACTIVE_GROUPS):
        _, first = _first_max(gscore, gio, (0,), N_GROUPS)
        hit = gio == first
        keep = jnp.logical_or(keep, hit)
        gscore = jnp.where(hit, -jnp.inf, gscore)
    masked = jnp.where(jnp.broadcast_to(keep, shape3), biased, -jnp.inf)
    eio = lax.broadcasted_iota(jnp.int32, shape3, 0) * GROUP_SIZE + io_in
    sel = jnp.zeros(shape3, jnp.bool_)
    tops = []
    for _ in range(N_ACTIVE):
        _, first = _first_max(masked, eio, (0, 1), N_EXPERTS)
        hit = eio == first
        sel = jnp.logical_or(sel, hit)
        masked = jnp.where(hit, -jnp.inf, masked)
        tops.append(first.reshape(1, tm))
    sel_aff = jnp.where(sel, aff3, 0.0)
    denom = jnp.sum(jnp.sum(sel_aff, axis=0, keepdims=True), axis=1, keepdims=True)
    gate_ref[...] = (sel_aff / denom * ROUTED_SCALE).reshape(N_EXPERTS, tm)
    tope_ref[...] = jnp.concatenate(tops, axis=0)


def out_proj(merged, x, gate1_g, scale_g, shift_g, norm_w, w_out, w_router_t, b_router, tm):
    t, d = x.shape
    gm = tm // MOD_GROUP
    row = lambda w: pl.BlockSpec((tm, w), lambda i: (i, 0))
    grp = pl.BlockSpec((gm, d), lambda i: (i, 0))
    const = lambda a, b: pl.BlockSpec((a, b), lambda i: (0, 0))
    return pl.pallas_call(
        _out_kernel,
        grid=(t // tm,),
        in_specs=[row(d), row(d), grp, grp, grp, const(1, d), const(d, d),
                  const(N_EXPERTS, d), const(N_EXPERTS, 1)],
        out_specs=[row(d), row(d // 2),
                   pl.BlockSpec((N_EXPERTS, tm), lambda i: (0, i)),
                   pl.BlockSpec((N_ACTIVE, tm), lambda i: (0, i))],
        out_shape=[jax.ShapeDtypeStruct((t, d), F32), jax.ShapeDtypeStruct((t, d // 2), jnp.uint32),
                   jax.ShapeDtypeStruct((N_EXPERTS, t), F32), jax.ShapeDtypeStruct((N_ACTIVE, t), jnp.int32)],
        compiler_params=_params("arbitrary"),
        name="out_proj",
    )(merged, x, gate1_g, scale_g, shift_g, norm_w, w_out, w_router_t, b_router)


def _dispatch_kernel(slot_ref, hu_ref, xs_in_ref, xs_ref, sem, *, td):
    del xs_in_ref
    base = pl.program_id(0) * td

    def row_copy(j, k):
        return pltpu.make_async_copy(hu_ref.at[pl.ds(base + j, 1)],
                                     xs_ref.at[pl.ds(slot_ref[j * N_ACTIVE + k], 1)], sem)

    def start(j, carry):
        for k in range(N_ACTIVE):
            row_copy(j, k).start()
        return carry

    def wait(j, carry):
        for k in range(N_ACTIVE):
            row_copy(j, k).wait()
        return carry

    lax.fori_loop(0, td, start, 0)
    lax.fori_loop(0, td, wait, 0)


def dispatch(slots_flat, hu, xs, td):
    t, w = hu.shape
    return pl.pallas_call(
        functools.partial(_dispatch_kernel, td=td),
        grid=(t // td,),
        in_specs=[pl.BlockSpec((td * N_ACTIVE,), lambda i: (i,), memory_space=pltpu.SMEM),
                  pl.BlockSpec(memory_space=pl.ANY),
                  pl.BlockSpec(memory_space=pl.ANY)],
        out_specs=pl.BlockSpec(memory_space=pl.ANY),
        out_shape=jax.ShapeDtypeStruct(xs.shape, xs.dtype),
        scratch_shapes=[pltpu.SemaphoreType.DMA(())],
        input_output_aliases={2: 0},
        compiler_params=_params("arbitrary"),
        name="dispatch",
    )(slots_flat, hu, xs)


def _expert_kernel(te_ref, nu_ref, xs_ref, wg_ref, wu_ref, wd_ref, o_ref):
    del te_ref

    @pl.when(pl.program_id(0) < nu_ref[0])
    def _():
        lo, hi = _unpack_bf16_pair(xs_ref[...])
        half = lo.shape[1]
        gate = (jnp.dot(lo, wg_ref[:half], preferred_element_type=F32)
                + jnp.dot(hi, wg_ref[half:], preferred_element_type=F32))
        up = (jnp.dot(lo, wu_ref[:half], preferred_element_type=F32)
              + jnp.dot(hi, wu_ref[half:], preferred_element_type=F32))
        hid = (_silu(gate) * up).astype(BF16)
        o_ref[...] = jnp.dot(hid, wd_ref[...], preferred_element_type=F32)

    @pl.when(pl.program_id(0) >= nu_ref[0])
    def _():
        o_ref[...] = jnp.zeros(o_ref.shape, o_ref.dtype)


def expert_ffn(tile_expert, n_used, xs, w_gate, w_up, w_down, tme):
    p, half = xs.shape
    e, d, f = w_gate.shape
    row_idx = lambda i, te, nu: (jnp.minimum(i, nu[0] - 1), 0)
    grid_spec = pltpu.PrefetchScalarGridSpec(
        num_scalar_prefetch=2,
        grid=(p // tme,),
        in_specs=[pl.BlockSpec((tme, half), row_idx),
                  pl.BlockSpec((None, d, f), lambda i, te, nu: (te[i], 0, 0)),
                  pl.BlockSpec((None, d, f), lambda i, te, nu: (te[i], 0, 0)),
                  pl.BlockSpec((None, f, d), lambda i, te, nu: (te[i], 0, 0))],
        out_specs=pl.BlockSpec((tme, d), lambda i, te, nu: (i, 0)),
    )
    return pl.pallas_call(
        _expert_kernel,
        grid_spec=grid_spec,
        out_shape=jax.ShapeDtypeStruct((p, d), F32),
        compiler_params=_params("arbitrary"),
        name="expert_ffn",
    )(tile_expert, n_used, xs, w_gate, w_up, w_down)


def _combine_kernel(slot_ref, hu_ref, x1_ref, g2_ref, gk_ref, wg_ref, wu_ref, wd_ref, ys_ref,
                    o_ref, rows_ref, sem, *, tmc):
    def row_copy(j, k):
        return pltpu.make_async_copy(ys_ref.at[pl.ds(slot_ref[j * N_ACTIVE + k], 1)],
                                     rows_ref.at[k, pl.ds(j, 1)], sem)

    def start(j, carry):
        for k in range(N_ACTIVE):
            row_copy(j, k).start()
        return carry

    def wait(j, carry):
        for k in range(N_ACTIVE):
            row_copy(j, k).wait()
        return carry

    lax.fori_loop(0, tmc, start, 0)
    lo, hi = _unpack_bf16_pair(hu_ref[...])
    half = lo.shape[1]
    gate = (jnp.dot(lo, wg_ref[:half], preferred_element_type=F32)
            + jnp.dot(hi, wg_ref[half:], preferred_element_type=F32))
    up = (jnp.dot(lo, wu_ref[:half], preferred_element_type=F32)
          + jnp.dot(hi, wu_ref[half:], preferred_element_type=F32))
    y = jnp.dot((_silu(gate) * up).astype(BF16), wd_ref[...], preferred_element_type=F32)
    lax.fori_loop(0, tmc, wait, 0)
    gk = gk_ref[...]
    for k in range(N_ACTIVE):
        y = y + gk[:, k:k + 1] * rows_ref[k]
    o_ref[...] = x1_ref[...] + _gate_rows(y, g2_ref[...])


def combine(slots_flat, hu, x1, gate2_g, gate_k, w_gs, w_us, w_ds, ys, tmc):
    t, d = x1.shape
    f = w_gs.shape[1]
    gm = tmc // MOD_GROUP
    row = lambda w: pl.BlockSpec((tmc, w), lambda i: (i, 0))
    const = lambda a, b: pl.BlockSpec((a, b), lambda i: (0, 0))
    return pl.pallas_call(
        functools.partial(_combine_kernel, tmc=tmc),
        grid=(t // tmc,),
        in_specs=[pl.BlockSpec((tmc * N_ACTIVE,), lambda i: (i,), memory_space=pltpu.SMEM),
                  row(d // 2), row(d), pl.BlockSpec((gm, d), lambda i: (i, 0)), row(N_ACTIVE),
                  const(d, f), const(d, f), const(f, d),
                  pl.BlockSpec(memory_space=pl.ANY)],
        out_specs=row(d),
        out_shape=jax.ShapeDtypeStruct((t, d), F32),
        scratch_shapes=[pltpu.VMEM((N_ACTIVE, tmc, d), F32), pltpu.SemaphoreType.DMA(())],
        compiler_params=_params("arbitrary"),
        name="combine",
    )(slots_flat, hu, x1, gate2_g, gate_k, w_gs, w_us, w_ds, ys)


def _group_rows(v, n):
    b, d = v.shape
    return jnp.broadcast_to(v[:, None, :], (b, n // MOD_GROUP, d)).reshape(b * n // MOD_GROUP, d)


def _pad_keys(a, l_pad):
    return jnp.pad(a, ((0, 0), (0, l_pad - a.shape[1]), (0, 0)))


def _token_stage(x, mod, w, caches):
    b, n, d = x.shape
    t = b * n
    shift1, scale1, gate1, shift2, scale2, gate2 = [_group_rows(m, n) for m in jnp.split(mod, 6, axis=-1)]
    xf = x.reshape(t, d)
    tm = _row_tile(t, 512)
    (h, qa, ka, va, kab, vab, qi, ki, kib, wi) = in_proj_a(
        xf, w["norm_mix"], scale1, shift1, w["w_in_a"], w["q_norm_a"], w["k_norm_a"], tm)
    qb, kb, vb, kbb, vbb = in_proj_b(h, w["w_in_b"], tm)

    r3 = lambda a: a.reshape(b, n, a.shape[-1])
    if caches is None:
        tq, tc = min(128, n), min(512, n)
        ya = dsa_attention(r3(qa), r3(qi), r3(wi), r3(kab), r3(vab), r3(kib),
                           q_pos0=0, n_keys=n, tq=tq, tc=tc)
        yb = stick_attention(r3(qb), r3(kbb), r3(vbb), q_pos0=0, tq=tq, tc=min(128, n))
    else:
        cka, cva, cki, ckb, cvb = caches
        past = cka.shape[1]
        n_keys = past + n
        tc_a, tc_b = 512, 128
        la = -(-n_keys // tc_a) * tc_a
        lb = -(-n_keys // tc_b) * tc_b
        cat = lambda c, new, lp: _pad_keys(
            jnp.concatenate([c.reshape(b, past, -1).astype(BF16), r3(new)], axis=1), lp)
        ya = dsa_attention(r3(qa), r3(qi), r3(wi), cat(cka, kab, la), cat(cva, vab, la), cat(cki, kib, la),
                           q_pos0=past, n_keys=n_keys, tq=n, tc=tc_a)
        yb = stick_attention(r3(qb), cat(ckb, kbb, lb), cat(cvb, vbb, lb), q_pos0=past, tq=n, tc=tc_b)

    merged = merge_branches(h, ya.reshape(t, A_WIDTH), yb.reshape(t, B_WIDTH),
                            w["w_ga"], w["w_gb"], w["w_branch_a"], w["w_branch_b"], tm, _row_tile(d, 512))
    x1, hu, gates_t, top_e = out_proj(merged, xf, gate1, scale2, shift2, w["norm_ffn"],
                                      w["w_out"], w["w_router_t"], w["b_router"], tm)
    rows = (ka.reshape(b, n, A_KV_HEADS, HEAD_DIM), va.reshape(b, n, A_KV_HEADS, HEAD_DIM),
            ki.reshape(b, n, IDX_DIM), kb.reshape(b, n, B_HEADS, HEAD_DIM), vb.reshape(b, n, B_HEADS, HEAD_DIM))
    return dict(x1=x1, hu=hu, gates_t=gates_t, top_e=top_e, gate2=gate2, rows=rows)


def _routing(gates_t, top_e, tme):
    e, t = gates_t.shape
    eids = jnp.arange(e, dtype=jnp.int32)[:, None, None]
    onehot = top_e[None, :, :] == eids
    sel = jnp.any(onehot, axis=1)
    counts = jnp.sum(sel, axis=1, dtype=jnp.int32)
    padded = (counts + tme - 1) // tme * tme
    ends = jnp.cumsum(padded)
    offsets = ends - padded
    rank = jnp.cumsum(sel.astype(jnp.int32), axis=1)
    slot_e = offsets[:, None] + rank - 1
    slots = jnp.sum(jnp.where(onehot, slot_e[:, None, :], 0), axis=0)
    gate_k = jnp.sum(jnp.where(onehot, gates_t[:, None, :], 0.0), axis=0)
    n_tiles = (t * N_ACTIVE + e * tme) // tme
    tile_start = jnp.arange(n_tiles, dtype=jnp.int32) * tme
    tile_expert = jnp.minimum(jnp.searchsorted(ends, tile_start, side="right"), e - 1).astype(jnp.int32)
    n_used = (ends[-1] // tme).astype(jnp.int32).reshape(1)
    return slots.T.reshape(-1), gate_k.T, tile_expert, n_used, n_tiles * tme


def kernel(x_prompt, x_sample, cache_a_k, cache_a_v, cache_a_idx_k, cache_b_k, cache_b_v, c_prompt, c_sample,
           norm_mix, norm_ffn, w_ada, b_ada, w_in, q_norm_a, k_norm_a, w_branch_a, w_branch_b, w_out,
           w_router, b_router, w_gate_e, w_up_e, w_down_e, w_gate_s, w_up_s, w_down_s):
    depth = norm_mix.shape[0]
    d = x_prompt.shape[-1]
    bp, bs = c_prompt.shape[0], c_sample.shape[0]
    c_all = jnp.concatenate([c_prompt, c_sample], axis=0)
    c_all = jnp.pad(c_all, ((0, -(bp + bs) % 16), (0, 0)))
    tme = 256

    xp, xs = x_prompt, x_sample
    prompt_rows, sample_rows = [], []
    for layer in range(depth):
        wl = w_in[layer]
        o_idx = A_WIDTH + 2 * A_KV_WIDTH + IDX_Q_WIDTH
        o_b = o_idx + IDX_DIM + IDX_HEADS
        o_g = o_b + 3 * B_WIDTH
        w_in_a = jnp.concatenate(
            [wl[:, :o_b], jnp.zeros((d, LANES - IDX_DIM - IDX_HEADS), wl.dtype)], axis=1).astype(BF16)
        w = dict(
            norm_mix=norm_mix[layer].reshape(1, d), norm_ffn=norm_ffn[layer].reshape(1, d),
            q_norm_a=q_norm_a[layer].reshape(1, HEAD_DIM), k_norm_a=k_norm_a[layer].reshape(1, HEAD_DIM),
            w_in_a=w_in_a, w_in_b=wl[:, o_b:o_g].astype(BF16),
            w_ga=wl[:, o_g:o_g + d].astype(BF16), w_gb=wl[:, o_g + d:].astype(BF16),
            w_branch_a=w_branch_a[layer].astype(BF16), w_branch_b=w_branch_b[layer].astype(BF16),
            w_out=w_out[layer].astype(BF16), w_router_t=w_router[layer].T.astype(BF16),
            b_router=b_router[layer].reshape(N_EXPERTS, 1))
        mod = ada_mod(c_all, w_ada[layer], b_ada[layer])
        caches = (cache_a_k[layer], cache_a_v[layer], cache_a_idx_k[layer], cache_b_k[layer], cache_b_v[layer])
        sp = _token_stage(xp, mod[:bp], w, None)
        ss = _token_stage(xs, mod[bp:bp + bs], w, caches)
        prompt_rows.append(sp["rows"])
        sample_rows.append(ss["rows"])

        tp, ts = sp["x1"].shape[0], ss["x1"].shape[0]
        gates_t = jnp.concatenate([sp["gates_t"], ss["gates_t"]], axis=1)
        top_e = jnp.concatenate([sp["top_e"], ss["top_e"]], axis=1)
        slots, gate_k, tile_expert, n_used, p_rows = _routing(gates_t, top_e, tme)
        xsorted = jnp.zeros((p_rows, d // 2), jnp.uint32)
        xsorted = dispatch(slots[:tp * N_ACTIVE], sp["hu"], xsorted, _row_tile(tp, 128))
        xsorted = dispatch(slots[tp * N_ACTIVE:], ss["hu"], xsorted, _row_tile(ts, 128))
        ysorted = expert_ffn(tile_expert, n_used, xsorted, w_gate_e[layer].astype(BF16),
                             w_up_e[layer].astype(BF16), w_down_e[layer].astype(BF16), tme)
        shared = (w_gate_s[layer].astype(BF16), w_up_s[layer].astype(BF16), w_down_s[layer].astype(BF16))
        tmc = 256
        yp = combine(slots[:tp * N_ACTIVE], sp["hu"], sp["x1"], sp["gate2"], gate_k[:tp], *shared, ysorted,
                     _row_tile(tp, tmc))
        ys = combine(slots[tp * N_ACTIVE:], ss["hu"], ss["x1"], ss["gate2"], gate_k[tp:], *shared, ysorted,
                     _row_tile(ts, tmc))
        xp = yp.reshape(xp.shape)
        xs = ys.reshape(xs.shape)

    stack = lambda rows: tuple(jnp.stack(r) for r in zip(*rows))
    return (xp, xs) + stack(prompt_rows) + stack(sample_rows)
```

```python
import functools

import jax
import jax.numpy as jnp
from jax import lax
from jax.experimental import pallas as pl
from jax.experimental.pallas import tpu as pltpu

CHUNK = 64
HEAD_DIM = 128
A_HEADS = 8
A_KV_HEADS = 2
A_GROUP = A_HEADS // A_KV_HEADS
IDX_HEADS = 8
IDX_DIM = 64
TOPK_KEYS = 256
B_HEADS = 8
N_EXPERTS = 64
N_ACTIVE = 8
N_GROUPS = 8
N_ACTIVE_GROUPS = 4
GROUP_SIZE = N_EXPERTS // N_GROUPS
ROUTED_SCALE = 2.5
EPS = 1e-6

A_WIDTH = A_HEADS * HEAD_DIM
A_KV_WIDTH = A_KV_HEADS * HEAD_DIM
IDX_Q_WIDTH = IDX_HEADS * IDX_DIM
B_WIDTH = B_HEADS * HEAD_DIM

LANES = 128
MOD_GROUP = 32
VMEM_LIMIT = 56 * 1024 * 1024
NEG_BIG = -1e30
STICK_UNDERFLOW = -110.0
INT_MIN = -(2 ** 31)

F32 = jnp.float32
BF16 = jnp.bfloat16
NT_DIMS = (((1,), (1,)), ((), ()))


def _params(*sem):
    return pltpu.CompilerParams(dimension_semantics=sem, vmem_limit_bytes=VMEM_LIMIT)


def _silu(x):
    return x * jax.nn.sigmoid(x)


def _rms(x, gain):
    return x * lax.rsqrt(jnp.mean(x * x, axis=-1, keepdims=True) + EPS) * gain


def _modulate(y, scale_g, shift_g):
    tm, d = y.shape
    y3 = y.reshape(tm // MOD_GROUP, MOD_GROUP, d)
    out = y3 * (1.0 + scale_g[:, None, :]) + shift_g[:, None, :]
    return out.reshape(tm, d)


def _gate_rows(y, gate_g):
    tm, d = y.shape
    return (y.reshape(tm // MOD_GROUP, MOD_GROUP, d) * gate_g[:, None, :]).reshape(tm, d)


def _row_tile(t, pref):
    tm = min(pref, t)
    while t % tm:
        tm //= 2
    return tm


def _ada_kernel(c_ref, w_ref, b_ref, o_ref):
    s = _silu(c_ref[...]).astype(BF16)
    o_ref[...] = jnp.dot(s, w_ref[...].astype(BF16), preferred_element_type=F32) + b_ref[...]


def ada_mod(c, w_ada, b_ada):
    r, d = c.shape
    n = w_ada.shape[1]
    tn = _row_tile(n, 1024)
    return pl.pallas_call(
        _ada_kernel,
        grid=(n // tn,),
        in_specs=[pl.BlockSpec((r, d), lambda j: (0, 0)),
                  pl.BlockSpec((d, tn), lambda j: (0, j)),
                  pl.BlockSpec((1, tn), lambda j: (0, j))],
        out_specs=pl.BlockSpec((r, tn), lambda j: (0, j)),
        out_shape=jax.ShapeDtypeStruct((r, n), F32),
        compiler_params=_params("arbitrary"),
        name="ada_mod",
    )(c, w_ada, b_ada.reshape(1, n))


def _in_a_kernel(x_ref, nw_ref, sc_ref, sh_ref, w_ref, qn_ref, kn_ref,
                 h_ref, qa_ref, ka_ref, va_ref, kab_ref, vab_ref, qi_ref, ki_ref, kib_ref, wi_ref):
    h = _modulate(_rms(x_ref[...], nw_ref[...]), sc_ref[...], sh_ref[...])
    hb = h.astype(BF16)
    h_ref[...] = hb
    off = 0
    for hd in range(A_HEADS):
        q = jnp.dot(hb, w_ref[:, off:off + HEAD_DIM], preferred_element_type=F32)
        qa_ref[:, hd * HEAD_DIM:(hd + 1) * HEAD_DIM] = _rms(q, qn_ref[...]).astype(BF16)
        off += HEAD_DIM
    for hd in range(A_KV_HEADS):
        k = jnp.dot(hb, w_ref[:, off:off + HEAD_DIM], preferred_element_type=F32)
        k = _rms(k, kn_ref[...])
        ka_ref[:, hd * HEAD_DIM:(hd + 1) * HEAD_DIM] = k
        kab_ref[:, hd * HEAD_DIM:(hd + 1) * HEAD_DIM] = k.astype(BF16)
        off += HEAD_DIM
    v = jnp.dot(hb, w_ref[:, off:off + A_KV_WIDTH], preferred_element_type=F32)
    va_ref[...] = v
    vab_ref[...] = v.astype(BF16)
    off += A_KV_WIDTH
    qi_ref[...] = jnp.dot(hb, w_ref[:, off:off + IDX_Q_WIDTH], preferred_element_type=F32).astype(BF16)
    off += IDX_Q_WIDTH
    kw = jnp.dot(hb, w_ref[:, off:off + LANES], preferred_element_type=F32)
    ki = kw[:, :IDX_DIM]
    ki_ref[...] = ki
    kib_ref[...] = ki.astype(BF16)
    wi_ref[...] = kw[:, IDX_DIM:IDX_DIM + IDX_HEADS]


def in_proj_a(x, norm_w, scale_g, shift_g, w_a, q_norm, k_norm, tm):
    t, d = x.shape
    wa = w_a.shape[1]
    gm = tm // MOD_GROUP
    row = lambda w: pl.BlockSpec((tm, w), lambda i: (i, 0))
    const = lambda a, b: pl.BlockSpec((a, b), lambda i: (0, 0))
    outs = [(d, BF16), (A_WIDTH, BF16), (A_KV_WIDTH, F32), (A_KV_WIDTH, F32), (A_KV_WIDTH, BF16),
            (A_KV_WIDTH, BF16), (IDX_Q_WIDTH, BF16), (IDX_DIM, F32), (IDX_DIM, BF16), (IDX_HEADS, F32)]
    return pl.pallas_call(
        _in_a_kernel,
        grid=(t // tm,),
        in_specs=[row(d), const(1, d), pl.BlockSpec((gm, d), lambda i: (i, 0)),
                  pl.BlockSpec((gm, d), lambda i: (i, 0)), const(d, wa),
                  const(1, HEAD_DIM), const(1, HEAD_DIM)],
        out_specs=[row(w) for w, _ in outs],
        out_shape=[jax.ShapeDtypeStruct((t, w), dt) for w, dt in outs],
        compiler_params=_params("arbitrary"),
        name="in_proj_a",
    )(x, norm_w, scale_g, shift_g, w_a, q_norm, k_norm)


def _in_b_kernel(h_ref, w_ref, qb_ref, kb_ref, vb_ref, kbb_ref, vbb_ref):
    hb = h_ref[...]
    qb_ref[...] = jnp.dot(hb, w_ref[:, :B_WIDTH], preferred_element_type=F32).astype(BF16)
    k = jnp.dot(hb, w_ref[:, B_WIDTH:2 * B_WIDTH], preferred_element_type=F32)
    kb_ref[...] = k
    kbb_ref[...] = k.astype(BF16)
    v = jnp.dot(hb, w_ref[:, 2 * B_WIDTH:], preferred_element_type=F32)
    vb_ref[...] = v
    vbb_ref[...] = v.astype(BF16)


def in_proj_b(h, w_b, tm):
    t, d = h.shape
    row = lambda w: pl.BlockSpec((tm, w), lambda i: (i, 0))
    outs = [BF16, F32, F32, BF16, BF16]
    return pl.pallas_call(
        _in_b_kernel,
        grid=(t // tm,),
        in_specs=[row(d), pl.BlockSpec((d, 3 * B_WIDTH), lambda i: (0, 0))],
        out_specs=[row(B_WIDTH) for _ in outs],
        out_shape=[jax.ShapeDtypeStruct((t, B_WIDTH), dt) for dt in outs],
        compiler_params=_params("arbitrary"),
        name="in_proj_b",
    )(h, w_b)


def _dsa_kernel(qa_ref, qi_ref, wi_ref, k_ref, v_ref, ki_ref, o_ref,
                key_ref, m_ref, l_ref, acc_ref, p_ref, al_ref,
                *, q_pos0, n_keys, n_sel, tq, tc):
    i = pl.program_id(1)
    q0 = q_pos0 + i * tq
    kmax = jnp.minimum(((q0 + tq - 1) // CHUNK + 1) * CHUNK, n_keys)
    nch = (kmax + tc - 1) // tc
    qpos = q0 + lax.broadcasted_iota(jnp.int32, (tq, 1), 0)
    qchunk = qpos // CHUNK

    head_w = wi_ref[...] * (IDX_HEADS ** -0.5)
    qi_heads = [qi_ref[:, h * IDX_DIM:(h + 1) * IDX_DIM] for h in range(IDX_HEADS)]

    def score_body(c, carry):
        start = pl.multiple_of(c * tc, tc)
        kic = ki_ref[pl.ds(start, tc), :]
        s = jnp.zeros((tq, tc), F32)
        for h in range(IDX_HEADS):
            lg = lax.dot_general(qi_heads[h], kic, NT_DIMS, preferred_element_type=F32) * (IDX_DIM ** -0.5)
            s = s + jnp.maximum(lg, 0.0) * head_w[:, h:h + 1]
        kpos = start + lax.broadcasted_iota(jnp.int32, (1, tc), 1)
        adm = jnp.logical_and(kpos // CHUNK <= qchunk, kpos < n_keys)
        bits = lax.bitcast_convert_type(s, jnp.int32)
        key = bits ^ ((bits >> 31) & 0x7FFFFFFF)
        key_ref[c] = jnp.where(adm, key, INT_MIN)
        return carry

    lax.fori_loop(0, nch, score_body, 0)

    def count_ge(cand):
        def body(c, acc):
            ge = jnp.where(key_ref[c] >= cand, 1.0, 0.0)
            part = ge[:, :LANES]
            for j in range(1, tc // LANES):
                part = part + ge[:, j * LANES:(j + 1) * LANES]
            return acc + part
        acc = lax.fori_loop(0, nch, body, jnp.zeros((tq, LANES), F32))
        return jnp.sum(acc, axis=1, keepdims=True)

    thr = jnp.where(count_ge(jnp.zeros((tq, 1), jnp.int32)) >= n_sel, 0, INT_MIN).astype(jnp.int32)

    def bit_body(b, thr):
        cand = thr | jnp.left_shift(jnp.int32(1), 30 - b)
        return jnp.where(count_ge(cand) >= n_sel, cand, thr)

    thr = lax.fori_loop(0, 31, bit_body, thr)

    scale = HEAD_DIM ** -0.5
    for g in range(A_KV_HEADS):
        q4 = jnp.concatenate(
            [qa_ref[:, (g * A_GROUP + r) * HEAD_DIM:(g * A_GROUP + r + 1) * HEAD_DIM] for r in range(A_GROUP)],
            axis=0)
        m_ref[...] = jnp.full(m_ref.shape, NEG_BIG, F32)
        l_ref[...] = jnp.zeros(l_ref.shape, F32)
        acc_ref[...] = jnp.zeros(acc_ref.shape, F32)

        def att_body(c, carry):
            start = pl.multiple_of(c * tc, tc)
            kc = k_ref[pl.ds(start, tc), g * HEAD_DIM:(g + 1) * HEAD_DIM]
            vc = v_ref[pl.ds(start, tc), g * HEAD_DIM:(g + 1) * HEAD_DIM]
            lg = lax.dot_general(q4, kc, NT_DIMS, preferred_element_type=F32) * scale
            key = key_ref[c]
            sel = jnp.logical_and(key >= thr, key != INT_MIN)
            kpos = start + lax.broadcasted_iota(jnp.int32, (1, tc), 1)
            dist = jnp.abs(qpos - kpos).astype(F32)
            for r in range(A_GROUP):
                slope = 2.0 ** (-8.0 * (g * A_GROUP + r + 1) / A_HEADS)
                rows = slice(r * tq, (r + 1) * tq)
                t = jnp.where(sel, lg[rows] - slope * dist, NEG_BIG)
                m_old = m_ref[rows]
                m_new = jnp.maximum(m_old, jnp.max(t, axis=1, keepdims=True))
                p = jnp.exp(t - m_new)
                alpha = jnp.exp(m_old - m_new)
                l_ref[rows] = alpha * l_ref[rows] + jnp.sum(p, axis=1, keepdims=True)
                m_ref[rows] = m_new
                al_ref[rows] = alpha
                p_ref[rows] = p.astype(BF16)
            pv = jnp.dot(p_ref[...], vc, preferred_element_type=F32)
            acc_ref[...] = al_ref[...] * acc_ref[...] + pv
            return carry

        lax.fori_loop(0, nch, att_body, 0)
        for r in range(A_GROUP):
            rows = slice(r * tq, (r + 1) * tq)
            hd = g * A_GROUP + r
            o_ref[:, hd * HEAD_DIM:(hd + 1) * HEAD_DIM] = (acc_ref[rows] / l_ref[rows]).astype(BF16)


def dsa_attention(qa, qi, wi, k, v, ki, *, q_pos0, n_keys, tq, tc):
    b, n, _ = qa.shape
    l_pad = k.shape[1]
    n_sel = min(TOPK_KEYS, n_keys // 4)
    qspec = lambda w: pl.BlockSpec((None, tq, w), lambda bi, i: (bi, i, 0))
    kspec = lambda w: pl.BlockSpec((None, l_pad, w), lambda bi, i: (bi, 0, 0))
    kern = functools.partial(_dsa_kernel, q_pos0=q_pos0, n_keys=n_keys, n_sel=n_sel, tq=tq, tc=tc)
    return pl.pallas_call(
        kern,
        grid=(b, n // tq),
        in_specs=[qspec(A_WIDTH), qspec(IDX_Q_WIDTH), qspec(IDX_HEADS),
                  kspec(A_KV_WIDTH), kspec(A_KV_WIDTH), kspec(IDX_DIM)],
        out_specs=qspec(A_WIDTH),
        out_shape=jax.ShapeDtypeStruct((b, n, A_WIDTH), BF16),
        scratch_shapes=[pltpu.VMEM((l_pad // tc, tq, tc), jnp.int32),
                        pltpu.VMEM((A_GROUP * tq, 1), F32),
                        pltpu.VMEM((A_GROUP * tq, 1), F32),
                        pltpu.VMEM((A_GROUP * tq, HEAD_DIM), F32),
                        pltpu.VMEM((A_GROUP * tq, tc), BF16),
                        pltpu.VMEM((A_GROUP * tq, 1), F32)],
        compiler_params=_params("arbitrary", "arbitrary"),
        name="dsa",
    )(qa, qi, wi, k, v, ki)


def _stick_kernel(q_ref, k_ref, v_ref, o_ref, *, q_pos0, tq, tc):
    i = pl.program_id(2)
    t0 = q_pos0 + i * tq
    tpos = t0 + lax.broadcasted_iota(jnp.int32, (tq, 1), 0)
    c_first = (t0 + tq - 2) // tc
    q = q_ref[...]
    upper = (lax.broadcasted_iota(jnp.int32, (tc, tc), 0)
             > lax.broadcasted_iota(jnp.int32, (tc, tc), 1)).astype(BF16)
    scale = HEAD_DIM ** -0.5

    def cond(carry):
        n, tail, _ = carry
        return jnp.logical_and(n <= c_first, jnp.max(tail) > STICK_UNDERFLOW)

    def body(carry):
        n, tail, acc = carry
        c = c_first - n
        start = pl.multiple_of(c * tc, tc)
        kc = k_ref[pl.ds(start, tc), :]
        vc = v_ref[pl.ds(start, tc), :]
        z = lax.dot_general(q, kc, NT_DIMS, preferred_element_type=F32) * scale
        spos = start + lax.broadcasted_iota(jnp.int32, (1, tc), 1)
        strict = spos < tpos
        softplus = jnp.log(1.0 + jnp.exp(-jnp.abs(z)))
        log_go = jnp.minimum(z, 0.0) - softplus
        log_stay = jnp.where(strict, jnp.minimum(-z, 0.0) - softplus, 0.0)
        hi = log_stay.astype(BF16)
        lo = (log_stay - hi.astype(F32)).astype(BF16)
        after = (jnp.dot(hi, upper, preferred_element_type=F32)
                 + jnp.dot(lo, upper, preferred_element_type=F32) + tail)
        w = jnp.where(strict, jnp.exp(log_go + after), 0.0)
        acc = acc + jnp.dot(w.astype(BF16), vc, preferred_element_type=F32)
        tail = tail + jnp.sum(log_stay, axis=1, keepdims=True)
        return n + 1, tail, acc

    _, _, acc = lax.while_loop(cond, body,
                               (jnp.int32(0), jnp.zeros((tq, 1), F32), jnp.zeros((tq, HEAD_DIM), F32)))
    o_ref[...] = acc.astype(BF16)


def stick_attention(qb, k, v, *, q_pos0, tq, tc):
    b, n, _ = qb.shape
    l_pad = k.shape[1]
    qspec = pl.BlockSpec((None, tq, HEAD_DIM), lambda bi, h, i: (bi, i, h))
    kspec = pl.BlockSpec((None, l_pad, HEAD_DIM), lambda bi, h, i: (bi, 0, h))
    kern = functools.partial(_stick_kernel, q_pos0=q_pos0, tq=tq, tc=tc)
    return pl.pallas_call(
        kern,
        grid=(b, B_HEADS, n // tq),
        in_specs=[qspec, kspec, kspec],
        out_specs=qspec,
        out_shape=jax.ShapeDtypeStruct((b, n, B_WIDTH), BF16),
        compiler_params=_params("arbitrary", "arbitrary", "arbitrary"),
        name="stick",
    )(qb, k, v)


def _merge_kernel(h_ref, ya_ref, yb_ref, wga_ref, wgb_ref, wa_ref, wb_ref, o_ref):
    hb = h_ref[...]
    ga = jax.nn.sigmoid(jnp.dot(hb, wga_ref[...], preferred_element_type=F32))
    gb = jax.nn.sigmoid(jnp.dot(hb, wgb_ref[...], preferred_element_type=F32))
    pa = jnp.dot(ya_ref[...], wa_ref[...], preferred_element_type=F32)
    pb = jnp.dot(yb_ref[...], wb_ref[...], preferred_element_type=F32)
    o_ref[...] = (ga * pa + gb * pb).astype(BF16)


def merge_branches(h, ya, yb, w_ga, w_gb, w_a, w_b, tm, tn):
    t, d = h.shape
    row = lambda w: pl.BlockSpec((tm, w), lambda i, j: (i, 0))
    col = lambda k: pl.BlockSpec((k, tn), lambda i, j: (0, j))
    return pl.pallas_call(
        _merge_kernel,
        grid=(t // tm, d // tn),
        in_specs=[row(d), row(A_WIDTH), row(B_WIDTH), col(d), col(d), col(A_WIDTH), col(B_WIDTH)],
        out_specs=pl.BlockSpec((tm, tn), lambda i, j: (i, j)),
        out_shape=jax.ShapeDtypeStruct((t, d), BF16),
        compiler_params=_params("arbitrary", "arbitrary"),
        name="merge",
    )(h, ya, yb, w_ga, w_gb, w_a, w_b)


def _pack_bf16_pair(lo, hi):
    lo_bits = lax.bitcast_convert_type(lo.astype(BF16).astype(F32), jnp.uint32)
    hi_bits = lax.bitcast_convert_type(hi.astype(BF16).astype(F32), jnp.uint32)
    return (lo_bits >> 16) | (hi_bits & jnp.uint32(0xFFFF0000))


def _unpack_bf16_pair(u):
    lo = lax.bitcast_convert_type(u << 16, F32).astype(BF16)
    hi = lax.bitcast_convert_type(u & jnp.uint32(0xFFFF0000), F32).astype(BF16)
    return lo, hi


def _first_max(vals, idx, axes, n):
    m = vals
    for ax in axes:
        m = jnp.max(m, axis=ax, keepdims=True)
    first = jnp.where(vals == m, idx, n)
    for ax in axes:
        first = jnp.min(first, axis=ax, keepdims=True)
    return m, first


def _sum01(a):
    return jnp.sum(jnp.sum(a, axis=0, keepdims=True), axis=1, keepdims=True)


def _out_kernel(mg_ref, x_ref, g1_ref, sc_ref, sh_ref, nw_ref, wo_ref, wr_ref, br_ref, cnt0_ref,
                x1_ref, hu_ref, tope_ref, rank_ref, gatek_ref, cnt_out_ref, cnt_ref):
    tm, d = x_ref.shape
    proj = jnp.dot(mg_ref[...], wo_ref[...], preferred_element_type=F32)
    x1 = x_ref[...] + _gate_rows(proj, g1_ref[...])
    x1_ref[...] = x1
    h2 = _modulate(_rms(x1, nw_ref[...]), sc_ref[...], sh_ref[...])
    hu_ref[...] = _pack_bf16_pair(h2[:, :d // 2], h2[:, d // 2:])

    logits = lax.dot_general(wr_ref[...], h2.astype(BF16), NT_DIMS, preferred_element_type=F32)
    aff = jax.nn.sigmoid(logits)
    shape3 = (N_GROUPS, GROUP_SIZE, tm)
    aff3 = aff.reshape(shape3)
    biased = (aff + br_ref[...]).reshape(shape3)
    io_in = lax.broadcasted_iota(jnp.int32, shape3, 1)
    m1, i1 = _first_max(biased, io_in, (1,), GROUP_SIZE)
    m2 = jnp.max(jnp.where(io_in == i1, -jnp.inf, biased), axis=1, keepdims=True)
    gscore = m1 + m2
    gio = lax.broadcasted_iota(jnp.int32, gscore.shape, 0)
    keep = jnp.zeros(gscore.shape, jnp.bool_)
    for _ in range(N_ACTIVE_GROUPS):
        _, first = _first_max(gscore, gio, (0,), N_GROUPS)
        hit = gio == first
        keep = jnp.logical_or(keep, hit)
        gscore = jnp.where(hit, -jnp.inf, gscore)
    masked = jnp.where(jnp.broadcast_to(keep, shape3), biased, -jnp.inf)
    eio = lax.broadcasted_iota(jnp.int32, shape3, 0) * GROUP_SIZE + io_in
    sel = jnp.zeros(shape3, jnp.bool_)
    hits = []
    for _ in range(N_ACTIVE):
        _, first = _first_max(masked, eio, (0, 1), N_EXPERTS)
        hit = eio == first
        sel = jnp.logical_or(sel, hit)
        masked = jnp.where(hit, -jnp.inf, masked)
        hits.append((hit, first.reshape(1, tm)))
    sel_aff = jnp.where(sel, aff3, 0.0)
    gates = sel_aff / _sum01(sel_aff) * ROUTED_SCALE

    @pl.when(pl.program_id(0) == 0)
    def _():
        cnt_ref[...] = cnt0_ref[...]

    sel2 = jnp.where(sel, 1.0, 0.0).reshape(N_EXPERTS, tm)
    incl = (lax.broadcasted_iota(jnp.int32, (tm, tm), 0)
            <= lax.broadcasted_iota(jnp.int32, (tm, tm), 1)).astype(BF16)
    rank = jnp.dot(sel2.astype(BF16), incl, preferred_element_type=F32) + cnt_ref[...]
    cnt_ref[...] = cnt_ref[...] + jnp.sum(sel2, axis=1, keepdims=True)
    cnt_out_ref[...] = cnt_ref[...]
    rank3 = rank.reshape(shape3)
    tope_ref[...] = jnp.concatenate([first for _, first in hits], axis=0)
    rank_ref[...] = jnp.concatenate(
        [_sum01(jnp.where(hit, rank3, 0.0)).reshape(1, tm) for hit, _ in hits], axis=0).astype(jnp.int32)
    gatek_ref[...] = jnp.concatenate(
        [_sum01(jnp.where(hit, gates, 0.0)).reshape(1, tm) for hit, _ in hits], axis=0)


def out_proj(merged, x, gate1_g, scale_g, shift_g, norm_w, w_out, w_router_t, b_router, count0, tm):
    t, d = x.shape
    gm = tm // MOD_GROUP
    row = lambda w: pl.BlockSpec((tm, w), lambda i: (i, 0))
    grp = pl.BlockSpec((gm, d), lambda i: (i, 0))
    const = lambda a, b: pl.BlockSpec((a, b), lambda i: (0, 0))
    per_k = pl.BlockSpec((N_ACTIVE, tm), lambda i: (0, i))
    return pl.pallas_call(
        _out_kernel,
        grid=(t // tm,),
        in_specs=[row(d), row(d), grp, grp, grp, const(1, d), const(d, d),
                  const(N_EXPERTS, d), const(N_EXPERTS, 1), const(N_EXPERTS, 1)],
        out_specs=[row(d), row(d // 2), per_k, per_k, per_k, const(N_EXPERTS, 1)],
        out_shape=[jax.ShapeDtypeStruct((t, d), F32), jax.ShapeDtypeStruct((t, d // 2), jnp.uint32),
                   jax.ShapeDtypeStruct((N_ACTIVE, t), jnp.int32), jax.ShapeDtypeStruct((N_ACTIVE, t), jnp.int32),
                   jax.ShapeDtypeStruct((N_ACTIVE, t), F32), jax.ShapeDtypeStruct((N_EXPERTS, 1), F32)],
        scratch_shapes=[pltpu.VMEM((N_EXPERTS, 1), F32)],
        compiler_params=_params("arbitrary"),
        name="out_proj",
    )(merged, x, gate1_g, scale_g, shift_g, norm_w, w_out, w_router_t, b_router, count0)


def _pad_fill_kernel(start_ref, cnt_ref, nu_ref, xs_ref, zero_ref, sem, *, tme, n_tiles):
    zero_ref[...] = jnp.zeros(zero_ref.shape, zero_ref.dtype)

    def pad_copy(e, j):
        return pltpu.make_async_copy(zero_ref.at[pl.ds(0, 1)], xs_ref.at[pl.ds(start_ref[e] + j, 1)], sem.at[0])

    def tile_copy(i):
        return pltpu.make_async_copy(zero_ref, xs_ref.at[pl.ds(pl.multiple_of(i * tme, tme), tme)], sem.at[1])

    def for_each_copy(wait):
        def pad_rows(e, carry):
            def one(j, c):
                pad_copy(e, j).wait() if wait else pad_copy(e, j).start()
                return c
            return lax.fori_loop(0, cnt_ref[e], one, carry)

        def tail_tiles(i, carry):
            tile_copy(i).wait() if wait else tile_copy(i).start()
            return carry

        lax.fori_loop(0, N_EXPERTS, pad_rows, 0)
        lax.fori_loop(nu_ref[0], n_tiles, tail_tiles, 0)

    for_each_copy(wait=False)
    for_each_copy(wait=True)


def pad_fill(pad_start, pad_count, n_used, p_rows, width, tme):
    grid_spec = pltpu.PrefetchScalarGridSpec(
        num_scalar_prefetch=3, grid=(1,), in_specs=[],
        out_specs=pl.BlockSpec(memory_space=pl.ANY),
        scratch_shapes=[pltpu.VMEM((tme, width), jnp.uint32), pltpu.SemaphoreType.DMA((2,))])
    return pl.pallas_call(
        functools.partial(_pad_fill_kernel, tme=tme, n_tiles=p_rows // tme),
        grid_spec=grid_spec,
        out_shape=jax.ShapeDtypeStruct((p_rows, width), jnp.uint32),
        compiler_params=_params("arbitrary"),
        name="pad_fill",
    )(pad_start, pad_count, n_used)


def _dispatch_kernel(slot_ref, hu_ref, xs_in_ref, xs_ref, sem, *, td):
    del xs_in_ref

    def row_copy(j, k):
        return pltpu.make_async_copy(hu_ref.at[pl.ds(j, 1)],
                                     xs_ref.at[pl.ds(slot_ref[j * N_ACTIVE + k], 1)], sem)

    def start(j, carry):
        for k in range(N_ACTIVE):
            row_copy(j, k).start()
        return carry

    def wait(j, carry):
        for k in range(N_ACTIVE):
            row_copy(j, k).wait()
        return carry

    lax.fori_loop(0, td, start, 0)
    lax.fori_loop(0, td, wait, 0)


def dispatch(slots_flat, hu, xs, td):
    t, w = hu.shape
    return pl.pallas_call(
        functools.partial(_dispatch_kernel, td=td),
        grid=(t // td,),
        in_specs=[pl.BlockSpec((td * N_ACTIVE,), lambda i: (i,), memory_space=pltpu.SMEM),
                  pl.BlockSpec((td, w), lambda i: (i, 0)),
                  pl.BlockSpec(memory_space=pl.ANY)],
        out_specs=pl.BlockSpec(memory_space=pl.ANY),
        out_shape=jax.ShapeDtypeStruct(xs.shape, xs.dtype),
        scratch_shapes=[pltpu.SemaphoreType.DMA(())],
        input_output_aliases={2: 0},
        compiler_params=_params("arbitrary"),
        name="dispatch",
    )(slots_flat, hu, xs)


def _expert_kernel(te_ref, nu_ref, xs_ref, wg_ref, wu_ref, wd_ref, o_ref):
    del te_ref

    @pl.when(pl.program_id(0) < nu_ref[0])
    def _():
        lo, hi = _unpack_bf16_pair(xs_ref[...])
        half = lo.shape[1]
        gate = (jnp.dot(lo, wg_ref[:half], preferred_element_type=F32)
                + jnp.dot(hi, wg_ref[half:], preferred_element_type=F32))
        up = (jnp.dot(lo, wu_ref[:half], preferred_element_type=F32)
              + jnp.dot(hi, wu_ref[half:], preferred_element_type=F32))
        hid = (_silu(gate) * up).astype(BF16)
        o_ref[...] = jnp.dot(hid, wd_ref[...], preferred_element_type=F32)

    @pl.when(pl.program_id(0) >= nu_ref[0])
    def _():
        o_ref[...] = jnp.zeros(o_ref.shape, o_ref.dtype)


def expert_ffn(tile_expert, n_used, xs, w_gate, w_up, w_down, tme):
    p, half = xs.shape
    e, d, f = w_gate.shape
    row_idx = lambda i, te, nu: (jnp.minimum(i, nu[0] - 1), 0)
    grid_spec = pltpu.PrefetchScalarGridSpec(
        num_scalar_prefetch=2,
        grid=(p // tme,),
        in_specs=[pl.BlockSpec((tme, half), row_idx),
                  pl.BlockSpec((None, d, f), lambda i, te, nu: (te[i], 0, 0)),
                  pl.BlockSpec((None, d, f), lambda i, te, nu: (te[i], 0, 0)),
                  pl.BlockSpec((None, f, d), lambda i, te, nu: (te[i], 0, 0))],
        out_specs=pl.BlockSpec((tme, d), lambda i, te, nu: (i, 0)),
    )
    return pl.pallas_call(
        _expert_kernel,
        grid_spec=grid_spec,
        out_shape=jax.ShapeDtypeStruct((p, d), F32),
        compiler_params=_params("arbitrary"),
        name="expert_ffn",
    )(tile_expert, n_used, xs, w_gate, w_up, w_down)


def _combine_kernel(slot_ref, hu_ref, x1_ref, g2_ref, gk_ref, wg_ref, wu_ref, wd_ref, ys_ref,
                    o_ref, rows_ref, sem, *, tmc):
    def row_copy(j, k):
        return pltpu.make_async_copy(ys_ref.at[pl.ds(slot_ref[j * N_ACTIVE + k], 1)],
                                     rows_ref.at[k, pl.ds(j, 1)], sem)

    def start(j, carry):
        for k in range(N_ACTIVE):
            row_copy(j, k).start()
        return carry

    def wait(j, carry):
        for k in range(N_ACTIVE):
            row_copy(j, k).wait()
        return carry

    lax.fori_loop(0, tmc, start, 0)
    lo, hi = _unpack_bf16_pair(hu_ref[...])
    half = lo.shape[1]
    gate = (jnp.dot(lo, wg_ref[:half], preferred_element_type=F32)
            + jnp.dot(hi, wg_ref[half:], preferred_element_type=F32))
    up = (jnp.dot(lo, wu_ref[:half], preferred_element_type=F32)
          + jnp.dot(hi, wu_ref[half:], preferred_element_type=F32))
    y = jnp.dot((_silu(gate) * up).astype(BF16), wd_ref[...], preferred_element_type=F32)
    lax.fori_loop(0, tmc, wait, 0)
    gk = gk_ref[...]
    for k in range(N_ACTIVE):
        y = y + gk[:, k:k + 1] * rows_ref[k]
    o_ref[...] = x1_ref[...] + _gate_rows(y, g2_ref[...])


def combine(slots_flat, hu, x1, gate2_g, gate_k, w_gs, w_us, w_ds, ys, tmc):
    t, d = x1.shape
    f = w_gs.shape[1]
    gm = tmc // MOD_GROUP
    row = lambda w: pl.BlockSpec((tmc, w), lambda i: (i, 0))
    const = lambda a, b: pl.BlockSpec((a, b), lambda i: (0, 0))
    return pl.pallas_call(
        functools.partial(_combine_kernel, tmc=tmc),
        grid=(t // tmc,),
        in_specs=[pl.BlockSpec((tmc * N_ACTIVE,), lambda i: (i,), memory_space=pltpu.SMEM),
                  row(d // 2), row(d), pl.BlockSpec((gm, d), lambda i: (i, 0)), row(N_ACTIVE),
                  const(d, f), const(d, f), const(f, d),
                  pl.BlockSpec(memory_space=pl.ANY)],
        out_specs=row(d),
        out_shape=jax.ShapeDtypeStruct((t, d), F32),
        scratch_shapes=[pltpu.VMEM((N_ACTIVE, tmc, d), F32), pltpu.SemaphoreType.DMA(())],
        compiler_params=_params("arbitrary"),
        name="combine",
    )(slots_flat, hu, x1, gate2_g, gate_k, w_gs, w_us, w_ds, ys)


def _group_rows(v, n):
    b, d = v.shape
    return jnp.broadcast_to(v[:, None, :], (b, n // MOD_GROUP, d)).reshape(b * n // MOD_GROUP, d)


def _pad_keys(a, l_pad):
    return jnp.pad(a, ((0, 0), (0, l_pad - a.shape[1]), (0, 0)))


def _token_stage(x, mod, w, caches, count0):
    b, n, d = x.shape
    t = b * n
    shift1, scale1, gate1, shift2, scale2, gate2 = [_group_rows(m, n) for m in jnp.split(mod, 6, axis=-1)]
    xf = x.reshape(t, d)
    tm = _row_tile(t, 512)
    (h, qa, ka, va, kab, vab, qi, ki, kib, wi) = in_proj_a(
        xf, w["norm_mix"], scale1, shift1, w["w_in_a"], w["q_norm_a"], w["k_norm_a"], tm)
    qb, kb, vb, kbb, vbb = in_proj_b(h, w["w_in_b"], tm)

    r3 = lambda a: a.reshape(b, n, a.shape[-1])
    if caches is None:
        ya = dsa_attention(r3(qa), r3(qi), r3(wi), r3(kab), r3(vab), r3(kib),
                           q_pos0=0, n_keys=n, tq=min(128, n), tc=min(512, n))
        yb = stick_attention(r3(qb), r3(kbb), r3(vbb), q_pos0=0, tq=min(256, n), tc=min(256, n))
    else:
        cka, cva, cki, ckb, cvb = caches
        past = cka.shape[1]
        n_keys = past + n
        tc_a, tc_b = 512, 256
        la = -(-n_keys // tc_a) * tc_a
        lb = -(-n_keys // tc_b) * tc_b
        cat = lambda c, new, lp: _pad_keys(
            jnp.concatenate([c.reshape(b, past, -1).astype(BF16), r3(new)], axis=1), lp)
        ya = dsa_attention(r3(qa), r3(qi), r3(wi), cat(cka, kab, la), cat(cva, vab, la), cat(cki, kib, la),
                           q_pos0=past, n_keys=n_keys, tq=n, tc=tc_a)
        yb = stick_attention(r3(qb), cat(ckb, kbb, lb), cat(cvb, vbb, lb), q_pos0=past, tq=n, tc=tc_b)

    merged = merge_branches(h, ya.reshape(t, A_WIDTH), yb.reshape(t, B_WIDTH),
                            w["w_ga"], w["w_gb"], w["w_branch_a"], w["w_branch_b"], tm, _row_tile(d, 512))
    x1, hu, top_e, rank_k, gate_k, counts = out_proj(
        merged, xf, gate1, scale2, shift2, w["norm_ffn"], w["w_out"], w["w_router_t"], w["b_router"], count0, tm)
    rows = (ka.reshape(b, n, A_KV_HEADS, HEAD_DIM), va.reshape(b, n, A_KV_HEADS, HEAD_DIM),
            ki.reshape(b, n, IDX_DIM), kb.reshape(b, n, B_HEADS, HEAD_DIM), vb.reshape(b, n, B_HEADS, HEAD_DIM))
    return dict(x1=x1, hu=hu, top_e=top_e, rank_k=rank_k, gate_k=gate_k, counts=counts, gate2=gate2, rows=rows)


def _routing(counts, top_e, rank_k, tme):
    e = counts.shape[0]
    t = top_e.shape[1]
    counts = counts.reshape(e).astype(jnp.int32)
    padded = (counts + tme - 1) // tme * tme
    ends = jnp.cumsum(padded)
    offsets = ends - padded
    eids = jnp.arange(e, dtype=jnp.int32)[:, None, None]
    offset_k = jnp.sum(jnp.where(top_e[None] == eids, offsets[:, None, None], 0), axis=0)
    slots = offset_k + rank_k - 1
    n_tiles = (t * N_ACTIVE + e * tme) // tme
    tile_start = jnp.arange(n_tiles, dtype=jnp.int32) * tme
    tile_expert = jnp.minimum(jnp.sum(ends[None, :] <= tile_start[:, None], axis=1), e - 1).astype(jnp.int32)
    n_used = (ends[-1] // tme).astype(jnp.int32).reshape(1)
    return dict(slots=slots.T.reshape(-1), tile_expert=tile_expert, n_used=n_used, p_rows=n_tiles * tme,
                pad_start=offsets + counts, pad_count=padded - counts)


def kernel(x_prompt, x_sample, cache_a_k, cache_a_v, cache_a_idx_k, cache_b_k, cache_b_v, c_prompt, c_sample,
           norm_mix, norm_ffn, w_ada, b_ada, w_in, q_norm_a, k_norm_a, w_branch_a, w_branch_b, w_out,
           w_router, b_router, w_gate_e, w_up_e, w_down_e, w_gate_s, w_up_s, w_down_s):
    depth = norm_mix.shape[0]
    d = x_prompt.shape[-1]
    bp, bs = c_prompt.shape[0], c_sample.shape[0]
    c_all = jnp.concatenate([c_prompt, c_sample], axis=0)
    c_all = jnp.pad(c_all, ((0, -(bp + bs) % 16), (0, 0)))
    tme = 256

    xp, xs = x_prompt, x_sample
    prompt_rows, sample_rows = [], []
    for layer in range(depth):
        wl = w_in[layer]
        o_idx = A_WIDTH + 2 * A_KV_WIDTH + IDX_Q_WIDTH
        o_b = o_idx + IDX_DIM + IDX_HEADS
        o_g = o_b + 3 * B_WIDTH
        w_in_a = jnp.concatenate(
            [wl[:, :o_b], jnp.zeros((d, LANES - IDX_DIM - IDX_HEADS), wl.dtype)], axis=1).astype(BF16)
        w = dict(
            norm_mix=norm_mix[layer].reshape(1, d), norm_ffn=norm_ffn[layer].reshape(1, d),
            q_norm_a=q_norm_a[layer].reshape(1, HEAD_DIM), k_norm_a=k_norm_a[layer].reshape(1, HEAD_DIM),
            w_in_a=w_in_a, w_in_b=wl[:, o_b:o_g].astype(BF16),
            w_ga=wl[:, o_g:o_g + d].astype(BF16), w_gb=wl[:, o_g + d:].astype(BF16),
            w_branch_a=w_branch_a[layer].astype(BF16), w_branch_b=w_branch_b[layer].astype(BF16),
            w_out=w_out[layer].astype(BF16), w_router_t=w_router[layer].T.astype(BF16),
            b_router=b_router[layer].reshape(N_EXPERTS, 1))
        mod = ada_mod(c_all, w_ada[layer], b_ada[layer])
        caches = (cache_a_k[layer], cache_a_v[layer], cache_a_idx_k[layer], cache_b_k[layer], cache_b_v[layer])
        sp = _token_stage(xp, mod[:bp], w, None, jnp.zeros((N_EXPERTS, 1), F32))
        ss = _token_stage(xs, mod[bp:bp + bs], w, caches, sp["counts"])
        prompt_rows.append(sp["rows"])
        sample_rows.append(ss["rows"])

        tp, ts = sp["x1"].shape[0], ss["x1"].shape[0]
        both = lambda name: jnp.concatenate([sp[name], ss[name]], axis=1)
        rt = _routing(ss["counts"], both("top_e"), both("rank_k"), tme)
        slots, gate_k = rt["slots"], both("gate_k").T
        xsorted = pad_fill(rt["pad_start"], rt["pad_count"], rt["n_used"], rt["p_rows"], d // 2, tme)
        xsorted = dispatch(slots[:tp * N_ACTIVE], sp["hu"], xsorted, _row_tile(tp, 256))
        xsorted = dispatch(slots[tp * N_ACTIVE:], ss["hu"], xsorted, _row_tile(ts, 256))
        ysorted = expert_ffn(rt["tile_expert"], rt["n_used"], xsorted, w_gate_e[layer].astype(BF16),
                             w_up_e[layer].astype(BF16), w_down_e[layer].astype(BF16), tme)
        shared = (w_gate_s[layer].astype(BF16), w_up_s[layer].astype(BF16), w_down_s[layer].astype(BF16))
        tmc = 256
        yp = combine(slots[:tp * N_ACTIVE], sp["hu"], sp["x1"], sp["gate2"], gate_k[:tp], *shared, ysorted,
                     _row_tile(tp, tmc))
        ys = combine(slots[tp * N_ACTIVE:], ss["hu"], ss["x1"], ss["gate2"], gate_k[tp:], *shared, ysorted,
                     _row_tile(ts, tmc))
        xp = yp.reshape(xp.shape)
        xs = ys.reshape(xs.shape)

    stack = lambda rows: tuple(jnp.stack(r) for r in zip(*rows))
    return (xp, xs) + stack(prompt_rows) + stack(sample_rows)
```

```python
import functools

import jax
import jax.numpy as jnp
from jax import lax
from jax.experimental import pallas as pl
from jax.experimental.pallas import tpu as pltpu

CHUNK = 64
HEAD_DIM = 128
A_HEADS = 8
A_KV_HEADS = 2
A_GROUP = A_HEADS // A_KV_HEADS
IDX_HEADS = 8
IDX_DIM = 64
TOPK_KEYS = 256
B_HEADS = 8
N_EXPERTS = 64
N_ACTIVE = 8
N_GROUPS = 8
N_ACTIVE_GROUPS = 4
GROUP_SIZE = N_EXPERTS // N_GROUPS
ROUTED_SCALE = 2.5
EPS = 1e-6

A_WIDTH = A_HEADS * HEAD_DIM
A_KV_WIDTH = A_KV_HEADS * HEAD_DIM
IDX_Q_WIDTH = IDX_HEADS * IDX_DIM
B_WIDTH = B_HEADS * HEAD_DIM

LANES = 128
MOD_GROUP = 32
VMEM_LIMIT = 56 * 1024 * 1024
NEG_BIG = -1e30
STICK_UNDERFLOW = -110.0
INT_MIN = -(2 ** 31)

F32 = jnp.float32
BF16 = jnp.bfloat16
NT_DIMS = (((1,), (1,)), ((), ()))


def _params(*sem):
    return pltpu.CompilerParams(dimension_semantics=sem, vmem_limit_bytes=VMEM_LIMIT)


def _silu(x):
    return x * jax.nn.sigmoid(x)


def _rms(x, gain):
    return x * lax.rsqrt(jnp.mean(x * x, axis=-1, keepdims=True) + EPS) * gain


def _modulate(y, scale_g, shift_g):
    tm, d = y.shape
    y3 = y.reshape(tm // MOD_GROUP, MOD_GROUP, d)
    out = y3 * (1.0 + scale_g[:, None, :]) + shift_g[:, None, :]
    return out.reshape(tm, d)


def _gate_rows(y, gate_g):
    tm, d = y.shape
    return (y.reshape(tm // MOD_GROUP, MOD_GROUP, d) * gate_g[:, None, :]).reshape(tm, d)


def _row_tile(t, pref):
    tm = min(pref, t)
    while t % tm:
        tm //= 2
    return tm


def _ada_kernel(c_ref, w_ref, b_ref, o_ref):
    s = _silu(c_ref[...]).astype(BF16)
    o_ref[...] = jnp.dot(s, w_ref[...].astype(BF16), preferred_element_type=F32) + b_ref[...]


def ada_mod(c, w_ada, b_ada):
    r, d = c.shape
    n = w_ada.shape[1]
    tn = _row_tile(n, 1024)
    return pl.pallas_call(
        _ada_kernel,
        grid=(n // tn,),
        in_specs=[pl.BlockSpec((r, d), lambda j: (0, 0)),
                  pl.BlockSpec((d, tn), lambda j: (0, j)),
                  pl.BlockSpec((1, tn), lambda j: (0, j))],
        out_specs=pl.BlockSpec((r, tn), lambda j: (0, j)),
        out_shape=jax.ShapeDtypeStruct((r, n), F32),
        compiler_params=_params("arbitrary"),
        name="ada_mod",
    )(c, w_ada, b_ada.reshape(1, n))


def _in_a_kernel(x_ref, nw_ref, sc_ref, sh_ref, w_ref, qn_ref, kn_ref,
                 h_ref, qa_ref, ka_ref, va_ref, kab_ref, vab_ref, qi_ref, ki_ref, kib_ref, wi_ref):
    h = _modulate(_rms(x_ref[...], nw_ref[...]), sc_ref[...], sh_ref[...])
    hb = h.astype(BF16)
    h_ref[...] = hb
    off = 0
    for hd in range(A_HEADS):
        q = jnp.dot(hb, w_ref[:, off:off + HEAD_DIM], preferred_element_type=F32)
        qa_ref[:, hd * HEAD_DIM:(hd + 1) * HEAD_DIM] = (_rms(q, qn_ref[...]) * (HEAD_DIM ** -0.5)).astype(BF16)
        off += HEAD_DIM
    for hd in range(A_KV_HEADS):
        k = jnp.dot(hb, w_ref[:, off:off + HEAD_DIM], preferred_element_type=F32)
        k = _rms(k, kn_ref[...])
        ka_ref[:, hd * HEAD_DIM:(hd + 1) * HEAD_DIM] = k
        kab_ref[:, hd * HEAD_DIM:(hd + 1) * HEAD_DIM] = k.astype(BF16)
        off += HEAD_DIM
    v = jnp.dot(hb, w_ref[:, off:off + A_KV_WIDTH], preferred_element_type=F32)
    va_ref[...] = v
    vab_ref[...] = v.astype(BF16)
    off += A_KV_WIDTH
    qi = jnp.dot(hb, w_ref[:, off:off + IDX_Q_WIDTH], preferred_element_type=F32)
    qi_ref[...] = (qi * (IDX_DIM ** -0.5)).astype(BF16)
    off += IDX_Q_WIDTH
    kw = jnp.dot(hb, w_ref[:, off:off + LANES], preferred_element_type=F32)
    ki = kw[:, :IDX_DIM]
    ki_ref[...] = ki
    kib_ref[...] = ki.astype(BF16)
    wi_ref[...] = kw[:, IDX_DIM:IDX_DIM + IDX_HEADS]


def in_proj_a(x, norm_w, scale_g, shift_g, w_a, q_norm, k_norm, tm):
    t, d = x.shape
    wa = w_a.shape[1]
    gm = tm // MOD_GROUP
    row = lambda w: pl.BlockSpec((tm, w), lambda i: (i, 0))
    const = lambda a, b: pl.BlockSpec((a, b), lambda i: (0, 0))
    outs = [(d, BF16), (A_WIDTH, BF16), (A_KV_WIDTH, F32), (A_KV_WIDTH, F32), (A_KV_WIDTH, BF16),
            (A_KV_WIDTH, BF16), (IDX_Q_WIDTH, BF16), (IDX_DIM, F32), (IDX_DIM, BF16), (IDX_HEADS, F32)]
    return pl.pallas_call(
        _in_a_kernel,
        grid=(t // tm,),
        in_specs=[row(d), const(1, d), pl.BlockSpec((gm, d), lambda i: (i, 0)),
                  pl.BlockSpec((gm, d), lambda i: (i, 0)), const(d, wa),
                  const(1, HEAD_DIM), const(1, HEAD_DIM)],
        out_specs=[row(w) for w, _ in outs],
        out_shape=[jax.ShapeDtypeStruct((t, w), dt) for w, dt in outs],
        compiler_params=_params("arbitrary"),
        name="in_proj_a",
    )(x, norm_w, scale_g, shift_g, w_a, q_norm, k_norm)


def _in_b_kernel(h_ref, w_ref, qb_ref, kb_ref, vb_ref, kbb_ref, vbb_ref):
    hb = h_ref[...]
    qb_ref[...] = jnp.dot(hb, w_ref[:, :B_WIDTH], preferred_element_type=F32).astype(BF16)
    k = jnp.dot(hb, w_ref[:, B_WIDTH:2 * B_WIDTH], preferred_element_type=F32)
    kb_ref[...] = k
    kbb_ref[...] = k.astype(BF16)
    v = jnp.dot(hb, w_ref[:, 2 * B_WIDTH:], preferred_element_type=F32)
    vb_ref[...] = v
    vbb_ref[...] = v.astype(BF16)


def in_proj_b(h, w_b, tm):
    t, d = h.shape
    row = lambda w: pl.BlockSpec((tm, w), lambda i: (i, 0))
    outs = [BF16, F32, F32, BF16, BF16]
    return pl.pallas_call(
        _in_b_kernel,
        grid=(t // tm,),
        in_specs=[row(d), pl.BlockSpec((d, 3 * B_WIDTH), lambda i: (0, 0))],
        out_specs=[row(B_WIDTH) for _ in outs],
        out_shape=[jax.ShapeDtypeStruct((t, B_WIDTH), dt) for dt in outs],
        compiler_params=_params("arbitrary"),
        name="in_proj_b",
    )(h, w_b)


def _lane_tiles(a):
    return [a[:, j * LANES:(j + 1) * LANES] for j in range(a.shape[1] // LANES)]


def _dsa_kernel(qa_ref, qi_ref, wi_ref, k_ref, v_ref, ki_ref, o_ref,
                key_ref, mb_ref, m_ref, l_ref, acc_ref,
                *, q_pos0, n_keys, n_sel, tq, tc, pos_bits):
    i = pl.program_id(1)
    q0 = q_pos0 + i * tq
    kmax = jnp.minimum(((q0 + tq - 1) // CHUNK + 1) * CHUNK, n_keys)
    nch = (kmax + tc - 1) // tc
    qpos = q0 + lax.broadcasted_iota(jnp.int32, (tq, 1), 0)
    qchunk = qpos // CHUNK

    head_w = wi_ref[...] * (IDX_HEADS ** -0.5)
    qi_all = jnp.concatenate([qi_ref[:, h * IDX_DIM:(h + 1) * IDX_DIM] for h in range(IDX_HEADS)], axis=0)

    def score_body(c, carry):
        start = pl.multiple_of(c * tc, tc)
        kic = ki_ref[pl.ds(start, tc), :]
        lg = lax.dot_general(qi_all, kic, NT_DIMS, preferred_element_type=F32)
        s = jnp.zeros((tq, tc), F32)
        for h in range(IDX_HEADS):
            s = s + jnp.maximum(lg[h * tq:(h + 1) * tq], 0.0) * head_w[:, h:h + 1]
        kpos = start + lax.broadcasted_iota(jnp.int32, (1, tc), 1)
        adm = jnp.logical_and(kpos // CHUNK <= qchunk, kpos < n_keys)
        bits = lax.bitcast_convert_type(s, jnp.int32)
        key = bits ^ ((bits >> 31) & 0x7FFFFFFF)
        key_ref[c] = jnp.where(adm, key, INT_MIN)
        return carry

    lax.fori_loop(0, nch, score_body, 0)

    def count_rows(pred):
        def body(c, acc):
            start = pl.multiple_of(c * tc, tc)
            hit = jnp.where(pred(key_ref[c], start), 1.0, 0.0)
            return acc + functools.reduce(jnp.add, _lane_tiles(hit))
        acc = lax.fori_loop(0, nch, body, jnp.zeros((tq, LANES), F32))
        return jnp.sum(acc, axis=1, keepdims=True)

    def count_ge(cand):
        return count_rows(lambda key, start: key >= cand)

    def pending(done):
        return jnp.min(done) < 0.5

    cnt = count_ge(jnp.zeros((tq, 1), jnp.int32))
    thr = jnp.where(cnt >= n_sel, 0, INT_MIN).astype(jnp.int32)
    n_adm = jnp.minimum((qchunk + 1) * CHUNK, n_keys)
    done = jnp.where(jnp.logical_or(n_adm <= n_sel, cnt == n_sel), 1.0, 0.0)

    def bit_cond(state):
        b, _, done = state
        return jnp.logical_and(b < 31, pending(done))

    def bit_body(state):
        b, thr, done = state
        cand = thr | jnp.left_shift(jnp.int32(1), 30 - b)
        cnt = count_ge(cand)
        return (b + 1, jnp.where(cnt >= n_sel, cand, thr), jnp.where(cnt == n_sel, 1.0, done))

    _, thr, done = lax.while_loop(bit_cond, bit_body, (jnp.int32(0), thr, done))

    def tie_limit():
        need = n_sel - count_rows(lambda key, start: key > thr)

        def tied_below(limit):
            return count_rows(lambda key, start: jnp.logical_and(
                key == thr, start + lax.broadcasted_iota(jnp.int32, (1, tc), 1) < limit))

        def body(b, lim):
            cand = lim | jnp.left_shift(jnp.int32(1), pos_bits - 1 - b)
            return jnp.where(tied_below(cand) < need, cand, lim)

        lim = lax.fori_loop(0, pos_bits, body, jnp.zeros((tq, 1), jnp.int32))
        return jnp.where(done > 0.5, jnp.int32(2 ** pos_bits), lim + 1)

    plim = lax.cond(pending(done), tie_limit, lambda: jnp.full((tq, 1), 2 ** pos_bits, jnp.int32))

    def mask_body(c, carry):
        start = pl.multiple_of(c * tc, tc)
        key = key_ref[c]
        kpos = start + lax.broadcasted_iota(jnp.int32, (1, tc), 1)
        sel = jnp.logical_or(key > thr, jnp.logical_and(key == thr, kpos < plim))
        mb_ref[c] = jnp.where(jnp.logical_and(sel, key != INT_MIN), 0.0, NEG_BIG)
        return carry

    lax.fori_loop(0, nch, mask_body, 0)

    slopes = [2.0 ** (-8.0 * (h + 1) / A_HEADS) for h in range(A_HEADS)]
    lane = lax.broadcasted_iota(jnp.int32, (1, LANES), 1)
    rows4 = A_GROUP * tq
    q_aug = []
    for g in range(A_KV_HEADS):
        heads = range(g * A_GROUP, (g + 1) * A_GROUP)
        q4 = jnp.concatenate([qa_ref[:, h * HEAD_DIM:(h + 1) * HEAD_DIM] for h in heads], axis=0)
        slope_col = jnp.concatenate([jnp.full((tq, 1), slopes[h], F32) for h in heads], axis=0)
        q_aug.append(jnp.concatenate([q4, jnp.where(lane < 3, slope_col, 0.0).astype(BF16)], axis=1))

    def chunk_logits(c, last):
        start = pl.multiple_of(c * tc, tc)
        kp = start + lax.broadcasted_iota(jnp.int32, (tc, 1), 0)
        pos_cols = jnp.where(lane == 0, kp & 63,
                             jnp.where(lane == 1, kp & (127 << 6),
                                       jnp.where(lane == 2, kp & ~8191, 0))).astype(F32).astype(BF16)
        mb = mb_ref[c]
        out = []
        for g in range(A_KV_HEADS):
            kc = jnp.concatenate([k_ref[pl.ds(start, tc), g * HEAD_DIM:(g + 1) * HEAD_DIM], pos_cols], axis=1)
            lg = lax.dot_general(q_aug[g], kc, NT_DIMS, preferred_element_type=F32).reshape(A_GROUP, tq, tc)
            if last:
                ahead = jnp.maximum(start + lax.broadcasted_iota(jnp.int32, (1, tc), 1) - qpos, 0).astype(F32)
                bias = jnp.stack([mb - (2.0 * slopes[g * A_GROUP + r]) * ahead for r in range(A_GROUP)])
            else:
                bias = mb[None]
            out.append((lg + bias).reshape(rows4, tc))
        return start, out

    m_ref[...] = jnp.full(m_ref.shape, NEG_BIG, F32)
    l_ref[...] = jnp.zeros(l_ref.shape, F32)
    acc_ref[...] = jnp.zeros(acc_ref.shape, F32)

    def max_pass(c, last):
        _, logits = chunk_logits(c, last)
        for g in range(A_KV_HEADS):
            rows = slice(g * rows4, (g + 1) * rows4)
            m_ref[rows] = functools.reduce(jnp.maximum, _lane_tiles(logits[g]), m_ref[rows])

    def sum_pass(c, last):
        start, logits = chunk_logits(c, last)
        for g in range(A_KV_HEADS):
            rows = slice(g * rows4, (g + 1) * rows4)
            m = m_ref[rows]
            p = [jnp.exp(t - m) for t in _lane_tiles(logits[g])]
            l_ref[rows] = functools.reduce(jnp.add, p, l_ref[rows])
            vc = v_ref[pl.ds(start, tc), g * HEAD_DIM:(g + 1) * HEAD_DIM]
            acc_ref[rows] = acc_ref[rows] + jnp.dot(jnp.concatenate(p, axis=1).astype(BF16), vc,
                                                    preferred_element_type=F32)

    def all_chunks(chunk_pass):
        def body(c, carry):
            chunk_pass(c, False)
            return carry
        lax.fori_loop(0, nch - 1, body, 0)
        chunk_pass(nch - 1, True)

    all_chunks(max_pass)
    m_ref[...] = jnp.broadcast_to(jnp.max(m_ref[...], axis=1, keepdims=True), m_ref.shape)
    all_chunks(sum_pass)
    for h in range(A_HEADS):
        rows = slice(h * tq, (h + 1) * tq)
        o_ref[:, h * HEAD_DIM:(h + 1) * HEAD_DIM] = (
            acc_ref[rows] / jnp.sum(l_ref[rows], axis=1, keepdims=True)).astype(BF16)


def dsa_attention(qa, qi, wi, k, v, ki, *, q_pos0, n_keys, tq, tc):
    b, n, _ = qa.shape
    l_pad = k.shape[1]
    n_sel = min(TOPK_KEYS, n_keys // 4)
    qspec = lambda w: pl.BlockSpec((None, tq, w), lambda bi, i: (bi, i, 0))
    kspec = lambda w: pl.BlockSpec((None, l_pad, w), lambda bi, i: (bi, 0, 0))
    kern = functools.partial(_dsa_kernel, q_pos0=q_pos0, n_keys=n_keys, n_sel=n_sel, tq=tq, tc=tc,
                             pos_bits=max(1, (l_pad - 1).bit_length()))
    return pl.pallas_call(
        kern,
        grid=(b, n // tq),
        in_specs=[qspec(A_WIDTH), qspec(IDX_Q_WIDTH), qspec(IDX_HEADS),
                  kspec(A_KV_WIDTH), kspec(A_KV_WIDTH), kspec(IDX_DIM)],
        out_specs=qspec(A_WIDTH),
        out_shape=jax.ShapeDtypeStruct((b, n, A_WIDTH), BF16),
        scratch_shapes=[pltpu.VMEM((l_pad // tc, tq, tc), jnp.int32),
                        pltpu.VMEM((l_pad // tc, tq, tc), F32),
                        pltpu.VMEM((A_HEADS * tq, LANES), F32),
                        pltpu.VMEM((A_HEADS * tq, LANES), F32),
                        pltpu.VMEM((A_HEADS * tq, HEAD_DIM), F32)],
        compiler_params=_params("arbitrary", "arbitrary"),
        name="dsa",
    )(qa, qi, wi, k, v, ki)


def _stick_kernel(q_ref, k_ref, v_ref, o_ref, *, q_pos0, tq, tc):
    i = pl.program_id(2)
    t0 = q_pos0 + i * tq
    tpos = t0 + lax.broadcasted_iota(jnp.int32, (tq, 1), 0)
    c_first = (t0 + tq - 2) // tc
    q = q_ref[...]
    upper = (lax.broadcasted_iota(jnp.int32, (tc, tc), 0)
             > lax.broadcasted_iota(jnp.int32, (tc, tc), 1)).astype(BF16)
    scale = HEAD_DIM ** -0.5

    def cond(carry):
        n, tail, _ = carry
        return jnp.logical_and(n <= c_first, jnp.max(tail) > STICK_UNDERFLOW)

    def body(carry):
        n, tail, acc = carry
        c = c_first - n
        start = pl.multiple_of(c * tc, tc)
        kc = k_ref[pl.ds(start, tc), :]
        vc = v_ref[pl.ds(start, tc), :]
        z = lax.dot_general(q, kc, NT_DIMS, preferred_element_type=F32) * scale
        spos = start + lax.broadcasted_iota(jnp.int32, (1, tc), 1)
        strict = spos < tpos
        softplus = jnp.log(1.0 + jnp.exp(-jnp.abs(z)))
        log_go = jnp.minimum(z, 0.0) - softplus
        log_stay = jnp.where(strict, jnp.minimum(-z, 0.0) - softplus, 0.0)
        hi = log_stay.astype(BF16)
        lo = (log_stay - hi.astype(F32)).astype(BF16)
        after = (jnp.dot(hi, upper, preferred_element_type=F32)
                 + jnp.dot(lo, upper, preferred_element_type=F32) + tail)
        w = jnp.where(strict, jnp.exp(log_go + after), 0.0)
        acc = acc + jnp.dot(w.astype(BF16), vc, preferred_element_type=F32)
        tail = tail + jnp.sum(log_stay, axis=1, keepdims=True)
        return n + 1, tail, acc

    _, _, acc = lax.while_loop(cond, body,
                               (jnp.int32(0), jnp.zeros((tq, 1), F32), jnp.zeros((tq, HEAD_DIM), F32)))
    o_ref[...] = acc.astype(BF16)


def stick_attention(qb, k, v, *, q_pos0, tq, tc):
    b, n, _ = qb.shape
    l_pad = k.shape[1]
    qspec = pl.BlockSpec((None, tq, HEAD_DIM), lambda bi, h, i: (bi, i, h))
    kspec = pl.BlockSpec((None, l_pad, HEAD_DIM), lambda bi, h, i: (bi, 0, h))
    kern = functools.partial(_stick_kernel, q_pos0=q_pos0, tq=tq, tc=tc)
    return pl.pallas_call(
        kern,
        grid=(b, B_HEADS, n // tq),
        in_specs=[qspec, kspec, kspec],
        out_specs=qspec,
        out_shape=jax.ShapeDtypeStruct((b, n, B_WIDTH), BF16),
        compiler_params=_params("arbitrary", "arbitrary", "arbitrary"),
        name="stick",
    )(qb, k, v)


def _merge_kernel(h_ref, ya_ref, yb_ref, wga_ref, wgb_ref, wa_ref, wb_ref, o_ref):
    hb = h_ref[...]
    ga = jax.nn.sigmoid(jnp.dot(hb, wga_ref[...], preferred_element_type=F32))
    gb = jax.nn.sigmoid(jnp.dot(hb, wgb_ref[...], preferred_element_type=F32))
    pa = jnp.dot(ya_ref[...], wa_ref[...], preferred_element_type=F32)
    pb = jnp.dot(yb_ref[...], wb_ref[...], preferred_element_type=F32)
    o_ref[...] = (ga * pa + gb * pb).astype(BF16)


def merge_branches(h, ya, yb, w_ga, w_gb, w_a, w_b, tm, tn):
    t, d = h.shape
    row = lambda w: pl.BlockSpec((tm, w), lambda i, j: (i, 0))
    col = lambda k: pl.BlockSpec((k, tn), lambda i, j: (0, j))
    return pl.pallas_call(
        _merge_kernel,
        grid=(t // tm, d // tn),
        in_specs=[row(d), row(A_WIDTH), row(B_WIDTH), col(d), col(d), col(A_WIDTH), col(B_WIDTH)],
        out_specs=pl.BlockSpec((tm, tn), lambda i, j: (i, j)),
        out_shape=jax.ShapeDtypeStruct((t, d), BF16),
        compiler_params=_params("arbitrary", "arbitrary"),
        name="merge",
    )(h, ya, yb, w_ga, w_gb, w_a, w_b)


def _pack_bf16_pair(lo, hi):
    lo_bits = lax.bitcast_convert_type(lo.astype(BF16).astype(F32), jnp.uint32)
    hi_bits = lax.bitcast_convert_type(hi.astype(BF16).astype(F32), jnp.uint32)
    return (lo_bits >> 16) | (hi_bits & jnp.uint32(0xFFFF0000))


def _unpack_bf16_pair(u):
    lo = lax.bitcast_convert_type(u << 16, F32).astype(BF16)
    hi = lax.bitcast_convert_type(u & jnp.uint32(0xFFFF0000), F32).astype(BF16)
    return lo, hi


def _first_max(vals, idx, axes, n):
    m = vals
    for ax in axes:
        m = jnp.max(m, axis=ax, keepdims=True)
    first = jnp.where(vals == m, idx, n)
    for ax in axes:
        first = jnp.min(first, axis=ax, keepdims=True)
    return m, first


def _sum01(a):
    return jnp.sum(jnp.sum(a, axis=0, keepdims=True), axis=1, keepdims=True)


def _out_kernel(mg_ref, x_ref, g1_ref, sc_ref, sh_ref, nw_ref, wo_ref, wr_ref, br_ref, cnt0_ref,
                x1_ref, hu_ref, tope_ref, rank_ref, gatek_ref, cnt_out_ref, cnt_ref):
    tm, d = x_ref.shape
    proj = jnp.dot(mg_ref[...], wo_ref[...], preferred_element_type=F32)
    x1 = x_ref[...] + _gate_rows(proj, g1_ref[...])
    x1_ref[...] = x1
    h2 = _modulate(_rms(x1, nw_ref[...]), sc_ref[...], sh_ref[...])
    hu_ref[...] = _pack_bf16_pair(h2[:, :d // 2], h2[:, d // 2:])

    logits = lax.dot_general(wr_ref[...], h2.astype(BF16), NT_DIMS, preferred_element_type=F32)
    aff = jax.nn.sigmoid(logits)
    shape3 = (N_GROUPS, GROUP_SIZE, tm)
    aff3 = aff.reshape(shape3)
    biased = (aff + br_ref[...]).reshape(shape3)
    io_in = lax.broadcasted_iota(jnp.int32, shape3, 1)
    m1, i1 = _first_max(biased, io_in, (1,), GROUP_SIZE)
    m2 = jnp.max(jnp.where(io_in == i1, -jnp.inf, biased), axis=1, keepdims=True)
    gscore = m1 + m2
    gio = lax.broadcasted_iota(jnp.int32, gscore.shape, 0)
    keep = jnp.zeros(gscore.shape, jnp.bool_)
    for _ in range(N_ACTIVE_GROUPS):
        _, first = _first_max(gscore, gio, (0,), N_GROUPS)
        hit = gio == first
        keep = jnp.logical_or(keep, hit)
        gscore = jnp.where(hit, -jnp.inf, gscore)
    masked = jnp.where(jnp.broadcast_to(keep, shape3), biased, -jnp.inf)
    eio = lax.broadcasted_iota(jnp.int32, shape3, 0) * GROUP_SIZE + io_in
    sel = jnp.zeros(shape3, jnp.bool_)
    hits = []
    for _ in range(N_ACTIVE):
        _, first = _first_max(masked, eio, (0, 1), N_EXPERTS)
        hit = eio == first
        sel = jnp.logical_or(sel, hit)
        masked = jnp.where(hit, -jnp.inf, masked)
        hits.append((hit, first.reshape(1, tm)))
    sel_aff = jnp.where(sel, aff3, 0.0)
    gates = sel_aff / _sum01(sel_aff) * ROUTED_SCALE

    @pl.when(pl.program_id(0) == 0)
    def _():
        cnt_ref[...] = cnt0_ref[...]

    sel2 = jnp.where(sel, 1.0, 0.0).reshape(N_EXPERTS, tm)
    incl = (lax.broadcasted_iota(jnp.int32, (tm, tm), 0)
            <= lax.broadcasted_iota(jnp.int32, (tm, tm), 1)).astype(BF16)
    rank = jnp.dot(sel2.astype(BF16), incl, preferred_element_type=F32) + cnt_ref[...]
    cnt_ref[...] = cnt_ref[...] + jnp.sum(sel2, axis=1, keepdims=True)
    cnt_out_ref[...] = cnt_ref[...]
    rank3 = rank.reshape(shape3)
    tope_ref[...] = jnp.concatenate([first for _, first in hits], axis=0)
    rank_ref[...] = jnp.concatenate(
        [_sum01(jnp.where(hit, rank3, 0.0)).reshape(1, tm) for hit, _ in hits], axis=0).astype(jnp.int32)
    gatek_ref[...] = jnp.concatenate(
        [_sum01(jnp.where(hit, gates, 0.0)).reshape(1, tm) for hit, _ in hits], axis=0)


def out_proj(merged, x, gate1_g, scale_g, shift_g, norm_w, w_out, w_router_t, b_router, count0, tm):
    t, d = x.shape
    gm = tm // MOD_GROUP
    row = lambda w: pl.BlockSpec((tm, w), lambda i: (i, 0))
    grp = pl.BlockSpec((gm, d), lambda i: (i, 0))
    const = lambda a, b: pl.BlockSpec((a, b), lambda i: (0, 0))
    per_k = pl.BlockSpec((N_ACTIVE, tm), lambda i: (0, i))
    return pl.pallas_call(
        _out_kernel,
        grid=(t // tm,),
        in_specs=[row(d), row(d), grp, grp, grp, const(1, d), const(d, d),
                  const(N_EXPERTS, d), const(N_EXPERTS, 1), const(N_EXPERTS, 1)],
        out_specs=[row(d), row(d // 2), per_k, per_k, per_k, const(N_EXPERTS, 1)],
        out_shape=[jax.ShapeDtypeStruct((t, d), F32), jax.ShapeDtypeStruct((t, d // 2), jnp.uint32),
                   jax.ShapeDtypeStruct((N_ACTIVE, t), jnp.int32), jax.ShapeDtypeStruct((N_ACTIVE, t), jnp.int32),
                   jax.ShapeDtypeStruct((N_ACTIVE, t), F32), jax.ShapeDtypeStruct((N_EXPERTS, 1), F32)],
        scratch_shapes=[pltpu.VMEM((N_EXPERTS, 1), F32)],
        compiler_params=_params("arbitrary"),
        name="out_proj",
    )(merged, x, gate1_g, scale_g, shift_g, norm_w, w_out, w_router_t, b_router, count0)


def _pad_fill_kernel(start_ref, cnt_ref, nu_ref, xs_ref, zero_ref, sem, *, tme, n_tiles):
    zero_ref[...] = jnp.zeros(zero_ref.shape, zero_ref.dtype)

    def pad_copy(e, j):
        return pltpu.make_async_copy(zero_ref.at[pl.ds(0, 1)], xs_ref.at[pl.ds(start_ref[e] + j, 1)], sem.at[0])

    def tile_copy(i):
        return pltpu.make_async_copy(zero_ref, xs_ref.at[pl.ds(pl.multiple_of(i * tme, tme), tme)], sem.at[1])

    def for_each_copy(wait):
        def pad_rows(e, carry):
            def one(j, c):
                pad_copy(e, j).wait() if wait else pad_copy(e, j).start()
                return c
            return lax.fori_loop(0, cnt_ref[e], one, carry)

        def tail_tiles(i, carry):
            tile_copy(i).wait() if wait else tile_copy(i).start()
            return carry

        lax.fori_loop(0, N_EXPERTS, pad_rows, 0)
        lax.fori_loop(nu_ref[0], n_tiles, tail_tiles, 0)

    for_each_copy(wait=False)
    for_each_copy(wait=True)


def pad_fill(pad_start, pad_count, n_used, p_rows, width, tme):
    grid_spec = pltpu.PrefetchScalarGridSpec(
        num_scalar_prefetch=3, grid=(1,), in_specs=[],
        out_specs=pl.BlockSpec(memory_space=pl.ANY),
        scratch_shapes=[pltpu.VMEM((tme, width), jnp.uint32), pltpu.SemaphoreType.DMA((2,))])
    return pl.pallas_call(
        functools.partial(_pad_fill_kernel, tme=tme, n_tiles=p_rows // tme),
        grid_spec=grid_spec,
        out_shape=jax.ShapeDtypeStruct((p_rows, width), jnp.uint32),
        compiler_params=_params("arbitrary"),
        name="pad_fill",
    )(pad_start, pad_count, n_used)


def _dispatch_kernel(slot_ref, hu_ref, xs_in_ref, xs_ref, sem, *, td):
    del xs_in_ref

    def row_copy(j, k):
        return pltpu.make_async_copy(hu_ref.at[pl.ds(j, 1)],
                                     xs_ref.at[pl.ds(slot_ref[j * N_ACTIVE + k], 1)], sem)

    def start(j, carry):
        for k in range(N_ACTIVE):
            row_copy(j, k).start()
        return carry

    def wait(j, carry):
        for k in range(N_ACTIVE):
            row_copy(j, k).wait()
        return carry

    lax.fori_loop(0, td, start, 0)
    lax.fori_loop(0, td, wait, 0)


def dispatch(slots_flat, hu, xs, td):
    t, w = hu.shape
    return pl.pallas_call(
        functools.partial(_dispatch_kernel, td=td),
        grid=(t // td,),
        in_specs=[pl.BlockSpec((td * N_ACTIVE,), lambda i: (i,), memory_space=pltpu.SMEM),
                  pl.BlockSpec((td, w), lambda i: (i, 0)),
                  pl.BlockSpec(memory_space=pl.ANY)],
        out_specs=pl.BlockSpec(memory_space=pl.ANY),
        out_shape=jax.ShapeDtypeStruct(xs.shape, xs.dtype),
        scratch_shapes=[pltpu.SemaphoreType.DMA(())],
        input_output_aliases={2: 0},
        compiler_params=_params("arbitrary"),
        name="dispatch",
    )(slots_flat, hu, xs)


def _expert_kernel(te_ref, nu_ref, xs_ref, wg_ref, wu_ref, wd_ref, o_ref):
    del te_ref

    @pl.when(pl.program_id(0) < nu_ref[0])
    def _():
        lo, hi = _unpack_bf16_pair(xs_ref[...])
        half = lo.shape[1]
        gate = (jnp.dot(lo, wg_ref[:half], preferred_element_type=F32)
                + jnp.dot(hi, wg_ref[half:], preferred_element_type=F32))
        up = (jnp.dot(lo, wu_ref[:half], preferred_element_type=F32)
              + jnp.dot(hi, wu_ref[half:], preferred_element_type=F32))
        hid = (_silu(gate) * up).astype(BF16)
        o_ref[...] = jnp.dot(hid, wd_ref[...], preferred_element_type=F32)

    @pl.when(pl.program_id(0) >= nu_ref[0])
    def _():
        o_ref[...] = jnp.zeros(o_ref.shape, o_ref.dtype)


def expert_ffn(tile_expert, n_used, xs, w_gate, w_up, w_down, tme):
    p, half = xs.shape
    e, d, f = w_gate.shape
    row_idx = lambda i, te, nu: (jnp.minimum(i, nu[0] - 1), 0)
    grid_spec = pltpu.PrefetchScalarGridSpec(
        num_scalar_prefetch=2,
        grid=(p // tme,),
        in_specs=[pl.BlockSpec((tme, half), row_idx),
                  pl.BlockSpec((None, d, f), lambda i, te, nu: (te[i], 0, 0)),
                  pl.BlockSpec((None, d, f), lambda i, te, nu: (te[i], 0, 0)),
                  pl.BlockSpec((None, f, d), lambda i, te, nu: (te[i], 0, 0))],
        out_specs=pl.BlockSpec((tme, d), lambda i, te, nu: (i, 0)),
    )
    return pl.pallas_call(
        _expert_kernel,
        grid_spec=grid_spec,
        out_shape=jax.ShapeDtypeStruct((p, d), F32),
        compiler_params=_params("arbitrary"),
        name="expert_ffn",
    )(tile_expert, n_used, xs, w_gate, w_up, w_down)


def _combine_kernel(slot_ref, hu_ref, x1_ref, g2_ref, gk_ref, wg_ref, wu_ref, wd_ref, ys_ref,
                    o_ref, rows_ref, sem, *, tmc):
    def row_copy(j, k):
        return pltpu.make_async_copy(ys_ref.at[pl.ds(slot_ref[j * N_ACTIVE + k], 1)],
                                     rows_ref.at[k, pl.ds(j, 1)], sem)

    def start(j, carry):
        for k in range(N_ACTIVE):
            row_copy(j, k).start()
        return carry

    def wait(j, carry):
        for k in range(N_ACTIVE):
            row_copy(j, k).wait()
        return carry

    lax.fori_loop(0, tmc, start, 0)
    lo, hi = _unpack_bf16_pair(hu_ref[...])
    half = lo.shape[1]
    gate = (jnp.dot(lo, wg_ref[:half], preferred_element_type=F32)
            + jnp.dot(hi, wg_ref[half:], preferred_element_type=F32))
    up = (jnp.dot(lo, wu_ref[:half], preferred_element_type=F32)
          + jnp.dot(hi, wu_ref[half:], preferred_element_type=F32))
    y = jnp.dot((_silu(gate) * up).astype(BF16), wd_ref[...], preferred_element_type=F32)
    lax.fori_loop(0, tmc, wait, 0)
    gk = gk_ref[...]
    for k in range(N_ACTIVE):
        y = y + gk[:, k:k + 1] * rows_ref[k]
    o_ref[...] = x1_ref[...] + _gate_rows(y, g2_ref[...])


def combine(slots_flat, hu, x1, gate2_g, gate_k, w_gs, w_us, w_ds, ys, tmc):
    t, d = x1.shape
    f = w_gs.shape[1]
    gm = tmc // MOD_GROUP
    row = lambda w: pl.BlockSpec((tmc, w), lambda i: (i, 0))
    const = lambda a, b: pl.BlockSpec((a, b), lambda i: (0, 0))
    return pl.pallas_call(
        functools.partial(_combine_kernel, tmc=tmc),
        grid=(t // tmc,),
        in_specs=[pl.BlockSpec((tmc * N_ACTIVE,), lambda i: (i,), memory_space=pltpu.SMEM),
                  row(d // 2), row(d), pl.BlockSpec((gm, d), lambda i: (i, 0)), row(N_ACTIVE),
                  const(d, f), const(d, f), const(f, d),
                  pl.BlockSpec(memory_space=pl.ANY)],
        out_specs=row(d),
        out_shape=jax.ShapeDtypeStruct((t, d), F32),
        scratch_shapes=[pltpu.VMEM((N_ACTIVE, tmc, d), F32), pltpu.SemaphoreType.DMA(())],
        compiler_params=_params("arbitrary"),
        name="combine",
    )(slots_flat, hu, x1, gate2_g, gate_k, w_gs, w_us, w_ds, ys)


def _group_rows(v, n):
    b, d = v.shape
    return jnp.broadcast_to(v[:, None, :], (b, n // MOD_GROUP, d)).reshape(b * n // MOD_GROUP, d)


def _pad_keys(a, l_pad):
    return jnp.pad(a, ((0, 0), (0, l_pad - a.shape[1]), (0, 0)))


def _token_stage(x, mod, w, caches, count0):
    b, n, d = x.shape
    t = b * n
    shift1, scale1, gate1, shift2, scale2, gate2 = [_group_rows(m, n) for m in jnp.split(mod, 6, axis=-1)]
    xf = x.reshape(t, d)
    tm = _row_tile(t, 512)
    (h, qa, ka, va, kab, vab, qi, ki, kib, wi) = in_proj_a(
        xf, w["norm_mix"], scale1, shift1, w["w_in_a"], w["q_norm_a"], w["k_norm_a"], tm)
    qb, kb, vb, kbb, vbb = in_proj_b(h, w["w_in_b"], tm)

    r3 = lambda a: a.reshape(b, n, a.shape[-1])
    if caches is None:
        ya = dsa_attention(r3(qa), r3(qi), r3(wi), r3(kab), r3(vab), r3(kib),
                           q_pos0=0, n_keys=n, tq=min(128, n), tc=min(512, n))
        yb = stick_attention(r3(qb), r3(kbb), r3(vbb), q_pos0=0, tq=min(256, n), tc=min(256, n))
    else:
        cka, cva, cki, ckb, cvb = caches
        past = cka.shape[1]
        n_keys = past + n
        tc_a, tc_b = 512, 256
        la = -(-n_keys // tc_a) * tc_a
        lb = -(-n_keys // tc_b) * tc_b
        cat = lambda c, new, lp: _pad_keys(
            jnp.concatenate([c.reshape(b, past, -1).astype(BF16), r3(new)], axis=1), lp)
        ya = dsa_attention(r3(qa), r3(qi), r3(wi), cat(cka, kab, la), cat(cva, vab, la), cat(cki, kib, la),
                           q_pos0=past, n_keys=n_keys, tq=n, tc=tc_a)
        yb = stick_attention(r3(qb), cat(ckb, kbb, lb), cat(cvb, vbb, lb), q_pos0=past, tq=n, tc=tc_b)

    merged = merge_branches(h, ya.reshape(t, A_WIDTH), yb.reshape(t, B_WIDTH),
                            w["w_ga"], w["w_gb"], w["w_branch_a"], w["w_branch_b"], tm, _row_tile(d, 512))
    x1, hu, top_e, rank_k, gate_k, counts = out_proj(
        merged, xf, gate1, scale2, shift2, w["norm_ffn"], w["w_out"], w["w_router_t"], w["b_router"], count0, tm)
    rows = (ka.reshape(b, n, A_KV_HEADS, HEAD_DIM), va.reshape(b, n, A_KV_HEADS, HEAD_DIM),
            ki.reshape(b, n, IDX_DIM), kb.reshape(b, n, B_HEADS, HEAD_DIM), vb.reshape(b, n, B_HEADS, HEAD_DIM))
    return dict(x1=x1, hu=hu, top_e=top_e, rank_k=rank_k, gate_k=gate_k, counts=counts, gate2=gate2, rows=rows)


def _routing(counts, top_e, rank_k, tme):
    e = counts.shape[0]
    t = top_e.shape[1]
    counts = counts.reshape(e).astype(jnp.int32)
    padded = (counts + tme - 1) // tme * tme
    ends = jnp.cumsum(padded)
    offsets = ends - padded
    eids = jnp.arange(e, dtype=jnp.int32)[:, None, None]
    offset_k = jnp.sum(jnp.where(top_e[None] == eids, offsets[:, None, None], 0), axis=0)
    slots = offset_k + rank_k - 1
    n_tiles = (t * N_ACTIVE + e * tme) // tme
    tile_start = jnp.arange(n_tiles, dtype=jnp.int32) * tme
    tile_expert = jnp.minimum(jnp.sum(ends[None, :] <= tile_start[:, None], axis=1), e - 1).astype(jnp.int32)
    n_used = (ends[-1] // tme).astype(jnp.int32).reshape(1)
    return dict(slots=slots.T.reshape(-1), tile_expert=tile_expert, n_used=n_used, p_rows=n_tiles * tme,
                pad_start=offsets + counts, pad_count=padded - counts)


def kernel(x_prompt, x_sample, cache_a_k, cache_a_v, cache_a_idx_k, cache_b_k, cache_b_v, c_prompt, c_sample,
           norm_mix, norm_ffn, w_ada, b_ada, w_in, q_norm_a, k_norm_a, w_branch_a, w_branch_b, w_out,
           w_router, b_router, w_gate_e, w_up_e, w_down_e, w_gate_s, w_up_s, w_down_s):
    depth = norm_mix.shape[0]
    d = x_prompt.shape[-1]
    bp, bs = c_prompt.shape[0], c_sample.shape[0]
    c_all = jnp.concatenate([c_prompt, c_sample], axis=0)
    c_all = jnp.pad(c_all, ((0, -(bp + bs) % 16), (0, 0)))
    tme = 256

    xp, xs = x_prompt, x_sample
    prompt_rows, sample_rows = [], []
    for layer in range(depth):
        wl = w_in[layer]
        o_idx = A_WIDTH + 2 * A_KV_WIDTH + IDX_Q_WIDTH
        o_b = o_idx + IDX_DIM + IDX_HEADS
        o_g = o_b + 3 * B_WIDTH
        w_in_a = jnp.concatenate(
            [wl[:, :o_b], jnp.zeros((d, LANES - IDX_DIM - IDX_HEADS), wl.dtype)], axis=1).astype(BF16)
        w = dict(
            norm_mix=norm_mix[layer].reshape(1, d), norm_ffn=norm_ffn[layer].reshape(1, d),
            q_norm_a=q_norm_a[layer].reshape(1, HEAD_DIM), k_norm_a=k_norm_a[layer].reshape(1, HEAD_DIM),
            w_in_a=w_in_a, w_in_b=wl[:, o_b:o_g].astype(BF16),
            w_ga=wl[:, o_g:o_g + d].astype(BF16), w_gb=wl[:, o_g + d:].astype(BF16),
            w_branch_a=w_branch_a[layer].astype(BF16), w_branch_b=w_branch_b[layer].astype(BF16),
            w_out=w_out[layer].astype(BF16), w_router_t=w_router[layer].T.astype(BF16),
            b_router=b_router[layer].reshape(N_EXPERTS, 1))
        mod = ada_mod(c_all, w_ada[layer], b_ada[layer])
        caches = (cache_a_k[layer], cache_a_v[layer], cache_a_idx_k[layer], cache_b_k[layer], cache_b_v[layer])
        sp = _token_stage(xp, mod[:bp], w, None, jnp.zeros((N_EXPERTS, 1), F32))
        ss = _token_stage(xs, mod[bp:bp + bs], w, caches, sp["counts"])
        prompt_rows.append(sp["rows"])
        sample_rows.append(ss["rows"])

        tp, ts = sp["x1"].shape[0], ss["x1"].shape[0]
        both = lambda name: jnp.concatenate([sp[name], ss[name]], axis=1)
        rt = _routing(ss["counts"], both("top_e"), both("rank_k"), tme)
        slots, gate_k = rt["slots"], both("gate_k").T
        xsorted = pad_fill(rt["pad_start"], rt["pad_count"], rt["n_used"], rt["p_rows"], d // 2, tme)
        xsorted = dispatch(slots[:tp * N_ACTIVE], sp["hu"], xsorted, _row_tile(tp, 256))
        xsorted = dispatch(slots[tp * N_ACTIVE:], ss["hu"], xsorted, _row_tile(ts, 256))
        ysorted = expert_ffn(rt["tile_expert"], rt["n_used"], xsorted, w_gate_e[layer].astype(BF16),
                             w_up_e[layer].astype(BF16), w_down_e[layer].astype(BF16), tme)
        shared = (w_gate_s[layer].astype(BF16), w_up_s[layer].astype(BF16), w_down_s[layer].astype(BF16))
        tmc = 256
        yp = combine(slots[:tp * N_ACTIVE], sp["hu"], sp["x1"], sp["gate2"], gate_k[:tp], *shared, ysorted,
                     _row_tile(tp, tmc))
        ys = combine(slots[tp * N_ACTIVE:], ss["hu"], ss["x1"], ss["gate2"], gate_k[tp:], *shared, ysorted,
                     _row_tile(ts, tmc))
        xp = yp.reshape(xp.shape)
        xs = ys.reshape(xs.shape)

    stack = lambda rows: tuple(jnp.stack(r) for r in zip(*rows))
    return (xp, xs) + stack(prompt_rows) + stack(sample_rows)
```

```python
import functools

import jax
import jax.numpy as jnp
from jax import lax
from jax.experimental import pallas as pl
from jax.experimental.pallas import tpu as pltpu

CHUNK = 64
HEAD_DIM = 128
A_HEADS = 8
A_KV_HEADS = 2
A_GROUP = A_HEADS // A_KV_HEADS
IDX_HEADS = 8
IDX_DIM = 64
TOPK_KEYS = 256
B_HEADS = 8
N_EXPERTS = 64
N_ACTIVE = 8
N_GROUPS = 8
N_ACTIVE_GROUPS = 4
GROUP_SIZE = N_EXPERTS // N_GROUPS
ROUTED_SCALE = 2.5
EPS = 1e-6

A_WIDTH = A_HEADS * HEAD_DIM
A_KV_WIDTH = A_KV_HEADS * HEAD_DIM
IDX_Q_WIDTH = IDX_HEADS * IDX_DIM
B_WIDTH = B_HEADS * HEAD_DIM

LANES = 128
MOD_GROUP = 32
VMEM_LIMIT = 56 * 1024 * 1024
NEG_BIG = -1e30
STICK_UNDERFLOW = -110.0
INT_MIN = -(2 ** 31)

F32 = jnp.float32
BF16 = jnp.bfloat16
NT_DIMS = (((1,), (1,)), ((), ()))


def _params(*sem):
    return pltpu.CompilerParams(dimension_semantics=sem, vmem_limit_bytes=VMEM_LIMIT)


def _silu(x):
    return x * jax.nn.sigmoid(x)


def _rms(x, gain):
    return x * lax.rsqrt(jnp.mean(x * x, axis=-1, keepdims=True) + EPS) * gain


def _modulate(y, scale_g, shift_g):
    tm, d = y.shape
    y3 = y.reshape(tm // MOD_GROUP, MOD_GROUP, d)
    out = y3 * (1.0 + scale_g[:, None, :]) + shift_g[:, None, :]
    return out.reshape(tm, d)


def _gate_rows(y, gate_g):
    tm, d = y.shape
    return (y.reshape(tm // MOD_GROUP, MOD_GROUP, d) * gate_g[:, None, :]).reshape(tm, d)


def _row_tile(t, pref):
    tm = min(pref, t)
    while t % tm:
        tm //= 2
    return tm


def _ada_kernel(c_ref, w_ref, b_ref, o_ref):
    s = _silu(c_ref[...]).astype(BF16)
    o_ref[...] = jnp.dot(s, w_ref[...].astype(BF16), preferred_element_type=F32) + b_ref[...]


def ada_mod(c, w_ada, b_ada):
    r, d = c.shape
    n = w_ada.shape[1]
    tn = _row_tile(n, 1024)
    return pl.pallas_call(
        _ada_kernel,
        grid=(n // tn,),
        in_specs=[pl.BlockSpec((r, d), lambda j: (0, 0)),
                  pl.BlockSpec((d, tn), lambda j: (0, j)),
                  pl.BlockSpec((1, tn), lambda j: (0, j))],
        out_specs=pl.BlockSpec((r, tn), lambda j: (0, j)),
        out_shape=jax.ShapeDtypeStruct((r, n), F32),
        compiler_params=_params("arbitrary"),
        name="ada_mod",
    )(c, w_ada, b_ada.reshape(1, n))


def _in_a_kernel(x_ref, nw_ref, sc_ref, sh_ref, w_ref, qn_ref, kn_ref,
                 h_ref, qa_ref, ka_ref, va_ref, kab_ref, vab_ref, qi_ref, ki_ref, kib_ref, wi_ref):
    h = _modulate(_rms(x_ref[...], nw_ref[...]), sc_ref[...], sh_ref[...])
    hb = h.astype(BF16)
    h_ref[...] = hb
    off = 0
    for hd in range(A_HEADS):
        q = jnp.dot(hb, w_ref[:, off:off + HEAD_DIM], preferred_element_type=F32)
        qa_ref[:, hd * HEAD_DIM:(hd + 1) * HEAD_DIM] = (_rms(q, qn_ref[...]) * (HEAD_DIM ** -0.5)).astype(BF16)
        off += HEAD_DIM
    for hd in range(A_KV_HEADS):
        k = jnp.dot(hb, w_ref[:, off:off + HEAD_DIM], preferred_element_type=F32)
        k = _rms(k, kn_ref[...])
        ka_ref[:, hd * HEAD_DIM:(hd + 1) * HEAD_DIM] = k
        kab_ref[:, hd * HEAD_DIM:(hd + 1) * HEAD_DIM] = k.astype(BF16)
        off += HEAD_DIM
    v = jnp.dot(hb, w_ref[:, off:off + A_KV_WIDTH], preferred_element_type=F32)
    va_ref[...] = v
    vab_ref[...] = v.astype(BF16)
    off += A_KV_WIDTH
    qi = jnp.dot(hb, w_ref[:, off:off + IDX_Q_WIDTH], preferred_element_type=F32)
    qi_ref[...] = (qi * (IDX_DIM ** -0.5)).astype(BF16)
    off += IDX_Q_WIDTH
    kw = jnp.dot(hb, w_ref[:, off:off + LANES], preferred_element_type=F32)
    ki = kw[:, :IDX_DIM]
    ki_ref[...] = ki
    kib_ref[...] = ki.astype(BF16)
    wi_ref[...] = kw[:, IDX_DIM:IDX_DIM + IDX_HEADS]


def in_proj_a(x, norm_w, scale_g, shift_g, w_a, q_norm, k_norm, tm):
    t, d = x.shape
    wa = w_a.shape[1]
    gm = tm // MOD_GROUP
    row = lambda w: pl.BlockSpec((tm, w), lambda i: (i, 0))
    const = lambda a, b: pl.BlockSpec((a, b), lambda i: (0, 0))
    outs = [(d, BF16), (A_WIDTH, BF16), (A_KV_WIDTH, F32), (A_KV_WIDTH, F32), (A_KV_WIDTH, BF16),
            (A_KV_WIDTH, BF16), (IDX_Q_WIDTH, BF16), (IDX_DIM, F32), (IDX_DIM, BF16), (IDX_HEADS, F32)]
    return pl.pallas_call(
        _in_a_kernel,
        grid=(t // tm,),
        in_specs=[row(d), const(1, d), pl.BlockSpec((gm, d), lambda i: (i, 0)),
                  pl.BlockSpec((gm, d), lambda i: (i, 0)), const(d, wa),
                  const(1, HEAD_DIM), const(1, HEAD_DIM)],
        out_specs=[row(w) for w, _ in outs],
        out_shape=[jax.ShapeDtypeStruct((t, w), dt) for w, dt in outs],
        compiler_params=_params("arbitrary"),
        name="in_proj_a",
    )(x, norm_w, scale_g, shift_g, w_a, q_norm, k_norm)


def _in_b_kernel(h_ref, w_ref, qb_ref, kb_ref, vb_ref, kbb_ref, vbb_ref):
    hb = h_ref[...]
    qb_ref[...] = jnp.dot(hb, w_ref[:, :B_WIDTH], preferred_element_type=F32).astype(BF16)
    k = jnp.dot(hb, w_ref[:, B_WIDTH:2 * B_WIDTH], preferred_element_type=F32)
    kbb_ref[...] = k.astype(BF16)
    v = jnp.dot(hb, w_ref[:, 2 * B_WIDTH:], preferred_element_type=F32)
    vbb_ref[...] = v.astype(BF16)
    for hd in range(B_HEADS):
        kb_ref[:, hd, :] = k[:, hd * HEAD_DIM:(hd + 1) * HEAD_DIM]
        vb_ref[:, hd, :] = v[:, hd * HEAD_DIM:(hd + 1) * HEAD_DIM]


def in_proj_b(h, w_b, tm):
    t, d = h.shape
    row = lambda w: pl.BlockSpec((tm, w), lambda i: (i, 0))
    heads = pl.BlockSpec((tm, B_HEADS, HEAD_DIM), lambda i: (i, 0, 0))
    flat = lambda dt: jax.ShapeDtypeStruct((t, B_WIDTH), dt)
    per_head = jax.ShapeDtypeStruct((t, B_HEADS, HEAD_DIM), F32)
    return pl.pallas_call(
        _in_b_kernel,
        grid=(t // tm,),
        in_specs=[row(d), pl.BlockSpec((d, 3 * B_WIDTH), lambda i: (0, 0))],
        out_specs=[row(B_WIDTH), heads, heads, row(B_WIDTH), row(B_WIDTH)],
        out_shape=[flat(BF16), per_head, per_head, flat(BF16), flat(BF16)],
        compiler_params=_params("arbitrary"),
        name="in_proj_b",
    )(h, w_b)


def _lane_tiles(a):
    return [a[:, j * LANES:(j + 1) * LANES] for j in range(a.shape[1] // LANES)]


def _dsa_kernel(qa_ref, qi_ref, wi_ref, k_ref, v_ref, ki_ref, o_ref,
                key_ref, mb_ref, m_ref, l_ref, acc_ref,
                *, q_pos0, n_keys, n_sel, tq, tc, pos_bits):
    i = pl.program_id(1)
    q0 = q_pos0 + i * tq
    kmax = jnp.minimum(((q0 + tq - 1) // CHUNK + 1) * CHUNK, n_keys)
    nch = (kmax + tc - 1) // tc
    qpos = q0 + lax.broadcasted_iota(jnp.int32, (tq, 1), 0)
    qchunk = qpos // CHUNK

    head_w = wi_ref[...] * (IDX_HEADS ** -0.5)
    qi_all = jnp.concatenate([qi_ref[:, h * IDX_DIM:(h + 1) * IDX_DIM] for h in range(IDX_HEADS)], axis=0)

    def score_body(c, carry):
        start = pl.multiple_of(c * tc, tc)
        kic = ki_ref[pl.ds(start, tc), :]
        lg = lax.dot_general(qi_all, kic, NT_DIMS, preferred_element_type=F32)
        s = jnp.zeros((tq, tc), F32)
        for h in range(IDX_HEADS):
            s = s + jnp.maximum(lg[h * tq:(h + 1) * tq], 0.0) * head_w[:, h:h + 1]
        kpos = start + lax.broadcasted_iota(jnp.int32, (1, tc), 1)
        adm = jnp.logical_and(kpos // CHUNK <= qchunk, kpos < n_keys)
        bits = lax.bitcast_convert_type(s, jnp.int32)
        key = bits ^ ((bits >> 31) & 0x7FFFFFFF)
        key_ref[c] = jnp.where(adm, key, INT_MIN)
        return carry

    lax.fori_loop(0, nch, score_body, 0)

    def count_rows(pred):
        def body(c, acc):
            start = pl.multiple_of(c * tc, tc)
            hit = jnp.where(pred(key_ref[c], start), 1.0, 0.0)
            return acc + functools.reduce(jnp.add, _lane_tiles(hit))
        acc = lax.fori_loop(0, nch, body, jnp.zeros((tq, LANES), F32))
        return jnp.sum(acc, axis=1, keepdims=True)

    def count_ge(cand):
        return count_rows(lambda key, start: key >= cand)

    def pending(done):
        return jnp.min(done) < 0.5

    cnt = count_ge(jnp.zeros((tq, 1), jnp.int32))
    thr = jnp.where(cnt >= n_sel, 0, INT_MIN).astype(jnp.int32)
    n_adm = jnp.minimum((qchunk + 1) * CHUNK, n_keys)
    done = jnp.where(jnp.logical_or(n_adm <= n_sel, cnt == n_sel), 1.0, 0.0)

    def bit_cond(state):
        b, _, done = state
        return jnp.logical_and(b < 31, pending(done))

    def bit_body(state):
        b, thr, done = state
        cand = thr | jnp.left_shift(jnp.int32(1), 30 - b)
        cnt = count_ge(cand)
        return (b + 1, jnp.where(cnt >= n_sel, cand, thr), jnp.where(cnt == n_sel, 1.0, done))

    _, thr, done = lax.while_loop(bit_cond, bit_body, (jnp.int32(0), thr, done))

    def tie_limit():
        need = n_sel - count_rows(lambda key, start: key > thr)

        def tied_below(limit):
            return count_rows(lambda key, start: jnp.logical_and(
                key == thr, start + lax.broadcasted_iota(jnp.int32, (1, tc), 1) < limit))

        def body(b, lim):
            cand = lim | jnp.left_shift(jnp.int32(1), pos_bits - 1 - b)
            return jnp.where(tied_below(cand) < need, cand, lim)

        lim = lax.fori_loop(0, pos_bits, body, jnp.zeros((tq, 1), jnp.int32))
        return jnp.where(done > 0.5, jnp.int32(2 ** pos_bits), lim + 1)

    plim = lax.cond(pending(done), tie_limit, lambda: jnp.full((tq, 1), 2 ** pos_bits, jnp.int32))

    def mask_body(c, carry):
        start = pl.multiple_of(c * tc, tc)
        key = key_ref[c]
        kpos = start + lax.broadcasted_iota(jnp.int32, (1, tc), 1)
        sel = jnp.logical_or(key > thr, jnp.logical_and(key == thr, kpos < plim))
        mb_ref[c] = jnp.where(jnp.logical_and(sel, key != INT_MIN), 0.0, NEG_BIG)
        return carry

    lax.fori_loop(0, nch, mask_body, 0)

    slopes = [2.0 ** (-8.0 * (h + 1) / A_HEADS) for h in range(A_HEADS)]
    lane = lax.broadcasted_iota(jnp.int32, (1, LANES), 1)
    rows4 = A_GROUP * tq
    q_aug = []
    for g in range(A_KV_HEADS):
        heads = range(g * A_GROUP, (g + 1) * A_GROUP)
        q4 = jnp.concatenate([qa_ref[:, h * HEAD_DIM:(h + 1) * HEAD_DIM] for h in heads], axis=0)
        slope_col = jnp.concatenate([jnp.full((tq, 1), slopes[h], F32) for h in heads], axis=0)
        q_aug.append(jnp.concatenate([q4, jnp.where(lane < 3, slope_col, 0.0).astype(BF16)], axis=1))

    def chunk_logits(c, last):
        start = pl.multiple_of(c * tc, tc)
        kp = start + lax.broadcasted_iota(jnp.int32, (tc, 1), 0)
        pos_cols = jnp.where(lane == 0, kp & 63,
                             jnp.where(lane == 1, kp & (127 << 6),
                                       jnp.where(lane == 2, kp & ~8191, 0))).astype(F32).astype(BF16)
        mb = mb_ref[c]
        out = []
        for g in range(A_KV_HEADS):
            kc = jnp.concatenate([k_ref[pl.ds(start, tc), g * HEAD_DIM:(g + 1) * HEAD_DIM], pos_cols], axis=1)
            lg = lax.dot_general(q_aug[g], kc, NT_DIMS, preferred_element_type=F32).reshape(A_GROUP, tq, tc)
            if last:
                ahead = jnp.maximum(start + lax.broadcasted_iota(jnp.int32, (1, tc), 1) - qpos, 0).astype(F32)
                bias = jnp.stack([mb - (2.0 * slopes[g * A_GROUP + r]) * ahead for r in range(A_GROUP)])
            else:
                bias = mb[None]
            out.append((lg + bias).reshape(rows4, tc))
        return start, out

    m_ref[...] = jnp.full(m_ref.shape, NEG_BIG, F32)
    l_ref[...] = jnp.zeros(l_ref.shape, F32)
    acc_ref[...] = jnp.zeros(acc_ref.shape, F32)

    def max_pass(c, last):
        _, logits = chunk_logits(c, last)
        for g in range(A_KV_HEADS):
            rows = slice(g * rows4, (g + 1) * rows4)
            m_ref[rows] = functools.reduce(jnp.maximum, _lane_tiles(logits[g]), m_ref[rows])

    def sum_pass(c, last):
        start, logits = chunk_logits(c, last)
        for g in range(A_KV_HEADS):
            rows = slice(g * rows4, (g + 1) * rows4)
            m = m_ref[rows]
            p = [jnp.exp(t - m) for t in _lane_tiles(logits[g])]
            l_ref[rows] = functools.reduce(jnp.add, p, l_ref[rows])
            vc = v_ref[pl.ds(start, tc), g * HEAD_DIM:(g + 1) * HEAD_DIM]
            acc_ref[rows] = acc_ref[rows] + jnp.dot(jnp.concatenate(p, axis=1).astype(BF16), vc,
                                                    preferred_element_type=F32)

    def all_chunks(chunk_pass):
        def body(c, carry):
            chunk_pass(c, False)
            return carry
        lax.fori_loop(0, nch - 1, body, 0)
        chunk_pass(nch - 1, True)

    all_chunks(max_pass)
    m_ref[...] = jnp.broadcast_to(jnp.max(m_ref[...], axis=1, keepdims=True), m_ref.shape)
    all_chunks(sum_pass)
    for h in range(A_HEADS):
        rows = slice(h * tq, (h + 1) * tq)
        o_ref[:, h * HEAD_DIM:(h + 1) * HEAD_DIM] = (
            acc_ref[rows] / jnp.sum(l_ref[rows], axis=1, keepdims=True)).astype(BF16)


def dsa_attention(qa, qi, wi, k, v, ki, *, q_pos0, n_keys, tq, tc):
    b, n, _ = qa.shape
    l_pad = k.shape[1]
    n_sel = min(TOPK_KEYS, n_keys // 4)
    qspec = lambda w: pl.BlockSpec((None, tq, w), lambda bi, i: (bi, i, 0))
    kspec = lambda w: pl.BlockSpec((None, l_pad, w), lambda bi, i: (bi, 0, 0))
    kern = functools.partial(_dsa_kernel, q_pos0=q_pos0, n_keys=n_keys, n_sel=n_sel, tq=tq, tc=tc,
                             pos_bits=max(1, (l_pad - 1).bit_length()))
    return pl.pallas_call(
        kern,
        grid=(b, n // tq),
        in_specs=[qspec(A_WIDTH), qspec(IDX_Q_WIDTH), qspec(IDX_HEADS),
                  kspec(A_KV_WIDTH), kspec(A_KV_WIDTH), kspec(IDX_DIM)],
        out_specs=qspec(A_WIDTH),
        out_shape=jax.ShapeDtypeStruct((b, n, A_WIDTH), BF16),
        scratch_shapes=[pltpu.VMEM((l_pad // tc, tq, tc), jnp.int32),
                        pltpu.VMEM((l_pad // tc, tq, tc), F32),
                        pltpu.VMEM((A_HEADS * tq, LANES), F32),
                        pltpu.VMEM((A_HEADS * tq, LANES), F32),
                        pltpu.VMEM((A_HEADS * tq, HEAD_DIM), F32)],
        compiler_params=_params("arbitrary", "arbitrary"),
        name="dsa",
    )(qa, qi, wi, k, v, ki)


STICK_HEADS_PER_STEP = 2


def _stick_kernel(q_ref, k_ref, v_ref, o_ref, *, q_pos0, tq, tc):
    i = pl.program_id(2)
    t0 = q_pos0 + i * tq
    tpos = t0 + lax.broadcasted_iota(jnp.int32, (tq, 1), 0)
    c_first = (t0 + tq - 2) // tc
    heads = [slice(h * HEAD_DIM, (h + 1) * HEAD_DIM) for h in range(STICK_HEADS_PER_STEP)]
    qs = [q_ref[:, hs] for hs in heads]
    upper = (lax.broadcasted_iota(jnp.int32, (tc, tc), 0)
             > lax.broadcasted_iota(jnp.int32, (tc, tc), 1)).astype(BF16)
    scale = HEAD_DIM ** -0.5

    def cond(carry):
        n, tails, _ = carry
        return jnp.logical_and(n <= c_first, jnp.max(functools.reduce(jnp.maximum, tails)) > STICK_UNDERFLOW)

    def body(carry):
        n, tails, accs = carry
        c = c_first - n
        start = pl.multiple_of(c * tc, tc)
        spos = start + lax.broadcasted_iota(jnp.int32, (1, tc), 1)
        strict = spos < tpos
        new_tails, new_accs = [], []
        for q, hs, tail, acc in zip(qs, heads, tails, accs):
            kc = k_ref[pl.ds(start, tc), hs]
            vc = v_ref[pl.ds(start, tc), hs]
            z = lax.dot_general(q, kc, NT_DIMS, preferred_element_type=F32) * scale
            softplus = jnp.log(1.0 + jnp.exp(-jnp.abs(z)))
            log_go = jnp.minimum(z, 0.0) - softplus
            log_stay = jnp.where(strict, jnp.minimum(-z, 0.0) - softplus, 0.0)
            hi = log_stay.astype(BF16)
            lo = (log_stay - hi.astype(F32)).astype(BF16)
            after = (jnp.dot(hi, upper, preferred_element_type=F32)
                     + jnp.dot(lo, upper, preferred_element_type=F32) + tail)
            w = jnp.where(strict, jnp.exp(log_go + after), 0.0)
            new_accs.append(acc + jnp.dot(w.astype(BF16), vc, preferred_element_type=F32))
            new_tails.append(tail + jnp.sum(log_stay, axis=1, keepdims=True))
        return n + 1, tuple(new_tails), tuple(new_accs)

    zeros = lambda w: tuple(jnp.zeros((tq, w), F32) for _ in heads)
    _, _, accs = lax.while_loop(cond, body, (jnp.int32(0), zeros(1), zeros(HEAD_DIM)))
    for hs, acc in zip(heads, accs):
        o_ref[:, hs] = acc.astype(BF16)


def stick_attention(qb, k, v, *, q_pos0, tq, tc):
    b, n, _ = qb.shape
    l_pad = k.shape[1]
    width = STICK_HEADS_PER_STEP * HEAD_DIM
    qspec = pl.BlockSpec((None, tq, width), lambda bi, h, i: (bi, i, h))
    kspec = pl.BlockSpec((None, l_pad, width), lambda bi, h, i: (bi, 0, h))
    kern = functools.partial(_stick_kernel, q_pos0=q_pos0, tq=tq, tc=tc)
    return pl.pallas_call(
        kern,
        grid=(b, B_HEADS // STICK_HEADS_PER_STEP, n // tq),
        in_specs=[qspec, kspec, kspec],
        out_specs=qspec,
        out_shape=jax.ShapeDtypeStruct((b, n, B_WIDTH), BF16),
        compiler_params=_params("arbitrary", "arbitrary", "arbitrary"),
        name="stick",
    )(qb, k, v)


def _merge_kernel(h_ref, ya_ref, yb_ref, wga_ref, wgb_ref, wa_ref, wb_ref, o_ref):
    hb = h_ref[...]
    ga = jax.nn.sigmoid(jnp.dot(hb, wga_ref[...], preferred_element_type=F32))
    gb = jax.nn.sigmoid(jnp.dot(hb, wgb_ref[...], preferred_element_type=F32))
    pa = jnp.dot(ya_ref[...], wa_ref[...], preferred_element_type=F32)
    pb = jnp.dot(yb_ref[...], wb_ref[...], preferred_element_type=F32)
    o_ref[...] = (ga * pa + gb * pb).astype(BF16)


def merge_branches(h, ya, yb, w_ga, w_gb, w_a, w_b, tm, tn):
    t, d = h.shape
    row = lambda w: pl.BlockSpec((tm, w), lambda i, j: (i, 0))
    col = lambda k: pl.BlockSpec((k, tn), lambda i, j: (0, j))
    return pl.pallas_call(
        _merge_kernel,
        grid=(t // tm, d // tn),
        in_specs=[row(d), row(A_WIDTH), row(B_WIDTH), col(d), col(d), col(A_WIDTH), col(B_WIDTH)],
        out_specs=pl.BlockSpec((tm, tn), lambda i, j: (i, j)),
        out_shape=jax.ShapeDtypeStruct((t, d), BF16),
        compiler_params=_params("arbitrary", "arbitrary"),
        name="merge",
    )(h, ya, yb, w_ga, w_gb, w_a, w_b)


def _pack_bf16_pair(lo, hi):
    lo_bits = lax.bitcast_convert_type(lo.astype(BF16).astype(F32), jnp.uint32)
    hi_bits = lax.bitcast_convert_type(hi.astype(BF16).astype(F32), jnp.uint32)
    return (lo_bits >> 16) | (hi_bits & jnp.uint32(0xFFFF0000))


def _unpack_bf16_pair(u):
    lo = lax.bitcast_convert_type(u << 16, F32).astype(BF16)
    hi = lax.bitcast_convert_type(u & jnp.uint32(0xFFFF0000), F32).astype(BF16)
    return lo, hi


def _first_max(vals, idx, axes, n):
    m = vals
    for ax in axes:
        m = jnp.max(m, axis=ax, keepdims=True)
    first = jnp.where(vals == m, idx, n)
    for ax in axes:
        first = jnp.min(first, axis=ax, keepdims=True)
    return m, first


def _sum01(a):
    return jnp.sum(jnp.sum(a, axis=0, keepdims=True), axis=1, keepdims=True)


def _out_kernel(mg_ref, x_ref, g1_ref, sc_ref, sh_ref, nw_ref, wo_ref, wr_ref, br_ref, cnt0_ref,
                x1_ref, hu_ref, tope_ref, rank_ref, gatek_ref, cnt_out_ref, cnt_ref):
    tm, d = x_ref.shape
    proj = jnp.dot(mg_ref[...], wo_ref[...], preferred_element_type=F32)
    x1 = x_ref[...] + _gate_rows(proj, g1_ref[...])
    x1_ref[...] = x1
    h2 = _modulate(_rms(x1, nw_ref[...]), sc_ref[...], sh_ref[...])
    hu_ref[...] = _pack_bf16_pair(h2[:, :d // 2], h2[:, d // 2:])

    logits = lax.dot_general(wr_ref[...], h2.astype(BF16), NT_DIMS, preferred_element_type=F32)
    aff = jax.nn.sigmoid(logits)
    shape3 = (N_GROUPS, GROUP_SIZE, tm)
    aff3 = aff.reshape(shape3)
    biased = (aff + br_ref[...]).reshape(shape3)
    io_in = lax.broadcasted_iota(jnp.int32, shape3, 1)
    m1, i1 = _first_max(biased, io_in, (1,), GROUP_SIZE)
    m2 = jnp.max(jnp.where(io_in == i1, -jnp.inf, biased), axis=1, keepdims=True)
    gscore = m1 + m2
    gio = lax.broadcasted_iota(jnp.int32, gscore.shape, 0)
    keep = jnp.zeros(gscore.shape, jnp.bool_)
    for _ in range(N_ACTIVE_GROUPS):
        _, first = _first_max(gscore, gio, (0,), N_GROUPS)
        hit = gio == first
        keep = jnp.logical_or(keep, hit)
        gscore = jnp.where(hit, -jnp.inf, gscore)
    masked = jnp.where(jnp.broadcast_to(keep, shape3), biased, -jnp.inf)
    eio = lax.broadcasted_iota(jnp.int32, shape3, 0) * GROUP_SIZE + io_in
    sel = jnp.zeros(shape3, jnp.bool_)
    hits = []
    for _ in range(N_ACTIVE):
        _, first = _first_max(masked, eio, (0, 1), N_EXPERTS)
        hit = eio == first
        sel = jnp.logical_or(sel, hit)
        masked = jnp.where(hit, -jnp.inf, masked)
        hits.append((hit, first.reshape(1, tm)))
    sel_aff = jnp.where(sel, aff3, 0.0)
    gates = sel_aff / _sum01(sel_aff) * ROUTED_SCALE

    @pl.when(pl.program_id(0) == 0)
    def _():
        cnt_ref[...] = cnt0_ref[...]

    sel2 = jnp.where(sel, 1.0, 0.0).reshape(N_EXPERTS, tm)
    incl = (lax.broadcasted_iota(jnp.int32, (tm, tm), 0)
            <= lax.broadcasted_iota(jnp.int32, (tm, tm), 1)).astype(BF16)
    rank = jnp.dot(sel2.astype(BF16), incl, preferred_element_type=F32) + cnt_ref[...]
    cnt_ref[...] = cnt_ref[...] + jnp.sum(sel2, axis=1, keepdims=True)
    cnt_out_ref[...] = cnt_ref[...]
    rank3 = rank.reshape(shape3)
    tope_ref[...] = jnp.concatenate([first for _, first in hits], axis=0)
    rank_ref[...] = jnp.concatenate(
        [_sum01(jnp.where(hit, rank3, 0.0)).reshape(1, tm) for hit, _ in hits], axis=0).astype(jnp.int32)
    gatek_ref[...] = jnp.concatenate(
        [_sum01(jnp.where(hit, gates, 0.0)).reshape(1, tm) for hit, _ in hits], axis=0)


def out_proj(merged, x, gate1_g, scale_g, shift_g, norm_w, w_out, w_router_t, b_router, count0, tm):
    t, d = x.shape
    gm = tm // MOD_GROUP
    row = lambda w: pl.BlockSpec((tm, w), lambda i: (i, 0))
    grp = pl.BlockSpec((gm, d), lambda i: (i, 0))
    const = lambda a, b: pl.BlockSpec((a, b), lambda i: (0, 0))
    per_k = pl.BlockSpec((N_ACTIVE, tm), lambda i: (0, i))
    return pl.pallas_call(
        _out_kernel,
        grid=(t // tm,),
        in_specs=[row(d), row(d), grp, grp, grp, const(1, d), const(d, d),
                  const(N_EXPERTS, d), const(N_EXPERTS, 1), const(N_EXPERTS, 1)],
        out_specs=[row(d), row(d // 2), per_k, per_k, per_k, const(N_EXPERTS, 1)],
        out_shape=[jax.ShapeDtypeStruct((t, d), F32), jax.ShapeDtypeStruct((t, d // 2), jnp.uint32),
                   jax.ShapeDtypeStruct((N_ACTIVE, t), jnp.int32), jax.ShapeDtypeStruct((N_ACTIVE, t), jnp.int32),
                   jax.ShapeDtypeStruct((N_ACTIVE, t), F32), jax.ShapeDtypeStruct((N_EXPERTS, 1), F32)],
        scratch_shapes=[pltpu.VMEM((N_EXPERTS, 1), F32)],
        compiler_params=_params("arbitrary"),
        name="out_proj",
    )(merged, x, gate1_g, scale_g, shift_g, norm_w, w_out, w_router_t, b_router, count0)


def _pad_fill_kernel(start_ref, cnt_ref, nu_ref, xs_ref, zero_ref, sem, *, tme, n_tiles):
    zero_ref[...] = jnp.zeros(zero_ref.shape, zero_ref.dtype)

    def pad_copy(e, j):
        return pltpu.make_async_copy(zero_ref.at[pl.ds(0, 1)], xs_ref.at[pl.ds(start_ref[e] + j, 1)], sem.at[0])

    def tile_copy(i):
        return pltpu.make_async_copy(zero_ref, xs_ref.at[pl.ds(pl.multiple_of(i * tme, tme), tme)], sem.at[1])

    def for_each_copy(wait):
        def pad_rows(e, carry):
            def one(j, c):
                pad_copy(e, j).wait() if wait else pad_copy(e, j).start()
                return c
            return lax.fori_loop(0, cnt_ref[e], one, carry)

        def tail_tiles(i, carry):
            tile_copy(i).wait() if wait else tile_copy(i).start()
            return carry

        lax.fori_loop(0, N_EXPERTS, pad_rows, 0)
        lax.fori_loop(nu_ref[0], n_tiles, tail_tiles, 0)

    for_each_copy(wait=False)
    for_each_copy(wait=True)


def pad_fill(pad_start, pad_count, n_used, p_rows, width, tme):
    grid_spec = pltpu.PrefetchScalarGridSpec(
        num_scalar_prefetch=3, grid=(1,), in_specs=[],
        out_specs=pl.BlockSpec(memory_space=pl.ANY),
        scratch_shapes=[pltpu.VMEM((tme, width), jnp.uint32), pltpu.SemaphoreType.DMA((2,))])
    return pl.pallas_call(
        functools.partial(_pad_fill_kernel, tme=tme, n_tiles=p_rows // tme),
        grid_spec=grid_spec,
        out_shape=jax.ShapeDtypeStruct((p_rows, width), jnp.uint32),
        compiler_params=_params("arbitrary"),
        name="pad_fill",
    )(pad_start, pad_count, n_used)


def _dispatch_kernel(slot_ref, hu_ref, xs_in_ref, xs_ref, sem, *, td):
    del xs_in_ref

    def row_copy(j, k):
        return pltpu.make_async_copy(hu_ref.at[pl.ds(j, 1)],
                                     xs_ref.at[pl.ds(slot_ref[j * N_ACTIVE + k], 1)], sem)

    def start(j, carry):
        for k in range(N_ACTIVE):
            row_copy(j, k).start()
        return carry

    def wait(j, carry):
        for k in range(N_ACTIVE):
            row_copy(j, k).wait()
        return carry

    lax.fori_loop(0, td, start, 0)
    lax.fori_loop(0, td, wait, 0)


def dispatch(slots_flat, hu, xs, td):
    t, w = hu.shape
    return pl.pallas_call(
        functools.partial(_dispatch_kernel, td=td),
        grid=(t // td,),
        in_specs=[pl.BlockSpec((td * N_ACTIVE,), lambda i: (i,), memory_space=pltpu.SMEM),
                  pl.BlockSpec((td, w), lambda i: (i, 0)),
                  pl.BlockSpec(memory_space=pl.ANY)],
        out_specs=pl.BlockSpec(memory_space=pl.ANY),
        out_shape=jax.ShapeDtypeStruct(xs.shape, xs.dtype),
        scratch_shapes=[pltpu.SemaphoreType.DMA(())],
        input_output_aliases={2: 0},
        compiler_params=_params("arbitrary"),
        name="dispatch",
    )(slots_flat, hu, xs)


def _expert_kernel(te_ref, nu_ref, xs_ref, wg_ref, wu_ref, wd_ref, o_ref, wgb_ref, wub_ref, wdb_ref):
    i = pl.program_id(0)
    used = i < nu_ref[0]

    @pl.when(jnp.logical_and(used, jnp.logical_or(i == 0, te_ref[i] != te_ref[jnp.maximum(i - 1, 0)])))
    def _():
        wgb_ref[...] = wg_ref[...].astype(BF16)
        wub_ref[...] = wu_ref[...].astype(BF16)
        wdb_ref[...] = wd_ref[...].astype(BF16)

    @pl.when(used)
    def _():
        lo, hi = _unpack_bf16_pair(xs_ref[...])
        half = lo.shape[1]
        gate = (jnp.dot(lo, wgb_ref[:half], preferred_element_type=F32)
                + jnp.dot(hi, wgb_ref[half:], preferred_element_type=F32))
        up = (jnp.dot(lo, wub_ref[:half], preferred_element_type=F32)
              + jnp.dot(hi, wub_ref[half:], preferred_element_type=F32))
        hid = (_silu(gate) * up).astype(BF16)
        o_ref[...] = jnp.dot(hid, wdb_ref[...], preferred_element_type=F32)

    @pl.when(jnp.logical_not(used))
    def _():
        o_ref[...] = jnp.zeros(o_ref.shape, o_ref.dtype)


def expert_ffn(tile_expert, n_used, xs, w_gate, w_up, w_down, tme):
    p, half = xs.shape
    e, d, f = w_gate.shape
    row_idx = lambda i, te, nu: (jnp.minimum(i, nu[0] - 1), 0)
    grid_spec = pltpu.PrefetchScalarGridSpec(
        num_scalar_prefetch=2,
        grid=(p // tme,),
        in_specs=[pl.BlockSpec((tme, half), row_idx),
                  pl.BlockSpec((None, d, f), lambda i, te, nu: (te[i], 0, 0)),
                  pl.BlockSpec((None, d, f), lambda i, te, nu: (te[i], 0, 0)),
                  pl.BlockSpec((None, f, d), lambda i, te, nu: (te[i], 0, 0))],
        out_specs=pl.BlockSpec((tme, d), lambda i, te, nu: (i, 0)),
        scratch_shapes=[pltpu.VMEM((d, f), BF16), pltpu.VMEM((d, f), BF16), pltpu.VMEM((f, d), BF16)],
    )
    return pl.pallas_call(
        _expert_kernel,
        grid_spec=grid_spec,
        out_shape=jax.ShapeDtypeStruct((p, d), F32),
        compiler_params=_params("arbitrary"),
        name="expert_ffn",
    )(tile_expert, n_used, xs, w_gate, w_up, w_down)


def _combine_kernel(slot_ref, hu_ref, x1_ref, g2_ref, gk_ref, wg_ref, wu_ref, wd_ref, ys_ref,
                    o_ref, rows_ref, sem, *, tmc):
    def row_copy(j, k):
        return pltpu.make_async_copy(ys_ref.at[pl.ds(slot_ref[j * N_ACTIVE + k], 1)],
                                     rows_ref.at[k, pl.ds(j, 1)], sem)

    def start(j, carry):
        for k in range(N_ACTIVE):
            row_copy(j, k).start()
        return carry

    def wait(j, carry):
        for k in range(N_ACTIVE):
            row_copy(j, k).wait()
        return carry

    lax.fori_loop(0, tmc, start, 0)
    lo, hi = _unpack_bf16_pair(hu_ref[...])
    half = lo.shape[1]
    gate = (jnp.dot(lo, wg_ref[:half], preferred_element_type=F32)
            + jnp.dot(hi, wg_ref[half:], preferred_element_type=F32))
    up = (jnp.dot(lo, wu_ref[:half], preferred_element_type=F32)
          + jnp.dot(hi, wu_ref[half:], preferred_element_type=F32))
    y = jnp.dot((_silu(gate) * up).astype(BF16), wd_ref[...], preferred_element_type=F32)
    lax.fori_loop(0, tmc, wait, 0)
    gk = gk_ref[...]
    for k in range(N_ACTIVE):
        y = y + gk[:, k:k + 1] * rows_ref[k]
    o_ref[...] = x1_ref[...] + _gate_rows(y, g2_ref[...])


def combine(slots_flat, hu, x1, gate2_g, gate_k, w_gs, w_us, w_ds, ys, tmc):
    t, d = x1.shape
    f = w_gs.shape[1]
    gm = tmc // MOD_GROUP
    row = lambda w: pl.BlockSpec((tmc, w), lambda i: (i, 0))
    const = lambda a, b: pl.BlockSpec((a, b), lambda i: (0, 0))
    return pl.pallas_call(
        functools.partial(_combine_kernel, tmc=tmc),
        grid=(t // tmc,),
        in_specs=[pl.BlockSpec((tmc * N_ACTIVE,), lambda i: (i,), memory_space=pltpu.SMEM),
                  row(d // 2), row(d), pl.BlockSpec((gm, d), lambda i: (i, 0)), row(N_ACTIVE),
                  const(d, f), const(d, f), const(f, d),
                  pl.BlockSpec(memory_space=pl.ANY)],
        out_specs=row(d),
        out_shape=jax.ShapeDtypeStruct((t, d), F32),
        scratch_shapes=[pltpu.VMEM((N_ACTIVE, tmc, d), F32), pltpu.SemaphoreType.DMA(())],
        compiler_params=_params("arbitrary"),
        name="combine",
    )(slots_flat, hu, x1, gate2_g, gate_k, w_gs, w_us, w_ds, ys)


def _group_rows(v, n):
    b, d = v.shape
    return jnp.broadcast_to(v[:, None, :], (b, n // MOD_GROUP, d)).reshape(b * n // MOD_GROUP, d)


def _pad_keys(a, l_pad):
    return jnp.pad(a, ((0, 0), (0, l_pad - a.shape[1]), (0, 0)))


def _token_stage(x, mod, w, caches, count0):
    b, n, d = x.shape
    t = b * n
    shift1, scale1, gate1, shift2, scale2, gate2 = [_group_rows(m, n) for m in jnp.split(mod, 6, axis=-1)]
    xf = x.reshape(t, d)
    tm = _row_tile(t, 512)
    (h, qa, ka, va, kab, vab, qi, ki, kib, wi) = in_proj_a(
        xf, w["norm_mix"], scale1, shift1, w["w_in_a"], w["q_norm_a"], w["k_norm_a"], tm)
    qb, kb, vb, kbb, vbb = in_proj_b(h, w["w_in_b"], tm)

    r3 = lambda a: a.reshape(b, n, a.shape[-1])
    if caches is None:
        ya = dsa_attention(r3(qa), r3(qi), r3(wi), r3(kab), r3(vab), r3(kib),
                           q_pos0=0, n_keys=n, tq=min(256, n), tc=min(512, n))
        yb = stick_attention(r3(qb), r3(kbb), r3(vbb), q_pos0=0, tq=min(256, n), tc=min(256, n))
    else:
        cka, cva, cki, ckb, cvb = caches
        past = cka.shape[1]
        n_keys = past + n
        tc_a, tc_b = 512, 256
        la = -(-n_keys // tc_a) * tc_a
        lb = -(-n_keys // tc_b) * tc_b
        cat = lambda c, new, lp: _pad_keys(
            jnp.concatenate([c.reshape(b, past, -1).astype(BF16), r3(new)], axis=1), lp)
        ya = dsa_attention(r3(qa), r3(qi), r3(wi), cat(cka, kab, la), cat(cva, vab, la), cat(cki, kib, la),
                           q_pos0=past, n_keys=n_keys, tq=n, tc=tc_a)
        yb = stick_attention(r3(qb), cat(ckb, kbb, lb), cat(cvb, vbb, lb), q_pos0=past, tq=n, tc=tc_b)

    merged = merge_branches(h, ya.reshape(t, A_WIDTH), yb.reshape(t, B_WIDTH),
                            w["w_ga"], w["w_gb"], w["w_branch_a"], w["w_branch_b"], tm, _row_tile(d, 512))
    x1, hu, top_e, rank_k, gate_k, counts = out_proj(
        merged, xf, gate1, scale2, shift2, w["norm_ffn"], w["w_out"], w["w_router_t"], w["b_router"], count0, tm)
    rows = (ka.reshape(b, n, A_KV_HEADS, HEAD_DIM), va.reshape(b, n, A_KV_HEADS, HEAD_DIM),
            ki.reshape(b, n, IDX_DIM), kb.reshape(b, n, B_HEADS, HEAD_DIM), vb.reshape(b, n, B_HEADS, HEAD_DIM))
    return dict(x1=x1, hu=hu, top_e=top_e, rank_k=rank_k, gate_k=gate_k, counts=counts, gate2=gate2, rows=rows)


def _routing(counts, top_e, rank_k, tme):
    e = counts.shape[0]
    t = top_e.shape[1]
    counts = counts.reshape(e).astype(jnp.int32)
    padded = (counts + tme - 1) // tme * tme
    ends = jnp.cumsum(padded)
    offsets = ends - padded
    eids = jnp.arange(e, dtype=jnp.int32)[:, None, None]
    offset_k = jnp.sum(jnp.where(top_e[None] == eids, offsets[:, None, None], 0), axis=0)
    slots = offset_k + rank_k - 1
    n_tiles = (t * N_ACTIVE + e * tme) // tme
    tile_start = jnp.arange(n_tiles, dtype=jnp.int32) * tme
    tile_expert = jnp.minimum(jnp.sum(ends[None, :] <= tile_start[:, None], axis=1), e - 1).astype(jnp.int32)
    n_used = (ends[-1] // tme).astype(jnp.int32).reshape(1)
    return dict(slots=slots.T.reshape(-1), tile_expert=tile_expert, n_used=n_used, p_rows=n_tiles * tme,
                pad_start=offsets + counts, pad_count=padded - counts)


def kernel(x_prompt, x_sample, cache_a_k, cache_a_v, cache_a_idx_k, cache_b_k, cache_b_v, c_prompt, c_sample,
           norm_mix, norm_ffn, w_ada, b_ada, w_in, q_norm_a, k_norm_a, w_branch_a, w_branch_b, w_out,
           w_router, b_router, w_gate_e, w_up_e, w_down_e, w_gate_s, w_up_s, w_down_s):
    depth = norm_mix.shape[0]
    d = x_prompt.shape[-1]
    bp, bs = c_prompt.shape[0], c_sample.shape[0]
    c_all = jnp.concatenate([c_prompt, c_sample], axis=0)
    c_all = jnp.pad(c_all, ((0, -(bp + bs) % 16), (0, 0)))
    tme = 256

    xp, xs = x_prompt, x_sample
    prompt_rows, sample_rows = [], []
    for layer in range(depth):
        wl = w_in[layer]
        o_idx = A_WIDTH + 2 * A_KV_WIDTH + IDX_Q_WIDTH
        o_b = o_idx + IDX_DIM + IDX_HEADS
        o_g = o_b + 3 * B_WIDTH
        w_in_a = jnp.concatenate(
            [wl[:, :o_b], jnp.zeros((d, LANES - IDX_DIM - IDX_HEADS), wl.dtype)], axis=1).astype(BF16)
        w = dict(
            norm_mix=norm_mix[layer].reshape(1, d), norm_ffn=norm_ffn[layer].reshape(1, d),
            q_norm_a=q_norm_a[layer].reshape(1, HEAD_DIM), k_norm_a=k_norm_a[layer].reshape(1, HEAD_DIM),
            w_in_a=w_in_a, w_in_b=wl[:, o_b:o_g].astype(BF16),
            w_ga=wl[:, o_g:o_g + d].astype(BF16), w_gb=wl[:, o_g + d:].astype(BF16),
            w_branch_a=w_branch_a[layer].astype(BF16), w_branch_b=w_branch_b[layer].astype(BF16),
            w_out=w_out[layer].astype(BF16), w_router_t=w_router[layer].T.astype(BF16),
            b_router=b_router[layer].reshape(N_EXPERTS, 1))
        mod = ada_mod(c_all, w_ada[layer], b_ada[layer])
        caches = (cache_a_k[layer], cache_a_v[layer], cache_a_idx_k[layer], cache_b_k[layer], cache_b_v[layer])
        sp = _token_stage(xp, mod[:bp], w, None, jnp.zeros((N_EXPERTS, 1), F32))
        ss = _token_stage(xs, mod[bp:bp + bs], w, caches, sp["counts"])
        prompt_rows.append(sp["rows"])
        sample_rows.append(ss["rows"])

        tp, ts = sp["x1"].shape[0], ss["x1"].shape[0]
        both = lambda name: jnp.concatenate([sp[name], ss[name]], axis=1)
        rt = _routing(ss["counts"], both("top_e"), both("rank_k"), tme)
        slots, gate_k = rt["slots"], both("gate_k").T
        xsorted = pad_fill(rt["pad_start"], rt["pad_count"], rt["n_used"], rt["p_rows"], d // 2, tme)
        xsorted = dispatch(slots[:tp * N_ACTIVE], sp["hu"], xsorted, _row_tile(tp, 256))
        xsorted = dispatch(slots[tp * N_ACTIVE:], ss["hu"], xsorted, _row_tile(ts, 256))
        ysorted = expert_ffn(rt["tile_expert"], rt["n_used"], xsorted, w_gate_e[layer], w_up_e[layer],
                             w_down_e[layer], tme)
        shared = (w_gate_s[layer].astype(BF16), w_up_s[layer].astype(BF16), w_down_s[layer].astype(BF16))
        tmc = 256
        yp = combine(slots[:tp * N_ACTIVE], sp["hu"], sp["x1"], sp["gate2"], gate_k[:tp], *shared, ysorted,
                     _row_tile(tp, tmc))
        ys = combine(slots[tp * N_ACTIVE:], ss["hu"], ss["x1"], ss["gate2"], gate_k[tp:], *shared, ysorted,
                     _row_tile(ts, tmc))
        xp = yp.reshape(xp.shape)
        xs = ys.reshape(xs.shape)

    stack = lambda rows: tuple(jnp.stack(r) for r in zip(*rows))
    return (xp, xs) + stack(prompt_rows) + stack(sample_rows)
```

```python
import functools

import jax
import jax.numpy as jnp
from jax import lax
from jax.experimental import pallas as pl
from jax.experimental.pallas import tpu as pltpu

CHUNK = 64
HEAD_DIM = 128
A_HEADS = 8
A_KV_HEADS = 2
A_GROUP = A_HEADS // A_KV_HEADS
IDX_HEADS = 8
IDX_DIM = 64
TOPK_KEYS = 256
B_HEADS = 8
N_EXPERTS = 64
N_ACTIVE = 8
N_GROUPS = 8
N_ACTIVE_GROUPS = 4
GROUP_SIZE = N_EXPERTS // N_GROUPS
ROUTED_SCALE = 2.5
EPS = 1e-6

A_WIDTH = A_HEADS * HEAD_DIM
A_KV_WIDTH = A_KV_HEADS * HEAD_DIM
IDX_Q_WIDTH = IDX_HEADS * IDX_DIM
B_WIDTH = B_HEADS * HEAD_DIM

LANES = 128
MOD_GROUP = 32
VMEM_LIMIT = 56 * 1024 * 1024
NEG_BIG = -1e30
STICK_UNDERFLOW = -110.0
INT_MIN = -(2 ** 31)

F32 = jnp.float32
BF16 = jnp.bfloat16
NT_DIMS = (((1,), (1,)), ((), ()))


def _params(*sem):
    return pltpu.CompilerParams(dimension_semantics=sem, vmem_limit_bytes=VMEM_LIMIT)


def _silu(x):
    return x * jax.nn.sigmoid(x)


def _rms(x, gain):
    return x * lax.rsqrt(jnp.mean(x * x, axis=-1, keepdims=True) + EPS) * gain


def _modulate(y, scale_g, shift_g):
    tm, d = y.shape
    y3 = y.reshape(tm // MOD_GROUP, MOD_GROUP, d)
    out = y3 * (1.0 + scale_g[:, None, :]) + shift_g[:, None, :]
    return out.reshape(tm, d)


def _gate_rows(y, gate_g):
    tm, d = y.shape
    return (y.reshape(tm // MOD_GROUP, MOD_GROUP, d) * gate_g[:, None, :]).reshape(tm, d)


def _row_tile(t, pref):
    tm = min(pref, t)
    while t % tm:
        tm //= 2
    return tm


def _ada_kernel(c_ref, w_ref, b_ref, o_ref):
    s = _silu(c_ref[...]).astype(BF16)
    o_ref[...] = jnp.dot(s, w_ref[...].astype(BF16), preferred_element_type=F32) + b_ref[...]


def ada_mod(c, w_ada, b_ada):
    r, d = c.shape
    n = w_ada.shape[1]
    tn = _row_tile(n, 1024)
    return pl.pallas_call(
        _ada_kernel,
        grid=(n // tn,),
        in_specs=[pl.BlockSpec((r, d), lambda j: (0, 0)),
                  pl.BlockSpec((d, tn), lambda j: (0, j)),
                  pl.BlockSpec((1, tn), lambda j: (0, j))],
        out_specs=pl.BlockSpec((r, tn), lambda j: (0, j)),
        out_shape=jax.ShapeDtypeStruct((r, n), F32),
        compiler_params=_params("arbitrary"),
        name="ada_mod",
    )(c, w_ada, b_ada.reshape(1, n))


def _in_a_kernel(x_ref, nw_ref, sc_ref, sh_ref, w_ref, qn_ref, kn_ref,
                 h_ref, qa_ref, ka_ref, va_ref, kab_ref, vab_ref, qi_ref, ki_ref, kib_ref, wi_ref):
    h = _modulate(_rms(x_ref[...], nw_ref[...]), sc_ref[...], sh_ref[...])
    hb = h.astype(BF16)
    h_ref[...] = hb
    off = 0
    for hd in range(A_HEADS):
        q = jnp.dot(hb, w_ref[:, off:off + HEAD_DIM], preferred_element_type=F32)
        qa_ref[:, hd * HEAD_DIM:(hd + 1) * HEAD_DIM] = (_rms(q, qn_ref[...]) * (HEAD_DIM ** -0.5)).astype(BF16)
        off += HEAD_DIM
    for hd in range(A_KV_HEADS):
        k = jnp.dot(hb, w_ref[:, off:off + HEAD_DIM], preferred_element_type=F32)
        k = _rms(k, kn_ref[...])
        ka_ref[:, hd * HEAD_DIM:(hd + 1) * HEAD_DIM] = k
        kab_ref[:, hd * HEAD_DIM:(hd + 1) * HEAD_DIM] = k.astype(BF16)
        off += HEAD_DIM
    v = jnp.dot(hb, w_ref[:, off:off + A_KV_WIDTH], preferred_element_type=F32)
    va_ref[...] = v
    vab_ref[...] = v.astype(BF16)
    off += A_KV_WIDTH
    qi = jnp.dot(hb, w_ref[:, off:off + IDX_Q_WIDTH], preferred_element_type=F32)
    qi_ref[...] = (qi * (IDX_DIM ** -0.5)).astype(BF16)
    off += IDX_Q_WIDTH
    kw = jnp.dot(hb, w_ref[:, off:off + LANES], preferred_element_type=F32)
    ki = kw[:, :IDX_DIM]
    ki_ref[...] = ki
    kib_ref[...] = ki.astype(BF16)
    wi_ref[...] = kw[:, IDX_DIM:IDX_DIM + IDX_HEADS]


def in_proj_a(x, norm_w, scale_g, shift_g, w_a, q_norm, k_norm, tm):
    t, d = x.shape
    wa = w_a.shape[1]
    gm = tm // MOD_GROUP
    row = lambda w: pl.BlockSpec((tm, w), lambda i: (i, 0))
    const = lambda a, b: pl.BlockSpec((a, b), lambda i: (0, 0))
    outs = [(d, BF16), (A_WIDTH, BF16), (A_KV_WIDTH, F32), (A_KV_WIDTH, F32), (A_KV_WIDTH, BF16),
            (A_KV_WIDTH, BF16), (IDX_Q_WIDTH, BF16), (IDX_DIM, F32), (IDX_DIM, BF16), (IDX_HEADS, F32)]
    return pl.pallas_call(
        _in_a_kernel,
        grid=(t // tm,),
        in_specs=[row(d), const(1, d), pl.BlockSpec((gm, d), lambda i: (i, 0)),
                  pl.BlockSpec((gm, d), lambda i: (i, 0)), const(d, wa),
                  const(1, HEAD_DIM), const(1, HEAD_DIM)],
        out_specs=[row(w) for w, _ in outs],
        out_shape=[jax.ShapeDtypeStruct((t, w), dt) for w, dt in outs],
        compiler_params=_params("arbitrary"),
        name="in_proj_a",
    )(x, norm_w, scale_g, shift_g, w_a, q_norm, k_norm)


def _in_b_kernel(h_ref, w_ref, qb_ref, kb_ref, vb_ref, kbb_ref, vbb_ref):
    hb = h_ref[...]
    qb_ref[...] = jnp.dot(hb, w_ref[:, :B_WIDTH], preferred_element_type=F32).astype(BF16)
    k = jnp.dot(hb, w_ref[:, B_WIDTH:2 * B_WIDTH], preferred_element_type=F32)
    kbb_ref[...] = k.astype(BF16)
    v = jnp.dot(hb, w_ref[:, 2 * B_WIDTH:], preferred_element_type=F32)
    vbb_ref[...] = v.astype(BF16)
    for hd in range(B_HEADS):
        kb_ref[:, hd, :] = k[:, hd * HEAD_DIM:(hd + 1) * HEAD_DIM]
        vb_ref[:, hd, :] = v[:, hd * HEAD_DIM:(hd + 1) * HEAD_DIM]


def in_proj_b(h, w_b, tm):
    t, d = h.shape
    row = lambda w: pl.BlockSpec((tm, w), lambda i: (i, 0))
    heads = pl.BlockSpec((tm, B_HEADS, HEAD_DIM), lambda i: (i, 0, 0))
    flat = lambda dt: jax.ShapeDtypeStruct((t, B_WIDTH), dt)
    per_head = jax.ShapeDtypeStruct((t, B_HEADS, HEAD_DIM), F32)
    return pl.pallas_call(
        _in_b_kernel,
        grid=(t // tm,),
        in_specs=[row(d), pl.BlockSpec((d, 3 * B_WIDTH), lambda i: (0, 0))],
        out_specs=[row(B_WIDTH), heads, heads, row(B_WIDTH), row(B_WIDTH)],
        out_shape=[flat(BF16), per_head, per_head, flat(BF16), flat(BF16)],
        compiler_params=_params("arbitrary"),
        name="in_proj_b",
    )(h, w_b)


def _score_key(x):
    bits = lax.bitcast_convert_type(x, jnp.int32)
    return bits ^ ((bits >> 31) & 0x7FFFFFFF)


def _lane_tiles(a):
    return [a[:, j * LANES:(j + 1) * LANES] for j in range(a.shape[1] // LANES)]


def _dsa_kernel(qa_ref, qi_ref, wi_ref, k_ref, v_ref, ki_ref, *rest, q_pos0, n_keys, n_sel, tq, tc, pos_bits,
                cached):
    if cached:
        ck_ref, cv_ref, cki_ref, o_ref, key_ref, mb_ref, m_ref, l_ref, acc_ref = rest
    else:
        o_ref, key_ref, mb_ref, m_ref, l_ref, acc_ref = rest

    def read_k(start, g, last):
        if cached and not last:
            return ck_ref[pl.ds(start, tc), g, :].astype(BF16)
        rows = slice(None) if cached else pl.ds(start, tc)
        return k_ref[rows, g * HEAD_DIM:(g + 1) * HEAD_DIM]

    def read_v(start, g, last):
        if cached and not last:
            return cv_ref[pl.ds(start, tc), g, :].astype(BF16)
        rows = slice(None) if cached else pl.ds(start, tc)
        return v_ref[rows, g * HEAD_DIM:(g + 1) * HEAD_DIM]

    def read_ki(start, last):
        if cached and not last:
            return cki_ref[pl.ds(start, tc), :].astype(BF16)
        return ki_ref[...] if cached else ki_ref[pl.ds(start, tc), :]

    i = pl.program_id(1)
    q0 = q_pos0 + i * tq
    kmax = jnp.minimum(((q0 + tq - 1) // CHUNK + 1) * CHUNK, n_keys)
    nch = (kmax + tc - 1) // tc
    qpos = q0 + lax.broadcasted_iota(jnp.int32, (tq, 1), 0)
    qchunk = qpos // CHUNK

    head_w = wi_ref[...] * (IDX_HEADS ** -0.5)
    qi_all = jnp.concatenate([qi_ref[:, h * IDX_DIM:(h + 1) * IDX_DIM] for h in range(IDX_HEADS)], axis=0)

    def all_chunks(chunk_pass):
        def body(c, carry):
            chunk_pass(c, False)
            return carry
        lax.fori_loop(0, nch - 1, body, 0)
        chunk_pass(nch - 1, True)

    def score_pass(c, last):
        start = pl.multiple_of(c * tc, tc)
        lg = lax.dot_general(qi_all, read_ki(start, last), NT_DIMS, preferred_element_type=F32)
        s = jnp.zeros((tq, tc), F32)
        for h in range(IDX_HEADS):
            s = s + jnp.maximum(lg[h * tq:(h + 1) * tq], 0.0) * head_w[:, h:h + 1]
        kpos = start + lax.broadcasted_iota(jnp.int32, (1, tc), 1)
        adm = jnp.logical_and(kpos // CHUNK <= qchunk, kpos < n_keys)
        key_ref[c] = jnp.where(adm, _score_key(s), INT_MIN)

    all_chunks(score_pass)

    def count_rows(pred):
        def body(c, acc):
            start = pl.multiple_of(c * tc, tc)
            hit = jnp.where(pred(key_ref[c], start), 1.0, 0.0)
            return acc + functools.reduce(jnp.add, _lane_tiles(hit))
        acc = lax.fori_loop(0, nch, body, jnp.zeros((tq, LANES), F32))
        return jnp.sum(acc, axis=1, keepdims=True)

    def count_ge(cand):
        return count_rows(lambda key, start: key >= cand)

    def pending(done):
        return jnp.min(done) < 0.5

    cnt = count_ge(jnp.zeros((tq, 1), jnp.int32))
    thr = jnp.where(cnt >= n_sel, 0, INT_MIN).astype(jnp.int32)
    n_adm = jnp.minimum((qchunk + 1) * CHUNK, n_keys)
    done = jnp.where(jnp.logical_or(n_adm <= n_sel, cnt == n_sel), 1.0, 0.0)

    def bit_cond(state):
        b, _, done = state
        return jnp.logical_and(b < 31, pending(done))

    def bit_body(state):
        b, thr, done = state
        cand = thr | jnp.left_shift(jnp.int32(1), 30 - b)
        cnt = count_ge(cand)
        return (b + 1, jnp.where(cnt >= n_sel, cand, thr), jnp.where(cnt == n_sel, 1.0, done))

    _, thr, done = lax.while_loop(bit_cond, bit_body, (jnp.int32(0), thr, done))

    def tie_limit():
        need = n_sel - count_rows(lambda key, start: key > thr)

        def tied_below(limit):
            return count_rows(lambda key, start: jnp.logical_and(
                key == thr, start + lax.broadcasted_iota(jnp.int32, (1, tc), 1) < limit))

        def body(b, lim):
            cand = lim | jnp.left_shift(jnp.int32(1), pos_bits - 1 - b)
            return jnp.where(tied_below(cand) < need, cand, lim)

        lim = lax.fori_loop(0, pos_bits, body, jnp.zeros((tq, 1), jnp.int32))
        return jnp.where(done > 0.5, jnp.int32(2 ** pos_bits), lim + 1)

    plim = lax.cond(pending(done), tie_limit, lambda: jnp.full((tq, 1), 2 ** pos_bits, jnp.int32))

    def mask_body(c, carry):
        start = pl.multiple_of(c * tc, tc)
        key = key_ref[c]
        kpos = start + lax.broadcasted_iota(jnp.int32, (1, tc), 1)
        sel = jnp.logical_or(key > thr, jnp.logical_and(key == thr, kpos < plim))
        mb_ref[c] = jnp.where(jnp.logical_and(sel, key != INT_MIN), 0.0, NEG_BIG)
        return carry

    lax.fori_loop(0, nch, mask_body, 0)

    slopes = [2.0 ** (-8.0 * (h + 1) / A_HEADS) for h in range(A_HEADS)]
    lane = lax.broadcasted_iota(jnp.int32, (1, LANES), 1)
    rows4 = A_GROUP * tq
    q_aug = []
    for g in range(A_KV_HEADS):
        heads = range(g * A_GROUP, (g + 1) * A_GROUP)
        q4 = jnp.concatenate([qa_ref[:, h * HEAD_DIM:(h + 1) * HEAD_DIM] for h in heads], axis=0)
        slope_col = jnp.concatenate([jnp.full((tq, 1), slopes[h], F32) for h in heads], axis=0)
        q_aug.append(jnp.concatenate([q4, jnp.where(lane < 3, slope_col, 0.0).astype(BF16)], axis=1))

    def chunk_logits(c, last):
        start = pl.multiple_of(c * tc, tc)
        kp = start + lax.broadcasted_iota(jnp.int32, (tc, 1), 0)
        pos_cols = jnp.where(lane == 0, kp & 63,
                             jnp.where(lane == 1, kp & (127 << 6),
                                       jnp.where(lane == 2, kp & ~8191, 0))).astype(F32).astype(BF16)
        mb = mb_ref[c]
        out = []
        for g in range(A_KV_HEADS):
            kc = jnp.concatenate([read_k(start, g, last), pos_cols], axis=1)
            lg = lax.dot_general(q_aug[g], kc, NT_DIMS, preferred_element_type=F32).reshape(A_GROUP, tq, tc)
            if last:
                ahead = jnp.maximum(start + lax.broadcasted_iota(jnp.int32, (1, tc), 1) - qpos, 0).astype(F32)
                bias = jnp.stack([mb - (2.0 * slopes[g * A_GROUP + r]) * ahead for r in range(A_GROUP)])
            else:
                bias = mb[None]
            out.append((lg + bias).reshape(rows4, tc))
        return start, out

    m_ref[...] = jnp.full(m_ref.shape, NEG_BIG, F32)
    l_ref[...] = jnp.zeros(l_ref.shape, F32)
    acc_ref[...] = jnp.zeros(acc_ref.shape, F32)

    def max_pass(c, last):
        _, logits = chunk_logits(c, last)
        for g in range(A_KV_HEADS):
            rows = slice(g * rows4, (g + 1) * rows4)
            m_ref[rows] = functools.reduce(jnp.maximum, _lane_tiles(logits[g]), m_ref[rows])

    def sum_pass(c, last):
        start, logits = chunk_logits(c, last)
        for g in range(A_KV_HEADS):
            rows = slice(g * rows4, (g + 1) * rows4)
            m = m_ref[rows]
            p = [jnp.exp(t - m) for t in _lane_tiles(logits[g])]
            l_ref[rows] = functools.reduce(jnp.add, p, l_ref[rows])
            acc_ref[rows] = acc_ref[rows] + jnp.dot(jnp.concatenate(p, axis=1).astype(BF16),
                                                    read_v(start, g, last), preferred_element_type=F32)

    all_chunks(max_pass)
    m_ref[...] = jnp.broadcast_to(jnp.max(m_ref[...], axis=1, keepdims=True), m_ref.shape)
    all_chunks(sum_pass)
    for h in range(A_HEADS):
        rows = slice(h * tq, (h + 1) * tq)
        o_ref[:, h * HEAD_DIM:(h + 1) * HEAD_DIM] = (
            acc_ref[rows] / jnp.sum(l_ref[rows], axis=1, keepdims=True)).astype(BF16)


def dsa_attention(qa, qi, wi, k, v, ki, *, q_pos0, n_keys, tq, tc, caches=()):
    b, n, _ = qa.shape
    if caches:
        assert q_pos0 % tc == 0 and n == tq <= tc and k.shape[1] == tc and n_keys == q_pos0 + n
        l_pad = q_pos0 + tc
    else:
        l_pad = k.shape[1]
    n_sel = min(TOPK_KEYS, n_keys // 4)
    qspec = lambda w: pl.BlockSpec((None, tq, w), lambda bi, i: (bi, i, 0))
    kspec = lambda w: pl.BlockSpec((None, k.shape[1], w), lambda bi, i: (bi, 0, 0))
    cache_specs = [pl.BlockSpec((None,) + c.shape[1:], lambda bi, i, nd=c.ndim: (bi,) + (0,) * (nd - 1))
                   for c in caches]
    kern = functools.partial(_dsa_kernel, q_pos0=q_pos0, n_keys=n_keys, n_sel=n_sel, tq=tq, tc=tc,
                             pos_bits=max(1, (l_pad - 1).bit_length()), cached=bool(caches))
    return pl.pallas_call(
        kern,
        grid=(b, n // tq),
        in_specs=[qspec(A_WIDTH), qspec(IDX_Q_WIDTH), qspec(IDX_HEADS),
                  kspec(A_KV_WIDTH), kspec(A_KV_WIDTH), kspec(IDX_DIM)] + cache_specs,
        out_specs=qspec(A_WIDTH),
        out_shape=jax.ShapeDtypeStruct((b, n, A_WIDTH), BF16),
        scratch_shapes=[pltpu.VMEM((l_pad // tc, tq, tc), jnp.int32),
                        pltpu.VMEM((l_pad // tc, tq, tc), F32),
                        pltpu.VMEM((A_HEADS * tq, LANES), F32),
                        pltpu.VMEM((A_HEADS * tq, LANES), F32),
                        pltpu.VMEM((A_HEADS * tq, HEAD_DIM), F32)],
        compiler_params=_params("arbitrary", "arbitrary"),
        name="dsa",
    )(qa, qi, wi, k, v, ki, *caches)


STICK_HEADS_PER_STEP = 2


def _strict_upper(n):
    return (lax.broadcasted_iota(jnp.int32, (n, n), 0) > lax.broadcasted_iota(jnp.int32, (n, n), 1)).astype(BF16)


def _stick_step(q, kc, vc, strict, upper, tail, acc):
    z = lax.dot_general(q, kc, NT_DIMS, preferred_element_type=F32) * (HEAD_DIM ** -0.5)
    softplus = jnp.log(1.0 + jnp.exp(-jnp.abs(z)))
    log_go = jnp.minimum(z, 0.0) - softplus
    log_stay = jnp.where(strict, jnp.minimum(-z, 0.0) - softplus, 0.0)
    hi = log_stay.astype(BF16)
    lo = (log_stay - hi.astype(F32)).astype(BF16)
    after = jnp.dot(hi, upper, preferred_element_type=F32) + jnp.dot(lo, upper, preferred_element_type=F32) + tail
    w = jnp.where(strict, jnp.exp(log_go + after), 0.0)
    return (tail + jnp.sum(log_stay, axis=1, keepdims=True),
            acc + jnp.dot(w.astype(BF16), vc, preferred_element_type=F32))


def _stick_walk(qs, tpos, c_first, tc, read_kv, tails, accs):
    upper = _strict_upper(tc)

    def cond(carry):
        n, tails, _ = carry
        return jnp.logical_and(n <= c_first, jnp.max(functools.reduce(jnp.maximum, tails)) > STICK_UNDERFLOW)

    def body(carry):
        n, tails, accs = carry
        start = pl.multiple_of((c_first - n) * tc, tc)
        strict = start + lax.broadcasted_iota(jnp.int32, (1, tc), 1) < tpos
        new = [_stick_step(q, *read_kv(start, h), strict, upper, tail, acc)
               for h, (q, tail, acc) in enumerate(zip(qs, tails, accs))]
        return n + 1, tuple(t for t, _ in new), tuple(a for _, a in new)

    _, _, accs = lax.while_loop(cond, body, (jnp.int32(0), tuple(tails), tuple(accs)))
    return accs


def _stick_kernel(q_ref, k_ref, v_ref, o_ref, *, q_pos0, tq, tc):
    t0 = q_pos0 + pl.program_id(2) * tq
    tpos = t0 + lax.broadcasted_iota(jnp.int32, (tq, 1), 0)
    heads = [slice(h * HEAD_DIM, (h + 1) * HEAD_DIM) for h in range(STICK_HEADS_PER_STEP)]
    read_kv = lambda start, h: (k_ref[pl.ds(start, tc), heads[h]], v_ref[pl.ds(start, tc), heads[h]])
    zeros = lambda w: [jnp.zeros((tq, w), F32) for _ in heads]
    accs = _stick_walk([q_ref[:, hs] for hs in heads], tpos, (t0 + tq - 2) // tc, tc, read_kv,
                       zeros(1), zeros(HEAD_DIM))
    for hs, acc in zip(heads, accs):
        o_ref[:, hs] = acc.astype(BF16)


def _stick_cached_kernel(q_ref, k_ref, v_ref, ck_ref, cv_ref, o_ref, *, past, tc):
    n = q_ref.shape[0]
    tpos = past + lax.broadcasted_iota(jnp.int32, (n, 1), 0)
    strict = past + lax.broadcasted_iota(jnp.int32, (1, n), 1) < tpos
    upper = _strict_upper(n)
    for h0 in range(0, B_HEADS, STICK_HEADS_PER_STEP):
        group = range(h0, h0 + STICK_HEADS_PER_STEP)
        cols = [slice(h * HEAD_DIM, (h + 1) * HEAD_DIM) for h in group]
        qs = [q_ref[:, hs] for hs in cols]
        first = [_stick_step(q, k_ref[:, hs], v_ref[:, hs], strict, upper,
                             jnp.zeros((n, 1), F32), jnp.zeros((n, HEAD_DIM), F32)) for q, hs in zip(qs, cols)]
        read_kv = lambda start, h, h0=h0: (ck_ref[pl.ds(start, tc), h0 + h, :].astype(BF16),
                                           cv_ref[pl.ds(start, tc), h0 + h, :].astype(BF16))
        accs = _stick_walk(qs, tpos, past // tc - 1, tc, read_kv, [t for t, _ in first], [a for _, a in first])
        for hs, acc in zip(cols, accs):
            o_ref[:, hs] = acc.astype(BF16)


def stick_attention(qb, k, v, *, q_pos0, tq, tc):
    b, n, _ = qb.shape
    l_pad = k.shape[1]
    width = STICK_HEADS_PER_STEP * HEAD_DIM
    qspec = pl.BlockSpec((None, tq, width), lambda bi, h, i: (bi, i, h))
    kspec = pl.BlockSpec((None, l_pad, width), lambda bi, h, i: (bi, 0, h))
    kern = functools.partial(_stick_kernel, q_pos0=q_pos0, tq=tq, tc=tc)
    return pl.pallas_call(
        kern,
        grid=(b, B_HEADS // STICK_HEADS_PER_STEP, n // tq),
        in_specs=[qspec, kspec, kspec],
        out_specs=qspec,
        out_shape=jax.ShapeDtypeStruct((b, n, B_WIDTH), BF16),
        compiler_params=_params("arbitrary", "arbitrary", "arbitrary"),
        name="stick",
    )(qb, k, v)


def stick_attention_cached(qb, k_new, v_new, cache_k, cache_v, *, tc):
    b, n, _ = qb.shape
    past = cache_k.shape[1]
    assert past % tc == 0
    new = pl.BlockSpec((None, n, B_WIDTH), lambda bi: (bi, 0, 0))
    cache = pl.BlockSpec((None, past, B_HEADS, HEAD_DIM), lambda bi: (bi, 0, 0, 0))
    return pl.pallas_call(
        functools.partial(_stick_cached_kernel, past=past, tc=tc),
        grid=(b,),
        in_specs=[new, new, new, cache, cache],
        out_specs=new,
        out_shape=jax.ShapeDtypeStruct((b, n, B_WIDTH), BF16),
        compiler_params=_params("arbitrary"),
        name="stick_cached",
    )(qb, k_new, v_new, cache_k, cache_v)


def _merge_kernel(h_ref, ya_ref, yb_ref, wga_ref, wgb_ref, wa_ref, wb_ref, o_ref):
    hb = h_ref[...]
    ga = jax.nn.sigmoid(jnp.dot(hb, wga_ref[...], preferred_element_type=F32))
    gb = jax.nn.sigmoid(jnp.dot(hb, wgb_ref[...], preferred_element_type=F32))
    pa = jnp.dot(ya_ref[...], wa_ref[...], preferred_element_type=F32)
    pb = jnp.dot(yb_ref[...], wb_ref[...], preferred_element_type=F32)
    o_ref[...] = (ga * pa + gb * pb).astype(BF16)


def merge_branches(h, ya, yb, w_ga, w_gb, w_a, w_b, tm, tn):
    t, d = h.shape
    row = lambda w: pl.BlockSpec((tm, w), lambda i, j: (i, 0))
    col = lambda k: pl.BlockSpec((k, tn), lambda i, j: (0, j))
    return pl.pallas_call(
        _merge_kernel,
        grid=(t // tm, d // tn),
        in_specs=[row(d), row(A_WIDTH), row(B_WIDTH), col(d), col(d), col(A_WIDTH), col(B_WIDTH)],
        out_specs=pl.BlockSpec((tm, tn), lambda i, j: (i, j)),
        out_shape=jax.ShapeDtypeStruct((t, d), BF16),
        compiler_params=_params("arbitrary", "arbitrary"),
        name="merge",
    )(h, ya, yb, w_ga, w_gb, w_a, w_b)


def _pack_bf16_pair(lo, hi):
    lo_bits = lax.bitcast_convert_type(lo.astype(BF16).astype(F32), jnp.uint32)
    hi_bits = lax.bitcast_convert_type(hi.astype(BF16).astype(F32), jnp.uint32)
    return (lo_bits >> 16) | (hi_bits & jnp.uint32(0xFFFF0000))


def _unpack_f32_pair(u):
    return (lax.bitcast_convert_type(u << 16, F32),
            lax.bitcast_convert_type(u & jnp.uint32(0xFFFF0000), F32))


def _unpack_bf16_pair(u):
    lo, hi = _unpack_f32_pair(u)
    return lo.astype(BF16), hi.astype(BF16)


def _store_row_tiles(ref, u):
    for s in range(ref.shape[-2]):
        ref[:, s, :] = u[:, s * LANES:(s + 1) * LANES]


def _load_row_tiles(ref):
    return jnp.concatenate([ref[:, s, :] for s in range(ref.shape[-2])], axis=1)


def _first_max(vals, idx, axes, n):
    m = vals
    for ax in axes:
        m = jnp.max(m, axis=ax, keepdims=True)
    first = jnp.where(vals == m, idx, n)
    for ax in axes:
        first = jnp.min(first, axis=ax, keepdims=True)
    return m, first


def _sum01(a):
    return jnp.sum(jnp.sum(a, axis=0, keepdims=True), axis=1, keepdims=True)


def _out_kernel(mg_ref, x_ref, g1_ref, sc_ref, sh_ref, nw_ref, wo_ref, wr_ref, br_ref, cnt0_ref,
                x1_ref, hu_ref, tope_ref, rank_ref, gatek_ref, cnt_out_ref, cnt_ref):
    tm, d = x_ref.shape
    proj = jnp.dot(mg_ref[...], wo_ref[...], preferred_element_type=F32)
    x1 = x_ref[...] + _gate_rows(proj, g1_ref[...])
    x1_ref[...] = x1
    h2 = _modulate(_rms(x1, nw_ref[...]), sc_ref[...], sh_ref[...])
    _store_row_tiles(hu_ref, _pack_bf16_pair(h2[:, :d // 2], h2[:, d // 2:]))

    logits = lax.dot_general(wr_ref[...], h2.astype(BF16), NT_DIMS, preferred_element_type=F32)
    aff = jax.nn.sigmoid(logits)
    shape3 = (N_GROUPS, GROUP_SIZE, tm)
    aff3 = aff.reshape(shape3)
    biased = (aff + br_ref[...]).reshape(shape3)
    io_in = lax.broadcasted_iota(jnp.int32, shape3, 1)
    m1, i1 = _first_max(biased, io_in, (1,), GROUP_SIZE)
    m2 = jnp.max(jnp.where(io_in == i1, -jnp.inf, biased), axis=1, keepdims=True)
    gscore = m1 + m2
    gio = lax.broadcasted_iota(jnp.int32, gscore.shape, 0)
    keep = jnp.zeros(gscore.shape, jnp.bool_)
    for _ in range(N_ACTIVE_GROUPS):
        _, first = _first_max(gscore, gio, (0,), N_GROUPS)
        hit = gio == first
        keep = jnp.logical_or(keep, hit)
        gscore = jnp.where(hit, -jnp.inf, gscore)
    masked = jnp.where(jnp.broadcast_to(keep, shape3), biased, -jnp.inf)
    eio = lax.broadcasted_iota(jnp.int32, shape3, 0) * GROUP_SIZE + io_in
    sel = jnp.zeros(shape3, jnp.bool_)
    hits = []
    for _ in range(N_ACTIVE):
        _, first = _first_max(masked, eio, (0, 1), N_EXPERTS)
        hit = eio == first
        sel = jnp.logical_or(sel, hit)
        masked = jnp.where(hit, -jnp.inf, masked)
        hits.append((hit, first.reshape(1, tm)))
    sel_aff = jnp.where(sel, aff3, 0.0)
    gates = sel_aff / _sum01(sel_aff) * ROUTED_SCALE

    @pl.when(pl.program_id(0) == 0)
    def _():
        cnt_ref[...] = cnt0_ref[...]

    sel2 = jnp.where(sel, 1.0, 0.0).reshape(N_EXPERTS, tm)
    incl = (lax.broadcasted_iota(jnp.int32, (tm, tm), 0)
            <= lax.broadcasted_iota(jnp.int32, (tm, tm), 1)).astype(BF16)
    rank = jnp.dot(sel2.astype(BF16), incl, preferred_element_type=F32) + cnt_ref[...]
    cnt_ref[...] = cnt_ref[...] + jnp.sum(sel2, axis=1, keepdims=True)
    cnt_out_ref[...] = cnt_ref[...]
    rank3 = rank.reshape(shape3)
    tope_ref[...] = jnp.concatenate([first for _, first in hits], axis=0)
    rank_ref[...] = jnp.concatenate(
        [_sum01(jnp.where(hit, rank3, 0.0)).reshape(1, tm) for hit, _ in hits], axis=0).astype(jnp.int32)
    gatek_ref[...] = jnp.concatenate(
        [_sum01(jnp.where(hit, gates, 0.0)).reshape(1, tm) for hit, _ in hits], axis=0)


def out_proj(merged, x, gate1_g, scale_g, shift_g, norm_w, w_out, w_router_t, b_router, count0, tm):
    t, d = x.shape
    gm = tm // MOD_GROUP
    row = lambda w: pl.BlockSpec((tm, w), lambda i: (i, 0))
    grp = pl.BlockSpec((gm, d), lambda i: (i, 0))
    const = lambda a, b: pl.BlockSpec((a, b), lambda i: (0, 0))
    per_k = pl.BlockSpec((N_ACTIVE, tm), lambda i: (0, i))
    return pl.pallas_call(
        _out_kernel,
        grid=(t // tm,),
        in_specs=[row(d), row(d), grp, grp, grp, const(1, d), const(d, d),
                  const(N_EXPERTS, d), const(N_EXPERTS, 1), const(N_EXPERTS, 1)],
        out_specs=[row(d), pl.BlockSpec((tm, d // 2 // LANES, LANES), lambda i: (i, 0, 0)),
                   per_k, per_k, per_k, const(N_EXPERTS, 1)],
        out_shape=[jax.ShapeDtypeStruct((t, d), F32),
                   jax.ShapeDtypeStruct((t, d // 2 // LANES, LANES), jnp.uint32),
                   jax.ShapeDtypeStruct((N_ACTIVE, t), jnp.int32), jax.ShapeDtypeStruct((N_ACTIVE, t), jnp.int32),
                   jax.ShapeDtypeStruct((N_ACTIVE, t), F32), jax.ShapeDtypeStruct((N_EXPERTS, 1), F32)],
        scratch_shapes=[pltpu.VMEM((N_EXPERTS, 1), F32)],
        compiler_params=_params("arbitrary"),
        name="out_proj",
    )(merged, x, gate1_g, scale_g, shift_g, norm_w, w_out, w_router_t, b_router, count0)


def _pad_fill_kernel(start_ref, cnt_ref, nu_ref, xs_ref, zero_ref, sem, *, tme, n_tiles):
    zero_ref[...] = jnp.zeros(zero_ref.shape, zero_ref.dtype)

    def pad_copy(e, j):
        return pltpu.make_async_copy(zero_ref.at[0], xs_ref.at[start_ref[e] + j], sem.at[0])

    def tile_copy(i):
        return pltpu.make_async_copy(zero_ref, xs_ref.at[pl.ds(pl.multiple_of(i * tme, tme), tme)], sem.at[1])

    def for_each_copy(wait):
        def pad_rows(e, carry):
            def one(j, c):
                pad_copy(e, j).wait() if wait else pad_copy(e, j).start()
                return c
            return lax.fori_loop(0, cnt_ref[e], one, carry)

        def tail_tiles(i, carry):
            tile_copy(i).wait() if wait else tile_copy(i).start()
            return carry

        lax.fori_loop(0, N_EXPERTS, pad_rows, 0)
        lax.fori_loop(nu_ref[0], n_tiles, tail_tiles, 0)

    for_each_copy(wait=False)
    for_each_copy(wait=True)


def pad_fill(pad_start, pad_count, n_used, p_rows, row_shape, tme):
    grid_spec = pltpu.PrefetchScalarGridSpec(
        num_scalar_prefetch=3, grid=(1,), in_specs=[],
        out_specs=pl.BlockSpec(memory_space=pl.ANY),
        scratch_shapes=[pltpu.VMEM((tme,) + row_shape, jnp.uint32), pltpu.SemaphoreType.DMA((2,))])
    return pl.pallas_call(
        functools.partial(_pad_fill_kernel, tme=tme, n_tiles=p_rows // tme),
        grid_spec=grid_spec,
        out_shape=jax.ShapeDtypeStruct((p_rows,) + row_shape, jnp.uint32),
        compiler_params=_params("arbitrary"),
        name="pad_fill",
    )(pad_start, pad_count, n_used)


def _dispatch_kernel(slot_ref, hu_ref, xs_in_ref, xs_ref, sem, *, td):
    del xs_in_ref

    def row_copy(j, k):
        return pltpu.make_async_copy(hu_ref.at[j], xs_ref.at[slot_ref[j * N_ACTIVE + k]], sem)

    def start(j, carry):
        for k in range(N_ACTIVE):
            row_copy(j, k).start()
        return carry

    def wait(j, carry):
        for k in range(N_ACTIVE):
            row_copy(j, k).wait()
        return carry

    lax.fori_loop(0, td, start, 0)
    lax.fori_loop(0, td, wait, 0)


def dispatch(slots_flat, hu, xs, td):
    t = hu.shape[0]
    return pl.pallas_call(
        functools.partial(_dispatch_kernel, td=td),
        grid=(t // td,),
        in_specs=[pl.BlockSpec((td * N_ACTIVE,), lambda i: (i,), memory_space=pltpu.SMEM),
                  pl.BlockSpec((td,) + hu.shape[1:], lambda i: (i, 0, 0)),
                  pl.BlockSpec(memory_space=pl.ANY)],
        out_specs=pl.BlockSpec(memory_space=pl.ANY),
        out_shape=jax.ShapeDtypeStruct(xs.shape, xs.dtype),
        scratch_shapes=[pltpu.SemaphoreType.DMA(())],
        input_output_aliases={2: 0},
        compiler_params=_params("arbitrary"),
        name="dispatch",
    )(slots_flat, hu, xs)


def _expert_kernel(te_ref, nu_ref, xs_ref, wg_ref, wu_ref, wd_ref, o_ref, wgb_ref, wub_ref, wdb_ref):
    i = pl.program_id(0)
    used = i < nu_ref[0]

    @pl.when(jnp.logical_and(used, jnp.logical_or(i == 0, te_ref[i] != te_ref[jnp.maximum(i - 1, 0)])))
    def _():
        wgb_ref[...] = wg_ref[...].astype(BF16)
        wub_ref[...] = wu_ref[...].astype(BF16)
        wdb_ref[...] = wd_ref[...].astype(BF16)

    @pl.when(used)
    def _():
        lo, hi = _unpack_bf16_pair(_load_row_tiles(xs_ref))
        half = lo.shape[1]
        gate = (jnp.dot(lo, wgb_ref[:half], preferred_element_type=F32)
                + jnp.dot(hi, wgb_ref[half:], preferred_element_type=F32))
        up = (jnp.dot(lo, wub_ref[:half], preferred_element_type=F32)
              + jnp.dot(hi, wub_ref[half:], preferred_element_type=F32))
        hid = (_silu(gate) * up).astype(BF16)
        o_ref[...] = jnp.dot(hid, wdb_ref[...], preferred_element_type=F32)

    @pl.when(jnp.logical_not(used))
    def _():
        o_ref[...] = jnp.zeros(o_ref.shape, o_ref.dtype)


def expert_ffn(tile_expert, n_used, xs, w_gate, w_up, w_down, tme):
    p = xs.shape[0]
    e, d, f = w_gate.shape
    rows = (tme,) + xs.shape[1:]
    grid_spec = pltpu.PrefetchScalarGridSpec(
        num_scalar_prefetch=2,
        grid=(p // tme,),
        in_specs=[pl.BlockSpec(rows, lambda i, te, nu: (jnp.minimum(i, nu[0] - 1), 0, 0)),
                  pl.BlockSpec((None, d, f), lambda i, te, nu: (te[i], 0, 0)),
                  pl.BlockSpec((None, d, f), lambda i, te, nu: (te[i], 0, 0)),
                  pl.BlockSpec((None, f, d), lambda i, te, nu: (te[i], 0, 0))],
        out_specs=pl.BlockSpec((tme, d), lambda i, te, nu: (i, 0)),
        scratch_shapes=[pltpu.VMEM((d, f), BF16), pltpu.VMEM((d, f), BF16), pltpu.VMEM((f, d), BF16)],
    )
    return pl.pallas_call(
        _expert_kernel,
        grid_spec=grid_spec,
        out_shape=jax.ShapeDtypeStruct((p, d), F32),
        compiler_params=_params("arbitrary"),
        name="expert_ffn",
    )(tile_expert, n_used, xs, w_gate, w_up, w_down)


def _combine_kernel(slot_ref, hu_ref, x1_ref, g2_ref, gk_ref, wg_ref, wu_ref, wd_ref, ys_ref,
                    o_ref, rows_ref, sem, *, tmc):
    def row_copy(j, k):
        return pltpu.make_async_copy(ys_ref.at[pl.ds(slot_ref[j * N_ACTIVE + k], 1)],
                                     rows_ref.at[k, pl.ds(j, 1)], sem)

    def start(j, carry):
        for k in range(N_ACTIVE):
            row_copy(j, k).start()
        return carry

    def wait(j, carry):
        for k in range(N_ACTIVE):
            row_copy(j, k).wait()
        return carry

    lax.fori_loop(0, tmc, start, 0)
    lo, hi = _unpack_bf16_pair(_load_row_tiles(hu_ref))
    half = lo.shape[1]
    gate = (jnp.dot(lo, wg_ref[:half], preferred_element_type=F32)
            + jnp.dot(hi, wg_ref[half:], preferred_element_type=F32))
    up = (jnp.dot(lo, wu_ref[:half], preferred_element_type=F32)
          + jnp.dot(hi, wu_ref[half:], preferred_element_type=F32))
    y = jnp.dot((_silu(gate) * up).astype(BF16), wd_ref[...], preferred_element_type=F32)
    lax.fori_loop(0, tmc, wait, 0)
    gk = gk_ref[...]
    for k in range(N_ACTIVE):
        y = y + gk[:, k:k + 1] * rows_ref[k]
    o_ref[...] = x1_ref[...] + _gate_rows(y, g2_ref[...])


def combine(slots_flat, hu, x1, gate2_g, gate_k, w_gs, w_us, w_ds, ys, tmc):
    t, d = x1.shape
    f = w_gs.shape[1]
    gm = tmc // MOD_GROUP
    row = lambda w: pl.BlockSpec((tmc, w), lambda i: (i, 0))
    const = lambda a, b: pl.BlockSpec((a, b), lambda i: (0, 0))
    return pl.pallas_call(
        functools.partial(_combine_kernel, tmc=tmc),
        grid=(t // tmc,),
        in_specs=[pl.BlockSpec((tmc * N_ACTIVE,), lambda i: (i,), memory_space=pltpu.SMEM),
                  pl.BlockSpec((tmc,) + hu.shape[1:], lambda i: (i, 0, 0)),
                  row(d), pl.BlockSpec((gm, d), lambda i: (i, 0)), row(N_ACTIVE),
                  const(d, f), const(d, f), const(f, d),
                  pl.BlockSpec(memory_space=pl.ANY)],
        out_specs=row(d),
        out_shape=jax.ShapeDtypeStruct((t, d), F32),
        scratch_shapes=[pltpu.VMEM((N_ACTIVE, tmc, d), F32), pltpu.SemaphoreType.DMA(())],
        compiler_params=_params("arbitrary"),
        name="combine",
    )(slots_flat, hu, x1, gate2_g, gate_k, w_gs, w_us, w_ds, ys)


def _group_rows(v, n):
    b, d = v.shape
    return jnp.broadcast_to(v[:, None, :], (b, n // MOD_GROUP, d)).reshape(b * n // MOD_GROUP, d)


def _pad_keys(a, l_pad):
    return jnp.pad(a, ((0, 0), (0, l_pad - a.shape[1]), (0, 0)))


def _token_stage(x, mod, w, caches, count0):
    b, n, d = x.shape
    t = b * n
    shift1, scale1, gate1, shift2, scale2, gate2 = [_group_rows(m, n) for m in jnp.split(mod, 6, axis=-1)]
    xf = x.reshape(t, d)
    tm = _row_tile(t, 512)
    (h, qa, ka, va, kab, vab, qi, ki, kib, wi) = in_proj_a(
        xf, w["norm_mix"], scale1, shift1, w["w_in_a"], w["q_norm_a"], w["k_norm_a"], tm)
    qb, kb, vb, kbb, vbb = in_proj_b(h, w["w_in_b"], tm)

    r3 = lambda a: a.reshape(b, n, a.shape[-1])
    if caches is None:
        ya = dsa_attention(r3(qa), r3(qi), r3(wi), r3(kab), r3(vab), r3(kib),
                           q_pos0=0, n_keys=n, tq=min(256, n), tc=min(512, n))
        yb = stick_attention(r3(qb), r3(kbb), r3(vbb), q_pos0=0, tq=min(256, n), tc=min(256, n))
    else:
        cka, cva, cki, ckb, cvb = caches
        past = cka.shape[1]
        tc_a, tc_b = min(512, past), min(256, past)
        new_chunk = lambda a: _pad_keys(r3(a), tc_a)
        ya = dsa_attention(r3(qa), r3(qi), r3(wi), new_chunk(kab), new_chunk(vab), new_chunk(kib),
                           q_pos0=past, n_keys=past + n, tq=n, tc=tc_a, caches=(cka, cva, cki))
        yb = stick_attention_cached(r3(qb), r3(kbb), r3(vbb), ckb, cvb, tc=tc_b)

    merged = merge_branches(h, ya.reshape(t, A_WIDTH), yb.reshape(t, B_WIDTH),
                            w["w_ga"], w["w_gb"], w["w_branch_a"], w["w_branch_b"], tm, _row_tile(d, 512))
    x1, hu, top_e, rank_k, gate_k, counts = out_proj(
        merged, xf, gate1, scale2, shift2, w["norm_ffn"], w["w_out"], w["w_router_t"], w["b_router"], count0, tm)
    rows = (ka.reshape(b, n, A_KV_HEADS, HEAD_DIM), va.reshape(b, n, A_KV_HEADS, HEAD_DIM),
            ki.reshape(b, n, IDX_DIM), kb.reshape(b, n, B_HEADS, HEAD_DIM), vb.reshape(b, n, B_HEADS, HEAD_DIM))
    return dict(x1=x1, hu=hu, top_e=top_e, rank_k=rank_k, gate_k=gate_k, counts=counts, gate2=gate2, rows=rows)


def _routing(counts, top_e, rank_k, tme):
    e = counts.shape[0]
    t = top_e.shape[1]
    counts = counts.reshape(e).astype(jnp.int32)
    padded = (counts + tme - 1) // tme * tme
    ends = jnp.cumsum(padded)
    offsets = ends - padded
    eids = jnp.arange(e, dtype=jnp.int32)[:, None, None]
    offset_k = jnp.sum(jnp.where(top_e[None] == eids, offsets[:, None, None], 0), axis=0)
    slots = offset_k + rank_k - 1
    n_tiles = (t * N_ACTIVE + e * tme) // tme
    tile_start = jnp.arange(n_tiles, dtype=jnp.int32) * tme
    tile_expert = jnp.minimum(jnp.sum(ends[None, :] <= tile_start[:, None], axis=1), e - 1).astype(jnp.int32)
    n_used = (ends[-1] // tme).astype(jnp.int32).reshape(1)
    return dict(slots=slots.T.reshape(-1), tile_expert=tile_expert, n_used=n_used, p_rows=n_tiles * tme,
                pad_start=offsets + counts, pad_count=padded - counts)


def kernel(x_prompt, x_sample, cache_a_k, cache_a_v, cache_a_idx_k, cache_b_k, cache_b_v, c_prompt, c_sample,
           norm_mix, norm_ffn, w_ada, b_ada, w_in, q_norm_a, k_norm_a, w_branch_a, w_branch_b, w_out,
           w_router, b_router, w_gate_e, w_up_e, w_down_e, w_gate_s, w_up_s, w_down_s):
    depth = norm_mix.shape[0]
    d = x_prompt.shape[-1]
    bp, bs = c_prompt.shape[0], c_sample.shape[0]
    c_all = jnp.concatenate([c_prompt, c_sample], axis=0)
    c_all = jnp.pad(c_all, ((0, -(bp + bs) % 16), (0, 0)))
    tme = 256

    xp, xs = x_prompt, x_sample
    prompt_rows, sample_rows = [], []
    for layer in range(depth):
        wl = w_in[layer]
        o_idx = A_WIDTH + 2 * A_KV_WIDTH + IDX_Q_WIDTH
        o_b = o_idx + IDX_DIM + IDX_HEADS
        o_g = o_b + 3 * B_WIDTH
        w_in_a = jnp.concatenate(
            [wl[:, :o_b], jnp.zeros((d, LANES - IDX_DIM - IDX_HEADS), wl.dtype)], axis=1).astype(BF16)
        w = dict(
            norm_mix=norm_mix[layer].reshape(1, d), norm_ffn=norm_ffn[layer].reshape(1, d),
            q_norm_a=q_norm_a[layer].reshape(1, HEAD_DIM), k_norm_a=k_norm_a[layer].reshape(1, HEAD_DIM),
            w_in_a=w_in_a, w_in_b=wl[:, o_b:o_g].astype(BF16),
            w_ga=wl[:, o_g:o_g + d].astype(BF16), w_gb=wl[:, o_g + d:].astype(BF16),
            w_branch_a=w_branch_a[layer].astype(BF16), w_branch_b=w_branch_b[layer].astype(BF16),
            w_out=w_out[layer].astype(BF16), w_router_t=w_router[layer].T.astype(BF16),
            b_router=b_router[layer].reshape(N_EXPERTS, 1))
        mod = ada_mod(c_all, w_ada[layer], b_ada[layer])
        caches = (cache_a_k[layer], cache_a_v[layer], cache_a_idx_k[layer], cache_b_k[layer], cache_b_v[layer])
        sp = _token_stage(xp, mod[:bp], w, None, jnp.zeros((N_EXPERTS, 1), F32))
        ss = _token_stage(xs, mod[bp:bp + bs], w, caches, sp["counts"])
        prompt_rows.append(sp["rows"])
        sample_rows.append(ss["rows"])

        tp, ts = sp["x1"].shape[0], ss["x1"].shape[0]
        both = lambda name: jnp.concatenate([sp[name], ss[name]], axis=1)
        rt = _routing(ss["counts"], both("top_e"), both("rank_k"), tme)
        slots, gate_k = rt["slots"], both("gate_k").T
        xsorted = pad_fill(rt["pad_start"], rt["pad_count"], rt["n_used"], rt["p_rows"],
                           sp["hu"].shape[1:], tme)
        xsorted = dispatch(slots[:tp * N_ACTIVE], sp["hu"], xsorted, _row_tile(tp, 256))
        xsorted = dispatch(slots[tp * N_ACTIVE:], ss["hu"], xsorted, _row_tile(ts, 256))
        ysorted = expert_ffn(rt["tile_expert"], rt["n_used"], xsorted, w_gate_e[layer], w_up_e[layer],
                             w_down_e[layer], tme)
        shared = (w_gate_s[layer].astype(BF16), w_up_s[layer].astype(BF16), w_down_s[layer].astype(BF16))
        tmc = 256
        yp = combine(slots[:tp * N_ACTIVE], sp["hu"], sp["x1"], sp["gate2"], gate_k[:tp], *shared, ysorted,
                     _row_tile(tp, tmc))
        ys = combine(slots[tp * N_ACTIVE:], ss["hu"], ss["x1"], ss["gate2"], gate_k[tp:], *shared, ysorted,
                     _row_tile(ts, tmc))
        xp = yp.reshape(xp.shape)
        xs = ys.reshape(xs.shape)

    stack = lambda rows: tuple(jnp.stack(r) for r in zip(*rows))
    return (xp, xs) + stack(prompt_rows) + stack(sample_rows)
```

```python
import functools

import jax
import jax.numpy as jnp
from jax import lax
from jax.experimental import pallas as pl
from jax.experimental.pallas import tpu as pltpu

CHUNK = 64
HEAD_DIM = 128
A_HEADS = 8
A_KV_HEADS = 2
A_GROUP = A_HEADS // A_KV_HEADS
IDX_HEADS = 8
IDX_DIM = 64
TOPK_KEYS = 256
B_HEADS = 8
N_EXPERTS = 64
N_ACTIVE = 8
N_GROUPS = 8
N_ACTIVE_GROUPS = 4
GROUP_SIZE = N_EXPERTS // N_GROUPS
ROUTED_SCALE = 2.5
EPS = 1e-6

A_WIDTH = A_HEADS * HEAD_DIM
A_KV_WIDTH = A_KV_HEADS * HEAD_DIM
IDX_Q_WIDTH = IDX_HEADS * IDX_DIM
B_WIDTH = B_HEADS * HEAD_DIM

LANES = 128
MOD_GROUP = 32
VMEM_LIMIT = 56 * 1024 * 1024
NEG_BIG = -1e30
STICK_UNDERFLOW = -110.0
INT_MIN = -(2 ** 31)

F32 = jnp.float32
BF16 = jnp.bfloat16
NT_DIMS = (((1,), (1,)), ((), ()))


def _params(*sem):
    return pltpu.CompilerParams(dimension_semantics=sem, vmem_limit_bytes=VMEM_LIMIT)


def _silu(x):
    return x * jax.nn.sigmoid(x)


def _rms(x, gain):
    return x * lax.rsqrt(jnp.mean(x * x, axis=-1, keepdims=True) + EPS) * gain


def _modulate(y, scale_g, shift_g):
    tm, d = y.shape
    y3 = y.reshape(tm // MOD_GROUP, MOD_GROUP, d)
    out = y3 * (1.0 + scale_g[:, None, :]) + shift_g[:, None, :]
    return out.reshape(tm, d)


def _gate_rows(y, gate_g):
    tm, d = y.shape
    return (y.reshape(tm // MOD_GROUP, MOD_GROUP, d) * gate_g[:, None, :]).reshape(tm, d)


def _row_tile(t, pref):
    tm = min(pref, t)
    while t % tm:
        tm //= 2
    return tm


def _ada_kernel(c_ref, w_ref, b_ref, o_ref):
    s = _silu(c_ref[...]).astype(BF16)
    o_ref[...] = jnp.dot(s, w_ref[...].astype(BF16), preferred_element_type=F32) + b_ref[...]


def ada_mod(c, w_ada, b_ada):
    r, d = c.shape
    n = w_ada.shape[1]
    tn = _row_tile(n, 1024)
    return pl.pallas_call(
        _ada_kernel,
        grid=(n // tn,),
        in_specs=[pl.BlockSpec((r, d), lambda j: (0, 0)),
                  pl.BlockSpec((d, tn), lambda j: (0, j)),
                  pl.BlockSpec((1, tn), lambda j: (0, j))],
        out_specs=pl.BlockSpec((r, tn), lambda j: (0, j)),
        out_shape=jax.ShapeDtypeStruct((r, n), F32),
        compiler_params=_params("arbitrary"),
        name="ada_mod",
    )(c, w_ada, b_ada.reshape(1, n))


def _in_a_kernel(x_ref, nw_ref, sc_ref, sh_ref, w_ref, qn_ref, kn_ref,
                 h_ref, qa_ref, ka_ref, va_ref, kab_ref, vab_ref, qi_ref, ki_ref, kib_ref, wi_ref):
    h = _modulate(_rms(x_ref[...], nw_ref[...]), sc_ref[...], sh_ref[...])
    hb = h.astype(BF16)
    h_ref[...] = hb
    off = 0
    for hd in range(A_HEADS):
        q = jnp.dot(hb, w_ref[:, off:off + HEAD_DIM], preferred_element_type=F32)
        qa_ref[:, hd * HEAD_DIM:(hd + 1) * HEAD_DIM] = (_rms(q, qn_ref[...]) * (HEAD_DIM ** -0.5)).astype(BF16)
        off += HEAD_DIM
    for hd in range(A_KV_HEADS):
        k = jnp.dot(hb, w_ref[:, off:off + HEAD_DIM], preferred_element_type=F32)
        k = _rms(k, kn_ref[...])
        ka_ref[:, hd * HEAD_DIM:(hd + 1) * HEAD_DIM] = k
        kab_ref[:, hd * HEAD_DIM:(hd + 1) * HEAD_DIM] = k.astype(BF16)
        off += HEAD_DIM
    v = jnp.dot(hb, w_ref[:, off:off + A_KV_WIDTH], preferred_element_type=F32)
    va_ref[...] = v
    vab_ref[...] = v.astype(BF16)
    off += A_KV_WIDTH
    qi = jnp.dot(hb, w_ref[:, off:off + IDX_Q_WIDTH], preferred_element_type=F32)
    qi_ref[...] = (qi * (IDX_DIM ** -0.5)).astype(BF16)
    off += IDX_Q_WIDTH
    kw = jnp.dot(hb, w_ref[:, off:off + LANES], preferred_element_type=F32)
    ki = kw[:, :IDX_DIM]
    ki_ref[...] = ki
    kib_ref[...] = ki.astype(BF16)
    wi_ref[...] = kw[:, IDX_DIM:IDX_DIM + IDX_HEADS]


def in_proj_a(x, norm_w, scale_g, shift_g, w_a, q_norm, k_norm, tm):
    t, d = x.shape
    wa = w_a.shape[1]
    gm = tm // MOD_GROUP
    row = lambda w: pl.BlockSpec((tm, w), lambda i: (i, 0))
    const = lambda a, b: pl.BlockSpec((a, b), lambda i: (0, 0))
    outs = [(d, BF16), (A_WIDTH, BF16), (A_KV_WIDTH, F32), (A_KV_WIDTH, F32), (A_KV_WIDTH, BF16),
            (A_KV_WIDTH, BF16), (IDX_Q_WIDTH, BF16), (IDX_DIM, F32), (IDX_DIM, BF16), (IDX_HEADS, F32)]
    return pl.pallas_call(
        _in_a_kernel,
        grid=(t // tm,),
        in_specs=[row(d), const(1, d), pl.BlockSpec((gm, d), lambda i: (i, 0)),
                  pl.BlockSpec((gm, d), lambda i: (i, 0)), const(d, wa),
                  const(1, HEAD_DIM), const(1, HEAD_DIM)],
        out_specs=[row(w) for w, _ in outs],
        out_shape=[jax.ShapeDtypeStruct((t, w), dt) for w, dt in outs],
        compiler_params=_params("arbitrary"),
        name="in_proj_a",
    )(x, norm_w, scale_g, shift_g, w_a, q_norm, k_norm)


def _in_b_kernel(h_ref, w_ref, qb_ref, kb_ref, vb_ref, kbb_ref, vbb_ref):
    hb = h_ref[...]
    qb_ref[...] = jnp.dot(hb, w_ref[:, :B_WIDTH], preferred_element_type=F32).astype(BF16)
    k = jnp.dot(hb, w_ref[:, B_WIDTH:2 * B_WIDTH], preferred_element_type=F32)
    kbb_ref[...] = k.astype(BF16)
    v = jnp.dot(hb, w_ref[:, 2 * B_WIDTH:], preferred_element_type=F32)
    vbb_ref[...] = v.astype(BF16)
    for hd in range(B_HEADS):
        kb_ref[:, hd, :] = k[:, hd * HEAD_DIM:(hd + 1) * HEAD_DIM]
        vb_ref[:, hd, :] = v[:, hd * HEAD_DIM:(hd + 1) * HEAD_DIM]


def in_proj_b(h, w_b, tm):
    t, d = h.shape
    row = lambda w: pl.BlockSpec((tm, w), lambda i: (i, 0))
    heads = pl.BlockSpec((tm, B_HEADS, HEAD_DIM), lambda i: (i, 0, 0))
    flat = lambda dt: jax.ShapeDtypeStruct((t, B_WIDTH), dt)
    per_head = jax.ShapeDtypeStruct((t, B_HEADS, HEAD_DIM), F32)
    return pl.pallas_call(
        _in_b_kernel,
        grid=(t // tm,),
        in_specs=[row(d), pl.BlockSpec((d, 3 * B_WIDTH), lambda i: (0, 0))],
        out_specs=[row(B_WIDTH), heads, heads, row(B_WIDTH), row(B_WIDTH)],
        out_shape=[flat(BF16), per_head, per_head, flat(BF16), flat(BF16)],
        compiler_params=_params("arbitrary"),
        name="in_proj_b",
    )(h, w_b)


def _score_key(x):
    bits = lax.bitcast_convert_type(x, jnp.int32)
    return bits ^ ((bits >> 31) & 0x7FFFFFFF)


def _lane_tiles(a):
    return [a[:, j * LANES:(j + 1) * LANES] for j in range(a.shape[1] // LANES)]


def _dsa_kernel(qa_ref, qi_ref, wi_ref, k_ref, v_ref, ki_ref, *rest, q_pos0, n_keys, n_sel, tq, tc, pos_bits,
                cached):
    if cached:
        ck_ref, cv_ref, cki_ref, o_ref, key_ref, mb_ref, m_ref, l_ref, acc_ref = rest
    else:
        o_ref, key_ref, mb_ref, m_ref, l_ref, acc_ref = rest

    def read_k(start, g, last):
        if cached and not last:
            return ck_ref[pl.ds(start, tc), g, :].astype(BF16)
        rows = slice(None) if cached else pl.ds(start, tc)
        return k_ref[rows, g * HEAD_DIM:(g + 1) * HEAD_DIM]

    def read_v(start, g, last):
        if cached and not last:
            return cv_ref[pl.ds(start, tc), g, :].astype(BF16)
        rows = slice(None) if cached else pl.ds(start, tc)
        return v_ref[rows, g * HEAD_DIM:(g + 1) * HEAD_DIM]

    def read_ki(start, last):
        if cached and not last:
            return cki_ref[pl.ds(start, tc), :].astype(BF16)
        return ki_ref[...] if cached else ki_ref[pl.ds(start, tc), :]

    i = pl.program_id(1)
    q0 = q_pos0 + i * tq
    kmax = jnp.minimum(((q0 + tq - 1) // CHUNK + 1) * CHUNK, n_keys)
    nch = (kmax + tc - 1) // tc
    qpos = q0 + lax.broadcasted_iota(jnp.int32, (tq, 1), 0)
    qchunk = qpos // CHUNK

    head_w = wi_ref[...] * (IDX_HEADS ** -0.5)
    qi_all = jnp.concatenate([qi_ref[:, h * IDX_DIM:(h + 1) * IDX_DIM] for h in range(IDX_HEADS)], axis=0)

    def all_chunks(chunk_pass):
        def body(c, carry):
            chunk_pass(c, False)
            return carry
        lax.fori_loop(0, nch - 1, body, 0)
        chunk_pass(nch - 1, True)

    def score_pass(c, last):
        start = pl.multiple_of(c * tc, tc)
        lg = lax.dot_general(qi_all, read_ki(start, last), NT_DIMS, preferred_element_type=F32)
        s = jnp.zeros((tq, tc), F32)
        for h in range(IDX_HEADS):
            s = s + jnp.maximum(lg[h * tq:(h + 1) * tq], 0.0) * head_w[:, h:h + 1]
        kpos = start + lax.broadcasted_iota(jnp.int32, (1, tc), 1)
        adm = jnp.logical_and(kpos // CHUNK <= qchunk, kpos < n_keys)
        key_ref[c] = jnp.where(adm, _score_key(s), INT_MIN)

    all_chunks(score_pass)

    def count_rows(pred):
        def body(c, acc):
            start = pl.multiple_of(c * tc, tc)
            hit = jnp.where(pred(key_ref[c], start), 1.0, 0.0)
            return acc + functools.reduce(jnp.add, _lane_tiles(hit))
        acc = lax.fori_loop(0, nch, body, jnp.zeros((tq, LANES), F32))
        return jnp.sum(acc, axis=1, keepdims=True)

    def count_ge(cand):
        return count_rows(lambda key, start: key >= cand)

    def pending(done):
        return jnp.min(done) < 0.5

    cnt = count_ge(jnp.zeros((tq, 1), jnp.int32))
    thr = jnp.where(cnt >= n_sel, 0, INT_MIN).astype(jnp.int32)
    n_adm = jnp.minimum((qchunk + 1) * CHUNK, n_keys)
    done = jnp.where(jnp.logical_or(n_adm <= n_sel, cnt == n_sel), 1.0, 0.0)

    def bit_cond(state):
        b, _, done = state
        return jnp.logical_and(b < 31, pending(done))

    def bit_body(state):
        b, thr, done = state
        cand = thr | jnp.left_shift(jnp.int32(1), 30 - b)
        cnt = count_ge(cand)
        return (b + 1, jnp.where(cnt >= n_sel, cand, thr), jnp.where(cnt == n_sel, 1.0, done))

    _, thr, done = lax.while_loop(bit_cond, bit_body, (jnp.int32(0), thr, done))

    def tie_limit():
        need = n_sel - count_rows(lambda key, start: key > thr)

        def tied_below(limit):
            return count_rows(lambda key, start: jnp.logical_and(
                key == thr, start + lax.broadcasted_iota(jnp.int32, (1, tc), 1) < limit))

        def body(b, lim):
            cand = lim | jnp.left_shift(jnp.int32(1), pos_bits - 1 - b)
            return jnp.where(tied_below(cand) < need, cand, lim)

        lim = lax.fori_loop(0, pos_bits, body, jnp.zeros((tq, 1), jnp.int32))
        return jnp.where(done > 0.5, jnp.int32(2 ** pos_bits), lim + 1)

    plim = lax.cond(pending(done), tie_limit, lambda: jnp.full((tq, 1), 2 ** pos_bits, jnp.int32))

    def mask_body(c, carry):
        start = pl.multiple_of(c * tc, tc)
        key = key_ref[c]
        kpos = start + lax.broadcasted_iota(jnp.int32, (1, tc), 1)
        sel = jnp.logical_or(key > thr, jnp.logical_and(key == thr, kpos < plim))
        mb_ref[c] = jnp.where(jnp.logical_and(sel, key != INT_MIN), 0.0, NEG_BIG)
        return carry

    lax.fori_loop(0, nch, mask_body, 0)

    slopes = [2.0 ** (-8.0 * (h + 1) / A_HEADS) for h in range(A_HEADS)]
    lane = lax.broadcasted_iota(jnp.int32, (1, LANES), 1)
    rows4 = A_GROUP * tq
    q_aug = []
    for g in range(A_KV_HEADS):
        heads = range(g * A_GROUP, (g + 1) * A_GROUP)
        q4 = jnp.concatenate([qa_ref[:, h * HEAD_DIM:(h + 1) * HEAD_DIM] for h in heads], axis=0)
        slope_col = jnp.concatenate([jnp.full((tq, 1), slopes[h], F32) for h in heads], axis=0)
        q_aug.append(jnp.concatenate([q4, jnp.where(lane < 3, slope_col, 0.0).astype(BF16)], axis=1))

    def chunk_logits(c, last):
        start = pl.multiple_of(c * tc, tc)
        kp = start + lax.broadcasted_iota(jnp.int32, (tc, 1), 0)
        pos_cols = jnp.where(lane == 0, kp & 63,
                             jnp.where(lane == 1, kp & (127 << 6),
                                       jnp.where(lane == 2, kp & ~8191, 0))).astype(F32).astype(BF16)
        mb = mb_ref[c]
        out = []
        for g in range(A_KV_HEADS):
            kc = jnp.concatenate([read_k(start, g, last), pos_cols], axis=1)
            lg = lax.dot_general(q_aug[g], kc, NT_DIMS, preferred_element_type=F32).reshape(A_GROUP, tq, tc)
            if last:
                ahead = jnp.maximum(start + lax.broadcasted_iota(jnp.int32, (1, tc), 1) - qpos, 0).astype(F32)
                bias = jnp.stack([mb - (2.0 * slopes[g * A_GROUP + r]) * ahead for r in range(A_GROUP)])
            else:
                bias = mb[None]
            out.append((lg + bias).reshape(rows4, tc))
        return start, out

    m_ref[...] = jnp.full(m_ref.shape, NEG_BIG, F32)
    l_ref[...] = jnp.zeros(l_ref.shape, F32)
    acc_ref[...] = jnp.zeros(acc_ref.shape, F32)

    def max_pass(c, last):
        _, logits = chunk_logits(c, last)
        for g in range(A_KV_HEADS):
            rows = slice(g * rows4, (g + 1) * rows4)
            m_ref[rows] = functools.reduce(jnp.maximum, _lane_tiles(logits[g]), m_ref[rows])

    def sum_pass(c, last):
        start, logits = chunk_logits(c, last)
        for g in range(A_KV_HEADS):
            rows = slice(g * rows4, (g + 1) * rows4)
            m = m_ref[rows]
            p = [jnp.exp(t - m) for t in _lane_tiles(logits[g])]
            l_ref[rows] = functools.reduce(jnp.add, p, l_ref[rows])
            acc_ref[rows] = acc_ref[rows] + jnp.dot(jnp.concatenate(p, axis=1).astype(BF16),
                                                    read_v(start, g, last), preferred_element_type=F32)

    all_chunks(max_pass)
    m_ref[...] = jnp.broadcast_to(jnp.max(m_ref[...], axis=1, keepdims=True), m_ref.shape)
    all_chunks(sum_pass)
    for h in range(A_HEADS):
        rows = slice(h * tq, (h + 1) * tq)
        o_ref[:, h * HEAD_DIM:(h + 1) * HEAD_DIM] = (
            acc_ref[rows] / jnp.sum(l_ref[rows], axis=1, keepdims=True)).astype(BF16)


def dsa_attention(qa, qi, wi, k, v, ki, *, q_pos0, n_keys, tq, tc, caches=()):
    b, n, _ = qa.shape
    if caches:
        assert q_pos0 % tc == 0 and n == tq <= tc and k.shape[1] == tc and n_keys == q_pos0 + n
        l_pad = q_pos0 + tc
    else:
        l_pad = k.shape[1]
    n_sel = min(TOPK_KEYS, n_keys // 4)
    qspec = lambda w: pl.BlockSpec((None, tq, w), lambda bi, i: (bi, i, 0))
    kspec = lambda w: pl.BlockSpec((None, k.shape[1], w), lambda bi, i: (bi, 0, 0))
    cache_specs = [pl.BlockSpec((None,) + c.shape[1:], lambda bi, i, nd=c.ndim: (bi,) + (0,) * (nd - 1))
                   for c in caches]
    kern = functools.partial(_dsa_kernel, q_pos0=q_pos0, n_keys=n_keys, n_sel=n_sel, tq=tq, tc=tc,
                             pos_bits=max(1, (l_pad - 1).bit_length()), cached=bool(caches))
    return pl.pallas_call(
        kern,
        grid=(b, n // tq),
        in_specs=[qspec(A_WIDTH), qspec(IDX_Q_WIDTH), qspec(IDX_HEADS),
                  kspec(A_KV_WIDTH), kspec(A_KV_WIDTH), kspec(IDX_DIM)] + cache_specs,
        out_specs=qspec(A_WIDTH),
        out_shape=jax.ShapeDtypeStruct((b, n, A_WIDTH), BF16),
        scratch_shapes=[pltpu.VMEM((l_pad // tc, tq, tc), jnp.int32),
                        pltpu.VMEM((l_pad // tc, tq, tc), F32),
                        pltpu.VMEM((A_HEADS * tq, LANES), F32),
                        pltpu.VMEM((A_HEADS * tq, LANES), F32),
                        pltpu.VMEM((A_HEADS * tq, HEAD_DIM), F32)],
        compiler_params=_params("arbitrary", "arbitrary"),
        name="dsa",
    )(qa, qi, wi, k, v, ki, *caches)


STICK_HEADS_PER_STEP = 2


def _strict_upper(n):
    return (lax.broadcasted_iota(jnp.int32, (n, n), 0) > lax.broadcasted_iota(jnp.int32, (n, n), 1)).astype(BF16)


def _stick_step(q, kc, vc, strict, upper, tail, acc):
    z = lax.dot_general(q, kc, NT_DIMS, preferred_element_type=F32) * (HEAD_DIM ** -0.5)
    softplus = jnp.log(1.0 + jnp.exp(-jnp.abs(z)))
    log_go = jnp.minimum(z, 0.0) - softplus
    log_stay = jnp.where(strict, jnp.minimum(-z, 0.0) - softplus, 0.0)
    hi = log_stay.astype(BF16)
    lo = (log_stay - hi.astype(F32)).astype(BF16)
    after = jnp.dot(hi, upper, preferred_element_type=F32) + jnp.dot(lo, upper, preferred_element_type=F32) + tail
    w = jnp.where(strict, jnp.exp(log_go + after), 0.0)
    return (tail + jnp.sum(log_stay, axis=1, keepdims=True),
            acc + jnp.dot(w.astype(BF16), vc, preferred_element_type=F32))


def _stick_walk(qs, tpos, c_first, tc, read_kv, tails, accs):
    upper = _strict_upper(tc)

    def cond(carry):
        n, tails, _ = carry
        return jnp.logical_and(n <= c_first, jnp.max(functools.reduce(jnp.maximum, tails)) > STICK_UNDERFLOW)

    def body(carry):
        n, tails, accs = carry
        start = pl.multiple_of((c_first - n) * tc, tc)
        strict = start + lax.broadcasted_iota(jnp.int32, (1, tc), 1) < tpos
        new = [_stick_step(q, *read_kv(start, h), strict, upper, tail, acc)
               for h, (q, tail, acc) in enumerate(zip(qs, tails, accs))]
        return n + 1, tuple(t for t, _ in new), tuple(a for _, a in new)

    _, _, accs = lax.while_loop(cond, body, (jnp.int32(0), tuple(tails), tuple(accs)))
    return accs


def _stick_kernel(q_ref, k_ref, v_ref, o_ref, *, q_pos0, tq, tc):
    t0 = q_pos0 + pl.program_id(2) * tq
    tpos = t0 + lax.broadcasted_iota(jnp.int32, (tq, 1), 0)
    heads = [slice(h * HEAD_DIM, (h + 1) * HEAD_DIM) for h in range(STICK_HEADS_PER_STEP)]
    read_kv = lambda start, h: (k_ref[pl.ds(start, tc), heads[h]], v_ref[pl.ds(start, tc), heads[h]])
    zeros = lambda w: [jnp.zeros((tq, w), F32) for _ in heads]
    accs = _stick_walk([q_ref[:, hs] for hs in heads], tpos, (t0 + tq - 2) // tc, tc, read_kv,
                       zeros(1), zeros(HEAD_DIM))
    for hs, acc in zip(heads, accs):
        o_ref[:, hs] = acc.astype(BF16)


def _stick_cached_kernel(q_ref, k_ref, v_ref, ck_ref, cv_ref, o_ref, *, past, tc):
    n = q_ref.shape[0]
    tpos = past + lax.broadcasted_iota(jnp.int32, (n, 1), 0)
    strict = past + lax.broadcasted_iota(jnp.int32, (1, n), 1) < tpos
    upper = _strict_upper(n)
    for h0 in range(0, B_HEADS, STICK_HEADS_PER_STEP):
        group = range(h0, h0 + STICK_HEADS_PER_STEP)
        cols = [slice(h * HEAD_DIM, (h + 1) * HEAD_DIM) for h in group]
        qs = [q_ref[:, hs] for hs in cols]
        first = [_stick_step(q, k_ref[:, hs], v_ref[:, hs], strict, upper,
                             jnp.zeros((n, 1), F32), jnp.zeros((n, HEAD_DIM), F32)) for q, hs in zip(qs, cols)]
        read_kv = lambda start, h, h0=h0: (ck_ref[pl.ds(start, tc), h0 + h, :].astype(BF16),
                                           cv_ref[pl.ds(start, tc), h0 + h, :].astype(BF16))
        accs = _stick_walk(qs, tpos, past // tc - 1, tc, read_kv, [t for t, _ in first], [a for _, a in first])
        for hs, acc in zip(cols, accs):
            o_ref[:, hs] = acc.astype(BF16)


def stick_attention(qb, k, v, *, q_pos0, tq, tc):
    b, n, _ = qb.shape
    l_pad = k.shape[1]
    width = STICK_HEADS_PER_STEP * HEAD_DIM
    qspec = pl.BlockSpec((None, tq, width), lambda bi, h, i: (bi, i, h))
    kspec = pl.BlockSpec((None, l_pad, width), lambda bi, h, i: (bi, 0, h))
    kern = functools.partial(_stick_kernel, q_pos0=q_pos0, tq=tq, tc=tc)
    return pl.pallas_call(
        kern,
        grid=(b, B_HEADS // STICK_HEADS_PER_STEP, n // tq),
        in_specs=[qspec, kspec, kspec],
        out_specs=qspec,
        out_shape=jax.ShapeDtypeStruct((b, n, B_WIDTH), BF16),
        compiler_params=_params("arbitrary", "arbitrary", "arbitrary"),
        name="stick",
    )(qb, k, v)


def stick_attention_cached(qb, k_new, v_new, cache_k, cache_v, *, tc):
    b, n, _ = qb.shape
    past = cache_k.shape[1]
    assert past % tc == 0
    new = pl.BlockSpec((None, n, B_WIDTH), lambda bi: (bi, 0, 0))
    cache = pl.BlockSpec((None, past, B_HEADS, HEAD_DIM), lambda bi: (bi, 0, 0, 0))
    return pl.pallas_call(
        functools.partial(_stick_cached_kernel, past=past, tc=tc),
        grid=(b,),
        in_specs=[new, new, new, cache, cache],
        out_specs=new,
        out_shape=jax.ShapeDtypeStruct((b, n, B_WIDTH), BF16),
        compiler_params=_params("arbitrary"),
        name="stick_cached",
    )(qb, k_new, v_new, cache_k, cache_v)


def _merge_kernel(h_ref, ya_ref, yb_ref, wga_ref, wgb_ref, wa_ref, wb_ref, o_ref):
    hb = h_ref[...]
    ga = jax.nn.sigmoid(jnp.dot(hb, wga_ref[...], preferred_element_type=F32))
    gb = jax.nn.sigmoid(jnp.dot(hb, wgb_ref[...], preferred_element_type=F32))
    pa = jnp.dot(ya_ref[...], wa_ref[...], preferred_element_type=F32)
    pb = jnp.dot(yb_ref[...], wb_ref[...], preferred_element_type=F32)
    o_ref[...] = (ga * pa + gb * pb).astype(BF16)


def merge_branches(h, ya, yb, w_ga, w_gb, w_a, w_b, tm, tn):
    t, d = h.shape
    row = lambda w: pl.BlockSpec((tm, w), lambda i, j: (i, 0))
    col = lambda k: pl.BlockSpec((k, tn), lambda i, j: (0, j))
    return pl.pallas_call(
        _merge_kernel,
        grid=(t // tm, d // tn),
        in_specs=[row(d), row(A_WIDTH), row(B_WIDTH), col(d), col(d), col(A_WIDTH), col(B_WIDTH)],
        out_specs=pl.BlockSpec((tm, tn), lambda i, j: (i, j)),
        out_shape=jax.ShapeDtypeStruct((t, d), BF16),
        compiler_params=_params("arbitrary", "arbitrary"),
        name="merge",
    )(h, ya, yb, w_ga, w_gb, w_a, w_b)


def _pack_bf16_pair(lo, hi):
    lo_bits = lax.bitcast_convert_type(lo.astype(BF16).astype(F32), jnp.uint32)
    hi_bits = lax.bitcast_convert_type(hi.astype(BF16).astype(F32), jnp.uint32)
    return (lo_bits >> 16) | (hi_bits & jnp.uint32(0xFFFF0000))


def _unpack_f32_pair(u):
    return (lax.bitcast_convert_type(u << 16, F32),
            lax.bitcast_convert_type(u & jnp.uint32(0xFFFF0000), F32))


def _unpack_bf16_pair(u):
    lo, hi = _unpack_f32_pair(u)
    return lo.astype(BF16), hi.astype(BF16)


def _first_max(vals, idx, axes, n):
    m = vals
    for ax in axes:
        m = jnp.max(m, axis=ax, keepdims=True)
    first = jnp.where(vals == m, idx, n)
    for ax in axes:
        first = jnp.min(first, axis=ax, keepdims=True)
    return m, first


def _sum01(a):
    return jnp.sum(jnp.sum(a, axis=0, keepdims=True), axis=1, keepdims=True)


def _out_kernel(mg_ref, x_ref, g1_ref, sc_ref, sh_ref, nw_ref, wo_ref, wr_ref, br_ref, cnt0_ref,
                x1_ref, hu_ref, tope_ref, rank_ref, gatek_ref, cnt_out_ref, cnt_ref):
    tm, d = x_ref.shape
    proj = jnp.dot(mg_ref[...], wo_ref[...], preferred_element_type=F32)
    x1 = x_ref[...] + _gate_rows(proj, g1_ref[...])
    x1_ref[...] = x1
    h2 = _modulate(_rms(x1, nw_ref[...]), sc_ref[...], sh_ref[...])
    hu_ref[...] = _pack_bf16_pair(h2[:, :d // 2], h2[:, d // 2:])

    logits = lax.dot_general(wr_ref[...], h2.astype(BF16), NT_DIMS, preferred_element_type=F32)
    aff = jax.nn.sigmoid(logits)
    shape3 = (N_GROUPS, GROUP_SIZE, tm)
    aff3 = aff.reshape(shape3)
    biased = (aff + br_ref[...]).reshape(shape3)
    io_in = lax.broadcasted_iota(jnp.int32, shape3, 1)
    m1, i1 = _first_max(biased, io_in, (1,), GROUP_SIZE)
    m2 = jnp.max(jnp.where(io_in == i1, -jnp.inf, biased), axis=1, keepdims=True)
    gscore = m1 + m2
    gio = lax.broadcasted_iota(jnp.int32, gscore.shape, 0)
    keep = jnp.zeros(gscore.shape, jnp.bool_)
    for _ in range(N_ACTIVE_GROUPS):
        _, first = _first_max(gscore, gio, (0,), N_GROUPS)
        hit = gio == first
        keep = jnp.logical_or(keep, hit)
        gscore = jnp.where(hit, -jnp.inf, gscore)
    masked = jnp.where(jnp.broadcast_to(keep, shape3), biased, -jnp.inf)
    eio = lax.broadcasted_iota(jnp.int32, shape3, 0) * GROUP_SIZE + io_in
    sel = jnp.zeros(shape3, jnp.bool_)
    hits = []
    for _ in range(N_ACTIVE):
        _, first = _first_max(masked, eio, (0, 1), N_EXPERTS)
        hit = eio == first
        sel = jnp.logical_or(sel, hit)
        masked = jnp.where(hit, -jnp.inf, masked)
        hits.append((hit, first.reshape(1, tm)))
    sel_aff = jnp.where(sel, aff3, 0.0)
    gates = sel_aff / _sum01(sel_aff) * ROUTED_SCALE

    @pl.when(pl.program_id(0) == 0)
    def _():
        cnt_ref[...] = cnt0_ref[...]

    sel2 = jnp.where(sel, 1.0, 0.0).reshape(N_EXPERTS, tm)
    incl = (lax.broadcasted_iota(jnp.int32, (tm, tm), 0)
            <= lax.broadcasted_iota(jnp.int32, (tm, tm), 1)).astype(BF16)
    rank = jnp.dot(sel2.astype(BF16), incl, preferred_element_type=F32) + cnt_ref[...]
    cnt_ref[...] = cnt_ref[...] + jnp.sum(sel2, axis=1, keepdims=True)
    cnt_out_ref[...] = cnt_ref[...]
    rank3 = rank.reshape(shape3)
    tope_ref[...] = jnp.concatenate([first for _, first in hits], axis=0)
    rank_ref[...] = jnp.concatenate(
        [_sum01(jnp.where(hit, rank3, 0.0)).reshape(1, tm) for hit, _ in hits], axis=0).astype(jnp.int32)
    gatek_ref[...] = jnp.concatenate(
        [_sum01(jnp.where(hit, gates, 0.0)).reshape(1, tm) for hit, _ in hits], axis=0)


def out_proj(merged, x, gate1_g, scale_g, shift_g, norm_w, w_out, w_router_t, b_router, count0, tm):
    t, d = x.shape
    gm = tm // MOD_GROUP
    row = lambda w: pl.BlockSpec((tm, w), lambda i: (i, 0))
    grp = pl.BlockSpec((gm, d), lambda i: (i, 0))
    const = lambda a, b: pl.BlockSpec((a, b), lambda i: (0, 0))
    per_k = pl.BlockSpec((N_ACTIVE, tm), lambda i: (0, i))
    return pl.pallas_call(
        _out_kernel,
        grid=(t // tm,),
        in_specs=[row(d), row(d), grp, grp, grp, const(1, d), const(d, d),
                  const(N_EXPERTS, d), const(N_EXPERTS, 1), const(N_EXPERTS, 1)],
        out_specs=[row(d), row(d // 2), per_k, per_k, per_k, const(N_EXPERTS, 1)],
        out_shape=[jax.ShapeDtypeStruct((t, d), F32), jax.ShapeDtypeStruct((t, d // 2), jnp.uint32),
                   jax.ShapeDtypeStruct((N_ACTIVE, t), jnp.int32), jax.ShapeDtypeStruct((N_ACTIVE, t), jnp.int32),
                   jax.ShapeDtypeStruct((N_ACTIVE, t), F32), jax.ShapeDtypeStruct((N_EXPERTS, 1), F32)],
        scratch_shapes=[pltpu.VMEM((N_EXPERTS, 1), F32)],
        compiler_params=_params("arbitrary"),
        name="out_proj",
    )(merged, x, gate1_g, scale_g, shift_g, norm_w, w_out, w_router_t, b_router, count0)


def _pad_fill_kernel(start_ref, cnt_ref, nu_ref, xs_ref, zero_ref, sem, *, tme, n_tiles):
    zero_ref[...] = jnp.zeros(zero_ref.shape, zero_ref.dtype)

    def pad_copy(e, j):
        return pltpu.make_async_copy(zero_ref.at[pl.ds(0, 1)], xs_ref.at[pl.ds(start_ref[e] + j, 1)], sem.at[0])

    def tile_copy(i):
        return pltpu.make_async_copy(zero_ref, xs_ref.at[pl.ds(pl.multiple_of(i * tme, tme), tme)], sem.at[1])

    def for_each_copy(wait):
        def pad_rows(e, carry):
            def one(j, c):
                pad_copy(e, j).wait() if wait else pad_copy(e, j).start()
                return c
            return lax.fori_loop(0, cnt_ref[e], one, carry)

        def tail_tiles(i, carry):
            tile_copy(i).wait() if wait else tile_copy(i).start()
            return carry

        lax.fori_loop(0, N_EXPERTS, pad_rows, 0)
        lax.fori_loop(nu_ref[0], n_tiles, tail_tiles, 0)

    for_each_copy(wait=False)
    for_each_copy(wait=True)


def pad_fill(pad_start, pad_count, n_used, p_rows, row_shape, tme):
    grid_spec = pltpu.PrefetchScalarGridSpec(
        num_scalar_prefetch=3, grid=(1,), in_specs=[],
        out_specs=pl.BlockSpec(memory_space=pl.ANY),
        scratch_shapes=[pltpu.VMEM((tme,) + row_shape, jnp.uint32), pltpu.SemaphoreType.DMA((2,))])
    return pl.pallas_call(
        functools.partial(_pad_fill_kernel, tme=tme, n_tiles=p_rows // tme),
        grid_spec=grid_spec,
        out_shape=jax.ShapeDtypeStruct((p_rows,) + row_shape, jnp.uint32),
        compiler_params=_params("arbitrary"),
        name="pad_fill",
    )(pad_start, pad_count, n_used)


def _dispatch_kernel(slot_ref, hu_ref, xs_in_ref, xs_ref, sem, *, td):
    del xs_in_ref

    def row_copy(j, k):
        return pltpu.make_async_copy(hu_ref.at[pl.ds(j, 1)],
                                     xs_ref.at[pl.ds(slot_ref[j * N_ACTIVE + k], 1)], sem)

    def start(j, carry):
        for k in range(N_ACTIVE):
            row_copy(j, k).start()
        return carry

    def wait(j, carry):
        for k in range(N_ACTIVE):
            row_copy(j, k).wait()
        return carry

    lax.fori_loop(0, td, start, 0)
    lax.fori_loop(0, td, wait, 0)


def dispatch(slots_flat, hu, xs, td):
    t = hu.shape[0]
    return pl.pallas_call(
        functools.partial(_dispatch_kernel, td=td),
        grid=(t // td,),
        in_specs=[pl.BlockSpec((td * N_ACTIVE,), lambda i: (i,), memory_space=pltpu.SMEM),
                  pl.BlockSpec((td,) + hu.shape[1:], lambda i: (i, 0)),
                  pl.BlockSpec(memory_space=pl.ANY)],
        out_specs=pl.BlockSpec(memory_space=pl.ANY),
        out_shape=jax.ShapeDtypeStruct(xs.shape, xs.dtype),
        scratch_shapes=[pltpu.SemaphoreType.DMA(())],
        input_output_aliases={2: 0},
        compiler_params=_params("arbitrary"),
        name="dispatch",
    )(slots_flat, hu, xs)


def _expert_kernel(te_ref, nu_ref, first_ref, next_ref, par_ref, xs_ref, wg_ref, wu_ref, wd_ref, o_ref,
                   wgf_ref, wuf_ref, wdf_ref, wgb_ref, wub_ref, wdb_ref, sem):
    i = pl.program_id(0)
    used = i < nu_ref[0]

    def fetch(e, buf):
        return [pltpu.make_async_copy(w.at[e], f.at[buf], sem.at[buf, j])
                for j, (w, f) in enumerate(((wg_ref, wgf_ref), (wu_ref, wuf_ref), (wd_ref, wdf_ref)))]

    @pl.when(i == 0)
    def _():
        for copy in fetch(te_ref[0], par_ref[0]):
            copy.start()

    @pl.when(jnp.logical_and(used, first_ref[i] == 1))
    def _():
        buf = par_ref[i]
        for copy in fetch(te_ref[i], buf):
            copy.wait()

        @pl.when(next_ref[i] >= 0)
        def _():
            for copy in fetch(next_ref[i], 1 - buf):
                copy.start()

        wgb_ref[...] = wgf_ref[buf].astype(BF16)
        wub_ref[...] = wuf_ref[buf].astype(BF16)
        wdb_ref[...] = wdf_ref[buf].astype(BF16)

    @pl.when(used)
    def _():
        lo, hi = _unpack_bf16_pair(xs_ref[...])
        half = lo.shape[1]
        gate = (jnp.dot(lo, wgb_ref[:half], preferred_element_type=F32)
                + jnp.dot(hi, wgb_ref[half:], preferred_element_type=F32))
        up = (jnp.dot(lo, wub_ref[:half], preferred_element_type=F32)
              + jnp.dot(hi, wub_ref[half:], preferred_element_type=F32))
        hid = (_silu(gate) * up).astype(BF16)
        o_ref[...] = jnp.dot(hid, wdb_ref[...], preferred_element_type=F32)

    @pl.when(jnp.logical_not(used))
    def _():
        o_ref[...] = jnp.zeros(o_ref.shape, o_ref.dtype)


def expert_ffn(tiles, xs, w_gate, w_up, w_down, tme):
    p, half = xs.shape
    e, d, f = w_gate.shape
    hbm = pl.BlockSpec(memory_space=pl.ANY)
    grid_spec = pltpu.PrefetchScalarGridSpec(
        num_scalar_prefetch=5,
        grid=(p // tme,),
        in_specs=[pl.BlockSpec((tme, half), lambda i, te, nu, *_: (jnp.minimum(i, nu[0] - 1), 0)),
                  hbm, hbm, hbm],
        out_specs=pl.BlockSpec((tme, d), lambda i, *_: (i, 0)),
        scratch_shapes=[pltpu.VMEM((2, d, f), F32), pltpu.VMEM((2, d, f), F32), pltpu.VMEM((2, f, d), F32),
                        pltpu.VMEM((d, f), BF16), pltpu.VMEM((d, f), BF16), pltpu.VMEM((f, d), BF16),
                        pltpu.SemaphoreType.DMA((2, 3))],
    )
    return pl.pallas_call(
        _expert_kernel,
        grid_spec=grid_spec,
        out_shape=jax.ShapeDtypeStruct((p, d), F32),
        compiler_params=_params("arbitrary"),
        name="expert_ffn",
    )(tiles["expert"], tiles["n_used"], tiles["first"], tiles["next"], tiles["parity"], xs, w_gate, w_up, w_down)


def _combine_kernel(slot_ref, hu_ref, x1_ref, g2_ref, gk_ref, wg_ref, wu_ref, wd_ref, ys_ref,
                    o_ref, rows_ref, sem, *, tmc):
    def row_copy(j, k):
        return pltpu.make_async_copy(ys_ref.at[pl.ds(slot_ref[j * N_ACTIVE + k], 1)],
                                     rows_ref.at[k, pl.ds(j, 1)], sem)

    def start(j, carry):
        for k in range(N_ACTIVE):
            row_copy(j, k).start()
        return carry

    def wait(j, carry):
        for k in range(N_ACTIVE):
            row_copy(j, k).wait()
        return carry

    lax.fori_loop(0, tmc, start, 0)
    lo, hi = _unpack_bf16_pair(hu_ref[...])
    half = lo.shape[1]
    gate = (jnp.dot(lo, wg_ref[:half], preferred_element_type=F32)
            + jnp.dot(hi, wg_ref[half:], preferred_element_type=F32))
    up = (jnp.dot(lo, wu_ref[:half], preferred_element_type=F32)
          + jnp.dot(hi, wu_ref[half:], preferred_element_type=F32))
    y = jnp.dot((_silu(gate) * up).astype(BF16), wd_ref[...], preferred_element_type=F32)
    lax.fori_loop(0, tmc, wait, 0)
    gk = gk_ref[...]
    for k in range(N_ACTIVE):
        y = y + gk[:, k:k + 1] * rows_ref[k]
    o_ref[...] = x1_ref[...] + _gate_rows(y, g2_ref[...])


def combine(slots_flat, hu, x1, gate2_g, gate_k, w_gs, w_us, w_ds, ys, tmc):
    t, d = x1.shape
    f = w_gs.shape[1]
    gm = tmc // MOD_GROUP
    row = lambda w: pl.BlockSpec((tmc, w), lambda i: (i, 0))
    const = lambda a, b: pl.BlockSpec((a, b), lambda i: (0, 0))
    return pl.pallas_call(
        functools.partial(_combine_kernel, tmc=tmc),
        grid=(t // tmc,),
        in_specs=[pl.BlockSpec((tmc * N_ACTIVE,), lambda i: (i,), memory_space=pltpu.SMEM),
                  pl.BlockSpec((tmc,) + hu.shape[1:], lambda i: (i, 0)),
                  row(d), pl.BlockSpec((gm, d), lambda i: (i, 0)), row(N_ACTIVE),
                  const(d, f), const(d, f), const(f, d),
                  pl.BlockSpec(memory_space=pl.ANY)],
        out_specs=row(d),
        out_shape=jax.ShapeDtypeStruct((t, d), F32),
        scratch_shapes=[pltpu.VMEM((N_ACTIVE, tmc, d), F32), pltpu.SemaphoreType.DMA(())],
        compiler_params=_params("arbitrary"),
        name="combine",
    )(slots_flat, hu, x1, gate2_g, gate_k, w_gs, w_us, w_ds, ys)


def _group_rows(v, n):
    b, d = v.shape
    return jnp.broadcast_to(v[:, None, :], (b, n // MOD_GROUP, d)).reshape(b * n // MOD_GROUP, d)


def _pad_keys(a, l_pad):
    return jnp.pad(a, ((0, 0), (0, l_pad - a.shape[1]), (0, 0)))


def _token_stage(x, mod, w, caches, count0):
    b, n, d = x.shape
    t = b * n
    shift1, scale1, gate1, shift2, scale2, gate2 = [_group_rows(m, n) for m in jnp.split(mod, 6, axis=-1)]
    xf = x.reshape(t, d)
    tm = _row_tile(t, 512)
    (h, qa, ka, va, kab, vab, qi, ki, kib, wi) = in_proj_a(
        xf, w["norm_mix"], scale1, shift1, w["w_in_a"], w["q_norm_a"], w["k_norm_a"], tm)
    qb, kb, vb, kbb, vbb = in_proj_b(h, w["w_in_b"], tm)

    r3 = lambda a: a.reshape(b, n, a.shape[-1])
    if caches is None:
        ya = dsa_attention(r3(qa), r3(qi), r3(wi), r3(kab), r3(vab), r3(kib),
                           q_pos0=0, n_keys=n, tq=min(256, n), tc=min(512, n))
        yb = stick_attention(r3(qb), r3(kbb), r3(vbb), q_pos0=0, tq=min(256, n), tc=min(256, n))
    else:
        cka, cva, cki, ckb, cvb = caches
        past = cka.shape[1]
        tc_a, tc_b = min(512, past), min(256, past)
        new_chunk = lambda a: _pad_keys(r3(a), tc_a)
        ya = dsa_attention(r3(qa), r3(qi), r3(wi), new_chunk(kab), new_chunk(vab), new_chunk(kib),
                           q_pos0=past, n_keys=past + n, tq=n, tc=tc_a, caches=(cka, cva, cki))
        yb = stick_attention_cached(r3(qb), r3(kbb), r3(vbb), ckb, cvb, tc=tc_b)

    merged = merge_branches(h, ya.reshape(t, A_WIDTH), yb.reshape(t, B_WIDTH),
                            w["w_ga"], w["w_gb"], w["w_branch_a"], w["w_branch_b"], tm, _row_tile(d, 512))
    x1, hu, top_e, rank_k, gate_k, counts = out_proj(
        merged, xf, gate1, scale2, shift2, w["norm_ffn"], w["w_out"], w["w_router_t"], w["b_router"], count0, tm)
    rows = (ka.reshape(b, n, A_KV_HEADS, HEAD_DIM), va.reshape(b, n, A_KV_HEADS, HEAD_DIM),
            ki.reshape(b, n, IDX_DIM), kb.reshape(b, n, B_HEADS, HEAD_DIM), vb.reshape(b, n, B_HEADS, HEAD_DIM))
    return dict(x1=x1, hu=hu, top_e=top_e, rank_k=rank_k, gate_k=gate_k, counts=counts, gate2=gate2, rows=rows)


def _routing(counts, top_e, rank_k, tme):
    e = counts.shape[0]
    t = top_e.shape[1]
    counts = counts.reshape(e).astype(jnp.int32)
    padded = (counts + tme - 1) // tme * tme
    ends = jnp.cumsum(padded)
    offsets = ends - padded
    eids = jnp.arange(e, dtype=jnp.int32)[:, None, None]
    offset_k = jnp.sum(jnp.where(top_e[None] == eids, offsets[:, None, None], 0), axis=0)
    slots = offset_k + rank_k - 1
    n_tiles = (t * N_ACTIVE + e * tme) // tme
    tile_start = jnp.arange(n_tiles, dtype=jnp.int32) * tme
    tile_expert = jnp.minimum(jnp.sum(ends[None, :] <= tile_start[:, None], axis=1), e - 1).astype(jnp.int32)
    n_used = (ends[-1] // tme).astype(jnp.int32).reshape(1)
    ids = jnp.arange(e, dtype=jnp.int32)
    nonempty = padded > 0
    later = jnp.where((ids[None, :] > ids[:, None]) & nonempty[None, :], ids[None, :], e)
    next_e = jnp.min(later, axis=1)
    next_e = jnp.where(next_e == e, -1, next_e).astype(jnp.int32)
    rank_e = jnp.cumsum(nonempty.astype(jnp.int32)) - 1
    tiles = dict(expert=tile_expert, n_used=n_used,
                 first=(tile_start == offsets[tile_expert]).astype(jnp.int32),
                 next=next_e[tile_expert], parity=(rank_e[tile_expert] % 2).astype(jnp.int32))
    return dict(slots=slots.T.reshape(-1), tiles=tiles, n_used=n_used, p_rows=n_tiles * tme,
                pad_start=offsets + counts, pad_count=padded - counts)


def kernel(x_prompt, x_sample, cache_a_k, cache_a_v, cache_a_idx_k, cache_b_k, cache_b_v, c_prompt, c_sample,
           norm_mix, norm_ffn, w_ada, b_ada, w_in, q_norm_a, k_norm_a, w_branch_a, w_branch_b, w_out,
           w_router, b_router, w_gate_e, w_up_e, w_down_e, w_gate_s, w_up_s, w_down_s):
    depth = norm_mix.shape[0]
    d = x_prompt.shape[-1]
    bp, bs = c_prompt.shape[0], c_sample.shape[0]
    c_all = jnp.concatenate([c_prompt, c_sample], axis=0)
    c_all = jnp.pad(c_all, ((0, -(bp + bs) % 16), (0, 0)))
    tme = 256

    xp, xs = x_prompt, x_sample
    prompt_rows, sample_rows = [], []
    for layer in range(depth):
        wl = w_in[layer]
        o_idx = A_WIDTH + 2 * A_KV_WIDTH + IDX_Q_WIDTH
        o_b = o_idx + IDX_DIM + IDX_HEADS
        o_g = o_b + 3 * B_WIDTH
        w_in_a = jnp.concatenate(
            [wl[:, :o_b], jnp.zeros((d, LANES - IDX_DIM - IDX_HEADS), wl.dtype)], axis=1).astype(BF16)
        w = dict(
            norm_mix=norm_mix[layer].reshape(1, d), norm_ffn=norm_ffn[layer].reshape(1, d),
            q_norm_a=q_norm_a[layer].reshape(1, HEAD_DIM), k_norm_a=k_norm_a[layer].reshape(1, HEAD_DIM),
            w_in_a=w_in_a, w_in_b=wl[:, o_b:o_g].astype(BF16),
            w_ga=wl[:, o_g:o_g + d].astype(BF16), w_gb=wl[:, o_g + d:].astype(BF16),
            w_branch_a=w_branch_a[layer].astype(BF16), w_branch_b=w_branch_b[layer].astype(BF16),
            w_out=w_out[layer].astype(BF16), w_router_t=w_router[layer].T.astype(BF16),
            b_router=b_router[layer].reshape(N_EXPERTS, 1))
        mod = ada_mod(c_all, w_ada[layer], b_ada[layer])
        caches = (cache_a_k[layer], cache_a_v[layer], cache_a_idx_k[layer], cache_b_k[layer], cache_b_v[layer])
        sp = _token_stage(xp, mod[:bp], w, None, jnp.zeros((N_EXPERTS, 1), F32))
        ss = _token_stage(xs, mod[bp:bp + bs], w, caches, sp["counts"])
        prompt_rows.append(sp["rows"])
        sample_rows.append(ss["rows"])

        tp, ts = sp["x1"].shape[0], ss["x1"].shape[0]
        both = lambda name: jnp.concatenate([sp[name], ss[name]], axis=1)
        rt = _routing(ss["counts"], both("top_e"), both("rank_k"), tme)
        slots, gate_k = rt["slots"], both("gate_k").T
        xsorted = pad_fill(rt["pad_start"], rt["pad_count"], rt["n_used"], rt["p_rows"],
                           sp["hu"].shape[1:], tme)
        xsorted = dispatch(slots[:tp * N_ACTIVE], sp["hu"], xsorted, _row_tile(tp, 256))
        xsorted = dispatch(slots[tp * N_ACTIVE:], ss["hu"], xsorted, _row_tile(ts, 256))
        ysorted = expert_ffn(rt["tiles"], xsorted, w_gate_e[layer], w_up_e[layer], w_down_e[layer], tme)
        shared = (w_gate_s[layer].astype(BF16), w_up_s[layer].astype(BF16), w_down_s[layer].astype(BF16))
        tmc = 256
        yp = combine(slots[:tp * N_ACTIVE], sp["hu"], sp["x1"], sp["gate2"], gate_k[:tp], *shared, ysorted,
                     _row_tile(tp, tmc))
        ys = combine(slots[tp * N_ACTIVE:], ss["hu"], ss["x1"], ss["gate2"], gate_k[tp:], *shared, ysorted,
                     _row_tile(ts, tmc))
        xp = yp.reshape(xp.shape)
        xs = ys.reshape(xs.shape)

    stack = lambda rows: tuple(jnp.stack(r) for r in zip(*rows))
    return (xp, xs) + stack(prompt_rows) + stack(sample_rows)
```

```python
import functools

import jax
import jax.numpy as jnp
from jax import lax
from jax.experimental import pallas as pl
from jax.experimental.pallas import tpu as pltpu

CHUNK = 64
HEAD_DIM = 128
A_HEADS = 8
A_KV_HEADS = 2
A_GROUP = A_HEADS // A_KV_HEADS
IDX_HEADS = 8
IDX_DIM = 64
TOPK_KEYS = 256
B_HEADS = 8
N_EXPERTS = 64
N_ACTIVE = 8
N_GROUPS = 8
N_ACTIVE_GROUPS = 4
GROUP_SIZE = N_EXPERTS // N_GROUPS
ROUTED_SCALE = 2.5
EPS = 1e-6

A_WIDTH = A_HEADS * HEAD_DIM
A_KV_WIDTH = A_KV_HEADS * HEAD_DIM
IDX_Q_WIDTH = IDX_HEADS * IDX_DIM
B_WIDTH = B_HEADS * HEAD_DIM

LANES = 128
MOD_GROUP = 32
VMEM_LIMIT = 56 * 1024 * 1024
NEG_BIG = -1e30
STICK_UNDERFLOW = -110.0
INT_MIN = -(2 ** 31)

F32 = jnp.float32
BF16 = jnp.bfloat16
NT_DIMS = (((1,), (1,)), ((), ()))


def _params(*sem):
    return pltpu.CompilerParams(dimension_semantics=sem, vmem_limit_bytes=VMEM_LIMIT)


def _silu(x):
    return x * jax.nn.sigmoid(x)


def _rms(x, gain):
    return x * lax.rsqrt(jnp.mean(x * x, axis=-1, keepdims=True) + EPS) * gain


def _modulate(y, scale_g, shift_g):
    tm, d = y.shape
    y3 = y.reshape(tm // MOD_GROUP, MOD_GROUP, d)
    out = y3 * (1.0 + scale_g[:, None, :]) + shift_g[:, None, :]
    return out.reshape(tm, d)


def _gate_rows(y, gate_g):
    tm, d = y.shape
    return (y.reshape(tm // MOD_GROUP, MOD_GROUP, d) * gate_g[:, None, :]).reshape(tm, d)


def _row_tile(t, pref):
    tm = min(pref, t)
    while t % tm:
        tm //= 2
    return tm


def _ada_kernel(c_ref, w_ref, b_ref, o_ref):
    s = _silu(c_ref[...]).astype(BF16)
    o_ref[...] = jnp.dot(s, w_ref[...].astype(BF16), preferred_element_type=F32) + b_ref[...]


def ada_mod(c, w_ada, b_ada):
    r, d = c.shape
    n = w_ada.shape[1]
    tn = _row_tile(n, 1024)
    return pl.pallas_call(
        _ada_kernel,
        grid=(n // tn,),
        in_specs=[pl.BlockSpec((r, d), lambda j: (0, 0)),
                  pl.BlockSpec((d, tn), lambda j: (0, j)),
                  pl.BlockSpec((1, tn), lambda j: (0, j))],
        out_specs=pl.BlockSpec((r, tn), lambda j: (0, j)),
        out_shape=jax.ShapeDtypeStruct((r, n), F32),
        compiler_params=_params("arbitrary"),
        name="ada_mod",
    )(c, w_ada, b_ada.reshape(1, n))


def _in_a_kernel(x_ref, nw_ref, sc_ref, sh_ref, w_ref, qn_ref, kn_ref,
                 h_ref, qa_ref, ka_ref, va_ref, kab_ref, vab_ref, qi_ref, ki_ref, kib_ref, wi_ref):
    h = _modulate(_rms(x_ref[...], nw_ref[...]), sc_ref[...], sh_ref[...])
    hb = h.astype(BF16)
    h_ref[...] = hb
    off = 0
    for hd in range(A_HEADS):
        q = jnp.dot(hb, w_ref[:, off:off + HEAD_DIM], preferred_element_type=F32)
        qa_ref[:, hd * HEAD_DIM:(hd + 1) * HEAD_DIM] = (_rms(q, qn_ref[...]) * (HEAD_DIM ** -0.5)).astype(BF16)
        off += HEAD_DIM
    for hd in range(A_KV_HEADS):
        k = jnp.dot(hb, w_ref[:, off:off + HEAD_DIM], preferred_element_type=F32)
        k = _rms(k, kn_ref[...])
        ka_ref[:, hd * HEAD_DIM:(hd + 1) * HEAD_DIM] = k
        kab_ref[:, hd * HEAD_DIM:(hd + 1) * HEAD_DIM] = k.astype(BF16)
        off += HEAD_DIM
    v = jnp.dot(hb, w_ref[:, off:off + A_KV_WIDTH], preferred_element_type=F32)
    va_ref[...] = v
    vab_ref[...] = v.astype(BF16)
    off += A_KV_WIDTH
    qi = jnp.dot(hb, w_ref[:, off:off + IDX_Q_WIDTH], preferred_element_type=F32)
    qi_ref[...] = (qi * (IDX_DIM ** -0.5)).astype(BF16)
    off += IDX_Q_WIDTH
    kw = jnp.dot(hb, w_ref[:, off:off + LANES], preferred_element_type=F32)
    ki = kw[:, :IDX_DIM]
    ki_ref[...] = ki
    kib_ref[...] = ki.astype(BF16)
    wi_ref[...] = kw[:, IDX_DIM:IDX_DIM + IDX_HEADS]


def in_proj_a(x, norm_w, scale_g, shift_g, w_a, q_norm, k_norm, tm):
    t, d = x.shape
    wa = w_a.shape[1]
    gm = tm // MOD_GROUP
    row = lambda w: pl.BlockSpec((tm, w), lambda i: (i, 0))
    const = lambda a, b: pl.BlockSpec((a, b), lambda i: (0, 0))
    outs = [(d, BF16), (A_WIDTH, BF16), (A_KV_WIDTH, F32), (A_KV_WIDTH, F32), (A_KV_WIDTH, BF16),
            (A_KV_WIDTH, BF16), (IDX_Q_WIDTH, BF16), (IDX_DIM, F32), (IDX_DIM, BF16), (IDX_HEADS, F32)]
    return pl.pallas_call(
        _in_a_kernel,
        grid=(t // tm,),
        in_specs=[row(d), const(1, d), pl.BlockSpec((gm, d), lambda i: (i, 0)),
                  pl.BlockSpec((gm, d), lambda i: (i, 0)), const(d, wa),
                  const(1, HEAD_DIM), const(1, HEAD_DIM)],
        out_specs=[row(w) for w, _ in outs],
        out_shape=[jax.ShapeDtypeStruct((t, w), dt) for w, dt in outs],
        compiler_params=_params("arbitrary"),
        name="in_proj_a",
    )(x, norm_w, scale_g, shift_g, w_a, q_norm, k_norm)


def _in_b_kernel(h_ref, w_ref, qb_ref, kb_ref, vb_ref, kbb_ref, vbb_ref):
    hb = h_ref[...]
    qb_ref[...] = jnp.dot(hb, w_ref[:, :B_WIDTH], preferred_element_type=F32).astype(BF16)
    k = jnp.dot(hb, w_ref[:, B_WIDTH:2 * B_WIDTH], preferred_element_type=F32)
    kbb_ref[...] = k.astype(BF16)
    v = jnp.dot(hb, w_ref[:, 2 * B_WIDTH:], preferred_element_type=F32)
    vbb_ref[...] = v.astype(BF16)
    for hd in range(B_HEADS):
        kb_ref[:, hd, :] = k[:, hd * HEAD_DIM:(hd + 1) * HEAD_DIM]
        vb_ref[:, hd, :] = v[:, hd * HEAD_DIM:(hd + 1) * HEAD_DIM]


def in_proj_b(h, w_b, tm):
    t, d = h.shape
    row = lambda w: pl.BlockSpec((tm, w), lambda i: (i, 0))
    heads = pl.BlockSpec((tm, B_HEADS, HEAD_DIM), lambda i: (i, 0, 0))
    flat = lambda dt: jax.ShapeDtypeStruct((t, B_WIDTH), dt)
    per_head = jax.ShapeDtypeStruct((t, B_HEADS, HEAD_DIM), F32)
    return pl.pallas_call(
        _in_b_kernel,
        grid=(t // tm,),
        in_specs=[row(d), pl.BlockSpec((d, 3 * B_WIDTH), lambda i: (0, 0))],
        out_specs=[row(B_WIDTH), heads, heads, row(B_WIDTH), row(B_WIDTH)],
        out_shape=[flat(BF16), per_head, per_head, flat(BF16), flat(BF16)],
        compiler_params=_params("arbitrary"),
        name="in_proj_b",
    )(h, w_b)


def _score_key(x):
    bits = lax.bitcast_convert_type(x, jnp.int32)
    return bits ^ ((bits >> 31) & 0x7FFFFFFF)


def _lane_tiles(a):
    return [a[:, j * LANES:(j + 1) * LANES] for j in range(a.shape[1] // LANES)]


def _dsa_kernel(qa_ref, qi_ref, wi_ref, k_ref, v_ref, ki_ref, *rest, q_pos0, n_keys, n_sel, tq, tc, pos_bits,
                cached):
    if cached:
        ck_ref, cv_ref, cki_ref, o_ref, key_ref, m_ref, l_ref, acc_ref = rest
    else:
        o_ref, key_ref, m_ref, l_ref, acc_ref = rest

    def read_k(start, g, last):
        if cached and not last:
            return ck_ref[pl.ds(start, tc), g, :].astype(BF16)
        rows = slice(None) if cached else pl.ds(start, tc)
        return k_ref[rows, g * HEAD_DIM:(g + 1) * HEAD_DIM]

    def read_v(start, g, last):
        if cached and not last:
            return cv_ref[pl.ds(start, tc), g, :].astype(BF16)
        rows = slice(None) if cached else pl.ds(start, tc)
        return v_ref[rows, g * HEAD_DIM:(g + 1) * HEAD_DIM]

    def read_ki(start, last):
        if cached and not last:
            return cki_ref[pl.ds(start, tc), :].astype(BF16)
        return ki_ref[...] if cached else ki_ref[pl.ds(start, tc), :]

    i = pl.program_id(1)
    q0 = q_pos0 + i * tq
    kmax = jnp.minimum(((q0 + tq - 1) // CHUNK + 1) * CHUNK, n_keys)
    nch = (kmax + tc - 1) // tc
    qpos = q0 + lax.broadcasted_iota(jnp.int32, (tq, 1), 0)
    qchunk = qpos // CHUNK

    head_w = wi_ref[...] * (IDX_HEADS ** -0.5)
    qi_all = jnp.concatenate([qi_ref[:, h * IDX_DIM:(h + 1) * IDX_DIM] for h in range(IDX_HEADS)], axis=0)

    def all_chunks(chunk_pass):
        def body(c, carry):
            chunk_pass(c, False)
            return carry
        lax.fori_loop(0, nch - 1, body, 0)
        chunk_pass(nch - 1, True)

    def score_pass(c, last):
        start = pl.multiple_of(c * tc, tc)
        lg = lax.dot_general(qi_all, read_ki(start, last), NT_DIMS, preferred_element_type=F32)
        s = jnp.zeros((tq, tc), F32)
        for h in range(IDX_HEADS):
            s = s + jnp.maximum(lg[h * tq:(h + 1) * tq], 0.0) * head_w[:, h:h + 1]
        kpos = start + lax.broadcasted_iota(jnp.int32, (1, tc), 1)
        adm = jnp.logical_and(kpos // CHUNK <= qchunk, kpos < n_keys)
        key_ref[c] = jnp.where(adm, _score_key(s), INT_MIN)

    all_chunks(score_pass)

    def count_rows(pred):
        def body(c, acc):
            start = pl.multiple_of(c * tc, tc)
            hit = jnp.where(pred(key_ref[c], start), 1.0, 0.0)
            return acc + functools.reduce(jnp.add, _lane_tiles(hit))
        acc = lax.fori_loop(0, nch, body, jnp.zeros((tq, LANES), F32))
        return jnp.sum(acc, axis=1, keepdims=True)

    def count_ge(cand):
        return count_rows(lambda key, start: key >= cand)

    def pending(done):
        return jnp.min(done) < 0.5

    cnt = count_ge(jnp.zeros((tq, 1), jnp.int32))
    thr = jnp.where(cnt >= n_sel, 0, INT_MIN).astype(jnp.int32)
    n_adm = jnp.minimum((qchunk + 1) * CHUNK, n_keys)
    done = jnp.where(jnp.logical_or(n_adm <= n_sel, cnt == n_sel), 1.0, 0.0)

    def bit_cond(state):
        b, _, done = state
        return jnp.logical_and(b < 31, pending(done))

    def bit_body(state):
        b, thr, done = state
        cand = thr | jnp.left_shift(jnp.int32(1), 30 - b)
        cnt = count_ge(cand)
        return (b + 1, jnp.where(cnt >= n_sel, cand, thr), jnp.where(cnt == n_sel, 1.0, done))

    _, thr, done = lax.while_loop(bit_cond, bit_body, (jnp.int32(0), thr, done))

    def tie_limit():
        need = n_sel - count_rows(lambda key, start: key > thr)

        def tied_below(limit):
            return count_rows(lambda key, start: jnp.logical_and(
                key == thr, start + lax.broadcasted_iota(jnp.int32, (1, tc), 1) < limit))

        def body(b, lim):
            cand = lim | jnp.left_shift(jnp.int32(1), pos_bits - 1 - b)
            return jnp.where(tied_below(cand) < need, cand, lim)

        lim = lax.fori_loop(0, pos_bits, body, jnp.zeros((tq, 1), jnp.int32))
        return jnp.where(done > 0.5, jnp.int32(2 ** pos_bits), lim + 1)

    plim = lax.cond(pending(done), tie_limit, lambda: jnp.full((tq, 1), 2 ** pos_bits, jnp.int32))

    slopes = [2.0 ** (-8.0 * (h + 1) / A_HEADS) for h in range(A_HEADS)]
    lane = lax.broadcasted_iota(jnp.int32, (1, LANES), 1)
    rows4 = A_GROUP * tq
    q_aug = []
    for g in range(A_KV_HEADS):
        heads = range(g * A_GROUP, (g + 1) * A_GROUP)
        q4 = jnp.concatenate([qa_ref[:, h * HEAD_DIM:(h + 1) * HEAD_DIM] for h in heads], axis=0)
        slope_col = jnp.concatenate([jnp.full((tq, 1), slopes[h], F32) for h in heads], axis=0)
        q_aug.append(jnp.concatenate([q4, jnp.where(lane < 3, slope_col, 0.0).astype(BF16)], axis=1))

    def chunk_logits(c, last):
        start = pl.multiple_of(c * tc, tc)
        kp = start + lax.broadcasted_iota(jnp.int32, (tc, 1), 0)
        pos_cols = jnp.where(lane == 0, kp & 63,
                             jnp.where(lane == 1, kp & (127 << 6),
                                       jnp.where(lane == 2, kp & ~8191, 0))).astype(F32).astype(BF16)
        key = key_ref[c]
        kpos = start + lax.broadcasted_iota(jnp.int32, (1, tc), 1)
        sel = jnp.logical_or(key > thr, jnp.logical_and(key == thr, kpos < plim))
        mb = jnp.where(jnp.logical_and(sel, key != INT_MIN), 0.0, NEG_BIG)
        out = []
        for g in range(A_KV_HEADS):
            kc = jnp.concatenate([read_k(start, g, last), pos_cols], axis=1)
            lg = lax.dot_general(q_aug[g], kc, NT_DIMS, preferred_element_type=F32).reshape(A_GROUP, tq, tc)
            if last:
                ahead = jnp.maximum(kpos - qpos, 0).astype(F32)
                bias = jnp.stack([mb - (2.0 * slopes[g * A_GROUP + r]) * ahead for r in range(A_GROUP)])
            else:
                bias = mb[None]
            out.append((lg + bias).reshape(rows4, tc))
        return start, out

    m_ref[...] = jnp.full(m_ref.shape, NEG_BIG, F32)
    l_ref[...] = jnp.zeros(l_ref.shape, F32)
    acc_ref[...] = jnp.zeros(acc_ref.shape, F32)

    def attend(c, last):
        start, logits = chunk_logits(c, last)
        for g in range(A_KV_HEADS):
            rows = slice(g * rows4, (g + 1) * rows4)
            tiles = _lane_tiles(logits[g])
            m_old = m_ref[rows]
            m_new = jnp.maximum(m_old, jnp.max(functools.reduce(jnp.maximum, tiles), axis=1, keepdims=True))
            alpha = jnp.exp(m_old - m_new)
            p = [jnp.exp(t - m_new) for t in tiles]
            m_ref[rows] = m_new
            l_ref[rows] = alpha * l_ref[rows] + functools.reduce(jnp.add, p)
            acc_ref[rows] = alpha * acc_ref[rows] + jnp.dot(jnp.concatenate(p, axis=1).astype(BF16),
                                                            read_v(start, g, last), preferred_element_type=F32)

    all_chunks(attend)
    for h in range(A_HEADS):
        rows = slice(h * tq, (h + 1) * tq)
        o_ref[:, h * HEAD_DIM:(h + 1) * HEAD_DIM] = (
            acc_ref[rows] / jnp.sum(l_ref[rows], axis=1, keepdims=True)).astype(BF16)


def dsa_attention(qa, qi, wi, k, v, ki, *, q_pos0, n_keys, tq, tc, caches=()):
    b, n, _ = qa.shape
    if caches:
        assert q_pos0 % tc == 0 and n == tq <= tc and k.shape[1] == tc and n_keys == q_pos0 + n
        l_pad = q_pos0 + tc
    else:
        l_pad = k.shape[1]
    n_sel = min(TOPK_KEYS, n_keys // 4)
    qspec = lambda w: pl.BlockSpec((None, tq, w), lambda bi, i: (bi, i, 0))
    kspec = lambda w: pl.BlockSpec((None, k.shape[1], w), lambda bi, i: (bi, 0, 0))
    cache_specs = [pl.BlockSpec((None,) + c.shape[1:], lambda bi, i, nd=c.ndim: (bi,) + (0,) * (nd - 1))
                   for c in caches]
    kern = functools.partial(_dsa_kernel, q_pos0=q_pos0, n_keys=n_keys, n_sel=n_sel, tq=tq, tc=tc,
                             pos_bits=max(1, (l_pad - 1).bit_length()), cached=bool(caches))
    return pl.pallas_call(
        kern,
        grid=(b, n // tq),
        in_specs=[qspec(A_WIDTH), qspec(IDX_Q_WIDTH), qspec(IDX_HEADS),
                  kspec(A_KV_WIDTH), kspec(A_KV_WIDTH), kspec(IDX_DIM)] + cache_specs,
        out_specs=qspec(A_WIDTH),
        out_shape=jax.ShapeDtypeStruct((b, n, A_WIDTH), BF16),
        scratch_shapes=[pltpu.VMEM((l_pad // tc, tq, tc), jnp.int32),
                        pltpu.VMEM((A_HEADS * tq, LANES), F32),
                        pltpu.VMEM((A_HEADS * tq, LANES), F32),
                        pltpu.VMEM((A_HEADS * tq, HEAD_DIM), F32)],
        compiler_params=_params("arbitrary", "arbitrary"),
        name="dsa",
    )(qa, qi, wi, k, v, ki, *caches)


STICK_HEADS_PER_STEP = 2


def _strict_upper(n):
    return (lax.broadcasted_iota(jnp.int32, (n, n), 0) > lax.broadcasted_iota(jnp.int32, (n, n), 1)).astype(BF16)


def _stick_step(q, kc, vc, strict, upper, tail, acc):
    z = lax.dot_general(q, kc, NT_DIMS, preferred_element_type=F32) * (HEAD_DIM ** -0.5)
    softplus = jnp.log(1.0 + jnp.exp(-jnp.abs(z)))
    log_go = jnp.minimum(z, 0.0) - softplus
    log_stay = jnp.where(strict, jnp.minimum(-z, 0.0) - softplus, 0.0)
    hi = log_stay.astype(BF16)
    lo = (log_stay - hi.astype(F32)).astype(BF16)
    after = jnp.dot(hi, upper, preferred_element_type=F32) + jnp.dot(lo, upper, preferred_element_type=F32) + tail
    w = jnp.where(strict, jnp.exp(log_go + after), 0.0)
    return (tail + jnp.sum(log_stay, axis=1, keepdims=True),
            acc + jnp.dot(w.astype(BF16), vc, preferred_element_type=F32))


def _stick_walk(qs, tpos, c_first, tc, read_kv, tails, accs):
    upper = _strict_upper(tc)

    def cond(carry):
        n, tails, _ = carry
        return jnp.logical_and(n <= c_first, jnp.max(functools.reduce(jnp.maximum, tails)) > STICK_UNDERFLOW)

    def body(carry):
        n, tails, accs = carry
        start = pl.multiple_of((c_first - n) * tc, tc)
        strict = start + lax.broadcasted_iota(jnp.int32, (1, tc), 1) < tpos
        new = [_stick_step(q, *read_kv(start, h), strict, upper, tail, acc)
               for h, (q, tail, acc) in enumerate(zip(qs, tails, accs))]
        return n + 1, tuple(t for t, _ in new), tuple(a for _, a in new)

    _, _, accs = lax.while_loop(cond, body, (jnp.int32(0), tuple(tails), tuple(accs)))
    return accs


def _stick_kernel(q_ref, k_ref, v_ref, o_ref, *, q_pos0, tq, tc):
    t0 = q_pos0 + pl.program_id(2) * tq
    tpos = t0 + lax.broadcasted_iota(jnp.int32, (tq, 1), 0)
    heads = [slice(h * HEAD_DIM, (h + 1) * HEAD_DIM) for h in range(STICK_HEADS_PER_STEP)]
    read_kv = lambda start, h: (k_ref[pl.ds(start, tc), heads[h]], v_ref[pl.ds(start, tc), heads[h]])
    zeros = lambda w: [jnp.zeros((tq, w), F32) for _ in heads]
    accs = _stick_walk([q_ref[:, hs] for hs in heads], tpos, (t0 + tq - 2) // tc, tc, read_kv,
                       zeros(1), zeros(HEAD_DIM))
    for hs, acc in zip(heads, accs):
        o_ref[:, hs] = acc.astype(BF16)


def _stick_cached_kernel(q_ref, k_ref, v_ref, ck_ref, cv_ref, o_ref, *, past, tc):
    n = q_ref.shape[0]
    tpos = past + lax.broadcasted_iota(jnp.int32, (n, 1), 0)
    strict = past + lax.broadcasted_iota(jnp.int32, (1, n), 1) < tpos
    upper = _strict_upper(n)
    for h0 in range(0, B_HEADS, STICK_HEADS_PER_STEP):
        group = range(h0, h0 + STICK_HEADS_PER_STEP)
        cols = [slice(h * HEAD_DIM, (h + 1) * HEAD_DIM) for h in group]
        qs = [q_ref[:, hs] for hs in cols]
        first = [_stick_step(q, k_ref[:, hs], v_ref[:, hs], strict, upper,
                             jnp.zeros((n, 1), F32), jnp.zeros((n, HEAD_DIM), F32)) for q, hs in zip(qs, cols)]
        read_kv = lambda start, h, h0=h0: (ck_ref[pl.ds(start, tc), h0 + h, :].astype(BF16),
                                           cv_ref[pl.ds(start, tc), h0 + h, :].astype(BF16))
        accs = _stick_walk(qs, tpos, past // tc - 1, tc, read_kv, [t for t, _ in first], [a for _, a in first])
        for hs, acc in zip(cols, accs):
            o_ref[:, hs] = acc.astype(BF16)


def stick_attention(qb, k, v, *, q_pos0, tq, tc):
    b, n, _ = qb.shape
    l_pad = k.shape[1]
    width = STICK_HEADS_PER_STEP * HEAD_DIM
    qspec = pl.BlockSpec((None, tq, width), lambda bi, h, i: (bi, i, h))
    kspec = pl.BlockSpec((None, l_pad, width), lambda bi, h, i: (bi, 0, h))
    kern = functools.partial(_stick_kernel, q_pos0=q_pos0, tq=tq, tc=tc)
    return pl.pallas_call(
        kern,
        grid=(b, B_HEADS // STICK_HEADS_PER_STEP, n // tq),
        in_specs=[qspec, kspec, kspec],
        out_specs=qspec,
        out_shape=jax.ShapeDtypeStruct((b, n, B_WIDTH), BF16),
        compiler_params=_params("arbitrary", "arbitrary", "arbitrary"),
        name="stick",
    )(qb, k, v)


def stick_attention_cached(qb, k_new, v_new, cache_k, cache_v, *, tc):
    b, n, _ = qb.shape
    past = cache_k.shape[1]
    assert past % tc == 0
    new = pl.BlockSpec((None, n, B_WIDTH), lambda bi: (bi, 0, 0))
    cache = pl.BlockSpec((None, past, B_HEADS, HEAD_DIM), lambda bi: (bi, 0, 0, 0))
    return pl.pallas_call(
        functools.partial(_stick_cached_kernel, past=past, tc=tc),
        grid=(b,),
        in_specs=[new, new, new, cache, cache],
        out_specs=new,
        out_shape=jax.ShapeDtypeStruct((b, n, B_WIDTH), BF16),
        compiler_params=_params("arbitrary"),
        name="stick_cached",
    )(qb, k_new, v_new, cache_k, cache_v)


def _merge_kernel(h_ref, ya_ref, yb_ref, wga_ref, wgb_ref, wa_ref, wb_ref, o_ref):
    hb = h_ref[...]
    ga = jax.nn.sigmoid(jnp.dot(hb, wga_ref[...], preferred_element_type=F32))
    gb = jax.nn.sigmoid(jnp.dot(hb, wgb_ref[...], preferred_element_type=F32))
    pa = jnp.dot(ya_ref[...], wa_ref[...], preferred_element_type=F32)
    pb = jnp.dot(yb_ref[...], wb_ref[...], preferred_element_type=F32)
    o_ref[...] = (ga * pa + gb * pb).astype(BF16)


def merge_branches(h, ya, yb, w_ga, w_gb, w_a, w_b, tm, tn):
    t, d = h.shape
    row = lambda w: pl.BlockSpec((tm, w), lambda i, j: (i, 0))
    col = lambda k: pl.BlockSpec((k, tn), lambda i, j: (0, j))
    return pl.pallas_call(
        _merge_kernel,
        grid=(t // tm, d // tn),
        in_specs=[row(d), row(A_WIDTH), row(B_WIDTH), col(d), col(d), col(A_WIDTH), col(B_WIDTH)],
        out_specs=pl.BlockSpec((tm, tn), lambda i, j: (i, j)),
        out_shape=jax.ShapeDtypeStruct((t, d), BF16),
        compiler_params=_params("arbitrary", "arbitrary"),
        name="merge",
    )(h, ya, yb, w_ga, w_gb, w_a, w_b)


def _pack_bf16_pair(lo, hi):
    lo_bits = lax.bitcast_convert_type(lo.astype(BF16).astype(F32), jnp.uint32)
    hi_bits = lax.bitcast_convert_type(hi.astype(BF16).astype(F32), jnp.uint32)
    return (lo_bits >> 16) | (hi_bits & jnp.uint32(0xFFFF0000))


def _unpack_f32_pair(u):
    return (lax.bitcast_convert_type(u << 16, F32),
            lax.bitcast_convert_type(u & jnp.uint32(0xFFFF0000), F32))


def _unpack_bf16_pair(u):
    lo, hi = _unpack_f32_pair(u)
    return lo.astype(BF16), hi.astype(BF16)


def _first_max(vals, idx, axes, n):
    m = vals
    for ax in axes:
        m = jnp.max(m, axis=ax, keepdims=True)
    first = jnp.where(vals == m, idx, n)
    for ax in axes:
        first = jnp.min(first, axis=ax, keepdims=True)
    return m, first


def _sum01(a):
    return jnp.sum(jnp.sum(a, axis=0, keepdims=True), axis=1, keepdims=True)


def _out_kernel(mg_ref, x_ref, g1_ref, sc_ref, sh_ref, nw_ref, wo_ref, wr_ref, br_ref, cnt0_ref,
                x1_ref, hu_ref, tope_ref, rank_ref, gatek_ref, cnt_out_ref, cnt_ref):
    tm, d = x_ref.shape
    proj = jnp.dot(mg_ref[...], wo_ref[...], preferred_element_type=F32)
    x1 = x_ref[...] + _gate_rows(proj, g1_ref[...])
    x1_ref[...] = x1
    h2 = _modulate(_rms(x1, nw_ref[...]), sc_ref[...], sh_ref[...])
    hu_ref[...] = _pack_bf16_pair(h2[:, :d // 2], h2[:, d // 2:])

    logits = lax.dot_general(wr_ref[...], h2.astype(BF16), NT_DIMS, preferred_element_type=F32)
    aff = jax.nn.sigmoid(logits)
    shape3 = (N_GROUPS, GROUP_SIZE, tm)
    aff3 = aff.reshape(shape3)
    biased = (aff + br_ref[...]).reshape(shape3)
    io_in = lax.broadcasted_iota(jnp.int32, shape3, 1)
    m1, i1 = _first_max(biased, io_in, (1,), GROUP_SIZE)
    m2 = jnp.max(jnp.where(io_in == i1, -jnp.inf, biased), axis=1, keepdims=True)
    gscore = m1 + m2
    gio = lax.broadcasted_iota(jnp.int32, gscore.shape, 0)
    keep = jnp.zeros(gscore.shape, jnp.bool_)
    for _ in range(N_ACTIVE_GROUPS):
        _, first = _first_max(gscore, gio, (0,), N_GROUPS)
        hit = gio == first
        keep = jnp.logical_or(keep, hit)
        gscore = jnp.where(hit, -jnp.inf, gscore)
    masked = jnp.where(jnp.broadcast_to(keep, shape3), biased, -jnp.inf)
    eio = lax.broadcasted_iota(jnp.int32, shape3, 0) * GROUP_SIZE + io_in
    sel = jnp.zeros(shape3, jnp.bool_)
    hits = []
    for _ in range(N_ACTIVE):
        _, first = _first_max(masked, eio, (0, 1), N_EXPERTS)
        hit = eio == first
        sel = jnp.logical_or(sel, hit)
        masked = jnp.where(hit, -jnp.inf, masked)
        hits.append((hit, first.reshape(1, tm)))
    sel_aff = jnp.where(sel, aff3, 0.0)
    gates = sel_aff / _sum01(sel_aff) * ROUTED_SCALE

    @pl.when(pl.program_id(0) == 0)
    def _():
        cnt_ref[...] = cnt0_ref[...]

    sel2 = jnp.where(sel, 1.0, 0.0).reshape(N_EXPERTS, tm)
    incl = (lax.broadcasted_iota(jnp.int32, (tm, tm), 0)
            <= lax.broadcasted_iota(jnp.int32, (tm, tm), 1)).astype(BF16)
    rank = jnp.dot(sel2.astype(BF16), incl, preferred_element_type=F32) + cnt_ref[...]
    cnt_ref[...] = cnt_ref[...] + jnp.sum(sel2, axis=1, keepdims=True)
    cnt_out_ref[...] = cnt_ref[...]
    rank3 = rank.reshape(shape3)
    tope_ref[...] = jnp.concatenate([first for _, first in hits], axis=0)
    rank_ref[...] = jnp.concatenate(
        [_sum01(jnp.where(hit, rank3, 0.0)).reshape(1, tm) for hit, _ in hits], axis=0).astype(jnp.int32)
    gatek_ref[...] = jnp.concatenate(
        [_sum01(jnp.where(hit, gates, 0.0)).reshape(1, tm) for hit, _ in hits], axis=0)


def out_proj(merged, x, gate1_g, scale_g, shift_g, norm_w, w_out, w_router_t, b_router, count0, tm):
    t, d = x.shape
    gm = tm // MOD_GROUP
    row = lambda w: pl.BlockSpec((tm, w), lambda i: (i, 0))
    grp = pl.BlockSpec((gm, d), lambda i: (i, 0))
    const = lambda a, b: pl.BlockSpec((a, b), lambda i: (0, 0))
    per_k = pl.BlockSpec((N_ACTIVE, tm), lambda i: (0, i))
    return pl.pallas_call(
        _out_kernel,
        grid=(t // tm,),
        in_specs=[row(d), row(d), grp, grp, grp, const(1, d), const(d, d),
                  const(N_EXPERTS, d), const(N_EXPERTS, 1), const(N_EXPERTS, 1)],
        out_specs=[row(d), row(d // 2), per_k, per_k, per_k, const(N_EXPERTS, 1)],
        out_shape=[jax.ShapeDtypeStruct((t, d), F32), jax.ShapeDtypeStruct((t, d // 2), jnp.uint32),
                   jax.ShapeDtypeStruct((N_ACTIVE, t), jnp.int32), jax.ShapeDtypeStruct((N_ACTIVE, t), jnp.int32),
                   jax.ShapeDtypeStruct((N_ACTIVE, t), F32), jax.ShapeDtypeStruct((N_EXPERTS, 1), F32)],
        scratch_shapes=[pltpu.VMEM((N_EXPERTS, 1), F32)],
        compiler_params=_params("arbitrary"),
        name="out_proj",
    )(merged, x, gate1_g, scale_g, shift_g, norm_w, w_out, w_router_t, b_router, count0)


def _pad_fill_kernel(start_ref, cnt_ref, nu_ref, xs_ref, zero_ref, sem, *, tme, n_tiles):
    zero_ref[...] = jnp.zeros(zero_ref.shape, zero_ref.dtype)

    def pad_copy(e, j):
        return pltpu.make_async_copy(zero_ref.at[pl.ds(0, 1)], xs_ref.at[pl.ds(start_ref[e] + j, 1)], sem.at[0])

    def tile_copy(i):
        return pltpu.make_async_copy(zero_ref, xs_ref.at[pl.ds(pl.multiple_of(i * tme, tme), tme)], sem.at[1])

    def for_each_copy(wait):
        def pad_rows(e, carry):
            def one(j, c):
                pad_copy(e, j).wait() if wait else pad_copy(e, j).start()
                return c
            return lax.fori_loop(0, cnt_ref[e], one, carry)

        def tail_tiles(i, carry):
            tile_copy(i).wait() if wait else tile_copy(i).start()
            return carry

        lax.fori_loop(0, N_EXPERTS, pad_rows, 0)
        lax.fori_loop(nu_ref[0], n_tiles, tail_tiles, 0)

    for_each_copy(wait=False)
    for_each_copy(wait=True)


def pad_fill(pad_start, pad_count, n_used, p_rows, row_shape, tme):
    grid_spec = pltpu.PrefetchScalarGridSpec(
        num_scalar_prefetch=3, grid=(1,), in_specs=[],
        out_specs=pl.BlockSpec(memory_space=pl.ANY),
        scratch_shapes=[pltpu.VMEM((tme,) + row_shape, jnp.uint32), pltpu.SemaphoreType.DMA((2,))])
    return pl.pallas_call(
        functools.partial(_pad_fill_kernel, tme=tme, n_tiles=p_rows // tme),
        grid_spec=grid_spec,
        out_shape=jax.ShapeDtypeStruct((p_rows,) + row_shape, jnp.uint32),
        compiler_params=_params("arbitrary"),
        name="pad_fill",
    )(pad_start, pad_count, n_used)


def _dispatch_kernel(slot_ref, hu_ref, xs_in_ref, xs_ref, sem, *, td):
    del xs_in_ref

    def row_copy(j, k):
        return pltpu.make_async_copy(hu_ref.at[pl.ds(j, 1)],
                                     xs_ref.at[pl.ds(slot_ref[j * N_ACTIVE + k], 1)], sem)

    def start(j, carry):
        for k in range(N_ACTIVE):
            row_copy(j, k).start(priority=k % 2)
        return carry

    def wait(j, carry):
        for k in range(N_ACTIVE):
            row_copy(j, k).wait()
        return carry

    lax.fori_loop(0, td, start, 0)
    lax.fori_loop(0, td, wait, 0)


def dispatch(slots_flat, hu, xs, td):
    t = hu.shape[0]
    return pl.pallas_call(
        functools.partial(_dispatch_kernel, td=td),
        grid=(t // td,),
        in_specs=[pl.BlockSpec((td * N_ACTIVE,), lambda i: (i,), memory_space=pltpu.SMEM),
                  pl.BlockSpec((td,) + hu.shape[1:], lambda i: (i, 0)),
                  pl.BlockSpec(memory_space=pl.ANY)],
        out_specs=pl.BlockSpec(memory_space=pl.ANY),
        out_shape=jax.ShapeDtypeStruct(xs.shape, xs.dtype),
        scratch_shapes=[pltpu.SemaphoreType.DMA(())],
        input_output_aliases={2: 0},
        compiler_params=_params("arbitrary"),
        name="dispatch",
    )(slots_flat, hu, xs)


def _expert_kernel(te_ref, nu_ref, first_ref, next_ref, par_ref, xs_ref, wg_ref, wu_ref, wd_ref, o_ref,
                   wgf_ref, wuf_ref, wdf_ref, wgb_ref, wub_ref, wdb_ref, sem):
    i = pl.program_id(0)
    used = i < nu_ref[0]

    def fetch(e, buf):
        return [pltpu.make_async_copy(w.at[e], f.at[buf], sem.at[buf, j])
                for j, (w, f) in enumerate(((wg_ref, wgf_ref), (wu_ref, wuf_ref), (wd_ref, wdf_ref)))]

    @pl.when(i == 0)
    def _():
        for copy in fetch(te_ref[0], par_ref[0]):
            copy.start()

    @pl.when(jnp.logical_and(used, first_ref[i] == 1))
    def _():
        buf = par_ref[i]
        for copy in fetch(te_ref[i], buf):
            copy.wait()

        @pl.when(next_ref[i] >= 0)
        def _():
            for copy in fetch(next_ref[i], 1 - buf):
                copy.start()

        wgb_ref[...] = wgf_ref[buf].astype(BF16)
        wub_ref[...] = wuf_ref[buf].astype(BF16)
        wdb_ref[...] = wdf_ref[buf].astype(BF16)

    @pl.when(used)
    def _():
        lo, hi = _unpack_bf16_pair(xs_ref[...])
        half = lo.shape[1]
        gate = (jnp.dot(lo, wgb_ref[:half], preferred_element_type=F32)
                + jnp.dot(hi, wgb_ref[half:], preferred_element_type=F32))
        up = (jnp.dot(lo, wub_ref[:half], preferred_element_type=F32)
              + jnp.dot(hi, wub_ref[half:], preferred_element_type=F32))
        hid = (_silu(gate) * up).astype(BF16)
        o_ref[...] = jnp.dot(hid, wdb_ref[...], preferred_element_type=F32)

    @pl.when(jnp.logical_not(used))
    def _():
        o_ref[...] = jnp.zeros(o_ref.shape, o_ref.dtype)


def expert_ffn(tiles, xs, w_gate, w_up, w_down, tme):
    p, half = xs.shape
    e, d, f = w_gate.shape
    hbm = pl.BlockSpec(memory_space=pl.ANY)
    grid_spec = pltpu.PrefetchScalarGridSpec(
        num_scalar_prefetch=5,
        grid=(p // tme,),
        in_specs=[pl.BlockSpec((tme, half), lambda i, te, nu, *_: (jnp.minimum(i, nu[0] - 1), 0)),
                  hbm, hbm, hbm],
        out_specs=pl.BlockSpec((tme, d), lambda i, *_: (i, 0)),
        scratch_shapes=[pltpu.VMEM((2, d, f), F32), pltpu.VMEM((2, d, f), F32), pltpu.VMEM((2, f, d), F32),
                        pltpu.VMEM((d, f), BF16), pltpu.VMEM((d, f), BF16), pltpu.VMEM((f, d), BF16),
                        pltpu.SemaphoreType.DMA((2, 3))],
    )
    return pl.pallas_call(
        _expert_kernel,
        grid_spec=grid_spec,
        out_shape=jax.ShapeDtypeStruct((p, d), F32),
        compiler_params=_params("arbitrary"),
        name="expert_ffn",
    )(tiles["expert"], tiles["n_used"], tiles["first"], tiles["next"], tiles["parity"], xs, w_gate, w_up, w_down)


def _combine_kernel(slot_ref, hu_ref, x1_ref, g2_ref, gk_ref, wg_ref, wu_ref, wd_ref, ys_ref,
                    o_ref, rows_ref, sem, *, tmc):
    def row_copy(j, k):
        return pltpu.make_async_copy(ys_ref.at[pl.ds(slot_ref[j * N_ACTIVE + k], 1)],
                                     rows_ref.at[k, pl.ds(j, 1)], sem)

    def start(j, carry):
        for k in range(N_ACTIVE):
            row_copy(j, k).start(priority=k % 2)
        return carry

    def wait(j, carry):
        for k in range(N_ACTIVE):
            row_copy(j, k).wait()
        return carry

    lax.fori_loop(0, tmc, start, 0)
    lo, hi = _unpack_bf16_pair(hu_ref[...])
    half = lo.shape[1]
    gate = (jnp.dot(lo, wg_ref[:half], preferred_element_type=F32)
            + jnp.dot(hi, wg_ref[half:], preferred_element_type=F32))
    up = (jnp.dot(lo, wu_ref[:half], preferred_element_type=F32)
          + jnp.dot(hi, wu_ref[half:], preferred_element_type=F32))
    y = jnp.dot((_silu(gate) * up).astype(BF16), wd_ref[...], preferred_element_type=F32)
    lax.fori_loop(0, tmc, wait, 0)
    gk = gk_ref[...]
    for k in range(N_ACTIVE):
        y = y + gk[:, k:k + 1] * rows_ref[k]
    o_ref[...] = x1_ref[...] + _gate_rows(y, g2_ref[...])


def combine(slots_flat, hu, x1, gate2_g, gate_k, w_gs, w_us, w_ds, ys, tmc):
    t, d = x1.shape
    f = w_gs.shape[1]
    gm = tmc // MOD_GROUP
    row = lambda w: pl.BlockSpec((tmc, w), lambda i: (i, 0))
    const = lambda a, b: pl.BlockSpec((a, b), lambda i: (0, 0))
    return pl.pallas_call(
        functools.partial(_combine_kernel, tmc=tmc),
        grid=(t // tmc,),
        in_specs=[pl.BlockSpec((tmc * N_ACTIVE,), lambda i: (i,), memory_space=pltpu.SMEM),
                  pl.BlockSpec((tmc,) + hu.shape[1:], lambda i: (i, 0)),
                  row(d), pl.BlockSpec((gm, d), lambda i: (i, 0)), row(N_ACTIVE),
                  const(d, f), const(d, f), const(f, d),
                  pl.BlockSpec(memory_space=pl.ANY)],
        out_specs=row(d),
        out_shape=jax.ShapeDtypeStruct((t, d), F32),
        scratch_shapes=[pltpu.VMEM((N_ACTIVE, tmc, d), F32), pltpu.SemaphoreType.DMA(())],
        compiler_params=_params("arbitrary"),
        name="combine",
    )(slots_flat, hu, x1, gate2_g, gate_k, w_gs, w_us, w_ds, ys)


def _group_rows(v, n):
    b, d = v.shape
    return jnp.broadcast_to(v[:, None, :], (b, n // MOD_GROUP, d)).reshape(b * n // MOD_GROUP, d)


def _pad_keys(a, l_pad):
    return jnp.pad(a, ((0, 0), (0, l_pad - a.shape[1]), (0, 0)))


def _token_stage(x, mod, w, caches, count0):
    b, n, d = x.shape
    t = b * n
    shift1, scale1, gate1, shift2, scale2, gate2 = [_group_rows(m, n) for m in jnp.split(mod, 6, axis=-1)]
    xf = x.reshape(t, d)
    tm = _row_tile(t, 512)
    (h, qa, ka, va, kab, vab, qi, ki, kib, wi) = in_proj_a(
        xf, w["norm_mix"], scale1, shift1, w["w_in_a"], w["q_norm_a"], w["k_norm_a"], tm)
    qb, kb, vb, kbb, vbb = in_proj_b(h, w["w_in_b"], tm)

    r3 = lambda a: a.reshape(b, n, a.shape[-1])
    if caches is None:
        ya = dsa_attention(r3(qa), r3(qi), r3(wi), r3(kab), r3(vab), r3(kib),
                           q_pos0=0, n_keys=n, tq=min(256, n), tc=min(512, n))
        yb = stick_attention(r3(qb), r3(kbb), r3(vbb), q_pos0=0, tq=min(256, n), tc=min(256, n))
    else:
        cka, cva, cki, ckb, cvb = caches
        past = cka.shape[1]
        tc_a, tc_b = min(512, past), min(256, past)
        new_chunk = lambda a: _pad_keys(r3(a), tc_a)
        ya = dsa_attention(r3(qa), r3(qi), r3(wi), new_chunk(kab), new_chunk(vab), new_chunk(kib),
                           q_pos0=past, n_keys=past + n, tq=n, tc=tc_a, caches=(cka, cva, cki))
        yb = stick_attention_cached(r3(qb), r3(kbb), r3(vbb), ckb, cvb, tc=tc_b)

    merged = merge_branches(h, ya.reshape(t, A_WIDTH), yb.reshape(t, B_WIDTH),
                            w["w_ga"], w["w_gb"], w["w_branch_a"], w["w_branch_b"], tm, _row_tile(d, 512))
    x1, hu, top_e, rank_k, gate_k, counts = out_proj(
        merged, xf, gate1, scale2, shift2, w["norm_ffn"], w["w_out"], w["w_router_t"], w["b_router"], count0, tm)
    rows = (ka.reshape(b, n, A_KV_HEADS, HEAD_DIM), va.reshape(b, n, A_KV_HEADS, HEAD_DIM),
            ki.reshape(b, n, IDX_DIM), kb.reshape(b, n, B_HEADS, HEAD_DIM), vb.reshape(b, n, B_HEADS, HEAD_DIM))
    return dict(x1=x1, hu=hu, top_e=top_e, rank_k=rank_k, gate_k=gate_k, counts=counts, gate2=gate2, rows=rows)


def _routing(counts, top_e, rank_k, tme):
    e = counts.shape[0]
    t = top_e.shape[1]
    counts = counts.reshape(e).astype(jnp.int32)
    padded = (counts + tme - 1) // tme * tme
    ends = jnp.cumsum(padded)
    offsets = ends - padded
    eids = jnp.arange(e, dtype=jnp.int32)[:, None, None]
    offset_k = jnp.sum(jnp.where(top_e[None] == eids, offsets[:, None, None], 0), axis=0)
    slots = offset_k + rank_k - 1
    n_tiles = (t * N_ACTIVE + e * tme) // tme
    tile_start = jnp.arange(n_tiles, dtype=jnp.int32) * tme
    tile_expert = jnp.minimum(jnp.sum(ends[None, :] <= tile_start[:, None], axis=1), e - 1).astype(jnp.int32)
    n_used = (ends[-1] // tme).astype(jnp.int32).reshape(1)
    ids = jnp.arange(e, dtype=jnp.int32)
    nonempty = padded > 0
    later = jnp.where((ids[None, :] > ids[:, None]) & nonempty[None, :], ids[None, :], e)
    next_e = jnp.min(later, axis=1)
    next_e = jnp.where(next_e == e, -1, next_e).astype(jnp.int32)
    rank_e = jnp.cumsum(nonempty.astype(jnp.int32)) - 1
    tiles = dict(expert=tile_expert, n_used=n_used,
                 first=(tile_start == offsets[tile_expert]).astype(jnp.int32),
                 next=next_e[tile_expert], parity=(rank_e[tile_expert] % 2).astype(jnp.int32))
    return dict(slots=slots.T.reshape(-1), tiles=tiles, n_used=n_used, p_rows=n_tiles * tme,
                pad_start=offsets + counts, pad_count=padded - counts)


def kernel(x_prompt, x_sample, cache_a_k, cache_a_v, cache_a_idx_k, cache_b_k, cache_b_v, c_prompt, c_sample,
           norm_mix, norm_ffn, w_ada, b_ada, w_in, q_norm_a, k_norm_a, w_branch_a, w_branch_b, w_out,
           w_router, b_router, w_gate_e, w_up_e, w_down_e, w_gate_s, w_up_s, w_down_s):
    depth = norm_mix.shape[0]
    d = x_prompt.shape[-1]
    bp, bs = c_prompt.shape[0], c_sample.shape[0]
    c_all = jnp.concatenate([c_prompt, c_sample], axis=0)
    c_all = jnp.pad(c_all, ((0, -(bp + bs) % 16), (0, 0)))
    tme = 256

    xp, xs = x_prompt, x_sample
    prompt_rows, sample_rows = [], []
    for layer in range(depth):
        wl = w_in[layer]
        o_idx = A_WIDTH + 2 * A_KV_WIDTH + IDX_Q_WIDTH
        o_b = o_idx + IDX_DIM + IDX_HEADS
        o_g = o_b + 3 * B_WIDTH
        w_in_a = jnp.concatenate(
            [wl[:, :o_b], jnp.zeros((d, LANES - IDX_DIM - IDX_HEADS), wl.dtype)], axis=1).astype(BF16)
        w = dict(
            norm_mix=norm_mix[layer].reshape(1, d), norm_ffn=norm_ffn[layer].reshape(1, d),
            q_norm_a=q_norm_a[layer].reshape(1, HEAD_DIM), k_norm_a=k_norm_a[layer].reshape(1, HEAD_DIM),
            w_in_a=w_in_a, w_in_b=wl[:, o_b:o_g].astype(BF16),
            w_ga=wl[:, o_g:o_g + d].astype(BF16), w_gb=wl[:, o_g + d:].astype(BF16),
            w_branch_a=w_branch_a[layer].astype(BF16), w_branch_b=w_branch_b[layer].astype(BF16),
            w_out=w_out[layer].astype(BF16), w_router_t=w_router[layer].T.astype(BF16),
            b_router=b_router[layer].reshape(N_EXPERTS, 1))
        mod = ada_mod(c_all, w_ada[layer], b_ada[layer])
        caches = (cache_a_k[layer], cache_a_v[layer], cache_a_idx_k[layer], cache_b_k[layer], cache_b_v[layer])
        sp = _token_stage(xp, mod[:bp], w, None, jnp.zeros((N_EXPERTS, 1), F32))
        ss = _token_stage(xs, mod[bp:bp + bs], w, caches, sp["counts"])
        prompt_rows.append(sp["rows"])
        sample_rows.append(ss["rows"])

        tp, ts = sp["x1"].shape[0], ss["x1"].shape[0]
        both = lambda name: jnp.concatenate([sp[name], ss[name]], axis=1)
        rt = _routing(ss["counts"], both("top_e"), both("rank_k"), tme)
        slots, gate_k = rt["slots"], both("gate_k").T
        xsorted = pad_fill(rt["pad_start"], rt["pad_count"], rt["n_used"], rt["p_rows"],
                           sp["hu"].shape[1:], tme)
        xsorted = dispatch(slots[:tp * N_ACTIVE], sp["hu"], xsorted, _row_tile(tp, 256))
        xsorted = dispatch(slots[tp * N_ACTIVE:], ss["hu"], xsorted, _row_tile(ts, 256))
        ysorted = expert_ffn(rt["tiles"], xsorted, w_gate_e[layer], w_up_e[layer], w_down_e[layer], tme)
        shared = (w_gate_s[layer].astype(BF16), w_up_s[layer].astype(BF16), w_down_s[layer].astype(BF16))
        tmc = 256
        yp = combine(slots[:tp * N_ACTIVE], sp["hu"], sp["x1"], sp["gate2"], gate_k[:tp], *shared, ysorted,
                     _row_tile(tp, tmc))
        ys = combine(slots[tp * N_ACTIVE:], ss["hu"], ss["x1"], ss["gate2"], gate_k[tp:], *shared, ysorted,
                     _row_tile(ts, tmc))
        xp = yp.reshape(xp.shape)
        xs = ys.reshape(xs.shape)

    stack = lambda rows: tuple(jnp.stack(r) for r in zip(*rows))
    return (xp, xs) + stack(prompt_rows) + stack(sample_rows)
```

```python
import functools

import jax
import jax.numpy as jnp
from jax import lax
from jax.experimental import pallas as pl
from jax.experimental.pallas import tpu as pltpu

CHUNK = 64
HEAD_DIM = 128
A_HEADS = 8
A_KV_HEADS = 2
A_GROUP = A_HEADS // A_KV_HEADS
IDX_HEADS = 8
IDX_DIM = 64
TOPK_KEYS = 256
B_HEADS = 8
N_EXPERTS = 64
N_ACTIVE = 8
N_GROUPS = 8
N_ACTIVE_GROUPS = 4
GROUP_SIZE = N_EXPERTS // N_GROUPS
ROUTED_SCALE = 2.5
EPS = 1e-6

A_WIDTH = A_HEADS * HEAD_DIM
A_KV_WIDTH = A_KV_HEADS * HEAD_DIM
IDX_Q_WIDTH = IDX_HEADS * IDX_DIM
B_WIDTH = B_HEADS * HEAD_DIM

LANES = 128
SUBLANES = 8
MOD_GROUP = 32
VMEM_LIMIT = 56 * 1024 * 1024
NEG_BIG = -1e30
STICK_UNDERFLOW = -110.0
INT_MIN = -(2 ** 31)

F32 = jnp.float32
BF16 = jnp.bfloat16
NT_DIMS = (((1,), (1,)), ((), ()))


def _params(*sem):
    return pltpu.CompilerParams(dimension_semantics=sem, vmem_limit_bytes=VMEM_LIMIT)


def _silu(x):
    return x * jax.nn.sigmoid(x)


def _rms(x, gain):
    return x * lax.rsqrt(jnp.mean(x * x, axis=-1, keepdims=True) + EPS) * gain


def _modulate(y, scale_g, shift_g):
    tm, d = y.shape
    y3 = y.reshape(tm // MOD_GROUP, MOD_GROUP, d)
    out = y3 * (1.0 + scale_g[:, None, :]) + shift_g[:, None, :]
    return out.reshape(tm, d)


def _gate_rows(y, gate_g):
    tm, d = y.shape
    return (y.reshape(tm // MOD_GROUP, MOD_GROUP, d) * gate_g[:, None, :]).reshape(tm, d)


def _row_tile(t, pref):
    tm = min(pref, t)
    while t % tm:
        tm //= 2
    return tm


def _ada_kernel(c_ref, w_ref, b_ref, o_ref):
    s = _silu(c_ref[...]).astype(BF16)
    o_ref[...] = jnp.dot(s, w_ref[...].astype(BF16), preferred_element_type=F32) + b_ref[...]


def ada_mod(c, w_ada, b_ada):
    r, d = c.shape
    n = w_ada.shape[1]
    tn = _row_tile(n, 1024)
    return pl.pallas_call(
        _ada_kernel,
        grid=(n // tn,),
        in_specs=[pl.BlockSpec((r, d), lambda j: (0, 0)),
                  pl.BlockSpec((d, tn), lambda j: (0, j)),
                  pl.BlockSpec((1, tn), lambda j: (0, j))],
        out_specs=pl.BlockSpec((r, tn), lambda j: (0, j)),
        out_shape=jax.ShapeDtypeStruct((r, n), F32),
        compiler_params=_params("arbitrary"),
        name="ada_mod",
    )(c, w_ada, b_ada.reshape(1, n))


def _in_a_kernel(x_ref, nw_ref, sc_ref, sh_ref, w_ref, qn_ref, kn_ref,
                 h_ref, qa_ref, ka_ref, va_ref, kab_ref, vab_ref, qi_ref, ki_ref, kib_ref, wi_ref):
    h = _modulate(_rms(x_ref[...], nw_ref[...]), sc_ref[...], sh_ref[...])
    hb = h.astype(BF16)
    h_ref[...] = hb
    off = 0
    for hd in range(A_HEADS):
        q = jnp.dot(hb, w_ref[:, off:off + HEAD_DIM], preferred_element_type=F32)
        qa_ref[:, hd * HEAD_DIM:(hd + 1) * HEAD_DIM] = (_rms(q, qn_ref[...]) * (HEAD_DIM ** -0.5)).astype(BF16)
        off += HEAD_DIM
    for hd in range(A_KV_HEADS):
        k = jnp.dot(hb, w_ref[:, off:off + HEAD_DIM], preferred_element_type=F32)
        k = _rms(k, kn_ref[...])
        ka_ref[:, hd * HEAD_DIM:(hd + 1) * HEAD_DIM] = k
        kab_ref[:, hd * HEAD_DIM:(hd + 1) * HEAD_DIM] = k.astype(BF16)
        off += HEAD_DIM
    v = jnp.dot(hb, w_ref[:, off:off + A_KV_WIDTH], preferred_element_type=F32)
    va_ref[...] = v
    vab_ref[...] = v.astype(BF16)
    off += A_KV_WIDTH
    qi = jnp.dot(hb, w_ref[:, off:off + IDX_Q_WIDTH], preferred_element_type=F32)
    qi_ref[...] = (qi * (IDX_DIM ** -0.5)).astype(BF16)
    off += IDX_Q_WIDTH
    kw = jnp.dot(hb, w_ref[:, off:off + LANES], preferred_element_type=F32)
    ki = kw[:, :IDX_DIM]
    ki_ref[...] = ki
    kib_ref[...] = ki.astype(BF16)
    wi_ref[...] = kw[:, IDX_DIM:IDX_DIM + IDX_HEADS]


def in_proj_a(x, norm_w, scale_g, shift_g, w_a, q_norm, k_norm, tm):
    t, d = x.shape
    wa = w_a.shape[1]
    gm = tm // MOD_GROUP
    row = lambda w: pl.BlockSpec((tm, w), lambda i: (i, 0))
    const = lambda a, b: pl.BlockSpec((a, b), lambda i: (0, 0))
    outs = [(d, BF16), (A_WIDTH, BF16), (A_KV_WIDTH, F32), (A_KV_WIDTH, F32), (A_KV_WIDTH, BF16),
            (A_KV_WIDTH, BF16), (IDX_Q_WIDTH, BF16), (IDX_DIM, F32), (IDX_DIM, BF16), (IDX_HEADS, F32)]
    return pl.pallas_call(
        _in_a_kernel,
        grid=(t // tm,),
        in_specs=[row(d), const(1, d), pl.BlockSpec((gm, d), lambda i: (i, 0)),
                  pl.BlockSpec((gm, d), lambda i: (i, 0)), const(d, wa),
                  const(1, HEAD_DIM), const(1, HEAD_DIM)],
        out_specs=[row(w) for w, _ in outs],
        out_shape=[jax.ShapeDtypeStruct((t, w), dt) for w, dt in outs],
        compiler_params=_params("arbitrary"),
        name="in_proj_a",
    )(x, norm_w, scale_g, shift_g, w_a, q_norm, k_norm)


def _in_b_kernel(h_ref, w_ref, qb_ref, kb_ref, vb_ref, kbb_ref, vbb_ref):
    hb = h_ref[...]
    qb_ref[...] = jnp.dot(hb, w_ref[:, :B_WIDTH], preferred_element_type=F32).astype(BF16)
    k = jnp.dot(hb, w_ref[:, B_WIDTH:2 * B_WIDTH], preferred_element_type=F32)
    kbb_ref[...] = k.astype(BF16)
    v = jnp.dot(hb, w_ref[:, 2 * B_WIDTH:], preferred_element_type=F32)
    vbb_ref[...] = v.astype(BF16)
    for hd in range(B_HEADS):
        kb_ref[:, hd, :] = k[:, hd * HEAD_DIM:(hd + 1) * HEAD_DIM]
        vb_ref[:, hd, :] = v[:, hd * HEAD_DIM:(hd + 1) * HEAD_DIM]


def in_proj_b(h, w_b, tm):
    t, d = h.shape
    row = lambda w: pl.BlockSpec((tm, w), lambda i: (i, 0))
    heads = pl.BlockSpec((tm, B_HEADS, HEAD_DIM), lambda i: (i, 0, 0))
    flat = lambda dt: jax.ShapeDtypeStruct((t, B_WIDTH), dt)
    per_head = jax.ShapeDtypeStruct((t, B_HEADS, HEAD_DIM), F32)
    return pl.pallas_call(
        _in_b_kernel,
        grid=(t // tm,),
        in_specs=[row(d), pl.BlockSpec((d, 3 * B_WIDTH), lambda i: (0, 0))],
        out_specs=[row(B_WIDTH), heads, heads, row(B_WIDTH), row(B_WIDTH)],
        out_shape=[flat(BF16), per_head, per_head, flat(BF16), flat(BF16)],
        compiler_params=_params("arbitrary"),
        name="in_proj_b",
    )(h, w_b)


def _score_key(x):
    bits = lax.bitcast_convert_type(x, jnp.int32)
    return bits ^ ((bits >> 31) & 0x7FFFFFFF)


def _lane_tiles(a):
    return [a[:, j * LANES:(j + 1) * LANES] for j in range(a.shape[1] // LANES)]


def _dsa_kernel(qa_ref, qi_ref, wi_ref, k_ref, v_ref, ki_ref, *rest, q_pos0, n_keys, n_sel, tq, tc, pos_bits,
                cached):
    if cached:
        ck_ref, cv_ref, cki_ref, o_ref, key_ref, m_ref, l_ref, acc_ref = rest
    else:
        o_ref, key_ref, m_ref, l_ref, acc_ref = rest

    def read_k(start, g, last):
        if cached and not last:
            return ck_ref[pl.ds(start, tc), g, :].astype(BF16)
        rows = slice(None) if cached else pl.ds(start, tc)
        return k_ref[rows, g * HEAD_DIM:(g + 1) * HEAD_DIM]

    def read_v(start, g, last):
        if cached and not last:
            return cv_ref[pl.ds(start, tc), g, :].astype(BF16)
        rows = slice(None) if cached else pl.ds(start, tc)
        return v_ref[rows, g * HEAD_DIM:(g + 1) * HEAD_DIM]

    def read_ki(start, last):
        if cached and not last:
            return cki_ref[pl.ds(start, tc), :].astype(BF16)
        return ki_ref[...] if cached else ki_ref[pl.ds(start, tc), :]

    i = pl.program_id(1)
    q0 = q_pos0 + i * tq
    kmax = jnp.minimum(((q0 + tq - 1) // CHUNK + 1) * CHUNK, n_keys)
    nch = (kmax + tc - 1) // tc
    qpos = q0 + lax.broadcasted_iota(jnp.int32, (tq, 1), 0)
    qchunk = qpos // CHUNK

    head_w = wi_ref[...] * (IDX_HEADS ** -0.5)
    qi_all = jnp.concatenate([qi_ref[:, h * IDX_DIM:(h + 1) * IDX_DIM] for h in range(IDX_HEADS)], axis=0)

    def all_chunks(chunk_pass):
        def body(c, carry):
            chunk_pass(c, False)
            return carry
        lax.fori_loop(0, nch - 1, body, 0)
        chunk_pass(nch - 1, True)

    def score_pass(c, last):
        start = pl.multiple_of(c * tc, tc)
        lg = lax.dot_general(qi_all, read_ki(start, last), NT_DIMS, preferred_element_type=F32)
        s = jnp.zeros((tq, tc), F32)
        for h in range(IDX_HEADS):
            s = s + jnp.maximum(lg[h * tq:(h + 1) * tq], 0.0) * head_w[:, h:h + 1]
        kpos = start + lax.broadcasted_iota(jnp.int32, (1, tc), 1)
        adm = jnp.logical_and(kpos // CHUNK <= qchunk, kpos < n_keys)
        key_ref[c] = jnp.where(adm, _score_key(s), INT_MIN)

    all_chunks(score_pass)

    def count_rows(pred):
        def body(c, acc):
            start = pl.multiple_of(c * tc, tc)
            hit = jnp.where(pred(key_ref[c], start), 1.0, 0.0)
            return acc + functools.reduce(jnp.add, _lane_tiles(hit))
        acc = lax.fori_loop(0, nch, body, jnp.zeros((tq, LANES), F32))
        return jnp.sum(acc, axis=1, keepdims=True)

    def count_ge(cand):
        return count_rows(lambda key, start: key >= cand)

    def pending(done):
        return jnp.min(done) < 0.5

    cnt = count_ge(jnp.zeros((tq, 1), jnp.int32))
    thr = jnp.where(cnt >= n_sel, 0, INT_MIN).astype(jnp.int32)
    n_adm = jnp.minimum((qchunk + 1) * CHUNK, n_keys)
    done = jnp.where(jnp.logical_or(n_adm <= n_sel, cnt == n_sel), 1.0, 0.0)

    def bit_cond(state):
        b, _, done = state
        return jnp.logical_and(b < 31, pending(done))

    def bit_body(state):
        b, thr, done = state
        cand = thr | jnp.left_shift(jnp.int32(1), 30 - b)
        cnt = count_ge(cand)
        return (b + 1, jnp.where(cnt >= n_sel, cand, thr), jnp.where(cnt == n_sel, 1.0, done))

    _, thr, done = lax.while_loop(bit_cond, bit_body, (jnp.int32(0), thr, done))

    def tie_limit():
        need = n_sel - count_rows(lambda key, start: key > thr)

        def tied_below(limit):
            return count_rows(lambda key, start: jnp.logical_and(
                key == thr, start + lax.broadcasted_iota(jnp.int32, (1, tc), 1) < limit))

        def body(b, lim):
            cand = lim | jnp.left_shift(jnp.int32(1), pos_bits - 1 - b)
            return jnp.where(tied_below(cand) < need, cand, lim)

        lim = lax.fori_loop(0, pos_bits, body, jnp.zeros((tq, 1), jnp.int32))
        return jnp.where(done > 0.5, jnp.int32(2 ** pos_bits), lim + 1)

    plim = lax.cond(pending(done), tie_limit, lambda: jnp.full((tq, 1), 2 ** pos_bits, jnp.int32))

    slopes = [2.0 ** (-8.0 * (h + 1) / A_HEADS) for h in range(A_HEADS)]
    lane = lax.broadcasted_iota(jnp.int32, (1, LANES), 1)
    rows4 = A_GROUP * tq
    q_aug = []
    for g in range(A_KV_HEADS):
        heads = range(g * A_GROUP, (g + 1) * A_GROUP)
        q4 = jnp.concatenate([qa_ref[:, h * HEAD_DIM:(h + 1) * HEAD_DIM] for h in heads], axis=0)
        slope_col = jnp.concatenate([jnp.full((tq, 1), slopes[h], F32) for h in heads], axis=0)
        q_aug.append(jnp.concatenate([q4, jnp.where(lane < 3, slope_col, 0.0).astype(BF16)], axis=1))

    def chunk_logits(c, last):
        start = pl.multiple_of(c * tc, tc)
        kp = start + lax.broadcasted_iota(jnp.int32, (tc, 1), 0)
        pos_cols = jnp.where(lane == 0, kp & 63,
                             jnp.where(lane == 1, kp & (127 << 6),
                                       jnp.where(lane == 2, kp & ~8191, 0))).astype(F32).astype(BF16)
        key = key_ref[c]
        kpos = start + lax.broadcasted_iota(jnp.int32, (1, tc), 1)
        sel = jnp.logical_or(key > thr, jnp.logical_and(key == thr, kpos < plim))
        mb = jnp.where(jnp.logical_and(sel, key != INT_MIN), 0.0, NEG_BIG)
        out = []
        for g in range(A_KV_HEADS):
            kc = jnp.concatenate([read_k(start, g, last), pos_cols], axis=1)
            lg = lax.dot_general(q_aug[g], kc, NT_DIMS, preferred_element_type=F32).reshape(A_GROUP, tq, tc)
            if last:
                ahead = jnp.maximum(kpos - qpos, 0).astype(F32)
                bias = jnp.stack([mb - (2.0 * slopes[g * A_GROUP + r]) * ahead for r in range(A_GROUP)])
            else:
                bias = mb[None]
            out.append((lg + bias).reshape(rows4, tc))
        return start, out

    m_ref[...] = jnp.full(m_ref.shape, NEG_BIG, F32)
    l_ref[...] = jnp.zeros(l_ref.shape, F32)
    acc_ref[...] = jnp.zeros(acc_ref.shape, F32)

    def attend(c, last):
        start, logits = chunk_logits(c, last)
        for g in range(A_KV_HEADS):
            rows = slice(g * rows4, (g + 1) * rows4)
            tiles = _lane_tiles(logits[g])
            m_old = m_ref[rows]
            m_new = jnp.maximum(m_old, jnp.max(functools.reduce(jnp.maximum, tiles), axis=1, keepdims=True))
            alpha = jnp.exp(m_old - m_new)
            p = [jnp.exp(t - m_new) for t in tiles]
            m_ref[rows] = m_new
            l_ref[rows] = alpha * l_ref[rows] + functools.reduce(jnp.add, p)
            acc_ref[rows] = alpha * acc_ref[rows] + jnp.dot(jnp.concatenate(p, axis=1).astype(BF16),
                                                            read_v(start, g, last), preferred_element_type=F32)

    all_chunks(attend)
    for h in range(A_HEADS):
        rows = slice(h * tq, (h + 1) * tq)
        o_ref[:, h * HEAD_DIM:(h + 1) * HEAD_DIM] = (
            acc_ref[rows] / jnp.sum(l_ref[rows], axis=1, keepdims=True)).astype(BF16)


def dsa_attention(qa, qi, wi, k, v, ki, *, q_pos0, n_keys, tq, tc, caches=()):
    b, n, _ = qa.shape
    if caches:
        assert q_pos0 % tc == 0 and n == tq <= tc and k.shape[1] == tc and n_keys == q_pos0 + n
        l_pad = q_pos0 + tc
    else:
        l_pad = k.shape[1]
    n_sel = min(TOPK_KEYS, n_keys // 4)
    qspec = lambda w: pl.BlockSpec((None, tq, w), lambda bi, i: (bi, i, 0))
    kspec = lambda w: pl.BlockSpec((None, k.shape[1], w), lambda bi, i: (bi, 0, 0))
    cache_specs = [pl.BlockSpec((None,) + c.shape[1:], lambda bi, i, nd=c.ndim: (bi,) + (0,) * (nd - 1))
                   for c in caches]
    kern = functools.partial(_dsa_kernel, q_pos0=q_pos0, n_keys=n_keys, n_sel=n_sel, tq=tq, tc=tc,
                             pos_bits=max(1, (l_pad - 1).bit_length()), cached=bool(caches))
    return pl.pallas_call(
        kern,
        grid=(b, n // tq),
        in_specs=[qspec(A_WIDTH), qspec(IDX_Q_WIDTH), qspec(IDX_HEADS),
                  kspec(A_KV_WIDTH), kspec(A_KV_WIDTH), kspec(IDX_DIM)] + cache_specs,
        out_specs=qspec(A_WIDTH),
        out_shape=jax.ShapeDtypeStruct((b, n, A_WIDTH), BF16),
        scratch_shapes=[pltpu.VMEM((l_pad // tc, tq, tc), jnp.int32),
                        pltpu.VMEM((A_HEADS * tq, LANES), F32),
                        pltpu.VMEM((A_HEADS * tq, LANES), F32),
                        pltpu.VMEM((A_HEADS * tq, HEAD_DIM), F32)],
        compiler_params=_params("arbitrary", "arbitrary"),
        name="dsa",
    )(qa, qi, wi, k, v, ki, *caches)


STICK_HEADS_PER_STEP = 2


def _strict_upper(n):
    return (lax.broadcasted_iota(jnp.int32, (n, n), 0) > lax.broadcasted_iota(jnp.int32, (n, n), 1)).astype(BF16)


def _stick_step(q, kc, vc, strict, upper, tail, acc):
    z = lax.dot_general(q, kc, NT_DIMS, preferred_element_type=F32) * (HEAD_DIM ** -0.5)
    softplus = jnp.log(1.0 + jnp.exp(-jnp.abs(z)))
    log_go = jnp.minimum(z, 0.0) - softplus
    log_stay = jnp.where(strict, jnp.minimum(-z, 0.0) - softplus, 0.0)
    hi = log_stay.astype(BF16)
    lo = (log_stay - hi.astype(F32)).astype(BF16)
    after = jnp.dot(hi, upper, preferred_element_type=F32) + jnp.dot(lo, upper, preferred_element_type=F32) + tail
    w = jnp.where(strict, jnp.exp(log_go + after), 0.0)
    return (tail + jnp.sum(log_stay, axis=1, keepdims=True),
            acc + jnp.dot(w.astype(BF16), vc, preferred_element_type=F32))


def _stick_walk(qs, tpos, c_first, tc, read_kv, tails, accs):
    upper = _strict_upper(tc)

    def cond(carry):
        n, tails, _ = carry
        return jnp.logical_and(n <= c_first, jnp.max(functools.reduce(jnp.maximum, tails)) > STICK_UNDERFLOW)

    def body(carry):
        n, tails, accs = carry
        start = pl.multiple_of((c_first - n) * tc, tc)
        strict = start + lax.broadcasted_iota(jnp.int32, (1, tc), 1) < tpos
        new = [_stick_step(q, *read_kv(start, h), strict, upper, tail, acc)
               for h, (q, tail, acc) in enumerate(zip(qs, tails, accs))]
        return n + 1, tuple(t for t, _ in new), tuple(a for _, a in new)

    _, _, accs = lax.while_loop(cond, body, (jnp.int32(0), tuple(tails), tuple(accs)))
    return accs


def _stick_kernel(q_ref, k_ref, v_ref, o_ref, *, q_pos0, tq, tc):
    t0 = q_pos0 + pl.program_id(2) * tq
    tpos = t0 + lax.broadcasted_iota(jnp.int32, (tq, 1), 0)
    heads = [slice(h * HEAD_DIM, (h + 1) * HEAD_DIM) for h in range(STICK_HEADS_PER_STEP)]
    read_kv = lambda start, h: (k_ref[pl.ds(start, tc), heads[h]], v_ref[pl.ds(start, tc), heads[h]])
    zeros = lambda w: [jnp.zeros((tq, w), F32) for _ in heads]
    accs = _stick_walk([q_ref[:, hs] for hs in heads], tpos, (t0 + tq - 2) // tc, tc, read_kv,
                       zeros(1), zeros(HEAD_DIM))
    for hs, acc in zip(heads, accs):
        o_ref[:, hs] = acc.astype(BF16)


def _stick_cached_kernel(q_ref, k_ref, v_ref, ck_ref, cv_ref, o_ref, *, past, tc):
    n = q_ref.shape[0]
    tpos = past + lax.broadcasted_iota(jnp.int32, (n, 1), 0)
    strict = past + lax.broadcasted_iota(jnp.int32, (1, n), 1) < tpos
    upper = _strict_upper(n)
    for h0 in range(0, B_HEADS, STICK_HEADS_PER_STEP):
        group = range(h0, h0 + STICK_HEADS_PER_STEP)
        cols = [slice(h * HEAD_DIM, (h + 1) * HEAD_DIM) for h in group]
        qs = [q_ref[:, hs] for hs in cols]
        first = [_stick_step(q, k_ref[:, hs], v_ref[:, hs], strict, upper,
                             jnp.zeros((n, 1), F32), jnp.zeros((n, HEAD_DIM), F32)) for q, hs in zip(qs, cols)]
        read_kv = lambda start, h, h0=h0: (ck_ref[pl.ds(start, tc), h0 + h, :].astype(BF16),
                                           cv_ref[pl.ds(start, tc), h0 + h, :].astype(BF16))
        accs = _stick_walk(qs, tpos, past // tc - 1, tc, read_kv, [t for t, _ in first], [a for _, a in first])
        for hs, acc in zip(cols, accs):
            o_ref[:, hs] = acc.astype(BF16)


def stick_attention(qb, k, v, *, q_pos0, tq, tc):
    b, n, _ = qb.shape
    l_pad = k.shape[1]
    width = STICK_HEADS_PER_STEP * HEAD_DIM
    qspec = pl.BlockSpec((None, tq, width), lambda bi, h, i: (bi, i, h))
    kspec = pl.BlockSpec((None, l_pad, width), lambda bi, h, i: (bi, 0, h))
    kern = functools.partial(_stick_kernel, q_pos0=q_pos0, tq=tq, tc=tc)
    return pl.pallas_call(
        kern,
        grid=(b, B_HEADS // STICK_HEADS_PER_STEP, n // tq),
        in_specs=[qspec, kspec, kspec],
        out_specs=qspec,
        out_shape=jax.ShapeDtypeStruct((b, n, B_WIDTH), BF16),
        compiler_params=_params("arbitrary", "arbitrary", "arbitrary"),
        name="stick",
    )(qb, k, v)


def stick_attention_cached(qb, k_new, v_new, cache_k, cache_v, *, tc):
    b, n, _ = qb.shape
    past = cache_k.shape[1]
    assert past % tc == 0
    new = pl.BlockSpec((None, n, B_WIDTH), lambda bi: (bi, 0, 0))
    cache = pl.BlockSpec((None, past, B_HEADS, HEAD_DIM), lambda bi: (bi, 0, 0, 0))
    return pl.pallas_call(
        functools.partial(_stick_cached_kernel, past=past, tc=tc),
        grid=(b,),
        in_specs=[new, new, new, cache, cache],
        out_specs=new,
        out_shape=jax.ShapeDtypeStruct((b, n, B_WIDTH), BF16),
        compiler_params=_params("arbitrary"),
        name="stick_cached",
    )(qb, k_new, v_new, cache_k, cache_v)


def _merge_kernel(h_ref, ya_ref, yb_ref, wga_ref, wgb_ref, wa_ref, wb_ref, o_ref):
    hb = h_ref[...]
    ga = jax.nn.sigmoid(jnp.dot(hb, wga_ref[...], preferred_element_type=F32))
    gb = jax.nn.sigmoid(jnp.dot(hb, wgb_ref[...], preferred_element_type=F32))
    pa = jnp.dot(ya_ref[...], wa_ref[...], preferred_element_type=F32)
    pb = jnp.dot(yb_ref[...], wb_ref[...], preferred_element_type=F32)
    o_ref[...] = (ga * pa + gb * pb).astype(BF16)


def merge_branches(h, ya, yb, w_ga, w_gb, w_a, w_b, tm, tn):
    t, d = h.shape
    row = lambda w: pl.BlockSpec((tm, w), lambda i, j: (i, 0))
    col = lambda k: pl.BlockSpec((k, tn), lambda i, j: (0, j))
    return pl.pallas_call(
        _merge_kernel,
        grid=(t // tm, d // tn),
        in_specs=[row(d), row(A_WIDTH), row(B_WIDTH), col(d), col(d), col(A_WIDTH), col(B_WIDTH)],
        out_specs=pl.BlockSpec((tm, tn), lambda i, j: (i, j)),
        out_shape=jax.ShapeDtypeStruct((t, d), BF16),
        compiler_params=_params("arbitrary", "arbitrary"),
        name="merge",
    )(h, ya, yb, w_ga, w_gb, w_a, w_b)


def _pack_bf16_pair(lo, hi):
    lo_bits = lax.bitcast_convert_type(lo.astype(BF16).astype(F32), jnp.uint32)
    hi_bits = lax.bitcast_convert_type(hi.astype(BF16).astype(F32), jnp.uint32)
    return (lo_bits >> 16) | (hi_bits & jnp.uint32(0xFFFF0000))


def _unpack_f32_pair(u):
    return (lax.bitcast_convert_type(u << 16, F32),
            lax.bitcast_convert_type(u & jnp.uint32(0xFFFF0000), F32))


def _unpack_bf16_pair(u):
    lo, hi = _unpack_f32_pair(u)
    return lo.astype(BF16), hi.astype(BF16)


def _first_max(vals, idx, axes, n):
    m = vals
    for ax in axes:
        m = jnp.max(m, axis=ax, keepdims=True)
    first = jnp.where(vals == m, idx, n)
    for ax in axes:
        first = jnp.min(first, axis=ax, keepdims=True)
    return m, first


def _sum01(a):
    return jnp.sum(jnp.sum(a, axis=0, keepdims=True), axis=1, keepdims=True)


def _out_kernel(mg_ref, x_ref, g1_ref, sc_ref, sh_ref, nw_ref, wo_ref, wr_ref, br_ref, cnt0_ref,
                x1_ref, hu_ref, tope_ref, rank_ref, gatek_ref, cnt_out_ref, cnt_ref):
    tm, d = x_ref.shape
    proj = jnp.dot(mg_ref[...], wo_ref[...], preferred_element_type=F32)
    x1 = x_ref[...] + _gate_rows(proj, g1_ref[...])
    x1_ref[...] = x1
    h2 = _modulate(_rms(x1, nw_ref[...]), sc_ref[...], sh_ref[...])
    hu_ref[...] = _pack_bf16_pair(h2[:, :d // 2], h2[:, d // 2:])

    logits = lax.dot_general(wr_ref[...], h2.astype(BF16), NT_DIMS, preferred_element_type=F32)
    aff = jax.nn.sigmoid(logits)
    shape3 = (N_GROUPS, GROUP_SIZE, tm)
    aff3 = aff.reshape(shape3)
    biased = (aff + br_ref[...]).reshape(shape3)
    io_in = lax.broadcasted_iota(jnp.int32, shape3, 1)
    m1, i1 = _first_max(biased, io_in, (1,), GROUP_SIZE)
    m2 = jnp.max(jnp.where(io_in == i1, -jnp.inf, biased), axis=1, keepdims=True)
    gscore = m1 + m2
    gio = lax.broadcasted_iota(jnp.int32, gscore.shape, 0)
    keep = jnp.zeros(gscore.shape, jnp.bool_)
    for _ in range(N_ACTIVE_GROUPS):
        _, first = _first_max(gscore, gio, (0,), N_GROUPS)
        hit = gio == first
        keep = jnp.logical_or(keep, hit)
        gscore = jnp.where(hit, -jnp.inf, gscore)
    masked = jnp.where(jnp.broadcast_to(keep, shape3), biased, -jnp.inf)
    eio = lax.broadcasted_iota(jnp.int32, shape3, 0) * GROUP_SIZE + io_in
    sel = jnp.zeros(shape3, jnp.bool_)
    hits = []
    for _ in range(N_ACTIVE):
        _, first = _first_max(masked, eio, (0, 1), N_EXPERTS)
        hit = eio == first
        sel = jnp.logical_or(sel, hit)
        masked = jnp.where(hit, -jnp.inf, masked)
        hits.append((hit, first.reshape(1, tm)))
    sel_aff = jnp.where(sel, aff3, 0.0)
    gates = sel_aff / _sum01(sel_aff) * ROUTED_SCALE

    @pl.when(pl.program_id(0) == 0)
    def _():
        cnt_ref[...] = cnt0_ref[...]

    sel2 = jnp.where(sel, 1.0, 0.0).reshape(N_EXPERTS, tm)
    incl = (lax.broadcasted_iota(jnp.int32, (tm, tm), 0)
            <= lax.broadcasted_iota(jnp.int32, (tm, tm), 1)).astype(BF16)
    rank = jnp.dot(sel2.astype(BF16), incl, preferred_element_type=F32) + cnt_ref[...]
    cnt_ref[...] = cnt_ref[...] + jnp.sum(sel2, axis=1, keepdims=True)
    cnt_out_ref[...] = cnt_ref[...]
    rank3 = rank.reshape(shape3)
    tope_ref[...] = jnp.concatenate([first for _, first in hits], axis=0)
    rank_ref[...] = jnp.concatenate(
        [_sum01(jnp.where(hit, rank3, 0.0)).reshape(1, tm) for hit, _ in hits], axis=0).astype(jnp.int32)
    gatek_ref[...] = jnp.concatenate(
        [_sum01(jnp.where(hit, gates, 0.0)).reshape(1, tm) for hit, _ in hits], axis=0)


def out_proj(merged, x, gate1_g, scale_g, shift_g, norm_w, w_out, w_router_t, b_router, count0, tm):
    t, d = x.shape
    gm = tm // MOD_GROUP
    row = lambda w: pl.BlockSpec((tm, w), lambda i: (i, 0))
    grp = pl.BlockSpec((gm, d), lambda i: (i, 0))
    const = lambda a, b: pl.BlockSpec((a, b), lambda i: (0, 0))
    per_k = pl.BlockSpec((N_ACTIVE, tm), lambda i: (0, i))
    return pl.pallas_call(
        _out_kernel,
        grid=(t // tm,),
        in_specs=[row(d), row(d), grp, grp, grp, const(1, d), const(d, d),
                  const(N_EXPERTS, d), const(N_EXPERTS, 1), const(N_EXPERTS, 1)],
        out_specs=[row(d), row(d // 2), per_k, per_k, per_k, const(N_EXPERTS, 1)],
        out_shape=[jax.ShapeDtypeStruct((t, d), F32), jax.ShapeDtypeStruct((t, d // 2), jnp.uint32),
                   jax.ShapeDtypeStruct((N_ACTIVE, t), jnp.int32), jax.ShapeDtypeStruct((N_ACTIVE, t), jnp.int32),
                   jax.ShapeDtypeStruct((N_ACTIVE, t), F32), jax.ShapeDtypeStruct((N_EXPERTS, 1), F32)],
        scratch_shapes=[pltpu.VMEM((N_EXPERTS, 1), F32)],
        compiler_params=_params("arbitrary"),
        name="out_proj",
    )(merged, x, gate1_g, scale_g, shift_g, norm_w, w_out, w_router_t, b_router, count0)


def _pad_fill_kernel(start_ref, cnt_ref, nu_ref, xs_ref, zero_ref, sem, *, tme, n_tiles):
    zero_ref[...] = jnp.zeros(zero_ref.shape, zero_ref.dtype)

    def tile_copy(i):
        return pltpu.make_async_copy(zero_ref, xs_ref.at[pl.ds(pl.multiple_of(i * tme, tme), tme)], sem.at[1])

    def for_each_copy(wait):
        def pad_rows(e, carry):
            pos = start_ref[e]
            for bit in range(tme.bit_length() - 1):
                size = 1 << bit

                @pl.when((cnt_ref[e] & size) != 0)
                def _(pos=pos, size=size, bit=bit):
                    copy = pltpu.make_async_copy(zero_ref.at[pl.ds(0, size)],
                                                 xs_ref.at[pl.ds(pl.multiple_of(pos, size), size)], sem.at[2 + bit])
                    copy.wait() if wait else copy.start()

                pos = pos + (cnt_ref[e] & size)
            return carry

        def tail_tiles(i, carry):
            tile_copy(i).wait() if wait else tile_copy(i).start()
            return carry

        lax.fori_loop(0, N_EXPERTS, pad_rows, 0)
        lax.fori_loop(nu_ref[0], n_tiles, tail_tiles, 0)

    for_each_copy(wait=False)
    for_each_copy(wait=True)


def pad_fill(pad_start, pad_count, n_used, p_rows, row_shape, tme):
    grid_spec = pltpu.PrefetchScalarGridSpec(
        num_scalar_prefetch=3, grid=(1,), in_specs=[],
        out_specs=pl.BlockSpec(memory_space=pl.ANY),
        scratch_shapes=[pltpu.VMEM((tme,) + row_shape, jnp.uint32),
                        pltpu.SemaphoreType.DMA((2 + tme.bit_length(),))])
    return pl.pallas_call(
        functools.partial(_pad_fill_kernel, tme=tme, n_tiles=p_rows // tme),
        grid_spec=grid_spec,
        out_shape=jax.ShapeDtypeStruct((p_rows,) + row_shape, jnp.uint32),
        compiler_params=_params("arbitrary"),
        name="pad_fill",
    )(pad_start, pad_count, n_used)


def _dispatch_kernel(slot_ref, hu_ref, xs_in_ref, xs_ref, sem, *, td):
    del xs_in_ref

    def row_copy(j, k):
        return pltpu.make_async_copy(hu_ref.at[pl.ds(j, 1)],
                                     xs_ref.at[pl.ds(slot_ref[j * N_ACTIVE + k], 1)], sem)

    def start(jj, carry):
        base = pl.multiple_of(jj * SUBLANES, SUBLANES)
        for r in range(SUBLANES):
            for k in range(N_ACTIVE):
                row_copy(base + r, k).start(priority=k % 2)
        return carry

    def wait(j, carry):
        for k in range(N_ACTIVE):
            row_copy(j, k).wait()
        return carry

    lax.fori_loop(0, td // SUBLANES, start, 0)
    lax.fori_loop(0, td, wait, 0)


def dispatch(slots_flat, hu, xs, td):
    t = hu.shape[0]
    return pl.pallas_call(
        functools.partial(_dispatch_kernel, td=td),
        grid=(t // td,),
        in_specs=[pl.BlockSpec((td * N_ACTIVE,), lambda i: (i,), memory_space=pltpu.SMEM),
                  pl.BlockSpec((td,) + hu.shape[1:], lambda i: (i, 0)),
                  pl.BlockSpec(memory_space=pl.ANY)],
        out_specs=pl.BlockSpec(memory_space=pl.ANY),
        out_shape=jax.ShapeDtypeStruct(xs.shape, xs.dtype),
        scratch_shapes=[pltpu.SemaphoreType.DMA(())],
        input_output_aliases={2: 0},
        compiler_params=_params("arbitrary"),
        name="dispatch",
    )(slots_flat, hu, xs)


def _expert_kernel(te_ref, nu_ref, first_ref, next_ref, par_ref, xs_ref, wg_ref, wu_ref, wd_ref, o_ref,
                   wgf_ref, wuf_ref, wdf_ref, wgb_ref, wub_ref, wdb_ref, sem):
    i = pl.program_id(0)
    used = i < nu_ref[0]

    def fetch(e, buf):
        return [pltpu.make_async_copy(w.at[e], f.at[buf], sem.at[buf, j])
                for j, (w, f) in enumerate(((wg_ref, wgf_ref), (wu_ref, wuf_ref), (wd_ref, wdf_ref)))]

    @pl.when(i == 0)
    def _():
        for copy in fetch(te_ref[0], par_ref[0]):
            copy.start()

    @pl.when(jnp.logical_and(used, first_ref[i] == 1))
    def _():
        buf = par_ref[i]
        for copy in fetch(te_ref[i], buf):
            copy.wait()

        @pl.when(next_ref[i] >= 0)
        def _():
            for copy in fetch(next_ref[i], 1 - buf):
                copy.start()

        wgb_ref[...] = wgf_ref[buf].astype(BF16)
        wub_ref[...] = wuf_ref[buf].astype(BF16)
        wdb_ref[...] = wdf_ref[buf].astype(BF16)

    @pl.when(used)
    def _():
        lo, hi = _unpack_bf16_pair(xs_ref[...])
        half = lo.shape[1]
        gate = (jnp.dot(lo, wgb_ref[:half], preferred_element_type=F32)
                + jnp.dot(hi, wgb_ref[half:], preferred_element_type=F32))
        up = (jnp.dot(lo, wub_ref[:half], preferred_element_type=F32)
              + jnp.dot(hi, wub_ref[half:], preferred_element_type=F32))
        hid = (_silu(gate) * up).astype(BF16)
        o_ref[...] = jnp.dot(hid, wdb_ref[...], preferred_element_type=F32)

    @pl.when(jnp.logical_not(used))
    def _():
        o_ref[...] = jnp.zeros(o_ref.shape, o_ref.dtype)


def expert_ffn(tiles, xs, w_gate, w_up, w_down, tme):
    p, half = xs.shape
    e, d, f = w_gate.shape
    hbm = pl.BlockSpec(memory_space=pl.ANY)
    grid_spec = pltpu.PrefetchScalarGridSpec(
        num_scalar_prefetch=5,
        grid=(p // tme,),
        in_specs=[pl.BlockSpec((tme, half), lambda i, te, nu, *_: (jnp.minimum(i, nu[0] - 1), 0)),
                  hbm, hbm, hbm],
        out_specs=pl.BlockSpec((tme, d), lambda i, *_: (i, 0)),
        scratch_shapes=[pltpu.VMEM((2, d, f), F32), pltpu.VMEM((2, d, f), F32), pltpu.VMEM((2, f, d), F32),
                        pltpu.VMEM((d, f), BF16), pltpu.VMEM((d, f), BF16), pltpu.VMEM((f, d), BF16),
                        pltpu.SemaphoreType.DMA((2, 3))],
    )
    return pl.pallas_call(
        _expert_kernel,
        grid_spec=grid_spec,
        out_shape=jax.ShapeDtypeStruct((p, d), F32),
        compiler_params=_params("arbitrary"),
        name="expert_ffn",
    )(tiles["expert"], tiles["n_used"], tiles["first"], tiles["next"], tiles["parity"], xs, w_gate, w_up, w_down)


def _combine_kernel(slot_ref, hu_ref, x1_ref, g2_ref, gk_ref, wg_ref, wu_ref, wd_ref, ys_ref,
                    o_ref, rows_ref, sem, *, tmc):
    def row_copy(j, k):
        return pltpu.make_async_copy(ys_ref.at[pl.ds(slot_ref[j * N_ACTIVE + k], 1)],
                                     rows_ref.at[k, pl.ds(j, 1)], sem)

    def start(jj, carry):
        base = pl.multiple_of(jj * SUBLANES, SUBLANES)
        for r in range(SUBLANES):
            for k in range(N_ACTIVE):
                row_copy(base + r, k).start(priority=k % 2)
        return carry

    def wait(j, carry):
        for k in range(N_ACTIVE):
            row_copy(j, k).wait()
        return carry

    lax.fori_loop(0, tmc // SUBLANES, start, 0)
    lo, hi = _unpack_bf16_pair(hu_ref[...])
    half = lo.shape[1]
    gate = (jnp.dot(lo, wg_ref[:half], preferred_element_type=F32)
            + jnp.dot(hi, wg_ref[half:], preferred_element_type=F32))
    up = (jnp.dot(lo, wu_ref[:half], preferred_element_type=F32)
          + jnp.dot(hi, wu_ref[half:], preferred_element_type=F32))
    y = jnp.dot((_silu(gate) * up).astype(BF16), wd_ref[...], preferred_element_type=F32)
    lax.fori_loop(0, tmc, wait, 0)
    gk = gk_ref[...]
    for k in range(N_ACTIVE):
        y = y + gk[:, k:k + 1] * rows_ref[k]
    o_ref[...] = x1_ref[...] + _gate_rows(y, g2_ref[...])


def combine(slots_flat, hu, x1, gate2_g, gate_k, w_gs, w_us, w_ds, ys, tmc):
    t, d = x1.shape
    f = w_gs.shape[1]
    gm = tmc // MOD_GROUP
    row = lambda w: pl.BlockSpec((tmc, w), lambda i: (i, 0))
    const = lambda a, b: pl.BlockSpec((a, b), lambda i: (0, 0))
    return pl.pallas_call(
        functools.partial(_combine_kernel, tmc=tmc),
        grid=(t // tmc,),
        in_specs=[pl.BlockSpec((tmc * N_ACTIVE,), lambda i: (i,), memory_space=pltpu.SMEM),
                  pl.BlockSpec((tmc,) + hu.shape[1:], lambda i: (i, 0)),
                  row(d), pl.BlockSpec((gm, d), lambda i: (i, 0)), row(N_ACTIVE),
                  const(d, f), const(d, f), const(f, d),
                  pl.BlockSpec(memory_space=pl.ANY)],
        out_specs=row(d),
        out_shape=jax.ShapeDtypeStruct((t, d), F32),
        scratch_shapes=[pltpu.VMEM((N_ACTIVE, tmc, d), F32), pltpu.SemaphoreType.DMA(())],
        compiler_params=_params("arbitrary"),
        name="combine",
    )(slots_flat, hu, x1, gate2_g, gate_k, w_gs, w_us, w_ds, ys)


def _group_rows(v, n):
    b, d = v.shape
    return jnp.broadcast_to(v[:, None, :], (b, n // MOD_GROUP, d)).reshape(b * n // MOD_GROUP, d)


def _pad_keys(a, l_pad):
    return jnp.pad(a, ((0, 0), (0, l_pad - a.shape[1]), (0, 0)))


def _token_stage(x, mod, w, caches, count0):
    b, n, d = x.shape
    t = b * n
    shift1, scale1, gate1, shift2, scale2, gate2 = [_group_rows(m, n) for m in jnp.split(mod, 6, axis=-1)]
    xf = x.reshape(t, d)
    tm = _row_tile(t, 512)
    (h, qa, ka, va, kab, vab, qi, ki, kib, wi) = in_proj_a(
        xf, w["norm_mix"], scale1, shift1, w["w_in_a"], w["q_norm_a"], w["k_norm_a"], tm)
    qb, kb, vb, kbb, vbb = in_proj_b(h, w["w_in_b"], tm)

    r3 = lambda a: a.reshape(b, n, a.shape[-1])
    if caches is None:
        ya = dsa_attention(r3(qa), r3(qi), r3(wi), r3(kab), r3(vab), r3(kib),
                           q_pos0=0, n_keys=n, tq=min(256, n), tc=min(512, n))
        yb = stick_attention(r3(qb), r3(kbb), r3(vbb), q_pos0=0, tq=min(256, n), tc=min(256, n))
    else:
        cka, cva, cki, ckb, cvb = caches
        past = cka.shape[1]
        tc_a, tc_b = min(512, past), min(256, past)
        new_chunk = lambda a: _pad_keys(r3(a), tc_a)
        ya = dsa_attention(r3(qa), r3(qi), r3(wi), new_chunk(kab), new_chunk(vab), new_chunk(kib),
                           q_pos0=past, n_keys=past + n, tq=n, tc=tc_a, caches=(cka, cva, cki))
        yb = stick_attention_cached(r3(qb), r3(kbb), r3(vbb), ckb, cvb, tc=tc_b)

    merged = merge_branches(h, ya.reshape(t, A_WIDTH), yb.reshape(t, B_WIDTH),
                            w["w_ga"], w["w_gb"], w["w_branch_a"], w["w_branch_b"], tm, _row_tile(d, 512))
    x1, hu, top_e, rank_k, gate_k, counts = out_proj(
        merged, xf, gate1, scale2, shift2, w["norm_ffn"], w["w_out"], w["w_router_t"], w["b_router"], count0, tm)
    rows = (ka.reshape(b, n, A_KV_HEADS, HEAD_DIM), va.reshape(b, n, A_KV_HEADS, HEAD_DIM),
            ki.reshape(b, n, IDX_DIM), kb.reshape(b, n, B_HEADS, HEAD_DIM), vb.reshape(b, n, B_HEADS, HEAD_DIM))
    return dict(x1=x1, hu=hu, top_e=top_e, rank_k=rank_k, gate_k=gate_k, counts=counts, gate2=gate2, rows=rows)


def _routing(counts, top_e, rank_k, tme):
    e = counts.shape[0]
    t = top_e.shape[1]
    counts = counts.reshape(e).astype(jnp.int32)
    padded = (counts + tme - 1) // tme * tme
    ends = jnp.cumsum(padded)
    offsets = ends - padded
    eids = jnp.arange(e, dtype=jnp.int32)[:, None, None]
    offset_k = jnp.sum(jnp.where(top_e[None] == eids, offsets[:, None, None], 0), axis=0)
    slots = offset_k + rank_k - 1
    n_tiles = (t * N_ACTIVE + e * tme) // tme
    tile_start = jnp.arange(n_tiles, dtype=jnp.int32) * tme
    tile_expert = jnp.minimum(jnp.sum(ends[None, :] <= tile_start[:, None], axis=1), e - 1).astype(jnp.int32)
    n_used = (ends[-1] // tme).astype(jnp.int32).reshape(1)
    ids = jnp.arange(e, dtype=jnp.int32)
    nonempty = padded > 0
    later = jnp.where((ids[None, :] > ids[:, None]) & nonempty[None, :], ids[None, :], e)
    next_e = jnp.min(later, axis=1)
    next_e = jnp.where(next_e == e, -1, next_e).astype(jnp.int32)
    rank_e = jnp.cumsum(nonempty.astype(jnp.int32)) - 1
    tiles = dict(expert=tile_expert, n_used=n_used,
                 first=(tile_start == offsets[tile_expert]).astype(jnp.int32),
                 next=next_e[tile_expert], parity=(rank_e[tile_expert] % 2).astype(jnp.int32))
    return dict(slots=slots.T.reshape(-1), tiles=tiles, n_used=n_used, p_rows=n_tiles * tme,
                pad_start=offsets + counts, pad_count=padded - counts)


def kernel(x_prompt, x_sample, cache_a_k, cache_a_v, cache_a_idx_k, cache_b_k, cache_b_v, c_prompt, c_sample,
           norm_mix, norm_ffn, w_ada, b_ada, w_in, q_norm_a, k_norm_a, w_branch_a, w_branch_b, w_out,
           w_router, b_router, w_gate_e, w_up_e, w_down_e, w_gate_s, w_up_s, w_down_s):
    depth = norm_mix.shape[0]
    d = x_prompt.shape[-1]
    bp, bs = c_prompt.shape[0], c_sample.shape[0]
    c_all = jnp.concatenate([c_prompt, c_sample], axis=0)
    c_all = jnp.pad(c_all, ((0, -(bp + bs) % 16), (0, 0)))
    tme = 256

    xp, xs = x_prompt, x_sample
    prompt_rows, sample_rows = [], []
    for layer in range(depth):
        wl = w_in[layer]
        o_idx = A_WIDTH + 2 * A_KV_WIDTH + IDX_Q_WIDTH
        o_b = o_idx + IDX_DIM + IDX_HEADS
        o_g = o_b + 3 * B_WIDTH
        w_in_a = jnp.concatenate(
            [wl[:, :o_b], jnp.zeros((d, LANES - IDX_DIM - IDX_HEADS), wl.dtype)], axis=1).astype(BF16)
        w = dict(
            norm_mix=norm_mix[layer].reshape(1, d), norm_ffn=norm_ffn[layer].reshape(1, d),
            q_norm_a=q_norm_a[layer].reshape(1, HEAD_DIM), k_norm_a=k_norm_a[layer].reshape(1, HEAD_DIM),
            w_in_a=w_in_a, w_in_b=wl[:, o_b:o_g].astype(BF16),
            w_ga=wl[:, o_g:o_g + d].astype(BF16), w_gb=wl[:, o_g + d:].astype(BF16),
            w_branch_a=w_branch_a[layer].astype(BF16), w_branch_b=w_branch_b[layer].astype(BF16),
            w_out=w_out[layer].astype(BF16), w_router_t=w_router[layer].T.astype(BF16),
            b_router=b_router[layer].reshape(N_EXPERTS, 1))
        mod = ada_mod(c_all, w_ada[layer], b_ada[layer])
        caches = (cache_a_k[layer], cache_a_v[layer], cache_a_idx_k[layer], cache_b_k[layer], cache_b_v[layer])
        sp = _token_stage(xp, mod[:bp], w, None, jnp.zeros((N_EXPERTS, 1), F32))
        ss = _token_stage(xs, mod[bp:bp + bs], w, caches, sp["counts"])
        prompt_rows.append(sp["rows"])
        sample_rows.append(ss["rows"])

        tp, ts = sp["x1"].shape[0], ss["x1"].shape[0]
        both = lambda name: jnp.concatenate([sp[name], ss[name]], axis=1)
        rt = _routing(ss["counts"], both("top_e"), both("rank_k"), tme)
        slots, gate_k = rt["slots"], both("gate_k").T
        xsorted = pad_fill(rt["pad_start"], rt["pad_count"], rt["n_used"], rt["p_rows"],
                           sp["hu"].shape[1:], tme)
        xsorted = dispatch(slots[:tp * N_ACTIVE], sp["hu"], xsorted, _row_tile(tp, 256))
        xsorted = dispatch(slots[tp * N_ACTIVE:], ss["hu"], xsorted, _row_tile(ts, 256))
        ysorted = expert_ffn(rt["tiles"], xsorted, w_gate_e[layer], w_up_e[layer], w_down_e[layer], tme)
        shared = (w_gate_s[layer].astype(BF16), w_up_s[layer].astype(BF16), w_down_s[layer].astype(BF16))
        tmc = 256
        yp = combine(slots[:tp * N_ACTIVE], sp["hu"], sp["x1"], sp["gate2"], gate_k[:tp], *shared, ysorted,
                     _row_tile(tp, tmc))
        ys = combine(slots[tp * N_ACTIVE:], ss["hu"], ss["x1"], ss["gate2"], gate_k[tp:], *shared, ysorted,
                     _row_tile(ts, tmc))
        xp = yp.reshape(xp.shape)
        xs = ys.reshape(xs.shape)

    stack = lambda rows: tuple(jnp.stack(r) for r in zip(*rows))
    return (xp, xs) + stack(prompt_rows) + stack(sample_rows)
```

```python
import functools

import jax
import jax.numpy as jnp
from jax import lax
from jax.experimental import pallas as pl
from jax.experimental.pallas import tpu as pltpu

CHUNK = 64
HEAD_DIM = 128
A_HEADS = 8
A_KV_HEADS = 2
A_GROUP = A_HEADS // A_KV_HEADS
IDX_HEADS = 8
IDX_DIM = 64
TOPK_KEYS = 256
B_HEADS = 8
N_EXPERTS = 64
N_ACTIVE = 8
N_GROUPS = 8
N_ACTIVE_GROUPS = 4
GROUP_SIZE = N_EXPERTS // N_GROUPS
ROUTED_SCALE = 2.5
EPS = 1e-6

A_WIDTH = A_HEADS * HEAD_DIM
A_KV_WIDTH = A_KV_HEADS * HEAD_DIM
IDX_Q_WIDTH = IDX_HEADS * IDX_DIM
B_WIDTH = B_HEADS * HEAD_DIM

LANES = 128
SUBLANES = 8
MOD_GROUP = 32
VMEM_LIMIT = 56 * 1024 * 1024
NEG_BIG = -1e30
STICK_UNDERFLOW = -110.0
INT_MIN = -(2 ** 31)

F32 = jnp.float32
BF16 = jnp.bfloat16
NT_DIMS = (((1,), (1,)), ((), ()))


def _params(*sem):
    return pltpu.CompilerParams(dimension_semantics=sem, vmem_limit_bytes=VMEM_LIMIT)


def _silu(x):
    return x * jax.nn.sigmoid(x)


def _rms(x, gain):
    return x * lax.rsqrt(jnp.mean(x * x, axis=-1, keepdims=True) + EPS) * gain


def _modulate(y, scale_g, shift_g):
    tm, d = y.shape
    y3 = y.reshape(tm // MOD_GROUP, MOD_GROUP, d)
    out = y3 * (1.0 + scale_g[:, None, :]) + shift_g[:, None, :]
    return out.reshape(tm, d)


def _gate_rows(y, gate_g):
    tm, d = y.shape
    return (y.reshape(tm // MOD_GROUP, MOD_GROUP, d) * gate_g[:, None, :]).reshape(tm, d)


def _row_tile(t, pref):
    tm = min(pref, t)
    while t % tm:
        tm //= 2
    return tm


def _ada_kernel(c_ref, w_ref, b_ref, o_ref):
    s = _silu(c_ref[...]).astype(BF16)
    o_ref[...] = jnp.dot(s, w_ref[...].astype(BF16), preferred_element_type=F32) + b_ref[...]


def ada_mod(c, w_ada, b_ada):
    r, d = c.shape
    n = w_ada.shape[1]
    tn = _row_tile(n, 1024)
    return pl.pallas_call(
        _ada_kernel,
        grid=(n // tn,),
        in_specs=[pl.BlockSpec((r, d), lambda j: (0, 0)),
                  pl.BlockSpec((d, tn), lambda j: (0, j)),
                  pl.BlockSpec((1, tn), lambda j: (0, j))],
        out_specs=pl.BlockSpec((r, tn), lambda j: (0, j)),
        out_shape=jax.ShapeDtypeStruct((r, n), F32),
        compiler_params=_params("arbitrary"),
        name="ada_mod",
    )(c, w_ada, b_ada.reshape(1, n))


def _in_a_kernel(x_ref, nw_ref, sc_ref, sh_ref, w_ref, qn_ref, kn_ref,
                 h_ref, qa_ref, ka_ref, va_ref, kab_ref, vab_ref, qi_ref, ki_ref, kib_ref, wi_ref):
    h = _modulate(_rms(x_ref[...], nw_ref[...]), sc_ref[...], sh_ref[...])
    hb = h.astype(BF16)
    h_ref[...] = hb
    off = 0
    for hd in range(A_HEADS):
        q = jnp.dot(hb, w_ref[:, off:off + HEAD_DIM], preferred_element_type=F32)
        qa_ref[:, hd * HEAD_DIM:(hd + 1) * HEAD_DIM] = (_rms(q, qn_ref[...]) * (HEAD_DIM ** -0.5)).astype(BF16)
        off += HEAD_DIM
    for hd in range(A_KV_HEADS):
        k = jnp.dot(hb, w_ref[:, off:off + HEAD_DIM], preferred_element_type=F32)
        k = _rms(k, kn_ref[...])
        ka_ref[:, hd * HEAD_DIM:(hd + 1) * HEAD_DIM] = k
        kab_ref[:, hd * HEAD_DIM:(hd + 1) * HEAD_DIM] = k.astype(BF16)
        off += HEAD_DIM
    v = jnp.dot(hb, w_ref[:, off:off + A_KV_WIDTH], preferred_element_type=F32)
    va_ref[...] = v
    vab_ref[...] = v.astype(BF16)
    off += A_KV_WIDTH
    qi = jnp.dot(hb, w_ref[:, off:off + IDX_Q_WIDTH], preferred_element_type=F32)
    qi_ref[...] = (qi * (IDX_DIM ** -0.5)).astype(BF16)
    off += IDX_Q_WIDTH
    kw = jnp.dot(hb, w_ref[:, off:off + LANES], preferred_element_type=F32)
    ki = kw[:, :IDX_DIM]
    ki_ref[...] = ki
    kib_ref[...] = ki.astype(BF16)
    wi_ref[...] = kw


def in_proj_a(x, norm_w, scale_g, shift_g, w_a, q_norm, k_norm, tm):
    t, d = x.shape
    wa = w_a.shape[1]
    gm = tm // MOD_GROUP
    row = lambda w: pl.BlockSpec((tm, w), lambda i: (i, 0))
    const = lambda a, b: pl.BlockSpec((a, b), lambda i: (0, 0))
    outs = [(d, BF16), (A_WIDTH, BF16), (A_KV_WIDTH, F32), (A_KV_WIDTH, F32), (A_KV_WIDTH, BF16),
            (A_KV_WIDTH, BF16), (IDX_Q_WIDTH, BF16), (IDX_DIM, F32), (IDX_DIM, BF16), (LANES, F32)]
    return pl.pallas_call(
        _in_a_kernel,
        grid=(t // tm,),
        in_specs=[row(d), const(1, d), pl.BlockSpec((gm, d), lambda i: (i, 0)),
                  pl.BlockSpec((gm, d), lambda i: (i, 0)), const(d, wa),
                  const(1, HEAD_DIM), const(1, HEAD_DIM)],
        out_specs=[row(w) for w, _ in outs],
        out_shape=[jax.ShapeDtypeStruct((t, w), dt) for w, dt in outs],
        compiler_params=_params("arbitrary"),
        name="in_proj_a",
    )(x, norm_w, scale_g, shift_g, w_a, q_norm, k_norm)


def _in_b_kernel(h_ref, w_ref, qb_ref, kb_ref, vb_ref, kbb_ref, vbb_ref):
    hb = h_ref[...]
    qb_ref[...] = jnp.dot(hb, w_ref[:, :B_WIDTH], preferred_element_type=F32).astype(BF16)
    k = jnp.dot(hb, w_ref[:, B_WIDTH:2 * B_WIDTH], preferred_element_type=F32)
    kbb_ref[...] = k.astype(BF16)
    v = jnp.dot(hb, w_ref[:, 2 * B_WIDTH:], preferred_element_type=F32)
    vbb_ref[...] = v.astype(BF16)
    for hd in range(B_HEADS):
        kb_ref[:, hd, :] = k[:, hd * HEAD_DIM:(hd + 1) * HEAD_DIM]
        vb_ref[:, hd, :] = v[:, hd * HEAD_DIM:(hd + 1) * HEAD_DIM]


def in_proj_b(h, w_b, tm):
    t, d = h.shape
    row = lambda w: pl.BlockSpec((tm, w), lambda i: (i, 0))
    heads = pl.BlockSpec((tm, B_HEADS, HEAD_DIM), lambda i: (i, 0, 0))
    flat = lambda dt: jax.ShapeDtypeStruct((t, B_WIDTH), dt)
    per_head = jax.ShapeDtypeStruct((t, B_HEADS, HEAD_DIM), F32)
    return pl.pallas_call(
        _in_b_kernel,
        grid=(t // tm,),
        in_specs=[row(d), pl.BlockSpec((d, 3 * B_WIDTH), lambda i: (0, 0))],
        out_specs=[row(B_WIDTH), heads, heads, row(B_WIDTH), row(B_WIDTH)],
        out_shape=[flat(BF16), per_head, per_head, flat(BF16), flat(BF16)],
        compiler_params=_params("arbitrary"),
        name="in_proj_b",
    )(h, w_b)


def _score_key(x):
    bits = lax.bitcast_convert_type(x, jnp.int32)
    return bits ^ ((bits >> 31) & 0x7FFFFFFF)


def _lane_tiles(a):
    return [a[:, j * LANES:(j + 1) * LANES] for j in range(a.shape[1] // LANES)]


def _dsa_kernel(qa_ref, qi_ref, kw_ref, k_ref, v_ref, ki_ref, *rest, q_pos0, n_keys, n_sel, tq, tc, pos_bits,
                cached):
    if cached:
        ck_ref, cv_ref, cki_ref, o_ref, key_ref, m_ref, l_ref, acc_ref = rest
    else:
        o_ref, key_ref, m_ref, l_ref, acc_ref = rest

    def read_k(start, g, last):
        if cached and not last:
            return ck_ref[pl.ds(start, tc), g, :].astype(BF16)
        rows = slice(None) if cached else pl.ds(start, tc)
        return k_ref[rows, g * HEAD_DIM:(g + 1) * HEAD_DIM]

    def read_v(start, g, last):
        if cached and not last:
            return cv_ref[pl.ds(start, tc), g, :].astype(BF16)
        rows = slice(None) if cached else pl.ds(start, tc)
        return v_ref[rows, g * HEAD_DIM:(g + 1) * HEAD_DIM]

    def read_ki(start, last):
        if cached and not last:
            return cki_ref[pl.ds(start, tc), :].astype(BF16)
        return ki_ref[...] if cached else ki_ref[pl.ds(start, tc), :]

    i = pl.program_id(1)
    q0 = q_pos0 + i * tq
    kmax = jnp.minimum(((q0 + tq - 1) // CHUNK + 1) * CHUNK, n_keys)
    nch = (kmax + tc - 1) // tc
    qpos = q0 + lax.broadcasted_iota(jnp.int32, (tq, 1), 0)

    tqs = key_ref.shape[2]
    lane_q = lax.broadcasted_iota(jnp.int32, (1, tqs), 1)
    n_adm = jnp.minimum(((q0 + lane_q) // CHUNK + 1) * CHUNK, n_keys)
    pad_q = lambda a: a if tqs == tq else jnp.concatenate(
        [a, jnp.zeros((tqs - tq, a.shape[1]), a.dtype)], axis=0)
    head_w = pad_q(kw_ref[...]).T[IDX_DIM:IDX_DIM + IDX_HEADS] * (IDX_HEADS ** -0.5)
    qi_all = jnp.concatenate([pad_q(qi_ref[:, h * IDX_DIM:(h + 1) * IDX_DIM]) for h in range(IDX_HEADS)], axis=0)

    def all_chunks(chunk_pass):
        def body(c, carry):
            chunk_pass(c, False)
            return carry
        lax.fori_loop(0, nch - 1, body, 0)
        chunk_pass(nch - 1, True)

    def score_pass(c, last):
        start = pl.multiple_of(c * tc, tc)
        lg = lax.dot_general(read_ki(start, last), qi_all, NT_DIMS, preferred_element_type=F32)
        s = jnp.zeros((tc, tqs), F32)
        for h in range(IDX_HEADS):
            s = s + jnp.maximum(lg[:, h * tqs:(h + 1) * tqs], 0.0) * head_w[h:h + 1, :]
        adm = start + lax.broadcasted_iota(jnp.int32, (tc, 1), 0) < n_adm
        key_ref[c] = jnp.where(adm, _score_key(s), INT_MIN)

    all_chunks(score_pass)

    def count_rows(pred):
        def body(c, acc):
            start = pl.multiple_of(c * tc, tc)
            hit = jnp.where(pred(key_ref[c], start), 1.0, 0.0)
            return acc + jnp.sum(hit.reshape(tc // acc_rows, acc_rows, tqs), axis=0)
        acc_rows = 4 * SUBLANES
        acc = lax.fori_loop(0, nch, body, jnp.zeros((acc_rows, tqs), F32))
        return jnp.sum(acc, axis=0, keepdims=True)

    def count_ge(cand):
        return count_rows(lambda key, start: key >= cand)

    def pending(done):
        return jnp.min(done) < 0.5

    cnt = count_ge(jnp.zeros((1, tqs), jnp.int32))
    thr = jnp.where(cnt >= n_sel, 0, INT_MIN).astype(jnp.int32)
    done =jnp.where(jnp.logical_or(jnp.logical_or(n_adm <= n_sel, cnt == n_sel), lane_q >= tq), 1.0, 0.0)

    def bit_cond(state):
        b, _, done = state
        return jnp.logical_and(b < 31, pending(done))

    def bit_body(state):
        b, thr, done = state
        cand = thr | jnp.left_shift(jnp.int32(1), 30 - b)
        cnt = count_ge(cand)
        return (b + 1, jnp.where(cnt >= n_sel, cand, thr), jnp.where(cnt == n_sel, 1.0, done))

    _, thr, done = lax.while_loop(bit_cond, bit_body, (jnp.int32(0), thr, done))

    def tie_limit():
        need = n_sel - count_rows(lambda key, start: key > thr)

        def tied_below(limit):
            return count_rows(lambda key, start: jnp.logical_and(
                key == thr, start + lax.broadcasted_iota(jnp.int32, (tc, 1), 0) < limit))

        def body(b, lim):
            cand = lim | jnp.left_shift(jnp.int32(1), pos_bits - 1 - b)
            return jnp.where(tied_below(cand) < need, cand, lim)

        lim = lax.fori_loop(0, pos_bits, body, jnp.zeros((1, tqs), jnp.int32))
        return jnp.where(done > 0.5, jnp.int32(2 ** pos_bits), lim + 1)

    plim = lax.cond(pending(done), tie_limit, lambda: jnp.full((1, tqs), 2 ** pos_bits, jnp.int32))

    slopes = [2.0 ** (-8.0 * (h + 1) / A_HEADS) for h in range(A_HEADS)]
    lane = lax.broadcasted_iota(jnp.int32, (1, LANES), 1)
    rows4 = A_GROUP * tq
    q_aug = []
    for g in range(A_KV_HEADS):
        heads = range(g * A_GROUP, (g + 1) * A_GROUP)
        q4 = jnp.concatenate([qa_ref[:, h * HEAD_DIM:(h + 1) * HEAD_DIM] for h in heads], axis=0)
        slope_col = jnp.concatenate([jnp.full((tq, 1), slopes[h], F32) for h in heads], axis=0)
        q_aug.append(jnp.concatenate([q4, jnp.where(lane < 3, slope_col, 0.0).astype(BF16)], axis=1))

    def chunk_logits(c, last):
        start = pl.multiple_of(c * tc, tc)
        kp = start + lax.broadcasted_iota(jnp.int32, (tc, 1), 0)
        pos_cols = jnp.where(lane == 0, kp & 63,
                             jnp.where(lane == 1, kp & (127 << 6),
                                       jnp.where(lane == 2, kp & ~8191, 0))).astype(F32).astype(BF16)
        key = key_ref[c]
        sel = jnp.logical_or(key > thr, jnp.logical_and(key == thr, kp < plim))
        mb = jnp.where(jnp.logical_and(sel, key != INT_MIN), 0.0, NEG_BIG).T[:tq]
        kpos = start + lax.broadcasted_iota(jnp.int32, (1, tc), 1)
        out = []
        for g in range(A_KV_HEADS):
            kc = jnp.concatenate([read_k(start, g, last), pos_cols], axis=1)
            lg = lax.dot_general(q_aug[g], kc, NT_DIMS, preferred_element_type=F32).reshape(A_GROUP, tq, tc)
            if last:
                ahead = jnp.maximum(kpos - qpos, 0).astype(F32)
                bias = jnp.stack([mb - (2.0 * slopes[g * A_GROUP + r]) * ahead for r in range(A_GROUP)])
            else:
                bias = mb[None]
            out.append((lg + bias).reshape(rows4, tc))
        return start, out

    m_ref[...] = jnp.full(m_ref.shape, NEG_BIG, F32)
    l_ref[...] = jnp.zeros(l_ref.shape, F32)
    acc_ref[...] = jnp.zeros(acc_ref.shape, F32)

    def attend(c, last):
        start, logits = chunk_logits(c, last)
        for g in range(A_KV_HEADS):
            rows = slice(g * rows4, (g + 1) * rows4)
            tiles = _lane_tiles(logits[g])
            m_old = m_ref[rows]
            m_new = jnp.maximum(m_old, jnp.max(functools.reduce(jnp.maximum, tiles), axis=1, keepdims=True))
            alpha = jnp.exp(m_old - m_new)
            p = [jnp.exp(t - m_new) for t in tiles]
            m_ref[rows] = m_new
            l_ref[rows] = alpha * l_ref[rows] + functools.reduce(jnp.add, p)
            acc_ref[rows] = alpha * acc_ref[rows] + jnp.dot(jnp.concatenate(p, axis=1).astype(BF16),
                                                            read_v(start, g, last), preferred_element_type=F32)

    all_chunks(attend)
    for h in range(A_HEADS):
        rows = slice(h * tq, (h + 1) * tq)
        o_ref[:, h * HEAD_DIM:(h + 1) * HEAD_DIM] = (
            acc_ref[rows] / jnp.sum(l_ref[rows], axis=1, keepdims=True)).astype(BF16)


def dsa_attention(qa, qi, wi, k, v, ki, *, q_pos0, n_keys, tq, tc, caches=()):
    b, n, _ = qa.shape
    if caches:
        assert q_pos0 % tc == 0 and n == tq <= tc and k.shape[1] == tc and n_keys == q_pos0 + n
        l_pad = q_pos0 + tc
    else:
        l_pad = k.shape[1]
    n_sel = min(TOPK_KEYS, n_keys // 4)
    qspec = lambda w: pl.BlockSpec((None, tq, w), lambda bi, i: (bi, i, 0))
    kspec = lambda w: pl.BlockSpec((None, k.shape[1], w), lambda bi, i: (bi, 0, 0))
    cache_specs = [pl.BlockSpec((None,) + c.shape[1:], lambda bi, i, nd=c.ndim: (bi,) + (0,) * (nd - 1))
                   for c in caches]
    kern = functools.partial(_dsa_kernel, q_pos0=q_pos0, n_keys=n_keys, n_sel=n_sel, tq=tq, tc=tc,
                             pos_bits=max(1, (l_pad - 1).bit_length()), cached=bool(caches))
    return pl.pallas_call(
        kern,
        grid=(b, n // tq),
        in_specs=[qspec(A_WIDTH), qspec(IDX_Q_WIDTH), qspec(LANES),
                  kspec(A_KV_WIDTH), kspec(A_KV_WIDTH), kspec(IDX_DIM)] + cache_specs,
        out_specs=qspec(A_WIDTH),
        out_shape=jax.ShapeDtypeStruct((b, n, A_WIDTH), BF16),
        scratch_shapes=[pltpu.VMEM((l_pad // tc, tc, max(tq, LANES)), jnp.int32),
                        pltpu.VMEM((A_HEADS * tq, LANES), F32),
                        pltpu.VMEM((A_HEADS * tq, LANES), F32),
                        pltpu.VMEM((A_HEADS * tq, HEAD_DIM), F32)],
        compiler_params=_params("arbitrary", "arbitrary"),
        name="dsa",
    )(qa, qi, wi, k, v, ki, *caches)


STICK_HEADS_PER_STEP = 2


def _strict_upper(n):
    return (lax.broadcasted_iota(jnp.int32, (n, n), 0) > lax.broadcasted_iota(jnp.int32, (n, n), 1)).astype(BF16)


def _stick_step(q, kc, vc, strict, upper, tail, acc):
    z = lax.dot_general(q, kc, NT_DIMS, preferred_element_type=F32) * (HEAD_DIM ** -0.5)
    softplus = jnp.log(1.0 + jnp.exp(-jnp.abs(z)))
    log_go = jnp.minimum(z, 0.0) - softplus
    log_stay = jnp.where(strict, jnp.minimum(-z, 0.0) - softplus, 0.0)
    hi = log_stay.astype(BF16)
    lo = (log_stay - hi.astype(F32)).astype(BF16)
    after = jnp.dot(hi, upper, preferred_element_type=F32) + jnp.dot(lo, upper, preferred_element_type=F32) + tail
    w = jnp.where(strict, jnp.exp(log_go + after), 0.0)
    return (tail + jnp.sum(log_stay, axis=1, keepdims=True),
            acc + jnp.dot(w.astype(BF16), vc, preferred_element_type=F32))


def _stick_walk(qs, tpos, c_first, tc, read_kv, tails, accs):
    upper = _strict_upper(tc)

    def cond(carry):
        n, tails, _ = carry
        return jnp.logical_and(n <= c_first, jnp.max(functools.reduce(jnp.maximum, tails)) > STICK_UNDERFLOW)

    def body(carry):
        n, tails, accs = carry
        start = pl.multiple_of((c_first - n) * tc, tc)
        strict = start + lax.broadcasted_iota(jnp.int32, (1, tc), 1) < tpos
        new = [_stick_step(q, *read_kv(start, h), strict, upper, tail, acc)
               for h, (q, tail, acc) in enumerate(zip(qs, tails, accs))]
        return n + 1, tuple(t for t, _ in new), tuple(a for _, a in new)

    _, _, accs = lax.while_loop(cond, body, (jnp.int32(0), tuple(tails), tuple(accs)))
    return accs


def _stick_kernel(q_ref, k_ref, v_ref, o_ref, *, q_pos0, tq, tc):
    t0 = q_pos0 + pl.program_id(2) * tq
    tpos = t0 + lax.broadcasted_iota(jnp.int32, (tq, 1), 0)
    heads = [slice(h * HEAD_DIM, (h + 1) * HEAD_DIM) for h in range(STICK_HEADS_PER_STEP)]
    read_kv = lambda start, h: (k_ref[pl.ds(start, tc), heads[h]], v_ref[pl.ds(start, tc), heads[h]])
    zeros = lambda w: [jnp.zeros((tq, w), F32) for _ in heads]
    accs = _stick_walk([q_ref[:, hs] for hs in heads], tpos, (t0 + tq - 2) // tc, tc, read_kv,
                       zeros(1), zeros(HEAD_DIM))
    for hs, acc in zip(heads, accs):
        o_ref[:, hs] = acc.astype(BF16)


def _stick_cached_kernel(q_ref, k_ref, v_ref, ck_ref, cv_ref, o_ref, *, past, tc):
    n = q_ref.shape[0]
    tpos = past + lax.broadcasted_iota(jnp.int32, (n, 1), 0)
    strict = past + lax.broadcasted_iota(jnp.int32, (1, n), 1) < tpos
    upper = _strict_upper(n)
    for h0 in range(0, B_HEADS, STICK_HEADS_PER_STEP):
        group = range(h0, h0 + STICK_HEADS_PER_STEP)
        cols = [slice(h * HEAD_DIM, (h + 1) * HEAD_DIM) for h in group]
        qs = [q_ref[:, hs] for hs in cols]
        first = [_stick_step(q, k_ref[:, hs], v_ref[:, hs], strict, upper,
                             jnp.zeros((n, 1), F32), jnp.zeros((n, HEAD_DIM), F32)) for q, hs in zip(qs, cols)]
        read_kv = lambda start, h, h0=h0: (ck_ref[pl.ds(start, tc), h0 + h, :].astype(BF16),
                                           cv_ref[pl.ds(start, tc), h0 + h, :].astype(BF16))
        accs = _stick_walk(qs, tpos, past // tc - 1, tc, read_kv, [t for t, _ in first], [a for _, a in first])
        for hs, acc in zip(cols, accs):
            o_ref[:, hs] = acc.astype(BF16)


def stick_attention(qb, k, v, *, q_pos0, tq, tc):
    b, n, _ = qb.shape
    l_pad = k.shape[1]
    width = STICK_HEADS_PER_STEP * HEAD_DIM
    qspec = pl.BlockSpec((None, tq, width), lambda bi, h, i: (bi, i, h))
    kspec = pl.BlockSpec((None, l_pad, width), lambda bi, h, i: (bi, 0, h))
    kern = functools.partial(_stick_kernel, q_pos0=q_pos0, tq=tq, tc=tc)
    return pl.pallas_call(
        kern,
        grid=(b, B_HEADS // STICK_HEADS_PER_STEP, n // tq),
        in_specs=[qspec, kspec, kspec],
        out_specs=qspec,
        out_shape=jax.ShapeDtypeStruct((b, n, B_WIDTH), BF16),
        compiler_params=_params("arbitrary", "arbitrary", "arbitrary"),
        name="stick",
    )(qb, k, v)


def stick_attention_cached(qb, k_new, v_new, cache_k, cache_v, *, tc):
    b, n, _ = qb.shape
    past = cache_k.shape[1]
    assert past % tc == 0
    new = pl.BlockSpec((None, n, B_WIDTH), lambda bi: (bi, 0, 0))
    cache = pl.BlockSpec((None, past, B_HEADS, HEAD_DIM), lambda bi: (bi, 0, 0, 0))
    return pl.pallas_call(
        functools.partial(_stick_cached_kernel, past=past, tc=tc),
        grid=(b,),
        in_specs=[new, new, new, cache, cache],
        out_specs=new,
        out_shape=jax.ShapeDtypeStruct((b, n, B_WIDTH), BF16),
        compiler_params=_params("arbitrary"),
        name="stick_cached",
    )(qb, k_new, v_new, cache_k, cache_v)


def _merge_kernel(h_ref, ya_ref, yb_ref, wga_ref, wgb_ref, wa_ref, wb_ref, o_ref):
    hb = h_ref[...]
    ga = jax.nn.sigmoid(jnp.dot(hb, wga_ref[...], preferred_element_type=F32))
    gb = jax.nn.sigmoid(jnp.dot(hb, wgb_ref[...], preferred_element_type=F32))
    pa = jnp.dot(ya_ref[...], wa_ref[...], preferred_element_type=F32)
    pb = jnp.dot(yb_ref[...], wb_ref[...], preferred_element_type=F32)
    o_ref[...] = (ga * pa + gb * pb).astype(BF16)


def merge_branches(h, ya, yb, w_ga, w_gb, w_a, w_b, tm, tn):
    t, d = h.shape
    row = lambda w: pl.BlockSpec((tm, w), lambda i, j: (i, 0))
    col = lambda k: pl.BlockSpec((k, tn), lambda i, j: (0, j))
    return pl.pallas_call(
        _merge_kernel,
        grid=(t // tm, d // tn),
        in_specs=[row(d), row(A_WIDTH), row(B_WIDTH), col(d), col(d), col(A_WIDTH), col(B_WIDTH)],
        out_specs=pl.BlockSpec((tm, tn), lambda i, j: (i, j)),
        out_shape=jax.ShapeDtypeStruct((t, d), BF16),
        compiler_params=_params("arbitrary", "arbitrary"),
        name="merge",
    )(h, ya, yb, w_ga, w_gb, w_a, w_b)


def _pack_bf16_pair(lo, hi):
    lo_bits = lax.bitcast_convert_type(lo.astype(BF16).astype(F32), jnp.uint32)
    hi_bits = lax.bitcast_convert_type(hi.astype(BF16).astype(F32), jnp.uint32)
    return (lo_bits >> 16) | (hi_bits & jnp.uint32(0xFFFF0000))


def _unpack_f32_pair(u):
    return (lax.bitcast_convert_type(u << 16, F32),
            lax.bitcast_convert_type(u & jnp.uint32(0xFFFF0000), F32))


def _unpack_bf16_pair(u):
    lo, hi = _unpack_f32_pair(u)
    return lo.astype(BF16), hi.astype(BF16)


def _first_max(vals, idx, axes, n):
    m = vals
    for ax in axes:
        m = jnp.max(m, axis=ax, keepdims=True)
    first = jnp.where(vals == m, idx, n)
    for ax in axes:
        first = jnp.min(first, axis=ax, keepdims=True)
    return m, first


def _sum01(a):
    return jnp.sum(jnp.sum(a, axis=0, keepdims=True), axis=1, keepdims=True)


def _out_kernel(mg_ref, x_ref, g1_ref, sc_ref, sh_ref, nw_ref, wo_ref, wr_ref, br_ref, cnt0_ref,
                x1_ref, hu_ref, tope_ref, rank_ref, gatek_ref, cnt_out_ref, cnt_ref):
    tm, d = x_ref.shape
    proj = jnp.dot(mg_ref[...], wo_ref[...], preferred_element_type=F32)
    x1 = x_ref[...] + _gate_rows(proj, g1_ref[...])
    x1_ref[...] = x1
    h2 = _modulate(_rms(x1, nw_ref[...]), sc_ref[...], sh_ref[...])
    hu_ref[...] = _pack_bf16_pair(h2[:, :d // 2], h2[:, d // 2:])

    logits = lax.dot_general(wr_ref[...], h2.astype(BF16), NT_DIMS, preferred_element_type=F32)
    aff = jax.nn.sigmoid(logits)
    shape3 = (N_GROUPS, GROUP_SIZE, tm)
    aff3 = aff.reshape(shape3)
    biased = (aff + br_ref[...]).reshape(shape3)
    io_in = lax.broadcasted_iota(jnp.int32, shape3, 1)
    m1, i1 = _first_max(biased, io_in, (1,), GROUP_SIZE)
    m2 = jnp.max(jnp.where(io_in == i1, -jnp.inf, biased), axis=1, keepdims=True)
    gscore = m1 + m2
    gio = lax.broadcasted_iota(jnp.int32, gscore.shape, 0)
    keep = jnp.zeros(gscore.shape, jnp.bool_)
    for _ in range(N_ACTIVE_GROUPS):
        _, first = _first_max(gscore, gio, (0,), N_GROUPS)
        hit = gio == first
        keep = jnp.logical_or(keep, hit)
        gscore = jnp.where(hit, -jnp.inf, gscore)
    masked = jnp.where(jnp.broadcast_to(keep, shape3), biased, -jnp.inf)
    eio = lax.broadcasted_iota(jnp.int32, shape3, 0) * GROUP_SIZE + io_in
    sel = jnp.zeros(shape3, jnp.bool_)
    hits = []
    for _ in range(N_ACTIVE):
        _, first = _first_max(masked, eio, (0, 1), N_EXPERTS)
        hit = eio == first
        sel = jnp.logical_or(sel, hit)
        masked = jnp.where(hit, -jnp.inf, masked)
        hits.append((hit, first.reshape(1, tm)))
    sel_aff = jnp.where(sel, aff3, 0.0)
    gates = sel_aff / _sum01(sel_aff) * ROUTED_SCALE

    @pl.when(pl.program_id(0) == 0)
    def _():
        cnt_ref[...] = cnt0_ref[...]

    sel2 = jnp.where(sel, 1.0, 0.0).reshape(N_EXPERTS, tm)
    incl = (lax.broadcasted_iota(jnp.int32, (tm, tm), 0)
            <= lax.broadcasted_iota(jnp.int32, (tm, tm), 1)).astype(BF16)
    rank = jnp.dot(sel2.astype(BF16), incl, preferred_element_type=F32) + cnt_ref[...]
    cnt_ref[...] = cnt_ref[...] + jnp.sum(sel2, axis=1, keepdims=True)
    cnt_out_ref[...] = cnt_ref[...]
    rank3 = rank.reshape(shape3)
    tope_ref[...] = jnp.concatenate([first for _, first in hits], axis=0)
    rank_ref[...] = jnp.concatenate(
        [_sum01(jnp.where(hit, rank3, 0.0)).reshape(1, tm) for hit, _ in hits], axis=0).astype(jnp.int32)
    gatek_ref[...] = jnp.concatenate(
        [_sum01(jnp.where(hit, gates, 0.0)).reshape(1, tm) for hit, _ in hits], axis=0)


def out_proj(merged, x, gate1_g, scale_g, shift_g, norm_w, w_out, w_router_t, b_router, count0, tm):
    t, d = x.shape
    gm = tm // MOD_GROUP
    row = lambda w: pl.BlockSpec((tm, w), lambda i: (i, 0))
    grp = pl.BlockSpec((gm, d), lambda i: (i, 0))
    const = lambda a, b: pl.BlockSpec((a, b), lambda i: (0, 0))
    per_k = pl.BlockSpec((N_ACTIVE, tm), lambda i: (0, i))
    return pl.pallas_call(
        _out_kernel,
        grid=(t // tm,),
        in_specs=[row(d), row(d), grp, grp, grp, const(1, d), const(d, d),
                  const(N_EXPERTS, d), const(N_EXPERTS, 1), const(N_EXPERTS, 1)],
        out_specs=[row(d), row(d // 2), per_k, per_k, per_k, const(N_EXPERTS, 1)],
        out_shape=[jax.ShapeDtypeStruct((t, d), F32), jax.ShapeDtypeStruct((t, d // 2), jnp.uint32),
                   jax.ShapeDtypeStruct((N_ACTIVE, t), jnp.int32), jax.ShapeDtypeStruct((N_ACTIVE, t), jnp.int32),
                   jax.ShapeDtypeStruct((N_ACTIVE, t), F32), jax.ShapeDtypeStruct((N_EXPERTS, 1), F32)],
        scratch_shapes=[pltpu.VMEM((N_EXPERTS, 1), F32)],
        compiler_params=_params("arbitrary"),
        name="out_proj",
    )(merged, x, gate1_g, scale_g, shift_g, norm_w, w_out, w_router_t, b_router, count0)


def _pad_fill_kernel(start_ref, cnt_ref, nu_ref, xs_ref, zero_ref, sem, *, tme, n_tiles):
    zero_ref[...] = jnp.zeros(zero_ref.shape, zero_ref.dtype)

    def tile_copy(i):
        return pltpu.make_async_copy(zero_ref, xs_ref.at[pl.ds(pl.multiple_of(i * tme, tme), tme)], sem.at[1])

    def for_each_copy(wait):
        def pad_rows(e, carry):
            pos = start_ref[e]
            for bit in range(tme.bit_length() - 1):
                size = 1 << bit

                @pl.when((cnt_ref[e] & size) != 0)
                def _(pos=pos, size=size, bit=bit):
                    copy = pltpu.make_async_copy(zero_ref.at[pl.ds(0, size)],
                                                 xs_ref.at[pl.ds(pl.multiple_of(pos, size), size)], sem.at[2 + bit])
                    copy.wait() if wait else copy.start()

                pos = pos + (cnt_ref[e] & size)
            return carry

        def tail_tiles(i, carry):
            tile_copy(i).wait() if wait else tile_copy(i).start()
            return carry

        lax.fori_loop(0, N_EXPERTS, pad_rows, 0)
        lax.fori_loop(nu_ref[0], n_tiles, tail_tiles, 0)

    for_each_copy(wait=False)
    for_each_copy(wait=True)


def pad_fill(pad_start, pad_count, n_used, p_rows, row_shape, tme):
    grid_spec = pltpu.PrefetchScalarGridSpec(
        num_scalar_prefetch=3, grid=(1,), in_specs=[],
        out_specs=pl.BlockSpec(memory_space=pl.ANY),
        scratch_shapes=[pltpu.VMEM((tme,) + row_shape, jnp.uint32),
                        pltpu.SemaphoreType.DMA((2 + tme.bit_length(),))])
    return pl.pallas_call(
        functools.partial(_pad_fill_kernel, tme=tme, n_tiles=p_rows // tme),
        grid_spec=grid_spec,
        out_shape=jax.ShapeDtypeStruct((p_rows,) + row_shape, jnp.uint32),
        compiler_params=_params("arbitrary"),
        name="pad_fill",
    )(pad_start, pad_count, n_used)


def _dispatch_kernel(slot_ref, hu_ref, xs_in_ref, xs_ref, sem, *, td):
    del xs_in_ref

    def row_copy(j, k):
        return pltpu.make_async_copy(hu_ref.at[pl.ds(j, 1)],
                                     xs_ref.at[pl.ds(slot_ref[j * N_ACTIVE + k], 1)], sem)

    def start(jj, carry):
        base = pl.multiple_of(jj * SUBLANES, SUBLANES)
        for r in range(SUBLANES):
            for k in range(N_ACTIVE):
                row_copy(base + r, k).start(priority=k % 2)
        return carry

    def wait(j, carry):
        for k in range(N_ACTIVE):
            row_copy(j, k).wait()
        return carry

    lax.fori_loop(0, td // SUBLANES, start, 0)
    lax.fori_loop(0, td, wait, 0)


def dispatch(slots_flat, hu, xs, td):
    t = hu.shape[0]
    return pl.pallas_call(
        functools.partial(_dispatch_kernel, td=td),
        grid=(t // td,),
        in_specs=[pl.BlockSpec((td * N_ACTIVE,), lambda i: (i,), memory_space=pltpu.SMEM),
                  pl.BlockSpec((td,) + hu.shape[1:], lambda i: (i, 0)),
                  pl.BlockSpec(memory_space=pl.ANY)],
        out_specs=pl.BlockSpec(memory_space=pl.ANY),
        out_shape=jax.ShapeDtypeStruct(xs.shape, xs.dtype),
        scratch_shapes=[pltpu.SemaphoreType.DMA(())],
        input_output_aliases={2: 0},
        compiler_params=_params("arbitrary"),
        name="dispatch",
    )(slots_flat, hu, xs)


def _expert_kernel(te_ref, nu_ref, first_ref, next_ref, par_ref, xs_ref, wg_ref, wu_ref, wd_ref, o_ref,
                   wgf_ref, wuf_ref, wdf_ref, wgb_ref, wub_ref, wdb_ref, sem):
    i = pl.program_id(0)
    used = i < nu_ref[0]

    def fetch(e, buf):
        return [pltpu.make_async_copy(w.at[e], f.at[buf], sem.at[buf, j])
                for j, (w, f) in enumerate(((wg_ref, wgf_ref), (wu_ref, wuf_ref), (wd_ref, wdf_ref)))]

    @pl.when(i == 0)
    def _():
        for copy in fetch(te_ref[0], par_ref[0]):
            copy.start()

    @pl.when(jnp.logical_and(used, first_ref[i] == 1))
    def _():
        buf = par_ref[i]
        for copy in fetch(te_ref[i], buf):
            copy.wait()

        @pl.when(next_ref[i] >= 0)
        def _():
            for copy in fetch(next_ref[i], 1 - buf):
                copy.start()

        wgb_ref[...] = wgf_ref[buf].astype(BF16)
        wub_ref[...] = wuf_ref[buf].astype(BF16)
        wdb_ref[...] = wdf_ref[buf].astype(BF16)

    @pl.when(used)
    def _():
        lo, hi = _unpack_bf16_pair(xs_ref[...])
        half = lo.shape[1]
        gate = (jnp.dot(lo, wgb_ref[:half], preferred_element_type=F32)
                + jnp.dot(hi, wgb_ref[half:], preferred_element_type=F32))
        up = (jnp.dot(lo, wub_ref[:half], preferred_element_type=F32)
              + jnp.dot(hi, wub_ref[half:], preferred_element_type=F32))
        hid = (_silu(gate) * up).astype(BF16)
        o_ref[...] = jnp.dot(hid, wdb_ref[...], preferred_element_type=F32)

    @pl.when(jnp.logical_not(used))
    def _():
        o_ref[...] = jnp.zeros(o_ref.shape, o_ref.dtype)


def expert_ffn(tiles, xs, w_gate, w_up, w_down, tme):
    p, half = xs.shape
    e, d, f = w_gate.shape
    hbm = pl.BlockSpec(memory_space=pl.ANY)
    grid_spec = pltpu.PrefetchScalarGridSpec(
        num_scalar_prefetch=5,
        grid=(p // tme,),
        in_specs=[pl.BlockSpec((tme, half), lambda i, te, nu, *_: (jnp.minimum(i, nu[0] - 1), 0)),
                  hbm, hbm, hbm],
        out_specs=pl.BlockSpec((tme, d), lambda i, *_: (i, 0)),
        scratch_shapes=[pltpu.VMEM((2, d, f), F32), pltpu.VMEM((2, d, f), F32), pltpu.VMEM((2, f, d), F32),
                        pltpu.VMEM((d, f), BF16), pltpu.VMEM((d, f), BF16), pltpu.VMEM((f, d), BF16),
                        pltpu.SemaphoreType.DMA((2, 3))],
    )
    return pl.pallas_call(
        _expert_kernel,
        grid_spec=grid_spec,
        out_shape=jax.ShapeDtypeStruct((p, d), F32),
        compiler_params=_params("arbitrary"),
        name="expert_ffn",
    )(tiles["expert"], tiles["n_used"], tiles["first"], tiles["next"], tiles["parity"], xs, w_gate, w_up, w_down)


def _combine_kernel(slot_ref, hu_ref, x1_ref, g2_ref, gk_ref, wg_ref, wu_ref, wd_ref, ys_ref,
                    o_ref, rows_ref, sem, *, tmc):
    def row_copy(j, k):
        return pltpu.make_async_copy(ys_ref.at[pl.ds(slot_ref[j * N_ACTIVE + k], 1)],
                                     rows_ref.at[k, pl.ds(j, 1)], sem)

    def start(jj, carry):
        base = pl.multiple_of(jj * SUBLANES, SUBLANES)
        for r in range(SUBLANES):
            for k in range(N_ACTIVE):
                row_copy(base + r, k).start(priority=k % 2)
        return carry

    def wait(j, carry):
        for k in range(N_ACTIVE):
            row_copy(j, k).wait()
        return carry

    lax.fori_loop(0, tmc // SUBLANES, start, 0)
    lo, hi = _unpack_bf16_pair(hu_ref[...])
    half = lo.shape[1]
    gate = (jnp.dot(lo, wg_ref[:half], preferred_element_type=F32)
            + jnp.dot(hi, wg_ref[half:], preferred_element_type=F32))
    up = (jnp.dot(lo, wu_ref[:half], preferred_element_type=F32)
          + jnp.dot(hi, wu_ref[half:], preferred_element_type=F32))
    y = jnp.dot((_silu(gate) * up).astype(BF16), wd_ref[...], preferred_element_type=F32)
    lax.fori_loop(0, tmc, wait, 0)
    gk = gk_ref[...]
    for k in range(N_ACTIVE):
        y = y + gk[:, k:k + 1] * rows_ref[k]
    o_ref[...] = x1_ref[...] + _gate_rows(y, g2_ref[...])


def combine(slots_flat, hu, x1, gate2_g, gate_k, w_gs, w_us, w_ds, ys, tmc):
    t, d = x1.shape
    f = w_gs.shape[1]
    gm = tmc // MOD_GROUP
    row = lambda w: pl.BlockSpec((tmc, w), lambda i: (i, 0))
    const = lambda a, b: pl.BlockSpec((a, b), lambda i: (0, 0))
    return pl.pallas_call(
        functools.partial(_combine_kernel, tmc=tmc),
        grid=(t // tmc,),
        in_specs=[pl.BlockSpec((tmc * N_ACTIVE,), lambda i: (i,), memory_space=pltpu.SMEM),
                  pl.BlockSpec((tmc,) + hu.shape[1:], lambda i: (i, 0)),
                  row(d), pl.BlockSpec((gm, d), lambda i: (i, 0)), row(N_ACTIVE),
                  const(d, f), const(d, f), const(f, d),
                  pl.BlockSpec(memory_space=pl.ANY)],
        out_specs=row(d),
        out_shape=jax.ShapeDtypeStruct((t, d), F32),
        scratch_shapes=[pltpu.VMEM((N_ACTIVE, tmc, d), F32), pltpu.SemaphoreType.DMA(())],
        compiler_params=_params("arbitrary"),
        name="combine",
    )(slots_flat, hu, x1, gate2_g, gate_k, w_gs, w_us, w_ds, ys)


def _group_rows(v, n):
    b, d = v.shape
    return jnp.broadcast_to(v[:, None, :], (b, n // MOD_GROUP, d)).reshape(b * n // MOD_GROUP, d)


def _pad_keys(a, l_pad):
    return jnp.pad(a, ((0, 0), (0, l_pad - a.shape[1]), (0, 0)))


def _token_stage(x, mod, w, caches, count0):
    b, n, d = x.shape
    t = b * n
    shift1, scale1, gate1, shift2, scale2, gate2 = [_group_rows(m, n) for m in jnp.split(mod, 6, axis=-1)]
    xf = x.reshape(t, d)
    tm = _row_tile(t, 512)
    (h, qa, ka, va, kab, vab, qi, ki, kib, wi) = in_proj_a(
        xf, w["norm_mix"], scale1, shift1, w["w_in_a"], w["q_norm_a"], w["k_norm_a"], tm)
    qb, kb, vb, kbb, vbb = in_proj_b(h, w["w_in_b"], tm)

    r3 = lambda a: a.reshape(b, n, a.shape[-1])
    if caches is None:
        ya = dsa_attention(r3(qa), r3(qi), r3(wi), r3(kab), r3(vab), r3(kib),
                           q_pos0=0, n_keys=n, tq=min(256, n), tc=min(512, n))
        yb = stick_attention(r3(qb), r3(kbb), r3(vbb), q_pos0=0, tq=min(256, n), tc=min(256, n))
    else:
        cka, cva, cki, ckb, cvb = caches
        past = cka.shape[1]
        tc_a, tc_b = min(512, past), min(256, past)
        new_chunk = lambda a: _pad_keys(r3(a), tc_a)
        ya = dsa_attention(r3(qa), r3(qi), r3(wi), new_chunk(kab), new_chunk(vab), new_chunk(kib),
                           q_pos0=past, n_keys=past + n, tq=n, tc=tc_a, caches=(cka, cva, cki))
        yb = stick_attention_cached(r3(qb), r3(kbb), r3(vbb), ckb, cvb, tc=tc_b)

    merged = merge_branches(h, ya.reshape(t, A_WIDTH), yb.reshape(t, B_WIDTH),
                            w["w_ga"], w["w_gb"], w["w_branch_a"], w["w_branch_b"], tm, _row_tile(d, 512))
    x1, hu, top_e, rank_k, gate_k, counts = out_proj(
        merged, xf, gate1, scale2, shift2, w["norm_ffn"], w["w_out"], w["w_router_t"], w["b_router"], count0, tm)
    rows = (ka.reshape(b, n, A_KV_HEADS, HEAD_DIM), va.reshape(b, n, A_KV_HEADS, HEAD_DIM),
            ki.reshape(b, n, IDX_DIM), kb.reshape(b, n, B_HEADS, HEAD_DIM), vb.reshape(b, n, B_HEADS, HEAD_DIM))
    return dict(x1=x1, hu=hu, top_e=top_e, rank_k=rank_k, gate_k=gate_k, counts=counts, gate2=gate2, rows=rows)


def _routing(counts, top_e, rank_k, tme):
    e = counts.shape[0]
    t = top_e.shape[1]
    counts = counts.reshape(e).astype(jnp.int32)
    padded = (counts + tme - 1) // tme * tme
    ends = jnp.cumsum(padded)
    offsets = ends - padded
    eids = jnp.arange(e, dtype=jnp.int32)[:, None, None]
    offset_k = jnp.sum(jnp.where(top_e[None] == eids, offsets[:, None, None], 0), axis=0)
    slots = offset_k + rank_k - 1
    n_tiles = (t * N_ACTIVE + e * tme) // tme
    tile_start = jnp.arange(n_tiles, dtype=jnp.int32) * tme
    tile_expert = jnp.minimum(jnp.sum(ends[None, :] <= tile_start[:, None], axis=1), e - 1).astype(jnp.int32)
    n_used = (ends[-1] // tme).astype(jnp.int32).reshape(1)
    ids = jnp.arange(e, dtype=jnp.int32)
    nonempty = padded > 0
    later = jnp.where((ids[None, :] > ids[:, None]) & nonempty[None, :], ids[None, :], e)
    next_e = jnp.min(later, axis=1)
    next_e = jnp.where(next_e == e, -1, next_e).astype(jnp.int32)
    rank_e = jnp.cumsum(nonempty.astype(jnp.int32)) - 1
    tiles = dict(expert=tile_expert, n_used=n_used,
                 first=(tile_start == offsets[tile_expert]).astype(jnp.int32),
                 next=next_e[tile_expert], parity=(rank_e[tile_expert] % 2).astype(jnp.int32))
    return dict(slots=slots.T.reshape(-1), tiles=tiles, n_used=n_used, p_rows=n_tiles * tme,
                pad_start=offsets + counts, pad_count=padded - counts)


def kernel(x_prompt, x_sample, cache_a_k, cache_a_v, cache_a_idx_k, cache_b_k, cache_b_v, c_prompt, c_sample,
           norm_mix, norm_ffn, w_ada, b_ada, w_in, q_norm_a, k_norm_a, w_branch_a, w_branch_b, w_out,
           w_router, b_router, w_gate_e, w_up_e, w_down_e, w_gate_s, w_up_s, w_down_s):
    depth = norm_mix.shape[0]
    d = x_prompt.shape[-1]
    bp, bs = c_prompt.shape[0], c_sample.shape[0]
    c_all = jnp.concatenate([c_prompt, c_sample], axis=0)
    c_all = jnp.pad(c_all, ((0, -(bp + bs) % 16), (0, 0)))
    tme = 256

    xp, xs = x_prompt, x_sample
    prompt_rows, sample_rows = [], []
    for layer in range(depth):
        wl = w_in[layer]
        o_idx = A_WIDTH + 2 * A_KV_WIDTH + IDX_Q_WIDTH
        o_b = o_idx + IDX_DIM + IDX_HEADS
        o_g = o_b + 3 * B_WIDTH
        w_in_a = jnp.concatenate(
            [wl[:, :o_b], jnp.zeros((d, LANES - IDX_DIM - IDX_HEADS), wl.dtype)], axis=1).astype(BF16)
        w = dict(
            norm_mix=norm_mix[layer].reshape(1, d), norm_ffn=norm_ffn[layer].reshape(1, d),
            q_norm_a=q_norm_a[layer].reshape(1, HEAD_DIM), k_norm_a=k_norm_a[layer].reshape(1, HEAD_DIM),
            w_in_a=w_in_a, w_in_b=wl[:, o_b:o_g].astype(BF16),
            w_ga=wl[:, o_g:o_g + d].astype(BF16), w_gb=wl[:, o_g + d:].astype(BF16),
            w_branch_a=w_branch_a[layer].astype(BF16), w_branch_b=w_branch_b[layer].astype(BF16),
            w_out=w_out[layer].astype(BF16), w_router_t=w_router[layer].T.astype(BF16),
            b_router=b_router[layer].reshape(N_EXPERTS, 1))
        mod = ada_mod(c_all, w_ada[layer], b_ada[layer])
        caches = (cache_a_k[layer], cache_a_v[layer], cache_a_idx_k[layer], cache_b_k[layer], cache_b_v[layer])
        sp = _token_stage(xp, mod[:bp], w, None, jnp.zeros((N_EXPERTS, 1), F32))
        ss = _token_stage(xs, mod[bp:bp + bs], w, caches, sp["counts"])
        prompt_rows.append(sp["rows"])
        sample_rows.append(ss["rows"])

        tp, ts = sp["x1"].shape[0], ss["x1"].shape[0]
        both = lambda name: jnp.concatenate([sp[name], ss[name]], axis=1)
        rt = _routing(ss["counts"], both("top_e"), both("rank_k"), tme)
        slots, gate_k = rt["slots"], both("gate_k").T
        xsorted = pad_fill(rt["pad_start"], rt["pad_count"], rt["n_used"], rt["p_rows"],
                           sp["hu"].shape[1:], tme)
        xsorted = dispatch(slots[:tp * N_ACTIVE], sp["hu"], xsorted, _row_tile(tp, 256))
        xsorted = dispatch(slots[tp * N_ACTIVE:], ss["hu"], xsorted, _row_tile(ts, 256))
        ysorted = expert_ffn(rt["tiles"], xsorted, w_gate_e[layer], w_up_e[layer], w_down_e[layer], tme)
        shared = (w_gate_s[layer].astype(BF16), w_up_s[layer].astype(BF16), w_down_s[layer].astype(BF16))
        tmc = 256
        yp = combine(slots[:tp * N_ACTIVE], sp["hu"], sp["x1"], sp["gate2"], gate_k[:tp], *shared, ysorted,
                     _row_tile(tp, tmc))
        ys = combine(slots[tp * N_ACTIVE:], ss["hu"], ss["x1"], ss["gate2"], gate_k[tp:], *shared, ysorted,
                     _row_tile(ts, tmc))
        xp = yp.reshape(xp.shape)
        xs = ys.reshape(xs.shape)

    stack = lambda rows: tuple(jnp.stack(r) for r in zip(*rows))
    return (xp, xs) + stack(prompt_rows) + stack(sample_rows)
```

```python
import functools

import jax
import jax.numpy as jnp
from jax import lax
from jax.experimental import pallas as pl
from jax.experimental.pallas import tpu as pltpu

CHUNK = 64
HEAD_DIM = 128
A_HEADS = 8
A_KV_HEADS = 2
A_GROUP = A_HEADS // A_KV_HEADS
IDX_HEADS = 8
IDX_DIM = 64
TOPK_KEYS = 256
B_HEADS = 8
N_EXPERTS = 64
N_ACTIVE = 8
N_GROUPS = 8
N_ACTIVE_GROUPS = 4
GROUP_SIZE = N_EXPERTS // N_GROUPS
ROUTED_SCALE = 2.5
EPS = 1e-6

A_WIDTH = A_HEADS * HEAD_DIM
A_KV_WIDTH = A_KV_HEADS * HEAD_DIM
IDX_Q_WIDTH = IDX_HEADS * IDX_DIM
B_WIDTH = B_HEADS * HEAD_DIM

LANES = 128
SUBLANES = 8
MOD_GROUP = 32
VMEM_LIMIT = 56 * 1024 * 1024
NEG_BIG = -1e30
STICK_UNDERFLOW = -110.0
INT_MIN = -(2 ** 31)

TM_DENSE = 512
TN_MERGE = 512
TN_ADA = 1024
TQ_DSA, TC_DSA = 256, 512
TQ_STICK, TC_STICK = 256, 256
TM_EXPERT = 256
TM_ROWS = 256

F32 = jnp.float32
BF16 = jnp.bfloat16
NT_DIMS = (((1,), (1,)), ((), ()))


def _params(*sem):
    return pltpu.CompilerParams(dimension_semantics=sem, vmem_limit_bytes=VMEM_LIMIT)


def _silu(x):
    return x * jax.nn.sigmoid(x)


def _rms(x, gain):
    return x * lax.rsqrt(jnp.mean(x * x, axis=-1, keepdims=True) + EPS) * gain


def _modulate(y, scale_g, shift_g):
    tm, d = y.shape
    y3 = y.reshape(tm // MOD_GROUP, MOD_GROUP, d)
    out = y3 * (1.0 + scale_g[:, None, :]) + shift_g[:, None, :]
    return out.reshape(tm, d)


def _gate_rows(y, gate_g):
    tm, d = y.shape
    return (y.reshape(tm // MOD_GROUP, MOD_GROUP, d) * gate_g[:, None, :]).reshape(tm, d)


def _row_tile(t, pref):
    tm = min(pref, t)
    while t % tm:
        tm //= 2
    return tm


def _ada_kernel(c_ref, w_ref, b_ref, o_ref):
    s = _silu(c_ref[...]).astype(BF16)
    o_ref[...] = jnp.dot(s, w_ref[...].astype(BF16), preferred_element_type=F32) + b_ref[...]


def ada_mod(c, w_ada, b_ada):
    r, d = c.shape
    n = w_ada.shape[1]
    tn = _row_tile(n, TN_ADA)
    return pl.pallas_call(
        _ada_kernel,
        grid=(n // tn,),
        in_specs=[pl.BlockSpec((r, d), lambda j: (0, 0)),
                  pl.BlockSpec((d, tn), lambda j: (0, j)),
                  pl.BlockSpec((1, tn), lambda j: (0, j))],
        out_specs=pl.BlockSpec((r, tn), lambda j: (0, j)),
        out_shape=jax.ShapeDtypeStruct((r, n), F32),
        compiler_params=_params("arbitrary"),
        name="ada_mod",
    )(c, w_ada, b_ada.reshape(1, n))


def _in_a_kernel(x_ref, nw_ref, sc_ref, sh_ref, w_ref, qn_ref, kn_ref,
                 h_ref, qa_ref, ka_ref, va_ref, kab_ref, vab_ref, qi_ref, ki_ref, kib_ref, wi_ref):
    h = _modulate(_rms(x_ref[...], nw_ref[...]), sc_ref[...], sh_ref[...])
    hb = h.astype(BF16)
    h_ref[...] = hb
    off = 0
    for hd in range(A_HEADS):
        q = jnp.dot(hb, w_ref[:, off:off + HEAD_DIM], preferred_element_type=F32)
        qa_ref[:, hd * HEAD_DIM:(hd + 1) * HEAD_DIM] = (_rms(q, qn_ref[...]) * (HEAD_DIM ** -0.5)).astype(BF16)
        off += HEAD_DIM
    for hd in range(A_KV_HEADS):
        k = jnp.dot(hb, w_ref[:, off:off + HEAD_DIM], preferred_element_type=F32)
        k = _rms(k, kn_ref[...])
        ka_ref[:, hd * HEAD_DIM:(hd + 1) * HEAD_DIM] = k
        kab_ref[:, hd * HEAD_DIM:(hd + 1) * HEAD_DIM] = k.astype(BF16)
        off += HEAD_DIM
    v = jnp.dot(hb, w_ref[:, off:off + A_KV_WIDTH], preferred_element_type=F32)
    va_ref[...] = v
    vab_ref[...] = v.astype(BF16)
    off += A_KV_WIDTH
    qi = jnp.dot(hb, w_ref[:, off:off + IDX_Q_WIDTH], preferred_element_type=F32)
    qi_ref[...] = (qi * (IDX_DIM ** -0.5)).astype(BF16)
    off += IDX_Q_WIDTH
    kw = jnp.dot(hb, w_ref[:, off:off + LANES], preferred_element_type=F32)
    ki = kw[:, :IDX_DIM]
    ki_ref[...] = ki
    kib_ref[...] = ki.astype(BF16)
    wi_ref[...] = kw


def in_proj_a(x, norm_w, scale_g, shift_g, w_a, q_norm, k_norm, tm):
    t, d = x.shape
    wa = w_a.shape[1]
    gm = tm // MOD_GROUP
    row = lambda w: pl.BlockSpec((tm, w), lambda i: (i, 0))
    const = lambda a, b: pl.BlockSpec((a, b), lambda i: (0, 0))
    outs = [(d, BF16), (A_WIDTH, BF16), (A_KV_WIDTH, F32), (A_KV_WIDTH, F32), (A_KV_WIDTH, BF16),
            (A_KV_WIDTH, BF16), (IDX_Q_WIDTH, BF16), (IDX_DIM, F32), (IDX_DIM, BF16), (LANES, F32)]
    return pl.pallas_call(
        _in_a_kernel,
        grid=(t // tm,),
        in_specs=[row(d), const(1, d), pl.BlockSpec((gm, d), lambda i: (i, 0)),
                  pl.BlockSpec((gm, d), lambda i: (i, 0)), const(d, wa),
                  const(1, HEAD_DIM), const(1, HEAD_DIM)],
        out_specs=[row(w) for w, _ in outs],
        out_shape=[jax.ShapeDtypeStruct((t, w), dt) for w, dt in outs],
        compiler_params=_params("arbitrary"),
        name="in_proj_a",
    )(x, norm_w, scale_g, shift_g, w_a, q_norm, k_norm)


def _in_b_kernel(h_ref, w_ref, qb_ref, kb_ref, vb_ref, kbb_ref, vbb_ref):
    hb = h_ref[...]
    qb_ref[...] = jnp.dot(hb, w_ref[:, :B_WIDTH], preferred_element_type=F32).astype(BF16)
    k = jnp.dot(hb, w_ref[:, B_WIDTH:2 * B_WIDTH], preferred_element_type=F32)
    kbb_ref[...] = k.astype(BF16)
    v = jnp.dot(hb, w_ref[:, 2 * B_WIDTH:], preferred_element_type=F32)
    vbb_ref[...] = v.astype(BF16)
    for hd in range(B_HEADS):
        kb_ref[:, hd, :] = k[:, hd * HEAD_DIM:(hd + 1) * HEAD_DIM]
        vb_ref[:, hd, :] = v[:, hd * HEAD_DIM:(hd + 1) * HEAD_DIM]


def in_proj_b(h, w_b, tm):
    t, d = h.shape
    row = lambda w: pl.BlockSpec((tm, w), lambda i: (i, 0))
    heads = pl.BlockSpec((tm, B_HEADS, HEAD_DIM), lambda i: (i, 0, 0))
    flat = lambda dt: jax.ShapeDtypeStruct((t, B_WIDTH), dt)
    per_head = jax.ShapeDtypeStruct((t, B_HEADS, HEAD_DIM), F32)
    return pl.pallas_call(
        _in_b_kernel,
        grid=(t // tm,),
        in_specs=[row(d), pl.BlockSpec((d, 3 * B_WIDTH), lambda i: (0, 0))],
        out_specs=[row(B_WIDTH), heads, heads, row(B_WIDTH), row(B_WIDTH)],
        out_shape=[flat(BF16), per_head, per_head, flat(BF16), flat(BF16)],
        compiler_params=_params("arbitrary"),
        name="in_proj_b",
    )(h, w_b)


def _score_key(x):
    bits = lax.bitcast_convert_type(x, jnp.int32)
    return bits ^ ((bits >> 31) & 0x7FFFFFFF)


def _lane_tiles(a):
    return [a[:, j * LANES:(j + 1) * LANES] for j in range(a.shape[1] // LANES)]


def _dsa_kernel(qa_ref, qi_ref, kw_ref, k_ref, v_ref, ki_ref, *rest, q_pos0, n_keys, n_sel, tq, tc, pos_bits,
                cached):
    if cached:
        ck_ref, cv_ref, cki_ref, o_ref, key_ref, m_ref, l_ref, acc_ref = rest
    else:
        o_ref, key_ref, m_ref, l_ref, acc_ref = rest

    def read_k(start, g, last):
        if cached and not last:
            return ck_ref[pl.ds(start, tc), g, :].astype(BF16)
        rows = slice(None) if cached else pl.ds(start, tc)
        return k_ref[rows, g * HEAD_DIM:(g + 1) * HEAD_DIM]

    def read_v(start, g, last):
        if cached and not last:
            return cv_ref[pl.ds(start, tc), g, :].astype(BF16)
        rows = slice(None) if cached else pl.ds(start, tc)
        return v_ref[rows, g * HEAD_DIM:(g + 1) * HEAD_DIM]

    def read_ki(start, last):
        if cached and not last:
            return cki_ref[pl.ds(start, tc), :].astype(BF16)
        return ki_ref[...] if cached else ki_ref[pl.ds(start, tc), :]

    i = pl.program_id(1)
    q0 = q_pos0 + i * tq
    kmax = jnp.minimum(((q0 + tq - 1) // CHUNK + 1) * CHUNK, n_keys)
    nch = (kmax + tc - 1) // tc
    qpos = q0 + lax.broadcasted_iota(jnp.int32, (tq, 1), 0)

    tqs = key_ref.shape[2]
    lane_q = lax.broadcasted_iota(jnp.int32, (1, tqs), 1)
    n_adm = jnp.minimum(((q0 + lane_q) // CHUNK + 1) * CHUNK, n_keys)
    pad_q = lambda a: a if tqs == tq else jnp.concatenate(
        [a, jnp.zeros((tqs - tq, a.shape[1]), a.dtype)], axis=0)
    head_w = pad_q(kw_ref[...]).T[IDX_DIM:IDX_DIM + IDX_HEADS] * (IDX_HEADS ** -0.5)
    qi_all = jnp.concatenate([pad_q(qi_ref[:, h * IDX_DIM:(h + 1) * IDX_DIM]) for h in range(IDX_HEADS)], axis=0)

    def all_chunks(chunk_pass):
        def body(c, carry):
            chunk_pass(c, False)
            return carry
        lax.fori_loop(0, nch - 1, body, 0)
        chunk_pass(nch - 1, True)

    def score_pass(c, last):
        start = pl.multiple_of(c * tc, tc)
        lg = lax.dot_general(read_ki(start, last), qi_all, NT_DIMS, preferred_element_type=F32)
        s = jnp.zeros((tc, tqs), F32)
        for h in range(IDX_HEADS):
            s = s + jnp.maximum(lg[:, h * tqs:(h + 1) * tqs], 0.0) * head_w[h:h + 1, :]
        adm = start + lax.broadcasted_iota(jnp.int32, (tc, 1), 0) < n_adm
        key_ref[c] = jnp.where(adm, _score_key(s), INT_MIN)

    all_chunks(score_pass)

    def count_rows(pred):
        def body(c, acc):
            start = pl.multiple_of(c * tc, tc)
            hit = jnp.where(pred(key_ref[c], start), 1.0, 0.0)
            return acc + jnp.sum(hit.reshape(tc // acc_rows, acc_rows, tqs), axis=0)
        acc_rows = 4 * SUBLANES
        acc = lax.fori_loop(0, nch, body, jnp.zeros((acc_rows, tqs), F32))
        return jnp.sum(acc, axis=0, keepdims=True)

    def count_ge(cand):
        return count_rows(lambda key, start: key >= cand)

    def pending(done):
        return jnp.min(done) < 0.5

    cnt = count_ge(jnp.zeros((1, tqs), jnp.int32))
    thr = jnp.where(cnt >= n_sel, 0, INT_MIN).astype(jnp.int32)
    done =jnp.where(jnp.logical_or(jnp.logical_or(n_adm <= n_sel, cnt == n_sel), lane_q >= tq), 1.0, 0.0)

    def bit_cond(state):
        b, _, done = state
        return jnp.logical_and(b < 31, pending(done))

    def bit_body(state):
        b, thr, done = state
        cand = thr | jnp.left_shift(jnp.int32(1), 30 - b)
        cnt = count_ge(cand)
        return (b + 1, jnp.where(cnt >= n_sel, cand, thr), jnp.where(cnt == n_sel, 1.0, done))

    _, thr, done = lax.while_loop(bit_cond, bit_body, (jnp.int32(0), thr, done))

    def tie_limit():
        need = n_sel - count_rows(lambda key, start: key > thr)

        def tied_below(limit):
            return count_rows(lambda key, start: jnp.logical_and(
                key == thr, start + lax.broadcasted_iota(jnp.int32, (tc, 1), 0) < limit))

        def body(b, lim):
            cand = lim | jnp.left_shift(jnp.int32(1), pos_bits - 1 - b)
            return jnp.where(tied_below(cand) < need, cand, lim)

        lim = lax.fori_loop(0, pos_bits, body, jnp.zeros((1, tqs), jnp.int32))
        return jnp.where(done > 0.5, jnp.int32(2 ** pos_bits), lim + 1)

    plim = lax.cond(pending(done), tie_limit, lambda: jnp.full((1, tqs), 2 ** pos_bits, jnp.int32))

    slopes = [2.0 ** (-8.0 * (h + 1) / A_HEADS) for h in range(A_HEADS)]
    lane = lax.broadcasted_iota(jnp.int32, (1, LANES), 1)
    rows4 = A_GROUP * tq
    q_aug = []
    for g in range(A_KV_HEADS):
        heads = range(g * A_GROUP, (g + 1) * A_GROUP)
        q4 = jnp.concatenate([qa_ref[:, h * HEAD_DIM:(h + 1) * HEAD_DIM] for h in heads], axis=0)
        slope_col = jnp.concatenate([jnp.full((tq, 1), slopes[h], F32) for h in heads], axis=0)
        q_aug.append(jnp.concatenate([q4, jnp.where(lane < 3, slope_col, 0.0).astype(BF16)], axis=1))

    def chunk_logits(c, last):
        start = pl.multiple_of(c * tc, tc)
        kp = start + lax.broadcasted_iota(jnp.int32, (tc, 1), 0)
        pos_cols = jnp.where(lane == 0, kp & 63,
                             jnp.where(lane == 1, kp & (127 << 6),
                                       jnp.where(lane == 2, kp & ~8191, 0))).astype(F32).astype(BF16)
        key = key_ref[c]
        sel = jnp.logical_or(key > thr, jnp.logical_and(key == thr, kp < plim))
        mb = jnp.where(jnp.logical_and(sel, key != INT_MIN), 0.0, NEG_BIG).T[:tq]
        kpos = start + lax.broadcasted_iota(jnp.int32, (1, tc), 1)
        out = []
        for g in range(A_KV_HEADS):
            kc = jnp.concatenate([read_k(start, g, last), pos_cols], axis=1)
            lg = lax.dot_general(q_aug[g], kc, NT_DIMS, preferred_element_type=F32).reshape(A_GROUP, tq, tc)
            if last:
                ahead = jnp.maximum(kpos - qpos, 0).astype(F32)
                bias = jnp.stack([mb - (2.0 * slopes[g * A_GROUP + r]) * ahead for r in range(A_GROUP)])
            else:
                bias = mb[None]
            out.append((lg + bias).reshape(rows4, tc))
        return start, out

    m_ref[...] = jnp.full(m_ref.shape, NEG_BIG, F32)
    l_ref[...] = jnp.zeros(l_ref.shape, F32)
    acc_ref[...] = jnp.zeros(acc_ref.shape, F32)

    def attend(c, last):
        start, logits = chunk_logits(c, last)
        for g in range(A_KV_HEADS):
            rows = slice(g * rows4, (g + 1) * rows4)
            tiles = _lane_tiles(logits[g])
            m_old = m_ref[rows]
            m_new = jnp.maximum(m_old, jnp.max(functools.reduce(jnp.maximum, tiles), axis=1, keepdims=True))
            alpha = jnp.exp(m_old - m_new)
            p = [jnp.exp(t - m_new) for t in tiles]
            m_ref[rows] = m_new
            l_ref[rows] = alpha * l_ref[rows] + functools.reduce(jnp.add, p)
            acc_ref[rows] = alpha * acc_ref[rows] + jnp.dot(jnp.concatenate(p, axis=1).astype(BF16),
                                                            read_v(start, g, last), preferred_element_type=F32)

    all_chunks(attend)
    for h in range(A_HEADS):
        rows = slice(h * tq, (h + 1) * tq)
        o_ref[:, h * HEAD_DIM:(h + 1) * HEAD_DIM] = (
            acc_ref[rows] / jnp.sum(l_ref[rows], axis=1, keepdims=True)).astype(BF16)


def dsa_attention(qa, qi, wi, k, v, ki, *, q_pos0, n_keys, tq, tc, caches=()):
    b, n, _ = qa.shape
    if caches:
        assert q_pos0 % tc == 0 and n == tq <= tc and k.shape[1] == tc and n_keys == q_pos0 + n
        l_pad = q_pos0 + tc
    else:
        l_pad = k.shape[1]
    n_sel = min(TOPK_KEYS, n_keys // 4)
    qspec = lambda w: pl.BlockSpec((None, tq, w), lambda bi, i: (bi, i, 0))
    kspec = lambda w: pl.BlockSpec((None, k.shape[1], w), lambda bi, i: (bi, 0, 0))
    cache_specs = [pl.BlockSpec((None,) + c.shape[1:], lambda bi, i, nd=c.ndim: (bi,) + (0,) * (nd - 1))
                   for c in caches]
    kern = functools.partial(_dsa_kernel, q_pos0=q_pos0, n_keys=n_keys, n_sel=n_sel, tq=tq, tc=tc,
                             pos_bits=max(1, (l_pad - 1).bit_length()), cached=bool(caches))
    return pl.pallas_call(
        kern,
        grid=(b, n // tq),
        in_specs=[qspec(A_WIDTH), qspec(IDX_Q_WIDTH), qspec(LANES),
                  kspec(A_KV_WIDTH), kspec(A_KV_WIDTH), kspec(IDX_DIM)] + cache_specs,
        out_specs=qspec(A_WIDTH),
        out_shape=jax.ShapeDtypeStruct((b, n, A_WIDTH), BF16),
        scratch_shapes=[pltpu.VMEM((l_pad // tc, tc, max(tq, LANES)), jnp.int32),
                        pltpu.VMEM((A_HEADS * tq, LANES), F32),
                        pltpu.VMEM((A_HEADS * tq, LANES), F32),
                        pltpu.VMEM((A_HEADS * tq, HEAD_DIM), F32)],
        compiler_params=_params("arbitrary", "arbitrary"),
        name="dsa",
    )(qa, qi, wi, k, v, ki, *caches)


STICK_HEADS_PER_STEP = 2


def _strict_upper(n):
    return (lax.broadcasted_iota(jnp.int32, (n, n), 0) > lax.broadcasted_iota(jnp.int32, (n, n), 1)).astype(BF16)


def _stick_step(q, kc, vc, strict, upper, tail, acc):
    z = lax.dot_general(q, kc, NT_DIMS, preferred_element_type=F32) * (HEAD_DIM ** -0.5)
    softplus = jnp.log(1.0 + jnp.exp(-jnp.abs(z)))
    log_go = jnp.minimum(z, 0.0) - softplus
    log_stay = jnp.where(strict, jnp.minimum(-z, 0.0) - softplus, 0.0)
    hi = log_stay.astype(BF16)
    lo = (log_stay - hi.astype(F32)).astype(BF16)
    after = jnp.dot(hi, upper, preferred_element_type=F32) + jnp.dot(lo, upper, preferred_element_type=F32) + tail
    w = jnp.where(strict, jnp.exp(log_go + after), 0.0)
    return (tail + jnp.sum(log_stay, axis=1, keepdims=True),
            acc + jnp.dot(w.astype(BF16), vc, preferred_element_type=F32))


def _stick_walk(qs, tpos, c_first, tc, read_kv, tails, accs):
    upper = _strict_upper(tc)

    def cond(carry):
        n, tails, _ = carry
        return jnp.logical_and(n <= c_first, jnp.max(functools.reduce(jnp.maximum, tails)) > STICK_UNDERFLOW)

    def body(carry):
        n, tails, accs = carry
        start = pl.multiple_of((c_first - n) * tc, tc)
        strict = start + lax.broadcasted_iota(jnp.int32, (1, tc), 1) < tpos
        new = [_stick_step(q, *read_kv(start, h), strict, upper, tail, acc)
               for h, (q, tail, acc) in enumerate(zip(qs, tails, accs))]
        return n + 1, tuple(t for t, _ in new), tuple(a for _, a in new)

    _, _, accs = lax.while_loop(cond, body, (jnp.int32(0), tuple(tails), tuple(accs)))
    return accs


def _stick_kernel(q_ref, k_ref, v_ref, o_ref, *, q_pos0, tq, tc):
    t0 = q_pos0 + pl.program_id(2) * tq
    tpos = t0 + lax.broadcasted_iota(jnp.int32, (tq, 1), 0)
    heads = [slice(h * HEAD_DIM, (h + 1) * HEAD_DIM) for h in range(STICK_HEADS_PER_STEP)]
    read_kv = lambda start, h: (k_ref[pl.ds(start, tc), heads[h]], v_ref[pl.ds(start, tc), heads[h]])
    zeros = lambda w: [jnp.zeros((tq, w), F32) for _ in heads]
    accs = _stick_walk([q_ref[:, hs] for hs in heads], tpos, (t0 + tq - 2) // tc, tc, read_kv,
                       zeros(1), zeros(HEAD_DIM))
    for hs, acc in zip(heads, accs):
        o_ref[:, hs] = acc.astype(BF16)


def _stick_cached_kernel(q_ref, k_ref, v_ref, ck_ref, cv_ref, o_ref, *, past, tc):
    n = q_ref.shape[0]
    tpos = past + lax.broadcasted_iota(jnp.int32, (n, 1), 0)
    strict = past + lax.broadcasted_iota(jnp.int32, (1, n), 1) < tpos
    upper = _strict_upper(n)
    for h0 in range(0, B_HEADS, STICK_HEADS_PER_STEP):
        group = range(h0, h0 + STICK_HEADS_PER_STEP)
        cols = [slice(h * HEAD_DIM, (h + 1) * HEAD_DIM) for h in group]
        qs = [q_ref[:, hs] for hs in cols]
        first = [_stick_step(q, k_ref[:, hs], v_ref[:, hs], strict, upper,
                             jnp.zeros((n, 1), F32), jnp.zeros((n, HEAD_DIM), F32)) for q, hs in zip(qs, cols)]
        read_kv = lambda start, h, h0=h0: (ck_ref[pl.ds(start, tc), h0 + h, :].astype(BF16),
                                           cv_ref[pl.ds(start, tc), h0 + h, :].astype(BF16))
        accs = _stick_walk(qs, tpos, past // tc - 1, tc, read_kv, [t for t, _ in first], [a for _, a in first])
        for hs, acc in zip(cols, accs):
            o_ref[:, hs] = acc.astype(BF16)


def stick_attention(qb, k, v, *, q_pos0, tq, tc):
    b, n, _ = qb.shape
    l_pad = k.shape[1]
    width = STICK_HEADS_PER_STEP * HEAD_DIM
    qspec = pl.BlockSpec((None, tq, width), lambda bi, h, i: (bi, i, h))
    kspec = pl.BlockSpec((None, l_pad, width), lambda bi, h, i: (bi, 0, h))
    kern = functools.partial(_stick_kernel, q_pos0=q_pos0, tq=tq, tc=tc)
    return pl.pallas_call(
        kern,
        grid=(b, B_HEADS // STICK_HEADS_PER_STEP, n // tq),
        in_specs=[qspec, kspec, kspec],
        out_specs=qspec,
        out_shape=jax.ShapeDtypeStruct((b, n, B_WIDTH), BF16),
        compiler_params=_params("arbitrary", "arbitrary", "arbitrary"),
        name="stick",
    )(qb, k, v)


def stick_attention_cached(qb, k_new, v_new, cache_k, cache_v, *, tc):
    b, n, _ = qb.shape
    past = cache_k.shape[1]
    assert past % tc == 0
    new = pl.BlockSpec((None, n, B_WIDTH), lambda bi: (bi, 0, 0))
    cache = pl.BlockSpec((None, past, B_HEADS, HEAD_DIM), lambda bi: (bi, 0, 0, 0))
    return pl.pallas_call(
        functools.partial(_stick_cached_kernel, past=past, tc=tc),
        grid=(b,),
        in_specs=[new, new, new, cache, cache],
        out_specs=new,
        out_shape=jax.ShapeDtypeStruct((b, n, B_WIDTH), BF16),
        compiler_params=_params("arbitrary"),
        name="stick_cached",
    )(qb, k_new, v_new, cache_k, cache_v)


def _merge_kernel(h_ref, ya_ref, yb_ref, wga_ref, wgb_ref, wa_ref, wb_ref, o_ref):
    hb = h_ref[...]
    ga = jax.nn.sigmoid(jnp.dot(hb, wga_ref[...], preferred_element_type=F32))
    gb = jax.nn.sigmoid(jnp.dot(hb, wgb_ref[...], preferred_element_type=F32))
    pa = jnp.dot(ya_ref[...], wa_ref[...], preferred_element_type=F32)
    pb = jnp.dot(yb_ref[...], wb_ref[...], preferred_element_type=F32)
    o_ref[...] = (ga * pa + gb * pb).astype(BF16)


def merge_branches(h, ya, yb, w_ga, w_gb, w_a, w_b, tm, tn):
    t, d = h.shape
    row = lambda w: pl.BlockSpec((tm, w), lambda i, j: (i, 0))
    col = lambda k: pl.BlockSpec((k, tn), lambda i, j: (0, j))
    return pl.pallas_call(
        _merge_kernel,
        grid=(t // tm, d // tn),
        in_specs=[row(d), row(A_WIDTH), row(B_WIDTH), col(d), col(d), col(A_WIDTH), col(B_WIDTH)],
        out_specs=pl.BlockSpec((tm, tn), lambda i, j: (i, j)),
        out_shape=jax.ShapeDtypeStruct((t, d), BF16),
        compiler_params=_params("arbitrary", "arbitrary"),
        name="merge",
    )(h, ya, yb, w_ga, w_gb, w_a, w_b)


def _pack_bf16_pair(lo, hi):
    lo_bits = lax.bitcast_convert_type(lo.astype(BF16).astype(F32), jnp.uint32)
    hi_bits = lax.bitcast_convert_type(hi.astype(BF16).astype(F32), jnp.uint32)
    return (lo_bits >> 16) | (hi_bits & jnp.uint32(0xFFFF0000))


def _unpack_f32_pair(u):
    return (lax.bitcast_convert_type(u << 16, F32),
            lax.bitcast_convert_type(u & jnp.uint32(0xFFFF0000), F32))


def _unpack_bf16_pair(u):
    lo, hi = _unpack_f32_pair(u)
    return lo.astype(BF16), hi.astype(BF16)


def _first_max(vals, idx, axes, n):
    m = vals
    for ax in axes:
        m = jnp.max(m, axis=ax, keepdims=True)
    first = jnp.where(vals == m, idx, n)
    for ax in axes:
        first = jnp.min(first, axis=ax, keepdims=True)
    return m, first


def _sum01(a):
    return jnp.sum(jnp.sum(a, axis=0, keepdims=True), axis=1, keepdims=True)


def _out_kernel(mg_ref, x_ref, g1_ref, sc_ref, sh_ref, nw_ref, wo_ref, wr_ref, br_ref, cnt0_ref,
                x1_ref, hu_ref, tope_ref, rank_ref, gatek_ref, cnt_out_ref, cnt_ref):
    tm, d = x_ref.shape
    proj = jnp.dot(mg_ref[...], wo_ref[...], preferred_element_type=F32)
    x1 = x_ref[...] + _gate_rows(proj, g1_ref[...])
    x1_ref[...] = x1
    h2 = _modulate(_rms(x1, nw_ref[...]), sc_ref[...], sh_ref[...])
    hu_ref[...] = _pack_bf16_pair(h2[:, :d // 2], h2[:, d // 2:])

    logits = lax.dot_general(wr_ref[...], h2.astype(BF16), NT_DIMS, preferred_element_type=F32)
    aff = jax.nn.sigmoid(logits)
    shape3 = (N_GROUPS, GROUP_SIZE, tm)
    aff3 = aff.reshape(shape3)
    biased = (aff + br_ref[...]).reshape(shape3)
    io_in = lax.broadcasted_iota(jnp.int32, shape3, 1)
    m1, i1 = _first_max(biased, io_in, (1,), GROUP_SIZE)
    m2 = jnp.max(jnp.where(io_in == i1, -jnp.inf, biased), axis=1, keepdims=True)
    gscore = m1 + m2
    gio = lax.broadcasted_iota(jnp.int32, gscore.shape, 0)
    keep = jnp.zeros(gscore.shape, jnp.bool_)
    for _ in range(N_ACTIVE_GROUPS):
        _, first = _first_max(gscore, gio, (0,), N_GROUPS)
        hit = gio == first
        keep = jnp.logical_or(keep, hit)
        gscore = jnp.where(hit, -jnp.inf, gscore)
    masked = jnp.where(jnp.broadcast_to(keep, shape3), biased, -jnp.inf)
    eio = lax.broadcasted_iota(jnp.int32, shape3, 0) * GROUP_SIZE + io_in
    sel = jnp.zeros(shape3, jnp.bool_)
    hits = []
    for _ in range(N_ACTIVE):
        _, first = _first_max(masked, eio, (0, 1), N_EXPERTS)
        hit = eio == first
        sel = jnp.logical_or(sel, hit)
        masked = jnp.where(hit, -jnp.inf, masked)
        hits.append((hit, first.reshape(1, tm)))
    sel_aff = jnp.where(sel, aff3, 0.0)
    gates = sel_aff / _sum01(sel_aff) * ROUTED_SCALE

    @pl.when(pl.program_id(0) == 0)
    def _():
        cnt_ref[...] = cnt0_ref[...]

    sel2 = jnp.where(sel, 1.0, 0.0).reshape(N_EXPERTS, tm)
    incl = (lax.broadcasted_iota(jnp.int32, (tm, tm), 0)
            <= lax.broadcasted_iota(jnp.int32, (tm, tm), 1)).astype(BF16)
    rank = jnp.dot(sel2.astype(BF16), incl, preferred_element_type=F32) + cnt_ref[...]
    cnt_ref[...] = cnt_ref[...] + jnp.sum(sel2, axis=1, keepdims=True)
    cnt_out_ref[...] = cnt_ref[...]
    rank3 = rank.reshape(shape3)
    tope_ref[...] = jnp.concatenate([first for _, first in hits], axis=0)
    rank_ref[...] = jnp.concatenate(
        [_sum01(jnp.where(hit, rank3, 0.0)).reshape(1, tm) for hit, _ in hits], axis=0).astype(jnp.int32)
    gatek_ref[...] = jnp.concatenate(
        [_sum01(jnp.where(hit, gates, 0.0)).reshape(1, tm) for hit, _ in hits], axis=0)


def out_proj(merged, x, gate1_g, scale_g, shift_g, norm_w, w_out, w_router_t, b_router, count0, tm):
    t, d = x.shape
    gm = tm // MOD_GROUP
    row = lambda w: pl.BlockSpec((tm, w), lambda i: (i, 0))
    grp = pl.BlockSpec((gm, d), lambda i: (i, 0))
    const = lambda a, b: pl.BlockSpec((a, b), lambda i: (0, 0))
    per_k = pl.BlockSpec((N_ACTIVE, tm), lambda i: (0, i))
    return pl.pallas_call(
        _out_kernel,
        grid=(t // tm,),
        in_specs=[row(d), row(d), grp, grp, grp, const(1, d), const(d, d),
                  const(N_EXPERTS, d), const(N_EXPERTS, 1), const(N_EXPERTS, 1)],
        out_specs=[row(d), row(d // 2), per_k, per_k, per_k, const(N_EXPERTS, 1)],
        out_shape=[jax.ShapeDtypeStruct((t, d), F32), jax.ShapeDtypeStruct((t, d // 2), jnp.uint32),
                   jax.ShapeDtypeStruct((N_ACTIVE, t), jnp.int32), jax.ShapeDtypeStruct((N_ACTIVE, t), jnp.int32),
                   jax.ShapeDtypeStruct((N_ACTIVE, t), F32), jax.ShapeDtypeStruct((N_EXPERTS, 1), F32)],
        scratch_shapes=[pltpu.VMEM((N_EXPERTS, 1), F32)],
        compiler_params=_params("arbitrary"),
        name="out_proj",
    )(merged, x, gate1_g, scale_g, shift_g, norm_w, w_out, w_router_t, b_router, count0)


def _pad_fill(start_ref, cnt_ref, nu_ref, xs_ref, zero_ref, sem, tme, n_tiles):
    zero_ref[...] = jnp.zeros(zero_ref.shape, zero_ref.dtype)

    def tile_copy(i):
        return pltpu.make_async_copy(zero_ref, xs_ref.at[pl.ds(pl.multiple_of(i * tme, tme), tme)], sem.at[1])

    def for_each_copy(wait):
        def pad_rows(e, carry):
            pos = start_ref[e]
            for bit in range(tme.bit_length() - 1):
                size = 1 << bit

                @pl.when((cnt_ref[e] & size) != 0)
                def _(pos=pos, size=size, bit=bit):
                    copy = pltpu.make_async_copy(zero_ref.at[pl.ds(0, size)],
                                                 xs_ref.at[pl.ds(pl.multiple_of(pos, size), size)], sem.at[2 + bit])
                    copy.wait() if wait else copy.start()

                pos = pos + (cnt_ref[e] & size)
            return carry

        def tail_tiles(i, carry):
            tile_copy(i).wait() if wait else tile_copy(i).start()
            return carry

        lax.fori_loop(0, N_EXPERTS, pad_rows, 0)
        lax.fori_loop(nu_ref[0], n_tiles, tail_tiles, 0)

    for_each_copy(wait=False)
    for_each_copy(wait=True)


def _dispatch_kernel(start_ref, cnt_ref, nu_ref, slot_ref, hp_ref, hs_ref, xs_ref, zero_ref, sem, pad_sem,
                     *, td, prompt_steps, tme, n_tiles):
    i = pl.program_id(0)

    @pl.when(i == 0)
    def _():
        _pad_fill(start_ref, cnt_ref, nu_ref, xs_ref, zero_ref, pad_sem, tme, n_tiles)

    def scatter(hu_ref):
        def row_copy(j, k):
            return pltpu.make_async_copy(hu_ref.at[pl.ds(j, 1)],
                                         xs_ref.at[pl.ds(slot_ref[j * N_ACTIVE + k], 1)], sem)

        def start(jj, carry):
            base = pl.multiple_of(jj * SUBLANES, SUBLANES)
            for r in range(SUBLANES):
                for k in range(N_ACTIVE):
                    row_copy(base + r, k).start(priority=k % 2)
            return carry

        def wait(j, carry):
            for k in range(N_ACTIVE):
                row_copy(j, k).wait()
            return carry

        lax.fori_loop(0, td // SUBLANES, start, 0)
        lax.fori_loop(0, td, wait, 0)

    @pl.when(i < prompt_steps)
    def _():
        scatter(hp_ref)

    @pl.when(i >= prompt_steps)
    def _():
        scatter(hs_ref)


def dispatch(route, slots_flat, hu_prompt, hu_sample, tme):
    tp, ts = hu_prompt.shape[0], hu_sample.shape[0]
    width = hu_prompt.shape[1]
    td = _row_tile(ts, _row_tile(tp, TM_ROWS))
    prompt_steps = tp // td
    p_rows = route["p_rows"]
    grid_spec = pltpu.PrefetchScalarGridSpec(
        num_scalar_prefetch=3,
        grid=(prompt_steps + ts // td,),
        in_specs=[pl.BlockSpec((td * N_ACTIVE,), lambda i, *_: (i,), memory_space=pltpu.SMEM),
                  pl.BlockSpec((td, width), lambda i, *_: (jnp.minimum(i, prompt_steps - 1), 0)),
                  pl.BlockSpec((td, width), lambda i, *_: (jnp.maximum(i - prompt_steps, 0), 0))],
        out_specs=pl.BlockSpec(memory_space=pl.ANY),
        scratch_shapes=[pltpu.VMEM((tme, width), jnp.uint32), pltpu.SemaphoreType.DMA(()),
                        pltpu.SemaphoreType.DMA((2 + tme.bit_length(),))])
    return pl.pallas_call(
        functools.partial(_dispatch_kernel, td=td, prompt_steps=prompt_steps, tme=tme, n_tiles=p_rows // tme),
        grid_spec=grid_spec,
        out_shape=jax.ShapeDtypeStruct((p_rows, width), jnp.uint32),
        compiler_params=_params("arbitrary"),
        name="dispatch",
    )(route["pad_start"], route["pad_count"], route["n_used"], slots_flat, hu_prompt, hu_sample)


def _expert_kernel(te_ref, nu_ref, first_ref, next_ref, par_ref, xs_ref, wg_ref, wu_ref, wd_ref, o_ref,
                   wgf_ref, wuf_ref, wdf_ref, wgb_ref, wub_ref, wdb_ref, sem):
    i = pl.program_id(0)
    used = i < nu_ref[0]

    def fetch(e, buf):
        return [pltpu.make_async_copy(w.at[e], f.at[buf], sem.at[buf, j])
                for j, (w, f) in enumerate(((wg_ref, wgf_ref), (wu_ref, wuf_ref), (wd_ref, wdf_ref)))]

    @pl.when(i == 0)
    def _():
        for copy in fetch(te_ref[0], par_ref[0]):
            copy.start()

    @pl.when(jnp.logical_and(used, first_ref[i] == 1))
    def _():
        buf = par_ref[i]
        for copy in fetch(te_ref[i], buf):
            copy.wait()

        @pl.when(next_ref[i] >= 0)
        def _():
            for copy in fetch(next_ref[i], 1 - buf):
                copy.start()

        wgb_ref[...] = wgf_ref[buf].astype(BF16)
        wub_ref[...] = wuf_ref[buf].astype(BF16)
        wdb_ref[...] = wdf_ref[buf].astype(BF16)

    @pl.when(used)
    def _():
        lo, hi = _unpack_bf16_pair(xs_ref[...])
        half = lo.shape[1]
        gate = (jnp.dot(lo, wgb_ref[:half], preferred_element_type=F32)
                + jnp.dot(hi, wgb_ref[half:], preferred_element_type=F32))
        up = (jnp.dot(lo, wub_ref[:half], preferred_element_type=F32)
              + jnp.dot(hi, wub_ref[half:], preferred_element_type=F32))
        hid = (_silu(gate) * up).astype(BF16)
        o_ref[...] = jnp.dot(hid, wdb_ref[...], preferred_element_type=F32)

    @pl.when(jnp.logical_not(used))
    def _():
        o_ref[...] = jnp.zeros(o_ref.shape, o_ref.dtype)


def expert_ffn(tiles, xs, w_gate, w_up, w_down, tme):
    p, half = xs.shape
    e, d, f = w_gate.shape
    hbm = pl.BlockSpec(memory_space=pl.ANY)
    grid_spec = pltpu.PrefetchScalarGridSpec(
        num_scalar_prefetch=5,
        grid=(p // tme,),
        in_specs=[pl.BlockSpec((tme, half), lambda i, te, nu, *_: (jnp.minimum(i, nu[0] - 1), 0)),
                  hbm, hbm, hbm],
        out_specs=pl.BlockSpec((tme, d), lambda i, *_: (i, 0)),
        scratch_shapes=[pltpu.VMEM((2, d, f), F32), pltpu.VMEM((2, d, f), F32), pltpu.VMEM((2, f, d), F32),
                        pltpu.VMEM((d, f), BF16), pltpu.VMEM((d, f), BF16), pltpu.VMEM((f, d), BF16),
                        pltpu.SemaphoreType.DMA((2, 3))],
    )
    return pl.pallas_call(
        _expert_kernel,
        grid_spec=grid_spec,
        out_shape=jax.ShapeDtypeStruct((p, d), F32),
        compiler_params=_params("arbitrary"),
        name="expert_ffn",
    )(tiles["expert"], tiles["n_used"], tiles["first"], tiles["next"], tiles["parity"], xs, w_gate, w_up, w_down)


def _combine_kernel(slot_ref, hu_ref, x1_ref, g2_ref, gk_ref, wg_ref, wu_ref, wd_ref, ys_ref,
                    o_ref, rows_ref, sem, *, tmc):
    def row_copy(j, k):
        return pltpu.make_async_copy(ys_ref.at[pl.ds(slot_ref[j * N_ACTIVE + k], 1)],
                                     rows_ref.at[k, pl.ds(j, 1)], sem)

    def start(jj, carry):
        base = pl.multiple_of(jj * SUBLANES, SUBLANES)
        for r in range(SUBLANES):
            for k in range(N_ACTIVE):
                row_copy(base + r, k).start(priority=k % 2)
        return carry

    def wait(j, carry):
        for k in range(N_ACTIVE):
            row_copy(j, k).wait()
        return carry

    lax.fori_loop(0, tmc // SUBLANES, start, 0)
    lo, hi = _unpack_bf16_pair(hu_ref[...])
    half = lo.shape[1]
    gate = (jnp.dot(lo, wg_ref[:half], preferred_element_type=F32)
            + jnp.dot(hi, wg_ref[half:], preferred_element_type=F32))
    up = (jnp.dot(lo, wu_ref[:half], preferred_element_type=F32)
          + jnp.dot(hi, wu_ref[half:], preferred_element_type=F32))
    y = jnp.dot((_silu(gate) * up).astype(BF16), wd_ref[...], preferred_element_type=F32)
    lax.fori_loop(0, tmc, wait, 0)
    gk = gk_ref[...]
    for k in range(N_ACTIVE):
        y = y + gk[:, k:k + 1] * rows_ref[k]
    o_ref[...] = x1_ref[...] + _gate_rows(y, g2_ref[...])


def combine(slots_flat, hu, x1, gate2_g, gate_k, w_gs, w_us, w_ds, ys, tmc):
    t, d = x1.shape
    f = w_gs.shape[1]
    gm = tmc // MOD_GROUP
    row = lambda w: pl.BlockSpec((tmc, w), lambda i: (i, 0))
    const = lambda a, b: pl.BlockSpec((a, b), lambda i: (0, 0))
    return pl.pallas_call(
        functools.partial(_combine_kernel, tmc=tmc),
        grid=(t // tmc,),
        in_specs=[pl.BlockSpec((tmc * N_ACTIVE,), lambda i: (i,), memory_space=pltpu.SMEM),
                  pl.BlockSpec((tmc,) + hu.shape[1:], lambda i: (i, 0)),
                  row(d), pl.BlockSpec((gm, d), lambda i: (i, 0)), row(N_ACTIVE),
                  const(d, f), const(d, f), const(f, d),
                  pl.BlockSpec(memory_space=pl.ANY)],
        out_specs=row(d),
        out_shape=jax.ShapeDtypeStruct((t, d), F32),
        scratch_shapes=[pltpu.VMEM((N_ACTIVE, tmc, d), F32), pltpu.SemaphoreType.DMA(())],
        compiler_params=_params("arbitrary"),
        name="combine",
    )(slots_flat, hu, x1, gate2_g, gate_k, w_gs, w_us, w_ds, ys)


def _group_rows(v, n):
    b, d = v.shape
    return jnp.broadcast_to(v[:, None, :], (b, n // MOD_GROUP, d)).reshape(b * n // MOD_GROUP, d)


def _pad_keys(a, l_pad):
    return jnp.pad(a, ((0, 0), (0, l_pad - a.shape[1]), (0, 0)))


def _token_stage(x, mod, w, caches, count0):
    b, n, d = x.shape
    t = b * n
    shift1, scale1, gate1, shift2, scale2, gate2 = [_group_rows(m, n) for m in jnp.split(mod, 6, axis=-1)]
    xf = x.reshape(t, d)
    tm = _row_tile(t, TM_DENSE)
    (h, qa, ka, va, kab, vab, qi, ki, kib, wi) = in_proj_a(
        xf, w["norm_mix"], scale1, shift1, w["w_in_a"], w["q_norm_a"], w["k_norm_a"], tm)
    qb, kb, vb, kbb, vbb = in_proj_b(h, w["w_in_b"], tm)

    r3 = lambda a: a.reshape(b, n, a.shape[-1])
    if caches is None:
        ya = dsa_attention(r3(qa), r3(qi), r3(wi), r3(kab), r3(vab), r3(kib),
                           q_pos0=0, n_keys=n, tq=min(TQ_DSA, n), tc=min(TC_DSA, n))
        yb = stick_attention(r3(qb), r3(kbb), r3(vbb), q_pos0=0, tq=min(TQ_STICK, n), tc=min(TC_STICK, n))
    else:
        cka, cva, cki, ckb, cvb = caches
        past = cka.shape[1]
        tc_a, tc_b = min(TC_DSA, past), min(TC_STICK, past)
        new_chunk = lambda a: _pad_keys(r3(a), tc_a)
        ya = dsa_attention(r3(qa), r3(qi), r3(wi), new_chunk(kab), new_chunk(vab), new_chunk(kib),
                           q_pos0=past, n_keys=past + n, tq=n, tc=tc_a, caches=(cka, cva, cki))
        yb = stick_attention_cached(r3(qb), r3(kbb), r3(vbb), ckb, cvb, tc=tc_b)

    merged = merge_branches(h, ya.reshape(t, A_WIDTH), yb.reshape(t, B_WIDTH),
                            w["w_ga"], w["w_gb"], w["w_branch_a"], w["w_branch_b"], tm, _row_tile(d, TN_MERGE))
    x1, hu, top_e, rank_k, gate_k, counts = out_proj(
        merged, xf, gate1, scale2, shift2, w["norm_ffn"], w["w_out"], w["w_router_t"], w["b_router"], count0, tm)
    rows = (ka.reshape(b, n, A_KV_HEADS, HEAD_DIM), va.reshape(b, n, A_KV_HEADS, HEAD_DIM),
            ki.reshape(b, n, IDX_DIM), kb.reshape(b, n, B_HEADS, HEAD_DIM), vb.reshape(b, n, B_HEADS, HEAD_DIM))
    return dict(x1=x1, hu=hu, top_e=top_e, rank_k=rank_k, gate_k=gate_k, counts=counts, gate2=gate2, rows=rows)


def _routing(counts, top_e, rank_k, tme):
    e = counts.shape[0]
    t = top_e.shape[1]
    counts = counts.reshape(e).astype(jnp.int32)
    padded = (counts + tme - 1) // tme * tme
    ends = jnp.cumsum(padded)
    offsets = ends - padded
    eids = jnp.arange(e, dtype=jnp.int32)[:, None, None]
    offset_k = jnp.sum(jnp.where(top_e[None] == eids, offsets[:, None, None], 0), axis=0)
    slots = offset_k + rank_k - 1
    n_tiles = (t * N_ACTIVE + e * tme) // tme
    tile_start = jnp.arange(n_tiles, dtype=jnp.int32) * tme
    tile_expert = jnp.minimum(jnp.sum(ends[None, :] <= tile_start[:, None], axis=1), e - 1).astype(jnp.int32)
    n_used = (ends[-1] // tme).astype(jnp.int32).reshape(1)
    ids = jnp.arange(e, dtype=jnp.int32)
    nonempty = padded > 0
    later = jnp.where((ids[None, :] > ids[:, None]) & nonempty[None, :], ids[None, :], e)
    next_e = jnp.min(later, axis=1)
    next_e = jnp.where(next_e == e, -1, next_e).astype(jnp.int32)
    rank_e = jnp.cumsum(nonempty.astype(jnp.int32)) - 1
    tiles = dict(expert=tile_expert, n_used=n_used,
                 first=(tile_start == offsets[tile_expert]).astype(jnp.int32),
                 next=next_e[tile_expert], parity=(rank_e[tile_expert] % 2).astype(jnp.int32))
    return dict(slots=slots.T.reshape(-1), tiles=tiles, n_used=n_used, p_rows=n_tiles * tme,
                pad_start=offsets + counts, pad_count=padded - counts)


def kernel(x_prompt, x_sample, cache_a_k, cache_a_v, cache_a_idx_k, cache_b_k, cache_b_v, c_prompt, c_sample,
           norm_mix, norm_ffn, w_ada, b_ada, w_in, q_norm_a, k_norm_a, w_branch_a, w_branch_b, w_out,
           w_router, b_router, w_gate_e, w_up_e, w_down_e, w_gate_s, w_up_s, w_down_s):
    depth = norm_mix.shape[0]
    d = x_prompt.shape[-1]
    bp, bs = c_prompt.shape[0], c_sample.shape[0]
    c_all = jnp.concatenate([c_prompt, c_sample], axis=0)
    c_all = jnp.pad(c_all, ((0, -(bp + bs) % (2 * SUBLANES)), (0, 0)))
    tme = TM_EXPERT

    xp, xs = x_prompt, x_sample
    prompt_rows, sample_rows = [], []
    for layer in range(depth):
        wl = w_in[layer]
        o_idx = A_WIDTH + 2 * A_KV_WIDTH + IDX_Q_WIDTH
        o_b = o_idx + IDX_DIM + IDX_HEADS
        o_g = o_b + 3 * B_WIDTH
        w_in_a = jnp.concatenate(
            [wl[:, :o_b], jnp.zeros((d, LANES - IDX_DIM - IDX_HEADS), wl.dtype)], axis=1).astype(BF16)
        w = dict(
            norm_mix=norm_mix[layer].reshape(1, d), norm_ffn=norm_ffn[layer].reshape(1, d),
            q_norm_a=q_norm_a[layer].reshape(1, HEAD_DIM), k_norm_a=k_norm_a[layer].reshape(1, HEAD_DIM),
            w_in_a=w_in_a, w_in_b=wl[:, o_b:o_g].astype(BF16),
            w_ga=wl[:, o_g:o_g + d].astype(BF16), w_gb=wl[:, o_g + d:].astype(BF16),
            w_branch_a=w_branch_a[layer].astype(BF16), w_branch_b=w_branch_b[layer].astype(BF16),
            w_out=w_out[layer].astype(BF16), w_router_t=w_router[layer].T.astype(BF16),
            b_router=b_router[layer].reshape(N_EXPERTS, 1))
        mod = ada_mod(c_all, w_ada[layer], b_ada[layer])
        caches = (cache_a_k[layer], cache_a_v[layer], cache_a_idx_k[layer], cache_b_k[layer], cache_b_v[layer])
        sp = _token_stage(xp, mod[:bp], w, None, jnp.zeros((N_EXPERTS, 1), F32))
        ss = _token_stage(xs, mod[bp:bp + bs], w, caches, sp["counts"])
        prompt_rows.append(sp["rows"])
        sample_rows.append(ss["rows"])

        tp, ts = sp["x1"].shape[0], ss["x1"].shape[0]
        both = lambda name: jnp.concatenate([sp[name], ss[name]], axis=1)
        rt = _routing(ss["counts"], both("top_e"), both("rank_k"), tme)
        slots, gate_k = rt["slots"], both("gate_k").T
        xsorted = dispatch(rt, slots, sp["hu"], ss["hu"], tme)
        ysorted = expert_ffn(rt["tiles"], xsorted, w_gate_e[layer], w_up_e[layer], w_down_e[layer], tme)
        shared = (w_gate_s[layer].astype(BF16), w_up_s[layer].astype(BF16), w_down_s[layer].astype(BF16))
        yp = combine(slots[:tp * N_ACTIVE], sp["hu"], sp["x1"], sp["gate2"], gate_k[:tp], *shared, ysorted,
                     _row_tile(tp, TM_ROWS))
        ys = combine(slots[tp * N_ACTIVE:], ss["hu"], ss["x1"], ss["gate2"], gate_k[tp:], *shared, ysorted,
                     _row_tile(ts, TM_ROWS))
        xp = yp.reshape(xp.shape)
        xs = ys.reshape(xs.shape)

    stack = lambda rows: tuple(jnp.stack(r) for r in zip(*rows))
    return (xp, xs) + stack(prompt_rows) + stack(sample_rows)
```

```python
import functools

import jax
import jax.numpy as jnp
from jax import lax
from jax.experimental import pallas as pl
from jax.experimental.pallas import tpu as pltpu

CHUNK = 64
HEAD_DIM = 128
A_HEADS = 8
A_KV_HEADS = 2
A_GROUP = A_HEADS // A_KV_HEADS
IDX_HEADS = 8
IDX_DIM = 64
TOPK_KEYS = 256
B_HEADS = 8
N_EXPERTS = 64
N_ACTIVE = 8
N_GROUPS = 8
N_ACTIVE_GROUPS = 4
GROUP_SIZE = N_EXPERTS // N_GROUPS
ROUTED_SCALE = 2.5
EPS = 1e-6

A_WIDTH = A_HEADS * HEAD_DIM
A_KV_WIDTH = A_KV_HEADS * HEAD_DIM
IDX_Q_WIDTH = IDX_HEADS * IDX_DIM
B_WIDTH = B_HEADS * HEAD_DIM

LANES = 128
SUBLANES = 8
MOD_GROUP = 32
VMEM_LIMIT = 56 * 1024 * 1024
NEG_BIG = -1e30
STICK_UNDERFLOW = -110.0
INT_MIN = -(2 ** 31)

TM_DENSE = 512
TN_MERGE = 512
TN_ADA = 1024
TQ_DSA, TC_DSA = 256, 512
TQ_STICK, TC_STICK = 256, 256
TM_EXPERT = 256
TM_ROWS = 256

F32 = jnp.float32
BF16 = jnp.bfloat16
NT_DIMS = (((1,), (1,)), ((), ()))


def _params(*sem):
    return pltpu.CompilerParams(dimension_semantics=sem, vmem_limit_bytes=VMEM_LIMIT)


def _silu(x):
    return x * jax.nn.sigmoid(x)


def _rms(x, gain):
    return x * lax.rsqrt(jnp.mean(x * x, axis=-1, keepdims=True) + EPS) * gain


def _modulate(y, scale_g, shift_g):
    tm, d = y.shape
    y3 = y.reshape(tm // MOD_GROUP, MOD_GROUP, d)
    out = y3 * (1.0 + scale_g[:, None, :]) + shift_g[:, None, :]
    return out.reshape(tm, d)


def _gate_rows(y, gate_g):
    tm, d = y.shape
    return (y.reshape(tm // MOD_GROUP, MOD_GROUP, d) * gate_g[:, None, :]).reshape(tm, d)


def _row_tile(t, pref):
    tm = min(pref, t)
    while t % tm:
        tm //= 2
    return tm


def _ada_kernel(c_ref, w_ref, b_ref, o_ref):
    s = _silu(c_ref[...]).astype(BF16)
    o_ref[...] = jnp.dot(s, w_ref[...].astype(BF16), preferred_element_type=F32) + b_ref[...]


def ada_mod(c, w_ada, b_ada):
    r, d = c.shape
    n = w_ada.shape[1]
    tn = _row_tile(n, TN_ADA)
    return pl.pallas_call(
        _ada_kernel,
        grid=(n // tn,),
        in_specs=[pl.BlockSpec((r, d), lambda j: (0, 0)),
                  pl.BlockSpec((d, tn), lambda j: (0, j)),
                  pl.BlockSpec((1, tn), lambda j: (0, j))],
        out_specs=pl.BlockSpec((r, tn), lambda j: (0, j)),
        out_shape=jax.ShapeDtypeStruct((r, n), F32),
        compiler_params=_params("arbitrary"),
        name="ada_mod",
    )(c, w_ada, b_ada.reshape(1, n))


def _in_a_kernel(x_ref, nw_ref, sc_ref, sh_ref, w_ref, qn_ref, kn_ref,
                 h_ref, qa_ref, ka_ref, va_ref, kab_ref, vab_ref, qi_ref, ki_ref, kib_ref, wi_ref):
    h = _modulate(_rms(x_ref[...], nw_ref[...]), sc_ref[...], sh_ref[...])
    hb = h.astype(BF16)
    h_ref[...] = hb
    off = 0
    for hd in range(A_HEADS):
        q = jnp.dot(hb, w_ref[:, off:off + HEAD_DIM], preferred_element_type=F32)
        qa_ref[:, hd * HEAD_DIM:(hd + 1) * HEAD_DIM] = (_rms(q, qn_ref[...]) * (HEAD_DIM ** -0.5)).astype(BF16)
        off += HEAD_DIM
    for hd in range(A_KV_HEADS):
        k = jnp.dot(hb, w_ref[:, off:off + HEAD_DIM], preferred_element_type=F32)
        k = _rms(k, kn_ref[...])
        ka_ref[:, hd * HEAD_DIM:(hd + 1) * HEAD_DIM] = k
        kab_ref[:, hd * HEAD_DIM:(hd + 1) * HEAD_DIM] = k.astype(BF16)
        off += HEAD_DIM
    v = jnp.dot(hb, w_ref[:, off:off + A_KV_WIDTH], preferred_element_type=F32)
    va_ref[...] = v
    vab_ref[...] = v.astype(BF16)
    off += A_KV_WIDTH
    qi = jnp.dot(hb, w_ref[:, off:off + IDX_Q_WIDTH], preferred_element_type=F32)
    qi_ref[...] = (qi * (IDX_DIM ** -0.5)).astype(BF16)
    off += IDX_Q_WIDTH
    kw = jnp.dot(hb, w_ref[:, off:off + LANES], preferred_element_type=F32)
    ki = kw[:, :IDX_DIM]
    ki_ref[...] = ki
    kib_ref[...] = ki.astype(BF16)
    wi_ref[...] = kw


def in_proj_a(x, norm_w, scale_g, shift_g, w_a, q_norm, k_norm, tm):
    t, d = x.shape
    wa = w_a.shape[1]
    gm = tm // MOD_GROUP
    row = lambda w: pl.BlockSpec((tm, w), lambda i: (i, 0))
    const = lambda a, b: pl.BlockSpec((a, b), lambda i: (0, 0))
    outs = [(d, BF16), (A_WIDTH, BF16), (A_KV_WIDTH, F32), (A_KV_WIDTH, F32), (A_KV_WIDTH, BF16),
            (A_KV_WIDTH, BF16), (IDX_Q_WIDTH, BF16), (IDX_DIM, F32), (IDX_DIM, BF16), (LANES, F32)]
    return pl.pallas_call(
        _in_a_kernel,
        grid=(t // tm,),
        in_specs=[row(d), const(1, d), pl.BlockSpec((gm, d), lambda i: (i, 0)),
                  pl.BlockSpec((gm, d), lambda i: (i, 0)), const(d, wa),
                  const(1, HEAD_DIM), const(1, HEAD_DIM)],
        out_specs=[row(w) for w, _ in outs],
        out_shape=[jax.ShapeDtypeStruct((t, w), dt) for w, dt in outs],
        compiler_params=_params("arbitrary"),
        name="in_proj_a",
    )(x, norm_w, scale_g, shift_g, w_a, q_norm, k_norm)


def _in_b_kernel(h_ref, w_ref, qb_ref, kb_ref, vb_ref, kbb_ref, vbb_ref):
    hb = h_ref[...]
    qb_ref[...] = jnp.dot(hb, w_ref[:, :B_WIDTH], preferred_element_type=F32).astype(BF16)
    k = jnp.dot(hb, w_ref[:, B_WIDTH:2 * B_WIDTH], preferred_element_type=F32)
    kbb_ref[...] = k.astype(BF16)
    v = jnp.dot(hb, w_ref[:, 2 * B_WIDTH:], preferred_element_type=F32)
    vbb_ref[...] = v.astype(BF16)
    for hd in range(B_HEADS):
        kb_ref[:, hd, :] = k[:, hd * HEAD_DIM:(hd + 1) * HEAD_DIM]
        vb_ref[:, hd, :] = v[:, hd * HEAD_DIM:(hd + 1) * HEAD_DIM]


def in_proj_b(h, w_b, tm):
    t, d = h.shape
    row = lambda w: pl.BlockSpec((tm, w), lambda i: (i, 0))
    heads = pl.BlockSpec((tm, B_HEADS, HEAD_DIM), lambda i: (i, 0, 0))
    flat = lambda dt: jax.ShapeDtypeStruct((t, B_WIDTH), dt)
    per_head = jax.ShapeDtypeStruct((t, B_HEADS, HEAD_DIM), F32)
    return pl.pallas_call(
        _in_b_kernel,
        grid=(t // tm,),
        in_specs=[row(d), pl.BlockSpec((d, 3 * B_WIDTH), lambda i: (0, 0))],
        out_specs=[row(B_WIDTH), heads, heads, row(B_WIDTH), row(B_WIDTH)],
        out_shape=[flat(BF16), per_head, per_head, flat(BF16), flat(BF16)],
        compiler_params=_params("arbitrary"),
        name="in_proj_b",
    )(h, w_b)


def _score_key(x):
    bits = lax.bitcast_convert_type(x, jnp.int32)
    return bits ^ ((bits >> 31) & 0x7FFFFFFF)


def _lane_tiles(a):
    return [a[:, j * LANES:(j + 1) * LANES] for j in range(a.shape[1] // LANES)]


def _dsa_kernel(qa_ref, qi_ref, kw_ref, k_ref, v_ref, ki_ref, *rest, q_pos0, n_keys, n_sel, tq, tc, pos_bits,
                cached):
    if cached:
        ck_ref, cv_ref, cki_ref, o_ref, key_ref, m_ref, l_ref, acc_ref = rest
    else:
        o_ref, key_ref, m_ref, l_ref, acc_ref = rest

    def read_k(start, g, last):
        if cached and not last:
            return ck_ref[pl.ds(start, tc), g, :].astype(BF16)
        rows = slice(None) if cached else pl.ds(start, tc)
        return k_ref[rows, g * HEAD_DIM:(g + 1) * HEAD_DIM]

    def read_v(start, g, last):
        if cached and not last:
            return cv_ref[pl.ds(start, tc), g, :].astype(BF16)
        rows = slice(None) if cached else pl.ds(start, tc)
        return v_ref[rows, g * HEAD_DIM:(g + 1) * HEAD_DIM]

    def read_ki(start, last):
        if cached and not last:
            return cki_ref[pl.ds(start, tc), :].astype(BF16)
        return ki_ref[...] if cached else ki_ref[pl.ds(start, tc), :]

    i = pl.program_id(1)
    q0 = q_pos0 + i * tq
    kmax = jnp.minimum(((q0 + tq - 1) // CHUNK + 1) * CHUNK, n_keys)
    nch = (kmax + tc - 1) // tc
    qpos = q0 + lax.broadcasted_iota(jnp.int32, (tq, 1), 0)

    tqs = key_ref.shape[2]
    lane_q = lax.broadcasted_iota(jnp.int32, (1, tqs), 1)
    n_adm = jnp.minimum(((q0 + lane_q) // CHUNK + 1) * CHUNK, n_keys)
    pad_q = lambda a: a if tqs == tq else jnp.concatenate(
        [a, jnp.zeros((tqs - tq, a.shape[1]), a.dtype)], axis=0)
    head_w = pad_q(kw_ref[...]).T[IDX_DIM:IDX_DIM + IDX_HEADS] * (IDX_HEADS ** -0.5)
    qi_all = jnp.concatenate([pad_q(qi_ref[:, h * IDX_DIM:(h + 1) * IDX_DIM]) for h in range(IDX_HEADS)], axis=0)

    def all_chunks(chunk_pass):
        def body(c, carry):
            chunk_pass(c, False)
            return carry
        lax.fori_loop(0, nch - 1, body, 0)
        chunk_pass(nch - 1, True)

    def score_pass(c, last):
        start = pl.multiple_of(c * tc, tc)
        lg = lax.dot_general(read_ki(start, last), qi_all, NT_DIMS, preferred_element_type=F32)
        s = jnp.zeros((tc, tqs), F32)
        for h in range(IDX_HEADS):
            s = s + jnp.maximum(lg[:, h * tqs:(h + 1) * tqs], 0.0) * head_w[h:h + 1, :]
        adm = start + lax.broadcasted_iota(jnp.int32, (tc, 1), 0) < n_adm
        key_ref[c] = jnp.where(adm, _score_key(s), INT_MIN)

    all_chunks(score_pass)

    def count_rows(pred):
        def body(c, acc):
            start = pl.multiple_of(c * tc, tc)
            hit = jnp.where(pred(key_ref[c], start), 1.0, 0.0)
            return acc + jnp.sum(hit.reshape(tc // acc_rows, acc_rows, tqs), axis=0)
        acc_rows = 4 * SUBLANES
        acc = lax.fori_loop(0, nch, body, jnp.zeros((acc_rows, tqs), F32))
        return jnp.sum(acc, axis=0, keepdims=True)

    def count_ge(cand):
        return count_rows(lambda key, start: key >= cand)

    def pending(done):
        return jnp.min(done) < 0.5

    cnt = count_ge(jnp.zeros((1, tqs), jnp.int32))
    thr = jnp.where(cnt >= n_sel, 0, INT_MIN).astype(jnp.int32)
    done =jnp.where(jnp.logical_or(jnp.logical_or(n_adm <= n_sel, cnt == n_sel), lane_q >= tq), 1.0, 0.0)

    def bit_cond(state):
        b, _, done = state
        return jnp.logical_and(b < 31, pending(done))

    def bit_body(state):
        b, thr, done = state
        cand = thr | jnp.left_shift(jnp.int32(1), 30 - b)
        cnt = count_ge(cand)
        return (b + 1, jnp.where(cnt >= n_sel, cand, thr), jnp.where(cnt == n_sel, 1.0, done))

    _, thr, done = lax.while_loop(bit_cond, bit_body, (jnp.int32(0), thr, done))

    def tie_limit():
        need = n_sel - count_rows(lambda key, start: key > thr)

        def tied_below(limit):
            return count_rows(lambda key, start: jnp.logical_and(
                key == thr, start + lax.broadcasted_iota(jnp.int32, (tc, 1), 0) < limit))

        def body(b, lim):
            cand = lim | jnp.left_shift(jnp.int32(1), pos_bits - 1 - b)
            return jnp.where(tied_below(cand) < need, cand, lim)

        lim = lax.fori_loop(0, pos_bits, body, jnp.zeros((1, tqs), jnp.int32))
        return jnp.where(done > 0.5, jnp.int32(2 ** pos_bits), lim + 1)

    plim = lax.cond(pending(done), tie_limit, lambda: jnp.full((1, tqs), 2 ** pos_bits, jnp.int32))

    slopes = [2.0 ** (-8.0 * (h + 1) / A_HEADS) for h in range(A_HEADS)]
    lane = lax.broadcasted_iota(jnp.int32, (1, LANES), 1)
    rows4 = A_GROUP * tq
    q_aug = []
    for g in range(A_KV_HEADS):
        heads = range(g * A_GROUP, (g + 1) * A_GROUP)
        q4 = jnp.concatenate([qa_ref[:, h * HEAD_DIM:(h + 1) * HEAD_DIM] for h in heads], axis=0)
        slope_col = jnp.concatenate([jnp.full((tq, 1), slopes[h], F32) for h in heads], axis=0)
        q_aug.append(jnp.concatenate([q4, jnp.where(lane < 3, slope_col, 0.0).astype(BF16)], axis=1))

    def chunk_logits(c, last):
        start = pl.multiple_of(c * tc, tc)
        kp = start + lax.broadcasted_iota(jnp.int32, (tc, 1), 0)
        pos_cols = jnp.where(lane == 0, kp & 63,
                             jnp.where(lane == 1, kp & (127 << 6),
                                       jnp.where(lane == 2, kp & ~8191, 0))).astype(F32).astype(BF16)
        key = key_ref[c]
        sel = jnp.logical_or(key > thr, jnp.logical_and(key == thr, kp < plim))
        mb = jnp.where(jnp.logical_and(sel, key != INT_MIN), 0.0, NEG_BIG).T[:tq]
        kpos = start + lax.broadcasted_iota(jnp.int32, (1, tc), 1)
        out = []
        for g in range(A_KV_HEADS):
            kc = jnp.concatenate([read_k(start, g, last), pos_cols], axis=1)
            lg = lax.dot_general(q_aug[g], kc, NT_DIMS, preferred_element_type=F32).reshape(A_GROUP, tq, tc)
            if last:
                ahead = jnp.maximum(kpos - qpos, 0).astype(F32)
                bias = jnp.stack([mb - (2.0 * slopes[g * A_GROUP + r]) * ahead for r in range(A_GROUP)])
            else:
                bias = mb[None]
            out.append((lg + bias).reshape(rows4, tc))
        return start, out

    m_ref[...] = jnp.full(m_ref.shape, NEG_BIG, F32)
    l_ref[...] = jnp.zeros(l_ref.shape, F32)
    acc_ref[...] = jnp.zeros(acc_ref.shape, F32)

    def attend(c, last):
        start, logits = chunk_logits(c, last)
        for g in range(A_KV_HEADS):
            rows = slice(g * rows4, (g + 1) * rows4)
            tiles = _lane_tiles(logits[g])
            m_old = m_ref[rows]
            m_new = jnp.maximum(m_old, jnp.max(functools.reduce(jnp.maximum, tiles), axis=1, keepdims=True))
            alpha = jnp.exp(m_old - m_new)
            p = [jnp.exp(t - m_new) for t in tiles]
            m_ref[rows] = m_new
            l_ref[rows] = alpha * l_ref[rows] + functools.reduce(jnp.add, p)
            acc_ref[rows] = alpha * acc_ref[rows] + jnp.dot(jnp.concatenate(p, axis=1).astype(BF16),
                                                            read_v(start, g, last), preferred_element_type=F32)

    all_chunks(attend)
    for h in range(A_HEADS):
        rows = slice(h * tq, (h + 1) * tq)
        o_ref[:, h * HEAD_DIM:(h + 1) * HEAD_DIM] = (
            acc_ref[rows] / jnp.sum(l_ref[rows], axis=1, keepdims=True)).astype(BF16)


def dsa_attention(qa, qi, wi, k, v, ki, *, q_pos0, n_keys, tq, tc, caches=()):
    b, n, _ = qa.shape
    if caches:
        assert q_pos0 % tc == 0 and n == tq <= tc and k.shape[1] == tc and n_keys == q_pos0 + n
        l_pad = q_pos0 + tc
    else:
        l_pad = k.shape[1]
    n_sel = min(TOPK_KEYS, n_keys // 4)
    qspec = lambda w: pl.BlockSpec((None, tq, w), lambda bi, i: (bi, i, 0))
    kspec = lambda w: pl.BlockSpec((None, k.shape[1], w), lambda bi, i: (bi, 0, 0))
    cache_specs = [pl.BlockSpec((None,) + c.shape[1:], lambda bi, i, nd=c.ndim: (bi,) + (0,) * (nd - 1))
                   for c in caches]
    kern = functools.partial(_dsa_kernel, q_pos0=q_pos0, n_keys=n_keys, n_sel=n_sel, tq=tq, tc=tc,
                             pos_bits=max(1, (l_pad - 1).bit_length()), cached=bool(caches))
    return pl.pallas_call(
        kern,
        grid=(b, n // tq),
        in_specs=[qspec(A_WIDTH), qspec(IDX_Q_WIDTH), qspec(LANES),
                  kspec(A_KV_WIDTH), kspec(A_KV_WIDTH), kspec(IDX_DIM)] + cache_specs,
        out_specs=qspec(A_WIDTH),
        out_shape=jax.ShapeDtypeStruct((b, n, A_WIDTH), BF16),
        scratch_shapes=[pltpu.VMEM((l_pad // tc, tc, max(tq, LANES)), jnp.int32),
                        pltpu.VMEM((A_HEADS * tq, LANES), F32),
                        pltpu.VMEM((A_HEADS * tq, LANES), F32),
                        pltpu.VMEM((A_HEADS * tq, HEAD_DIM), F32)],
        compiler_params=_params("arbitrary", "arbitrary"),
        name="dsa",
    )(qa, qi, wi, k, v, ki, *caches)


STICK_HEADS_PER_STEP = 4


def _strict_upper(n):
    return (lax.broadcasted_iota(jnp.int32, (n, n), 0) > lax.broadcasted_iota(jnp.int32, (n, n), 1)).astype(BF16)


def _stick_step(q, kc, vc, strict, upper, tail, acc):
    z = lax.dot_general(q, kc, NT_DIMS, preferred_element_type=F32) * (HEAD_DIM ** -0.5)
    softplus = jnp.log(1.0 + jnp.exp(-jnp.abs(z)))
    log_go = jnp.minimum(z, 0.0) - softplus
    log_stay = jnp.where(strict, jnp.minimum(-z, 0.0) - softplus, 0.0)
    hi = log_stay.astype(BF16)
    lo = (log_stay - hi.astype(F32)).astype(BF16)
    after = jnp.dot(hi, upper, preferred_element_type=F32) + jnp.dot(lo, upper, preferred_element_type=F32) + tail
    w = jnp.where(strict, jnp.exp(log_go + after), 0.0)
    return (tail + jnp.sum(log_stay, axis=1, keepdims=True),
            acc + jnp.dot(w.astype(BF16), vc, preferred_element_type=F32))


def _stick_walk(qs, tpos, c_first, tc, read_kv, tails, accs):
    upper = _strict_upper(tc)

    def cond(carry):
        n, tails, _ = carry
        return jnp.logical_and(n <= c_first, jnp.max(functools.reduce(jnp.maximum, tails)) > STICK_UNDERFLOW)

    def body(carry):
        n, tails, accs = carry
        start = pl.multiple_of((c_first - n) * tc, tc)
        strict = start + lax.broadcasted_iota(jnp.int32, (1, tc), 1) < tpos
        new = [_stick_step(q, *read_kv(start, h), strict, upper, tail, acc)
               for h, (q, tail, acc) in enumerate(zip(qs, tails, accs))]
        return n + 1, tuple(t for t, _ in new), tuple(a for _, a in new)

    _, _, accs = lax.while_loop(cond, body, (jnp.int32(0), tuple(tails), tuple(accs)))
    return accs


def _stick_kernel(q_ref, k_ref, v_ref, o_ref, *, q_pos0, tq, tc):
    t0 = q_pos0 + pl.program_id(2) * tq
    tpos = t0 + lax.broadcasted_iota(jnp.int32, (tq, 1), 0)
    heads = [slice(h * HEAD_DIM, (h + 1) * HEAD_DIM) for h in range(STICK_HEADS_PER_STEP)]
    read_kv = lambda start, h: (k_ref[pl.ds(start, tc), heads[h]], v_ref[pl.ds(start, tc), heads[h]])
    zeros = lambda w: [jnp.zeros((tq, w), F32) for _ in heads]
    accs = _stick_walk([q_ref[:, hs] for hs in heads], tpos, (t0 + tq - 2) // tc, tc, read_kv,
                       zeros(1), zeros(HEAD_DIM))
    for hs, acc in zip(heads, accs):
        o_ref[:, hs] = acc.astype(BF16)


def _stick_cached_kernel(q_ref, k_ref, v_ref, ck_ref, cv_ref, o_ref, *, past, tc):
    n = q_ref.shape[0]
    tpos = past + lax.broadcasted_iota(jnp.int32, (n, 1), 0)
    strict = past + lax.broadcasted_iota(jnp.int32, (1, n), 1) < tpos
    upper = _strict_upper(n)
    for h0 in range(0, B_HEADS, STICK_HEADS_PER_STEP):
        group = range(h0, h0 + STICK_HEADS_PER_STEP)
        cols = [slice(h * HEAD_DIM, (h + 1) * HEAD_DIM) for h in group]
        qs = [q_ref[:, hs] for hs in cols]
        first = [_stick_step(q, k_ref[:, hs], v_ref[:, hs], strict, upper,
                             jnp.zeros((n, 1), F32), jnp.zeros((n, HEAD_DIM), F32)) for q, hs in zip(qs, cols)]
        read_kv = lambda start, h, h0=h0: (ck_ref[pl.ds(start, tc), h0 + h, :].astype(BF16),
                                           cv_ref[pl.ds(start, tc), h0 + h, :].astype(BF16))
        accs = _stick_walk(qs, tpos, past // tc - 1, tc, read_kv, [t for t, _ in first], [a for _, a in first])
        for hs, acc in zip(cols, accs):
            o_ref[:, hs] = acc.astype(BF16)


def stick_attention(qb, k, v, *, q_pos0, tq, tc):
    b, n, _ = qb.shape
    l_pad = k.shape[1]
    width = STICK_HEADS_PER_STEP * HEAD_DIM
    qspec = pl.BlockSpec((None, tq, width), lambda bi, h, i: (bi, i, h))
    kspec = pl.BlockSpec((None, l_pad, width), lambda bi, h, i: (bi, 0, h))
    kern = functools.partial(_stick_kernel, q_pos0=q_pos0, tq=tq, tc=tc)
    return pl.pallas_call(
        kern,
        grid=(b, B_HEADS // STICK_HEADS_PER_STEP, n // tq),
        in_specs=[qspec, kspec, kspec],
        out_specs=qspec,
        out_shape=jax.ShapeDtypeStruct((b, n, B_WIDTH), BF16),
        compiler_params=_params("arbitrary", "arbitrary", "arbitrary"),
        name="stick",
    )(qb, k, v)


def stick_attention_cached(qb, k_new, v_new, cache_k, cache_v, *, tc):
    b, n, _ = qb.shape
    past = cache_k.shape[1]
    assert past % tc == 0
    new = pl.BlockSpec((None, n, B_WIDTH), lambda bi: (bi, 0, 0))
    cache = pl.BlockSpec((None, past, B_HEADS, HEAD_DIM), lambda bi: (bi, 0, 0, 0))
    return pl.pallas_call(
        functools.partial(_stick_cached_kernel, past=past, tc=tc),
        grid=(b,),
        in_specs=[new, new, new, cache, cache],
        out_specs=new,
        out_shape=jax.ShapeDtypeStruct((b, n, B_WIDTH), BF16),
        compiler_params=_params("arbitrary"),
        name="stick_cached",
    )(qb, k_new, v_new, cache_k, cache_v)


def _merge_kernel(h_ref, ya_ref, yb_ref, wga_ref, wgb_ref, wa_ref, wb_ref, o_ref):
    hb = h_ref[...]
    ga = jax.nn.sigmoid(jnp.dot(hb, wga_ref[...], preferred_element_type=F32))
    gb = jax.nn.sigmoid(jnp.dot(hb, wgb_ref[...], preferred_element_type=F32))
    pa = jnp.dot(ya_ref[...], wa_ref[...], preferred_element_type=F32)
    pb = jnp.dot(yb_ref[...], wb_ref[...], preferred_element_type=F32)
    o_ref[...] = (ga * pa + gb * pb).astype(BF16)


def merge_branches(h, ya, yb, w_ga, w_gb, w_a, w_b, tm, tn):
    t, d = h.shape
    row = lambda w: pl.BlockSpec((tm, w), lambda i, j: (i, 0))
    col = lambda k: pl.BlockSpec((k, tn), lambda i, j: (0, j))
    return pl.pallas_call(
        _merge_kernel,
        grid=(t // tm, d // tn),
        in_specs=[row(d), row(A_WIDTH), row(B_WIDTH), col(d), col(d), col(A_WIDTH), col(B_WIDTH)],
        out_specs=pl.BlockSpec((tm, tn), lambda i, j: (i, j)),
        out_shape=jax.ShapeDtypeStruct((t, d), BF16),
        compiler_params=_params("arbitrary", "arbitrary"),
        name="merge",
    )(h, ya, yb, w_ga, w_gb, w_a, w_b)


def _pack_bf16_pair(lo, hi):
    lo_bits = lax.bitcast_convert_type(lo.astype(BF16).astype(F32), jnp.uint32)
    hi_bits = lax.bitcast_convert_type(hi.astype(BF16).astype(F32), jnp.uint32)
    return (lo_bits >> 16) | (hi_bits & jnp.uint32(0xFFFF0000))


def _unpack_f32_pair(u):
    return (lax.bitcast_convert_type(u << 16, F32),
            lax.bitcast_convert_type(u & jnp.uint32(0xFFFF0000), F32))


def _unpack_bf16_pair(u):
    lo, hi = _unpack_f32_pair(u)
    return lo.astype(BF16), hi.astype(BF16)


def _first_max(vals, idx, axes, n):
    m = vals
    for ax in axes:
        m = jnp.max(m, axis=ax, keepdims=True)
    first = jnp.where(vals == m, idx, n)
    for ax in axes:
        first = jnp.min(first, axis=ax, keepdims=True)
    return m, first


def _sum01(a):
    return jnp.sum(jnp.sum(a, axis=0, keepdims=True), axis=1, keepdims=True)


def _out_kernel(mg_ref, x_ref, g1_ref, sc_ref, sh_ref, nw_ref, wo_ref, wr_ref, br_ref, cnt0_ref,
                x1_ref, hu_ref, tope_ref, rank_ref, gatek_ref, cnt_out_ref, cnt_ref):
    tm, d = x_ref.shape
    proj = jnp.dot(mg_ref[...], wo_ref[...], preferred_element_type=F32)
    x1 = x_ref[...] + _gate_rows(proj, g1_ref[...])
    x1_ref[...] = x1
    h2 = _modulate(_rms(x1, nw_ref[...]), sc_ref[...], sh_ref[...])
    hu_ref[...] = _pack_bf16_pair(h2[:, :d // 2], h2[:, d // 2:])

    logits = lax.dot_general(wr_ref[...], h2.astype(BF16), NT_DIMS, preferred_element_type=F32)
    aff = jax.nn.sigmoid(logits)
    shape3 = (N_GROUPS, GROUP_SIZE, tm)
    aff3 = aff.reshape(shape3)
    biased = (aff + br_ref[...]).reshape(shape3)
    io_in = lax.broadcasted_iota(jnp.int32, shape3, 1)
    m1, i1 = _first_max(biased, io_in, (1,), GROUP_SIZE)
    m2 = jnp.max(jnp.where(io_in == i1, -jnp.inf, biased), axis=1, keepdims=True)
    gscore = m1 + m2
    gio = lax.broadcasted_iota(jnp.int32, gscore.shape, 0)
    keep = jnp.zeros(gscore.shape, jnp.bool_)
    for _ in range(N_ACTIVE_GROUPS):
        _, first = _first_max(gscore, gio, (0,), N_GROUPS)
        hit = gio == first
        keep = jnp.logical_or(keep, hit)
        gscore = jnp.where(hit, -jnp.inf, gscore)
    masked = jnp.where(jnp.broadcast_to(keep, shape3), biased, -jnp.inf)
    eio = lax.broadcasted_iota(jnp.int32, shape3, 0) * GROUP_SIZE + io_in
    sel = jnp.zeros(shape3, jnp.bool_)
    hits = []
    for _ in range(N_ACTIVE):
        _, first = _first_max(masked, eio, (0, 1), N_EXPERTS)
        hit = eio == first
        sel = jnp.logical_or(sel, hit)
        masked = jnp.where(hit, -jnp.inf, masked)
        hits.append((hit, first.reshape(1, tm)))
    sel_aff = jnp.where(sel, aff3, 0.0)
    gates = sel_aff / _sum01(sel_aff) * ROUTED_SCALE

    @pl.when(pl.program_id(0) == 0)
    def _():
        cnt_ref[...] = cnt0_ref[...]

    sel2 = jnp.where(sel, 1.0, 0.0).reshape(N_EXPERTS, tm)
    incl = (lax.broadcasted_iota(jnp.int32, (tm, tm), 0)
            <= lax.broadcasted_iota(jnp.int32, (tm, tm), 1)).astype(BF16)
    rank = jnp.dot(sel2.astype(BF16), incl, preferred_element_type=F32) + cnt_ref[...]
    cnt_ref[...] = cnt_ref[...] + jnp.sum(sel2, axis=1, keepdims=True)
    cnt_out_ref[...] = cnt_ref[...]
    rank3 = rank.reshape(shape3)
    tope_ref[...] = jnp.concatenate([first for _, first in hits], axis=0)
    rank_ref[...] = jnp.concatenate(
        [_sum01(jnp.where(hit, rank3, 0.0)).reshape(1, tm) for hit, _ in hits], axis=0).astype(jnp.int32)
    gatek_ref[...] = jnp.concatenate(
        [_sum01(jnp.where(hit, gates, 0.0)).reshape(1, tm) for hit, _ in hits], axis=0)


def out_proj(merged, x, gate1_g, scale_g, shift_g, norm_w, w_out, w_router_t, b_router, count0, tm):
    t, d = x.shape
    gm = tm // MOD_GROUP
    row = lambda w: pl.BlockSpec((tm, w), lambda i: (i, 0))
    grp = pl.BlockSpec((gm, d), lambda i: (i, 0))
    const = lambda a, b: pl.BlockSpec((a, b), lambda i: (0, 0))
    per_k = pl.BlockSpec((N_ACTIVE, tm), lambda i: (0, i))
    return pl.pallas_call(
        _out_kernel,
        grid=(t // tm,),
        in_specs=[row(d), row(d), grp, grp, grp, const(1, d), const(d, d),
                  const(N_EXPERTS, d), const(N_EXPERTS, 1), const(N_EXPERTS, 1)],
        out_specs=[row(d), row(d // 2), per_k, per_k, per_k, const(N_EXPERTS, 1)],
        out_shape=[jax.ShapeDtypeStruct((t, d), F32), jax.ShapeDtypeStruct((t, d // 2), jnp.uint32),
                   jax.ShapeDtypeStruct((N_ACTIVE, t), jnp.int32), jax.ShapeDtypeStruct((N_ACTIVE, t), jnp.int32),
                   jax.ShapeDtypeStruct((N_ACTIVE, t), F32), jax.ShapeDtypeStruct((N_EXPERTS, 1), F32)],
        scratch_shapes=[pltpu.VMEM((N_EXPERTS, 1), F32)],
        compiler_params=_params("arbitrary"),
        name="out_proj",
    )(merged, x, gate1_g, scale_g, shift_g, norm_w, w_out, w_router_t, b_router, count0)


def _pad_fill(start_ref, cnt_ref, nu_ref, xs_ref, zero_ref, sem, tme, n_tiles):
    zero_ref[...] = jnp.zeros(zero_ref.shape, zero_ref.dtype)

    def tile_copy(i):
        return pltpu.make_async_copy(zero_ref, xs_ref.at[pl.ds(pl.multiple_of(i * tme, tme), tme)], sem.at[1])

    def for_each_copy(wait):
        def pad_rows(e, carry):
            pos = start_ref[e]
            for bit in range(tme.bit_length() - 1):
                size = 1 << bit

                @pl.when((cnt_ref[e] & size) != 0)
                def _(pos=pos, size=size, bit=bit):
                    copy = pltpu.make_async_copy(zero_ref.at[pl.ds(0, size)],
                                                 xs_ref.at[pl.ds(pl.multiple_of(pos, size), size)], sem.at[2 + bit])
                    copy.wait() if wait else copy.start()

                pos = pos + (cnt_ref[e] & size)
            return carry

        def tail_tiles(i, carry):
            tile_copy(i).wait() if wait else tile_copy(i).start()
            return carry

        lax.fori_loop(0, N_EXPERTS, pad_rows, 0)
        lax.fori_loop(nu_ref[0], n_tiles, tail_tiles, 0)

    for_each_copy(wait=False)
    for_each_copy(wait=True)


def _dispatch_kernel(start_ref, cnt_ref, nu_ref, slot_ref, hp_ref, hs_ref, xs_ref, zero_ref, sem, pad_sem,
                     *, td, prompt_steps, tme, n_tiles):
    i = pl.program_id(0)

    @pl.when(i == 0)
    def _():
        _pad_fill(start_ref, cnt_ref, nu_ref, xs_ref, zero_ref, pad_sem, tme, n_tiles)

    def scatter(hu_ref):
        def row_copy(j, k):
            return pltpu.make_async_copy(hu_ref.at[pl.ds(j, 1)],
                                         xs_ref.at[pl.ds(slot_ref[j * N_ACTIVE + k], 1)], sem)

        def start(jj, carry):
            base = pl.multiple_of(jj * SUBLANES, SUBLANES)
            for r in range(SUBLANES):
                for k in range(N_ACTIVE):
                    row_copy(base + r, k).start(priority=k % 2)
            return carry

        def wait(j, carry):
            for k in range(N_ACTIVE):
                row_copy(j, k).wait()
            return carry

        lax.fori_loop(0, td // SUBLANES, start, 0)
        lax.fori_loop(0, td, wait, 0)

    @pl.when(i < prompt_steps)
    def _():
        scatter(hp_ref)

    @pl.when(i >= prompt_steps)
    def _():
        scatter(hs_ref)


def dispatch(route, slots_flat, hu_prompt, hu_sample, tme):
    tp, ts = hu_prompt.shape[0], hu_sample.shape[0]
    width = hu_prompt.shape[1]
    td = _row_tile(ts, _row_tile(tp, TM_ROWS))
    prompt_steps = tp // td
    p_rows = route["p_rows"]
    grid_spec = pltpu.PrefetchScalarGridSpec(
        num_scalar_prefetch=3,
        grid=(prompt_steps + ts // td,),
        in_specs=[pl.BlockSpec((td * N_ACTIVE,), lambda i, *_: (i,), memory_space=pltpu.SMEM),
                  pl.BlockSpec((td, width), lambda i, *_: (jnp.minimum(i, prompt_steps - 1), 0)),
                  pl.BlockSpec((td, width), lambda i, *_: (jnp.maximum(i - prompt_steps, 0), 0))],
        out_specs=pl.BlockSpec(memory_space=pl.ANY),
        scratch_shapes=[pltpu.VMEM((tme, width), jnp.uint32), pltpu.SemaphoreType.DMA(()),
                        pltpu.SemaphoreType.DMA((2 + tme.bit_length(),))])
    return pl.pallas_call(
        functools.partial(_dispatch_kernel, td=td, prompt_steps=prompt_steps, tme=tme, n_tiles=p_rows // tme),
        grid_spec=grid_spec,
        out_shape=jax.ShapeDtypeStruct((p_rows, width), jnp.uint32),
        compiler_params=_params("arbitrary"),
        name="dispatch",
    )(route["pad_start"], route["pad_count"], route["n_used"], slots_flat, hu_prompt, hu_sample)


def _expert_kernel(te_ref, nu_ref, first_ref, next_ref, par_ref, xs_ref, wg_ref, wu_ref, wd_ref, o_ref,
                   wgf_ref, wuf_ref, wdf_ref, wgb_ref, wub_ref, wdb_ref, sem):
    i = pl.program_id(0)
    used = i < nu_ref[0]

    def fetch(e, buf):
        return [pltpu.make_async_copy(w.at[e], f.at[buf], sem.at[buf, j])
                for j, (w, f) in enumerate(((wg_ref, wgf_ref), (wu_ref, wuf_ref), (wd_ref, wdf_ref)))]

    @pl.when(i == 0)
    def _():
        for copy in fetch(te_ref[0], par_ref[0]):
            copy.start()

    @pl.when(jnp.logical_and(used, first_ref[i] == 1))
    def _():
        buf = par_ref[i]
        for copy in fetch(te_ref[i], buf):
            copy.wait()

        @pl.when(next_ref[i] >= 0)
        def _():
            for copy in fetch(next_ref[i], 1 - buf):
                copy.start()

        wgb_ref[...] = wgf_ref[buf].astype(BF16)
        wub_ref[...] = wuf_ref[buf].astype(BF16)
        wdb_ref[...] = wdf_ref[buf].astype(BF16)

    @pl.when(used)
    def _():
        lo, hi = _unpack_bf16_pair(xs_ref[...])
        half = lo.shape[1]
        gate = (jnp.dot(lo, wgb_ref[:half], preferred_element_type=F32)
                + jnp.dot(hi, wgb_ref[half:], preferred_element_type=F32))
        up = (jnp.dot(lo, wub_ref[:half], preferred_element_type=F32)
              + jnp.dot(hi, wub_ref[half:], preferred_element_type=F32))
        hid = (_silu(gate) * up).astype(BF16)
        o_ref[...] = jnp.dot(hid, wdb_ref[...], preferred_element_type=F32)

    @pl.when(jnp.logical_not(used))
    def _():
        o_ref[...] = jnp.zeros(o_ref.shape, o_ref.dtype)


def expert_ffn(tiles, xs, w_gate, w_up, w_down, tme):
    p, half = xs.shape
    e, d, f = w_gate.shape
    hbm = pl.BlockSpec(memory_space=pl.ANY)
    grid_spec = pltpu.PrefetchScalarGridSpec(
        num_scalar_prefetch=5,
        grid=(p // tme,),
        in_specs=[pl.BlockSpec((tme, half), lambda i, te, nu, *_: (jnp.minimum(i, nu[0] - 1), 0)),
                  hbm, hbm, hbm],
        out_specs=pl.BlockSpec((tme, d), lambda i, *_: (i, 0)),
        scratch_shapes=[pltpu.VMEM((2, d, f), F32), pltpu.VMEM((2, d, f), F32), pltpu.VMEM((2, f, d), F32),
                        pltpu.VMEM((d, f), BF16), pltpu.VMEM((d, f), BF16), pltpu.VMEM((f, d), BF16),
                        pltpu.SemaphoreType.DMA((2, 3))],
    )
    return pl.pallas_call(
        _expert_kernel,
        grid_spec=grid_spec,
        out_shape=jax.ShapeDtypeStruct((p, d), F32),
        compiler_params=_params("arbitrary"),
        name="expert_ffn",
    )(tiles["expert"], tiles["n_used"], tiles["first"], tiles["next"], tiles["parity"], xs, w_gate, w_up, w_down)


def _combine_kernel(slot_ref, hu_ref, x1_ref, g2_ref, gk_ref, wg_ref, wu_ref, wd_ref, ys_ref,
                    o_ref, rows_ref, sem, *, tmc):
    def row_copy(j, k):
        return pltpu.make_async_copy(ys_ref.at[pl.ds(slot_ref[j * N_ACTIVE + k], 1)],
                                     rows_ref.at[k, pl.ds(j, 1)], sem)

    def start(jj, carry):
        base = pl.multiple_of(jj * SUBLANES, SUBLANES)
        for r in range(SUBLANES):
            for k in range(N_ACTIVE):
                row_copy(base + r, k).start(priority=k % 2)
        return carry

    def wait(j, carry):
        for k in range(N_ACTIVE):
            row_copy(j, k).wait()
        return carry

    lax.fori_loop(0, tmc // SUBLANES, start, 0)
    lo, hi = _unpack_bf16_pair(hu_ref[...])
    half = lo.shape[1]
    gate = (jnp.dot(lo, wg_ref[:half], preferred_element_type=F32)
            + jnp.dot(hi, wg_ref[half:], preferred_element_type=F32))
    up = (jnp.dot(lo, wu_ref[:half], preferred_element_type=F32)
          + jnp.dot(hi, wu_ref[half:], preferred_element_type=F32))
    y = jnp.dot((_silu(gate) * up).astype(BF16), wd_ref[...], preferred_element_type=F32)
    lax.fori_loop(0, tmc, wait, 0)
    gk = gk_ref[...]
    for k in range(N_ACTIVE):
        y = y + gk[:, k:k + 1] * rows_ref[k]
    o_ref[...] = x1_ref[...] + _gate_rows(y, g2_ref[...])


def combine(slots_flat, hu, x1, gate2_g, gate_k, w_gs, w_us, w_ds, ys, tmc):
    t, d = x1.shape
    f = w_gs.shape[1]
    gm = tmc // MOD_GROUP
    row = lambda w: pl.BlockSpec((tmc, w), lambda i: (i, 0))
    const = lambda a, b: pl.BlockSpec((a, b), lambda i: (0, 0))
    return pl.pallas_call(
        functools.partial(_combine_kernel, tmc=tmc),
        grid=(t // tmc,),
        in_specs=[pl.BlockSpec((tmc * N_ACTIVE,), lambda i: (i,), memory_space=pltpu.SMEM),
                  pl.BlockSpec((tmc,) + hu.shape[1:], lambda i: (i, 0)),
                  row(d), pl.BlockSpec((gm, d), lambda i: (i, 0)), row(N_ACTIVE),
                  const(d, f), const(d, f), const(f, d),
                  pl.BlockSpec(memory_space=pl.ANY)],
        out_specs=row(d),
        out_shape=jax.ShapeDtypeStruct((t, d), F32),
        scratch_shapes=[pltpu.VMEM((N_ACTIVE, tmc, d), F32), pltpu.SemaphoreType.DMA(())],
        compiler_params=_params("arbitrary"),
        name="combine",
    )(slots_flat, hu, x1, gate2_g, gate_k, w_gs, w_us, w_ds, ys)


def _group_rows(v, n):
    b, d = v.shape
    return jnp.broadcast_to(v[:, None, :], (b, n // MOD_GROUP, d)).reshape(b * n // MOD_GROUP, d)


def _pad_keys(a, l_pad):
    return jnp.pad(a, ((0, 0), (0, l_pad - a.shape[1]), (0, 0)))


def _token_stage(x, mod, w, caches, count0):
    b, n, d = x.shape
    t = b * n
    shift1, scale1, gate1, shift2, scale2, gate2 = [_group_rows(m, n) for m in jnp.split(mod, 6, axis=-1)]
    xf = x.reshape(t, d)
    tm = _row_tile(t, TM_DENSE)
    (h, qa, ka, va, kab, vab, qi, ki, kib, wi) = in_proj_a(
        xf, w["norm_mix"], scale1, shift1, w["w_in_a"], w["q_norm_a"], w["k_norm_a"], tm)
    qb, kb, vb, kbb, vbb = in_proj_b(h, w["w_in_b"], tm)

    r3 = lambda a: a.reshape(b, n, a.shape[-1])
    if caches is None:
        ya = dsa_attention(r3(qa), r3(qi), r3(wi), r3(kab), r3(vab), r3(kib),
                           q_pos0=0, n_keys=n, tq=min(TQ_DSA, n), tc=min(TC_DSA, n))
        yb = stick_attention(r3(qb), r3(kbb), r3(vbb), q_pos0=0, tq=min(TQ_STICK, n), tc=min(TC_STICK, n))
    else:
        cka, cva, cki, ckb, cvb = caches
        past = cka.shape[1]
        tc_a, tc_b = min(TC_DSA, past), min(TC_STICK, past)
        new_chunk = lambda a: _pad_keys(r3(a), tc_a)
        ya = dsa_attention(r3(qa), r3(qi), r3(wi), new_chunk(kab), new_chunk(vab), new_chunk(kib),
                           q_pos0=past, n_keys=past + n, tq=n, tc=tc_a, caches=(cka, cva, cki))
        yb = stick_attention_cached(r3(qb), r3(kbb), r3(vbb), ckb, cvb, tc=tc_b)

    merged = merge_branches(h, ya.reshape(t, A_WIDTH), yb.reshape(t, B_WIDTH),
                            w["w_ga"], w["w_gb"], w["w_branch_a"], w["w_branch_b"], tm, _row_tile(d, TN_MERGE))
    x1, hu, top_e, rank_k, gate_k, counts = out_proj(
        merged, xf, gate1, scale2, shift2, w["norm_ffn"], w["w_out"], w["w_router_t"], w["b_router"], count0, tm)
    rows = (ka.reshape(b, n, A_KV_HEADS, HEAD_DIM), va.reshape(b, n, A_KV_HEADS, HEAD_DIM),
            ki.reshape(b, n, IDX_DIM), kb.reshape(b, n, B_HEADS, HEAD_DIM), vb.reshape(b, n, B_HEADS, HEAD_DIM))
    return dict(x1=x1, hu=hu, top_e=top_e, rank_k=rank_k, gate_k=gate_k, counts=counts, gate2=gate2, rows=rows)


def _routing(counts, top_e, rank_k, tme):
    e = counts.shape[0]
    t = top_e.shape[1]
    counts = counts.reshape(e).astype(jnp.int32)
    padded = (counts + tme - 1) // tme * tme
    ends = jnp.cumsum(padded)
    offsets = ends - padded
    eids = jnp.arange(e, dtype=jnp.int32)[:, None, None]
    offset_k = jnp.sum(jnp.where(top_e[None] == eids, offsets[:, None, None], 0), axis=0)
    slots = offset_k + rank_k - 1
    n_tiles = (t * N_ACTIVE + e * tme) // tme
    tile_start = jnp.arange(n_tiles, dtype=jnp.int32) * tme
    tile_expert = jnp.minimum(jnp.sum(ends[None, :] <= tile_start[:, None], axis=1), e - 1).astype(jnp.int32)
    n_used = (ends[-1] // tme).astype(jnp.int32).reshape(1)
    ids = jnp.arange(e, dtype=jnp.int32)
    nonempty = padded > 0
    later = jnp.where((ids[None, :] > ids[:, None]) & nonempty[None, :], ids[None, :], e)
    next_e = jnp.min(later, axis=1)
    next_e = jnp.where(next_e == e, -1, next_e).astype(jnp.int32)
    rank_e = jnp.cumsum(nonempty.astype(jnp.int32)) - 1
    tiles = dict(expert=tile_expert, n_used=n_used,
                 first=(tile_start == offsets[tile_expert]).astype(jnp.int32),
                 next=next_e[tile_expert], parity=(rank_e[tile_expert] % 2).astype(jnp.int32))
    return dict(slots=slots.T.reshape(-1), tiles=tiles, n_used=n_used, p_rows=n_tiles * tme,
                pad_start=offsets + counts, pad_count=padded - counts)


def kernel(x_prompt, x_sample, cache_a_k, cache_a_v, cache_a_idx_k, cache_b_k, cache_b_v, c_prompt, c_sample,
           norm_mix, norm_ffn, w_ada, b_ada, w_in, q_norm_a, k_norm_a, w_branch_a, w_branch_b, w_out,
           w_router, b_router, w_gate_e, w_up_e, w_down_e, w_gate_s, w_up_s, w_down_s):
    depth = norm_mix.shape[0]
    d = x_prompt.shape[-1]
    bp, bs = c_prompt.shape[0], c_sample.shape[0]
    c_all = jnp.concatenate([c_prompt, c_sample], axis=0)
    c_all = jnp.pad(c_all, ((0, -(bp + bs) % (2 * SUBLANES)), (0, 0)))
    tme = TM_EXPERT

    xp, xs = x_prompt, x_sample
    prompt_rows, sample_rows = [], []
    for layer in range(depth):
        wl = w_in[layer]
        o_idx = A_WIDTH + 2 * A_KV_WIDTH + IDX_Q_WIDTH
        o_b = o_idx + IDX_DIM + IDX_HEADS
        o_g = o_b + 3 * B_WIDTH
        w_in_a = jnp.concatenate(
            [wl[:, :o_b], jnp.zeros((d, LANES - IDX_DIM - IDX_HEADS), wl.dtype)], axis=1).astype(BF16)
        w = dict(
            norm_mix=norm_mix[layer].reshape(1, d), norm_ffn=norm_ffn[layer].reshape(1, d),
            q_norm_a=q_norm_a[layer].reshape(1, HEAD_DIM), k_norm_a=k_norm_a[layer].reshape(1, HEAD_DIM),
            w_in_a=w_in_a, w_in_b=wl[:, o_b:o_g].astype(BF16),
            w_ga=wl[:, o_g:o_g + d].astype(BF16), w_gb=wl[:, o_g + d:].astype(BF16),
            w_branch_a=w_branch_a[layer].astype(BF16), w_branch_b=w_branch_b[layer].astype(BF16),
            w_out=w_out[layer].astype(BF16), w_router_t=w_router[layer].T.astype(BF16),
            b_router=b_router[layer].reshape(N_EXPERTS, 1))
        mod = ada_mod(c_all, w_ada[layer], b_ada[layer])
        caches = (cache_a_k[layer], cache_a_v[layer], cache_a_idx_k[layer], cache_b_k[layer], cache_b_v[layer])
        sp = _token_stage(xp, mod[:bp], w, None, jnp.zeros((N_EXPERTS, 1), F32))
        ss = _token_stage(xs, mod[bp:bp + bs], w, caches, sp["counts"])
        prompt_rows.append(sp["rows"])
        sample_rows.append(ss["rows"])

        tp, ts = sp["x1"].shape[0], ss["x1"].shape[0]
        both = lambda name: jnp.concatenate([sp[name], ss[name]], axis=1)
        rt = _routing(ss["counts"], both("top_e"), both("rank_k"), tme)
        slots, gate_k = rt["slots"], both("gate_k").T
        xsorted = dispatch(rt, slots, sp["hu"], ss["hu"], tme)
        ysorted = expert_ffn(rt["tiles"], xsorted, w_gate_e[layer], w_up_e[layer], w_down_e[layer], tme)
        shared = (w_gate_s[layer].astype(BF16), w_up_s[layer].astype(BF16), w_down_s[layer].astype(BF16))
        yp = combine(slots[:tp * N_ACTIVE], sp["hu"], sp["x1"], sp["gate2"], gate_k[:tp], *shared, ysorted,
                     _row_tile(tp, TM_ROWS))
        ys = combine(slots[tp * N_ACTIVE:], ss["hu"], ss["x1"], ss["gate2"], gate_k[tp:], *shared, ysorted,
                     _row_tile(ts, TM_ROWS))
        xp = yp.reshape(xp.shape)
        xs = ys.reshape(xs.shape)

    stack = lambda rows: tuple(jnp.stack(r) for r in zip(*rows))
    return (xp, xs) + stack(prompt_rows) + stack(sample_rows)
```

```python
import functools

import jax
import jax.numpy as jnp
from jax import lax
from jax.experimental import pallas as pl
from jax.experimental.pallas import tpu as pltpu

CHUNK = 64
HEAD_DIM = 128
A_HEADS = 8
A_KV_HEADS = 2
A_GROUP = A_HEADS // A_KV_HEADS
IDX_HEADS = 8
IDX_DIM = 64
TOPK_KEYS = 256
B_HEADS = 8
N_EXPERTS = 64
N_ACTIVE = 8
N_GROUPS = 8
N_ACTIVE_GROUPS = 4
GROUP_SIZE = N_EXPERTS // N_GROUPS
ROUTED_SCALE = 2.5
EPS = 1e-6

A_WIDTH = A_HEADS * HEAD_DIM
A_KV_WIDTH = A_KV_HEADS * HEAD_DIM
IDX_Q_WIDTH = IDX_HEADS * IDX_DIM
B_WIDTH = B_HEADS * HEAD_DIM

LANES = 128
SUBLANES = 8
MOD_GROUP = 32
VMEM_LIMIT = 56 * 1024 * 1024
NEG_BIG = -1e30
STICK_UNDERFLOW = -110.0
INT_MIN = -(2 ** 31)

TM_DENSE = 512
TN_MERGE = 512
TN_ADA = 1024
TQ_DSA, TC_DSA = 256, 512
TQ_STICK, TC_STICK = 256, 256
TM_EXPERT = 256
TM_ROWS = 256

F32 = jnp.float32
BF16 = jnp.bfloat16
NT_DIMS = (((1,), (1,)), ((), ()))


def _params(*sem, flags=None):
    return pltpu.CompilerParams(dimension_semantics=sem, vmem_limit_bytes=VMEM_LIMIT, flags=flags)


def _silu(x):
    return x * jax.nn.sigmoid(x)


def _rms(x, gain):
    return x * lax.rsqrt(jnp.mean(x * x, axis=-1, keepdims=True) + EPS) * gain


def _modulate(y, scale_g, shift_g):
    tm, d = y.shape
    y3 = y.reshape(tm // MOD_GROUP, MOD_GROUP, d)
    out = y3 * (1.0 + scale_g[:, None, :]) + shift_g[:, None, :]
    return out.reshape(tm, d)


def _gate_rows(y, gate_g):
    tm, d = y.shape
    return (y.reshape(tm // MOD_GROUP, MOD_GROUP, d) * gate_g[:, None, :]).reshape(tm, d)


def _row_tile(t, pref):
    tm = min(pref, t)
    while t % tm:
        tm //= 2
    return tm


def _ada_kernel(c_ref, w_ref, b_ref, o_ref):
    s = _silu(c_ref[...]).astype(BF16)
    o_ref[...] = jnp.dot(s, w_ref[...].astype(BF16), preferred_element_type=F32) + b_ref[...]


def ada_mod(c, w_ada, b_ada):
    r, d = c.shape
    n = w_ada.shape[1]
    tn = _row_tile(n, TN_ADA)
    return pl.pallas_call(
        _ada_kernel,
        grid=(n // tn,),
        in_specs=[pl.BlockSpec((r, d), lambda j: (0, 0)),
                  pl.BlockSpec((d, tn), lambda j: (0, j)),
                  pl.BlockSpec((1, tn), lambda j: (0, j))],
        out_specs=pl.BlockSpec((r, tn), lambda j: (0, j)),
        out_shape=jax.ShapeDtypeStruct((r, n), F32),
        compiler_params=_params("arbitrary"),
        name="ada_mod",
    )(c, w_ada, b_ada.reshape(1, n))


def _in_a_kernel(x_ref, nw_ref, sc_ref, sh_ref, w_ref, qn_ref, kn_ref,
                 h_ref, qa_ref, ka_ref, va_ref, kab_ref, vab_ref, qi_ref, ki_ref, kib_ref, wi_ref):
    h = _modulate(_rms(x_ref[...], nw_ref[...]), sc_ref[...], sh_ref[...])
    hb = h.astype(BF16)
    h_ref[...] = hb
    off = 0
    for hd in range(A_HEADS):
        q = jnp.dot(hb, w_ref[:, off:off + HEAD_DIM], preferred_element_type=F32)
        qa_ref[:, hd * HEAD_DIM:(hd + 1) * HEAD_DIM] = (_rms(q, qn_ref[...]) * (HEAD_DIM ** -0.5)).astype(BF16)
        off += HEAD_DIM
    for hd in range(A_KV_HEADS):
        k = jnp.dot(hb, w_ref[:, off:off + HEAD_DIM], preferred_element_type=F32)
        k = _rms(k, kn_ref[...])
        ka_ref[:, hd * HEAD_DIM:(hd + 1) * HEAD_DIM] = k
        kab_ref[:, hd * HEAD_DIM:(hd + 1) * HEAD_DIM] = k.astype(BF16)
        off += HEAD_DIM
    v = jnp.dot(hb, w_ref[:, off:off + A_KV_WIDTH], preferred_element_type=F32)
    va_ref[...] = v
    vab_ref[...] = v.astype(BF16)
    off += A_KV_WIDTH
    qi = jnp.dot(hb, w_ref[:, off:off + IDX_Q_WIDTH], preferred_element_type=F32)
    qi_ref[...] = (qi * (IDX_DIM ** -0.5)).astype(BF16)
    off += IDX_Q_WIDTH
    kw = jnp.dot(hb, w_ref[:, off:off + LANES], preferred_element_type=F32)
    ki = kw[:, :IDX_DIM]
    ki_ref[...] = ki
    kib_ref[...] = ki.astype(BF16)
    wi_ref[...] = kw


def in_proj_a(x, norm_w, scale_g, shift_g, w_a, q_norm, k_norm, tm):
    t, d = x.shape
    wa = w_a.shape[1]
    gm = tm // MOD_GROUP
    row = lambda w: pl.BlockSpec((tm, w), lambda i: (i, 0))
    const = lambda a, b: pl.BlockSpec((a, b), lambda i: (0, 0))
    outs = [(d, BF16), (A_WIDTH, BF16), (A_KV_WIDTH, F32), (A_KV_WIDTH, F32), (A_KV_WIDTH, BF16),
            (A_KV_WIDTH, BF16), (IDX_Q_WIDTH, BF16), (IDX_DIM, F32), (IDX_DIM, BF16), (LANES, F32)]
    return pl.pallas_call(
        _in_a_kernel,
        grid=(t // tm,),
        in_specs=[row(d), const(1, d), pl.BlockSpec((gm, d), lambda i: (i, 0)),
                  pl.BlockSpec((gm, d), lambda i: (i, 0)), const(d, wa),
                  const(1, HEAD_DIM), const(1, HEAD_DIM)],
        out_specs=[row(w) for w, _ in outs],
        out_shape=[jax.ShapeDtypeStruct((t, w), dt) for w, dt in outs],
        compiler_params=_params("arbitrary"),
        name="in_proj_a",
    )(x, norm_w, scale_g, shift_g, w_a, q_norm, k_norm)


def _in_b_kernel(h_ref, w_ref, qb_ref, kb_ref, vb_ref, kbb_ref, vbb_ref):
    hb = h_ref[...]
    qb_ref[...] = jnp.dot(hb, w_ref[:, :B_WIDTH], preferred_element_type=F32).astype(BF16)
    k = jnp.dot(hb, w_ref[:, B_WIDTH:2 * B_WIDTH], preferred_element_type=F32)
    kbb_ref[...] = k.astype(BF16)
    v = jnp.dot(hb, w_ref[:, 2 * B_WIDTH:], preferred_element_type=F32)
    vbb_ref[...] = v.astype(BF16)
    for hd in range(B_HEADS):
        kb_ref[:, hd, :] = k[:, hd * HEAD_DIM:(hd + 1) * HEAD_DIM]
        vb_ref[:, hd, :] = v[:, hd * HEAD_DIM:(hd + 1) * HEAD_DIM]


def in_proj_b(h, w_b, tm):
    t, d = h.shape
    row = lambda w: pl.BlockSpec((tm, w), lambda i: (i, 0))
    heads = pl.BlockSpec((tm, B_HEADS, HEAD_DIM), lambda i: (i, 0, 0))
    flat = lambda dt: jax.ShapeDtypeStruct((t, B_WIDTH), dt)
    per_head = jax.ShapeDtypeStruct((t, B_HEADS, HEAD_DIM), F32)
    return pl.pallas_call(
        _in_b_kernel,
        grid=(t // tm,),
        in_specs=[row(d), pl.BlockSpec((d, 3 * B_WIDTH), lambda i: (0, 0))],
        out_specs=[row(B_WIDTH), heads, heads, row(B_WIDTH), row(B_WIDTH)],
        out_shape=[flat(BF16), per_head, per_head, flat(BF16), flat(BF16)],
        compiler_params=_params("arbitrary"),
        name="in_proj_b",
    )(h, w_b)


def _score_key(x):
    bits = lax.bitcast_convert_type(x, jnp.int32)
    return bits ^ ((bits >> 31) & 0x7FFFFFFF)


def _lane_tiles(a):
    return [a[:, j * LANES:(j + 1) * LANES] for j in range(a.shape[1] // LANES)]


def _dsa_kernel(qa_ref, qi_ref, kw_ref, k_ref, v_ref, ki_ref, *rest, q_pos0, n_keys, n_sel, tq, tc, pos_bits,
                cached):
    if cached:
        ck_ref, cv_ref, cki_ref, o_ref, key_ref, m_ref, l_ref, acc_ref = rest
    else:
        o_ref, key_ref, m_ref, l_ref, acc_ref = rest

    def read_k(start, g, last):
        if cached and not last:
            return ck_ref[pl.ds(start, tc), g, :].astype(BF16)
        rows = slice(None) if cached else pl.ds(start, tc)
        return k_ref[rows, g * HEAD_DIM:(g + 1) * HEAD_DIM]

    def read_v(start, g, last):
        if cached and not last:
            return cv_ref[pl.ds(start, tc), g, :].astype(BF16)
        rows = slice(None) if cached else pl.ds(start, tc)
        return v_ref[rows, g * HEAD_DIM:(g + 1) * HEAD_DIM]

    def read_ki(start, last):
        if cached and not last:
            return cki_ref[pl.ds(start, tc), :].astype(BF16)
        return ki_ref[...] if cached else ki_ref[pl.ds(start, tc), :]

    i = pl.program_id(1)
    q0 = q_pos0 + i * tq
    kmax = jnp.minimum(((q0 + tq - 1) // CHUNK + 1) * CHUNK, n_keys)
    nch = (kmax + tc - 1) // tc
    qpos = q0 + lax.broadcasted_iota(jnp.int32, (tq, 1), 0)

    tqs = key_ref.shape[2]
    lane_q = lax.broadcasted_iota(jnp.int32, (1, tqs), 1)
    n_adm = jnp.minimum(((q0 + lane_q) // CHUNK + 1) * CHUNK, n_keys)
    pad_q = lambda a: a if tqs == tq else jnp.concatenate(
        [a, jnp.zeros((tqs - tq, a.shape[1]), a.dtype)], axis=0)
    head_w = pad_q(kw_ref[...]).T[IDX_DIM:IDX_DIM + IDX_HEADS] * (IDX_HEADS ** -0.5)
    qi_all = jnp.concatenate([pad_q(qi_ref[:, h * IDX_DIM:(h + 1) * IDX_DIM]) for h in range(IDX_HEADS)], axis=0)

    def all_chunks(chunk_pass):
        def body(c, carry):
            chunk_pass(c, False)
            return carry
        lax.fori_loop(0, nch - 1, body, 0)
        chunk_pass(nch - 1, True)

    def score_pass(c, last):
        start = pl.multiple_of(c * tc, tc)
        lg = lax.dot_general(read_ki(start, last), qi_all, NT_DIMS, preferred_element_type=F32)
        s = jnp.zeros((tc, tqs), F32)
        for h in range(IDX_HEADS):
            s = s + jnp.maximum(lg[:, h * tqs:(h + 1) * tqs], 0.0) * head_w[h:h + 1, :]
        adm = start + lax.broadcasted_iota(jnp.int32, (tc, 1), 0) < n_adm
        key_ref[c] = jnp.where(adm, _score_key(s), INT_MIN)

    all_chunks(score_pass)

    def count_rows(pred):
        def body(c, acc):
            start = pl.multiple_of(c * tc, tc)
            hit = jnp.where(pred(key_ref[c], start), 1.0, 0.0)
            return acc + jnp.sum(hit.reshape(tc // acc_rows, acc_rows, tqs), axis=0)
        acc_rows = 4 * SUBLANES
        acc = lax.fori_loop(0, nch, body, jnp.zeros((acc_rows, tqs), F32))
        return jnp.sum(acc, axis=0, keepdims=True)

    def count_ge(cand):
        return count_rows(lambda key, start: key >= cand)

    def pending(done):
        return jnp.min(done) < 0.5

    cnt = count_ge(jnp.zeros((1, tqs), jnp.int32))
    thr = jnp.where(cnt >= n_sel, 0, INT_MIN).astype(jnp.int32)
    done =jnp.where(jnp.logical_or(jnp.logical_or(n_adm <= n_sel, cnt == n_sel), lane_q >= tq), 1.0, 0.0)

    def bit_cond(state):
        b, _, done = state
        return jnp.logical_and(b < 31, pending(done))

    def bit_body(state):
        b, thr, done = state
        cand = thr | jnp.left_shift(jnp.int32(1), 30 - b)
        cnt = count_ge(cand)
        return (b + 1, jnp.where(cnt >= n_sel, cand, thr), jnp.where(cnt == n_sel, 1.0, done))

    _, thr, done = lax.while_loop(bit_cond, bit_body, (jnp.int32(0), thr, done))

    def tie_limit():
        need = n_sel - count_rows(lambda key, start: key > thr)

        def tied_below(limit):
            return count_rows(lambda key, start: jnp.logical_and(
                key == thr, start + lax.broadcasted_iota(jnp.int32, (tc, 1), 0) < limit))

        def body(b, lim):
            cand = lim | jnp.left_shift(jnp.int32(1), pos_bits - 1 - b)
            return jnp.where(tied_below(cand) < need, cand, lim)

        lim = lax.fori_loop(0, pos_bits, body, jnp.zeros((1, tqs), jnp.int32))
        return jnp.where(done > 0.5, jnp.int32(2 ** pos_bits), lim + 1)

    plim = lax.cond(pending(done), tie_limit, lambda: jnp.full((1, tqs), 2 ** pos_bits, jnp.int32))

    slopes = [2.0 ** (-8.0 * (h + 1) / A_HEADS) for h in range(A_HEADS)]
    lane = lax.broadcasted_iota(jnp.int32, (1, LANES), 1)
    rows4 = A_GROUP * tq
    q_aug = []
    for g in range(A_KV_HEADS):
        heads = range(g * A_GROUP, (g + 1) * A_GROUP)
        q4 = jnp.concatenate([qa_ref[:, h * HEAD_DIM:(h + 1) * HEAD_DIM] for h in heads], axis=0)
        slope_col = jnp.concatenate([jnp.full((tq, 1), slopes[h], F32) for h in heads], axis=0)
        q_aug.append(jnp.concatenate([q4, jnp.where(lane < 3, slope_col, 0.0).astype(BF16)], axis=1))

    def chunk_logits(c, last):
        start = pl.multiple_of(c * tc, tc)
        kp = start + lax.broadcasted_iota(jnp.int32, (tc, 1), 0)
        pos_cols = jnp.where(lane == 0, kp & 63,
                             jnp.where(lane == 1, kp & (127 << 6),
                                       jnp.where(lane == 2, kp & ~8191, 0))).astype(F32).astype(BF16)
        key = key_ref[c]
        sel = jnp.logical_or(key > thr, jnp.logical_and(key == thr, kp < plim))
        mb = jnp.where(jnp.logical_and(sel, key != INT_MIN), 0.0, NEG_BIG).T[:tq]
        kpos = start + lax.broadcasted_iota(jnp.int32, (1, tc), 1)
        out = []
        for g in range(A_KV_HEADS):
            kc = jnp.concatenate([read_k(start, g, last), pos_cols], axis=1)
            lg = lax.dot_general(q_aug[g], kc, NT_DIMS, preferred_element_type=F32).reshape(A_GROUP, tq, tc)
            if last:
                ahead = jnp.maximum(kpos - qpos, 0).astype(F32)
                bias = jnp.stack([mb - (2.0 * slopes[g * A_GROUP + r]) * ahead for r in range(A_GROUP)])
            else:
                bias = mb[None]
            out.append((lg + bias).reshape(rows4, tc))
        return start, out

    m_ref[...] = jnp.full(m_ref.shape, NEG_BIG, F32)
    l_ref[...] = jnp.zeros(l_ref.shape, F32)
    acc_ref[...] = jnp.zeros(acc_ref.shape, F32)

    def attend(c, last):
        start, logits = chunk_logits(c, last)
        for g in range(A_KV_HEADS):
            rows = slice(g * rows4, (g + 1) * rows4)
            tiles = _lane_tiles(logits[g])
            m_old = m_ref[rows]
            m_new = jnp.maximum(m_old, jnp.max(functools.reduce(jnp.maximum, tiles), axis=1, keepdims=True))
            alpha = jnp.exp(m_old - m_new)
            p = [jnp.exp(t - m_new) for t in tiles]
            m_ref[rows] = m_new
            l_ref[rows] = alpha * l_ref[rows] + functools.reduce(jnp.add, p)
            acc_ref[rows] = alpha * acc_ref[rows] + jnp.dot(jnp.concatenate(p, axis=1).astype(BF16),
                                                            read_v(start, g, last), preferred_element_type=F32)

    all_chunks(attend)
    for h in range(A_HEADS):
        rows = slice(h * tq, (h + 1) * tq)
        o_ref[:, h * HEAD_DIM:(h + 1) * HEAD_DIM] = (
            acc_ref[rows] / jnp.sum(l_ref[rows], axis=1, keepdims=True)).astype(BF16)


def dsa_attention(qa, qi, wi, k, v, ki, *, q_pos0, n_keys, tq, tc, caches=()):
    b, n, _ = qa.shape
    if caches:
        assert q_pos0 % tc == 0 and n == tq <= tc and k.shape[1] == tc and n_keys == q_pos0 + n
        l_pad = q_pos0 + tc
    else:
        l_pad = k.shape[1]
    n_sel = min(TOPK_KEYS, n_keys // 4)
    qspec = lambda w: pl.BlockSpec((None, tq, w), lambda bi, i: (bi, i, 0))
    kspec = lambda w: pl.BlockSpec((None, k.shape[1], w), lambda bi, i: (bi, 0, 0))
    cache_specs = [pl.BlockSpec((None,) + c.shape[1:], lambda bi, i, nd=c.ndim: (bi,) + (0,) * (nd - 1))
                   for c in caches]
    kern = functools.partial(_dsa_kernel, q_pos0=q_pos0, n_keys=n_keys, n_sel=n_sel, tq=tq, tc=tc,
                             pos_bits=max(1, (l_pad - 1).bit_length()), cached=bool(caches))
    return pl.pallas_call(
        kern,
        grid=(b, n // tq),
        in_specs=[qspec(A_WIDTH), qspec(IDX_Q_WIDTH), qspec(LANES),
                  kspec(A_KV_WIDTH), kspec(A_KV_WIDTH), kspec(IDX_DIM)] + cache_specs,
        out_specs=qspec(A_WIDTH),
        out_shape=jax.ShapeDtypeStruct((b, n, A_WIDTH), BF16),
        scratch_shapes=[pltpu.VMEM((l_pad // tc, tc, max(tq, LANES)), jnp.int32),
                        pltpu.VMEM((A_HEADS * tq, LANES), F32),
                        pltpu.VMEM((A_HEADS * tq, LANES), F32),
                        pltpu.VMEM((A_HEADS * tq, HEAD_DIM), F32)],
        compiler_params=_params("arbitrary", "arbitrary"),
        name="dsa",
    )(qa, qi, wi, k, v, ki, *caches)


STICK_HEADS_PER_STEP = 4


def _strict_upper(n):
    return (lax.broadcasted_iota(jnp.int32, (n, n), 0) > lax.broadcasted_iota(jnp.int32, (n, n), 1)).astype(BF16)


def _stick_step(q, kc, vc, strict, upper, tail, acc):
    z = lax.dot_general(q, kc, NT_DIMS, preferred_element_type=F32) * (HEAD_DIM ** -0.5)
    softplus = jnp.log(1.0 + jnp.exp(-jnp.abs(z)))
    log_go = jnp.minimum(z, 0.0) - softplus
    log_stay = jnp.where(strict, jnp.minimum(-z, 0.0) - softplus, 0.0)
    hi = log_stay.astype(BF16)
    lo = (log_stay - hi.astype(F32)).astype(BF16)
    after = jnp.dot(hi, upper, preferred_element_type=F32) + jnp.dot(lo, upper, preferred_element_type=F32) + tail
    w = jnp.where(strict, jnp.exp(log_go + after), 0.0)
    return (tail + jnp.sum(log_stay, axis=1, keepdims=True),
            acc + jnp.dot(w.astype(BF16), vc, preferred_element_type=F32))


def _stick_walk(qs, tpos, c_first, tc, read_kv, tails, accs):
    upper = _strict_upper(tc)

    def cond(carry):
        n, tails, _ = carry
        return jnp.logical_and(n <= c_first, jnp.max(functools.reduce(jnp.maximum, tails)) > STICK_UNDERFLOW)

    def body(carry):
        n, tails, accs = carry
        start = pl.multiple_of((c_first - n) * tc, tc)
        strict = start + lax.broadcasted_iota(jnp.int32, (1, tc), 1) < tpos
        new = [_stick_step(q, *read_kv(start, h), strict, upper, tail, acc)
               for h, (q, tail, acc) in enumerate(zip(qs, tails, accs))]
        return n + 1, tuple(t for t, _ in new), tuple(a for _, a in new)

    _, _, accs = lax.while_loop(cond, body, (jnp.int32(0), tuple(tails), tuple(accs)))
    return accs


def _stick_kernel(q_ref, k_ref, v_ref, o_ref, *, q_pos0, tq, tc):
    t0 = q_pos0 + pl.program_id(2) * tq
    tpos = t0 + lax.broadcasted_iota(jnp.int32, (tq, 1), 0)
    heads = [slice(h * HEAD_DIM, (h + 1) * HEAD_DIM) for h in range(STICK_HEADS_PER_STEP)]
    read_kv = lambda start, h: (k_ref[pl.ds(start, tc), heads[h]], v_ref[pl.ds(start, tc), heads[h]])
    zeros = lambda w: [jnp.zeros((tq, w), F32) for _ in heads]
    accs = _stick_walk([q_ref[:, hs] for hs in heads], tpos, (t0 + tq - 2) // tc, tc, read_kv,
                       zeros(1), zeros(HEAD_DIM))
    for hs, acc in zip(heads, accs):
        o_ref[:, hs] = acc.astype(BF16)


def _stick_cached_kernel(q_ref, k_ref, v_ref, ck_ref, cv_ref, o_ref, *, past, tc):
    n = q_ref.shape[0]
    tpos = past + lax.broadcasted_iota(jnp.int32, (n, 1), 0)
    strict = past + lax.broadcasted_iota(jnp.int32, (1, n), 1) < tpos
    upper = _strict_upper(n)
    for h0 in range(0, B_HEADS, STICK_HEADS_PER_STEP):
        group = range(h0, h0 + STICK_HEADS_PER_STEP)
        cols = [slice(h * HEAD_DIM, (h + 1) * HEAD_DIM) for h in group]
        qs = [q_ref[:, hs] for hs in cols]
        first = [_stick_step(q, k_ref[:, hs], v_ref[:, hs], strict, upper,
                             jnp.zeros((n, 1), F32), jnp.zeros((n, HEAD_DIM), F32)) for q, hs in zip(qs, cols)]
        read_kv = lambda start, h, h0=h0: (ck_ref[pl.ds(start, tc), h0 + h, :].astype(BF16),
                                           cv_ref[pl.ds(start, tc), h0 + h, :].astype(BF16))
        accs = _stick_walk(qs, tpos, past // tc - 1, tc, read_kv, [t for t, _ in first], [a for _, a in first])
        for hs, acc in zip(cols, accs):
            o_ref[:, hs] = acc.astype(BF16)


def stick_attention(qb, k, v, *, q_pos0, tq, tc):
    b, n, _ = qb.shape
    l_pad = k.shape[1]
    width = STICK_HEADS_PER_STEP * HEAD_DIM
    qspec = pl.BlockSpec((None, tq, width), lambda bi, h, i: (bi, i, h))
    kspec = pl.BlockSpec((None, l_pad, width), lambda bi, h, i: (bi, 0, h))
    kern = functools.partial(_stick_kernel, q_pos0=q_pos0, tq=tq, tc=tc)
    return pl.pallas_call(
        kern,
        grid=(b, B_HEADS // STICK_HEADS_PER_STEP, n // tq),
        in_specs=[qspec, kspec, kspec],
        out_specs=qspec,
        out_shape=jax.ShapeDtypeStruct((b, n, B_WIDTH), BF16),
        compiler_params=_params("arbitrary", "arbitrary", "arbitrary"),
        name="stick",
    )(qb, k, v)


def stick_attention_cached(qb, k_new, v_new, cache_k, cache_v, *, tc):
    b, n, _ = qb.shape
    past = cache_k.shape[1]
    assert past % tc == 0
    new = pl.BlockSpec((None, n, B_WIDTH), lambda bi: (bi, 0, 0))
    cache = pl.BlockSpec((None, past, B_HEADS, HEAD_DIM), lambda bi: (bi, 0, 0, 0))
    return pl.pallas_call(
        functools.partial(_stick_cached_kernel, past=past, tc=tc),
        grid=(b,),
        in_specs=[new, new, new, cache, cache],
        out_specs=new,
        out_shape=jax.ShapeDtypeStruct((b, n, B_WIDTH), BF16),
        compiler_params=_params("arbitrary"),
        name="stick_cached",
    )(qb, k_new, v_new, cache_k, cache_v)


def _merge_kernel(h_ref, ya_ref, yb_ref, wga_ref, wgb_ref, wa_ref, wb_ref, o_ref):
    hb = h_ref[...]
    ga = jax.nn.sigmoid(jnp.dot(hb, wga_ref[...], preferred_element_type=F32))
    gb = jax.nn.sigmoid(jnp.dot(hb, wgb_ref[...], preferred_element_type=F32))
    pa = jnp.dot(ya_ref[...], wa_ref[...], preferred_element_type=F32)
    pb = jnp.dot(yb_ref[...], wb_ref[...], preferred_element_type=F32)
    o_ref[...] = (ga * pa + gb * pb).astype(BF16)


def merge_branches(h, ya, yb, w_ga, w_gb, w_a, w_b, tm, tn):
    t, d = h.shape
    row = lambda w: pl.BlockSpec((tm, w), lambda i, j: (i, 0))
    col = lambda k: pl.BlockSpec((k, tn), lambda i, j: (0, j))
    return pl.pallas_call(
        _merge_kernel,
        grid=(t // tm, d // tn),
        in_specs=[row(d), row(A_WIDTH), row(B_WIDTH), col(d), col(d), col(A_WIDTH), col(B_WIDTH)],
        out_specs=pl.BlockSpec((tm, tn), lambda i, j: (i, j)),
        out_shape=jax.ShapeDtypeStruct((t, d), BF16),
        compiler_params=_params("arbitrary", "arbitrary"),
        name="merge",
    )(h, ya, yb, w_ga, w_gb, w_a, w_b)


def _pack_bf16_pair(lo, hi):
    lo_bits = lax.bitcast_convert_type(lo.astype(BF16).astype(F32), jnp.uint32)
    hi_bits = lax.bitcast_convert_type(hi.astype(BF16).astype(F32), jnp.uint32)
    return (lo_bits >> 16) | (hi_bits & jnp.uint32(0xFFFF0000))


def _unpack_f32_pair(u):
    return (lax.bitcast_convert_type(u << 16, F32),
            lax.bitcast_convert_type(u & jnp.uint32(0xFFFF0000), F32))


def _unpack_bf16_pair(u):
    lo, hi = _unpack_f32_pair(u)
    return lo.astype(BF16), hi.astype(BF16)


def _first_max(vals, idx, axes, n):
    m = vals
    for ax in axes:
        m = jnp.max(m, axis=ax, keepdims=True)
    first = jnp.where(vals == m, idx, n)
    for ax in axes:
        first = jnp.min(first, axis=ax, keepdims=True)
    return m, first


def _sum01(a):
    return jnp.sum(jnp.sum(a, axis=0, keepdims=True), axis=1, keepdims=True)


def _out_kernel(mg_ref, x_ref, g1_ref, sc_ref, sh_ref, nw_ref, wo_ref, wr_ref, br_ref, cnt0_ref,
                x1_ref, hu_ref, tope_ref, rank_ref, gatek_ref, cnt_out_ref, cnt_ref):
    tm, d = x_ref.shape
    proj = jnp.dot(mg_ref[...], wo_ref[...], preferred_element_type=F32)
    x1 = x_ref[...] + _gate_rows(proj, g1_ref[...])
    x1_ref[...] = x1
    h2 = _modulate(_rms(x1, nw_ref[...]), sc_ref[...], sh_ref[...])
    hu_ref[...] = _pack_bf16_pair(h2[:, :d // 2], h2[:, d // 2:])

    logits = lax.dot_general(wr_ref[...], h2.astype(BF16), NT_DIMS, preferred_element_type=F32)
    aff = jax.nn.sigmoid(logits)
    shape3 = (N_GROUPS, GROUP_SIZE, tm)
    aff3 = aff.reshape(shape3)
    biased = (aff + br_ref[...]).reshape(shape3)
    io_in = lax.broadcasted_iota(jnp.int32, shape3, 1)
    m1, i1 = _first_max(biased, io_in, (1,), GROUP_SIZE)
    m2 = jnp.max(jnp.where(io_in == i1, -jnp.inf, biased), axis=1, keepdims=True)
    gscore = m1 + m2
    gio = lax.broadcasted_iota(jnp.int32, gscore.shape, 0)
    keep = jnp.zeros(gscore.shape, jnp.bool_)
    for _ in range(N_ACTIVE_GROUPS):
        _, first = _first_max(gscore, gio, (0,), N_GROUPS)
        hit = gio == first
        keep = jnp.logical_or(keep, hit)
        gscore = jnp.where(hit, -jnp.inf, gscore)
    masked = jnp.where(jnp.broadcast_to(keep, shape3), biased, -jnp.inf)
    eio = lax.broadcasted_iota(jnp.int32, shape3, 0) * GROUP_SIZE + io_in
    sel = jnp.zeros(shape3, jnp.bool_)
    hits = []
    for _ in range(N_ACTIVE):
        _, first = _first_max(masked, eio, (0, 1), N_EXPERTS)
        hit = eio == first
        sel = jnp.logical_or(sel, hit)
        masked = jnp.where(hit, -jnp.inf, masked)
        hits.append((hit, first.reshape(1, tm)))
    sel_aff = jnp.where(sel, aff3, 0.0)
    gates = sel_aff / _sum01(sel_aff) * ROUTED_SCALE

    @pl.when(pl.program_id(0) == 0)
    def _():
        cnt_ref[...] = cnt0_ref[...]

    sel2 = jnp.where(sel, 1.0, 0.0).reshape(N_EXPERTS, tm)
    incl = (lax.broadcasted_iota(jnp.int32, (tm, tm), 0)
            <= lax.broadcasted_iota(jnp.int32, (tm, tm), 1)).astype(BF16)
    rank = jnp.dot(sel2.astype(BF16), incl, preferred_element_type=F32) + cnt_ref[...]
    cnt_ref[...] = cnt_ref[...] + jnp.sum(sel2, axis=1, keepdims=True)
    cnt_out_ref[...] = cnt_ref[...]
    rank3 = rank.reshape(shape3)
    tope_ref[...] = jnp.concatenate([first for _, first in hits], axis=0)
    rank_ref[...] = jnp.concatenate(
        [_sum01(jnp.where(hit, rank3, 0.0)).reshape(1, tm) for hit, _ in hits], axis=0).astype(jnp.int32)
    gatek_ref[...] = jnp.concatenate(
        [_sum01(jnp.where(hit, gates, 0.0)).reshape(1, tm) for hit, _ in hits], axis=0)


def out_proj(merged, x, gate1_g, scale_g, shift_g, norm_w, w_out, w_router_t, b_router, count0, tm):
    t, d = x.shape
    gm = tm // MOD_GROUP
    row = lambda w: pl.BlockSpec((tm, w), lambda i: (i, 0))
    grp = pl.BlockSpec((gm, d), lambda i: (i, 0))
    const = lambda a, b: pl.BlockSpec((a, b), lambda i: (0, 0))
    per_k = pl.BlockSpec((N_ACTIVE, tm), lambda i: (0, i))
    return pl.pallas_call(
        _out_kernel,
        grid=(t // tm,),
        in_specs=[row(d), row(d), grp, grp, grp, const(1, d), const(d, d),
                  const(N_EXPERTS, d), const(N_EXPERTS, 1), const(N_EXPERTS, 1)],
        out_specs=[row(d), row(d // 2), per_k, per_k, per_k, const(N_EXPERTS, 1)],
        out_shape=[jax.ShapeDtypeStruct((t, d), F32), jax.ShapeDtypeStruct((t, d // 2), jnp.uint32),
                   jax.ShapeDtypeStruct((N_ACTIVE, t), jnp.int32), jax.ShapeDtypeStruct((N_ACTIVE, t), jnp.int32),
                   jax.ShapeDtypeStruct((N_ACTIVE, t), F32), jax.ShapeDtypeStruct((N_EXPERTS, 1), F32)],
        scratch_shapes=[pltpu.VMEM((N_EXPERTS, 1), F32)],
        compiler_params=_params("arbitrary"),
        name="out_proj",
    )(merged, x, gate1_g, scale_g, shift_g, norm_w, w_out, w_router_t, b_router, count0)


def _pad_fill(start_ref, cnt_ref, nu_ref, xs_ref, zero_ref, sem, tme, n_tiles):
    zero_ref[...] = jnp.zeros(zero_ref.shape, zero_ref.dtype)

    def tile_copy(i):
        return pltpu.make_async_copy(zero_ref, xs_ref.at[pl.ds(pl.multiple_of(i * tme, tme), tme)], sem.at[1])

    def for_each_copy(wait):
        def pad_rows(e, carry):
            pos = start_ref[e]
            for bit in range(tme.bit_length() - 1):
                size = 1 << bit

                @pl.when((cnt_ref[e] & size) != 0)
                def _(pos=pos, size=size, bit=bit):
                    copy = pltpu.make_async_copy(zero_ref.at[pl.ds(0, size)],
                                                 xs_ref.at[pl.ds(pl.multiple_of(pos, size), size)], sem.at[2 + bit])
                    copy.wait() if wait else copy.start()

                pos = pos + (cnt_ref[e] & size)
            return carry

        def tail_tiles(i, carry):
            tile_copy(i).wait() if wait else tile_copy(i).start()
            return carry

        lax.fori_loop(0, N_EXPERTS, pad_rows, 0)
        lax.fori_loop(nu_ref[0], n_tiles, tail_tiles, 0)

    for_each_copy(wait=False)
    for_each_copy(wait=True)


def _dispatch_kernel(start_ref, cnt_ref, nu_ref, slot_ref, hp_ref, hs_ref, xs_ref, zero_ref, sem, pad_sem,
                     *, td, prompt_steps, tme, n_tiles):
    i = pl.program_id(0)

    @pl.when(i == 0)
    def _():
        _pad_fill(start_ref, cnt_ref, nu_ref, xs_ref, zero_ref, pad_sem, tme, n_tiles)

    def scatter(hu_ref):
        def row_copy(j, k):
            return pltpu.make_async_copy(hu_ref.at[pl.ds(j, 1)],
                                         xs_ref.at[pl.ds(slot_ref[j * N_ACTIVE + k], 1)], sem)

        def start(jj, carry):
            base = pl.multiple_of(jj * SUBLANES, SUBLANES)
            for r in range(SUBLANES):
                for k in range(N_ACTIVE):
                    row_copy(base + r, k).start(priority=k % 2)
            return carry

        lax.fori_loop(0, td // SUBLANES, start, 0)
        done_rows = xs_ref.at[pl.ds(0, td * N_ACTIVE)]
        pltpu.make_async_copy(done_rows, done_rows, sem).wait()

    @pl.when(i < prompt_steps)
    def _():
        scatter(hp_ref)

    @pl.when(i >= prompt_steps)
    def _():
        scatter(hs_ref)


def dispatch(route, slots_flat, hu_prompt, hu_sample, tme):
    tp, ts = hu_prompt.shape[0], hu_sample.shape[0]
    width = hu_prompt.shape[1]
    td = _row_tile(ts, _row_tile(tp, TM_ROWS))
    prompt_steps = tp // td
    p_rows = route["p_rows"]
    grid_spec = pltpu.PrefetchScalarGridSpec(
        num_scalar_prefetch=3,
        grid=(prompt_steps + ts // td,),
        in_specs=[pl.BlockSpec((td * N_ACTIVE,), lambda i, *_: (i,), memory_space=pltpu.SMEM),
                  pl.BlockSpec((td, width), lambda i, *_: (jnp.minimum(i, prompt_steps - 1), 0)),
                  pl.BlockSpec((td, width), lambda i, *_: (jnp.maximum(i - prompt_steps, 0), 0))],
        out_specs=pl.BlockSpec(memory_space=pl.ANY),
        scratch_shapes=[pltpu.VMEM((tme, width), jnp.uint32), pltpu.SemaphoreType.DMA(()),
                        pltpu.SemaphoreType.DMA((2 + tme.bit_length(),))])
    return pl.pallas_call(
        functools.partial(_dispatch_kernel, td=td, prompt_steps=prompt_steps, tme=tme, n_tiles=p_rows // tme),
        grid_spec=grid_spec,
        out_shape=jax.ShapeDtypeStruct((p_rows, width), jnp.uint32),
        compiler_params=_params("arbitrary"),
        name="dispatch",
    )(route["pad_start"], route["pad_count"], route["n_used"], slots_flat, hu_prompt, hu_sample)


def _expert_kernel(te_ref, nu_ref, first_ref, next_ref, par_ref, xs_ref, wg_ref, wu_ref, wd_ref, o_ref,
                   wgf_ref, wuf_ref, wdf_ref, wgb_ref, wub_ref, wdb_ref, sem):
    i = pl.program_id(0)
    used = i < nu_ref[0]

    def fetch(e, buf):
        return [pltpu.make_async_copy(w.at[e], f.at[buf], sem.at[buf, j])
                for j, (w, f) in enumerate(((wg_ref, wgf_ref), (wu_ref, wuf_ref), (wd_ref, wdf_ref)))]

    @pl.when(i == 0)
    def _():
        for copy in fetch(te_ref[0], par_ref[0]):
            copy.start()

    @pl.when(jnp.logical_and(used, first_ref[i] == 1))
    def _():
        buf = par_ref[i]
        for copy in fetch(te_ref[i], buf):
            copy.wait()

        @pl.when(next_ref[i] >= 0)
        def _():
            for copy in fetch(next_ref[i], 1 - buf):
                copy.start()

        wgb_ref[...] = wgf_ref[buf].astype(BF16)
        wub_ref[...] = wuf_ref[buf].astype(BF16)
        wdb_ref[...] = wdf_ref[buf].astype(BF16)

    @pl.when(used)
    def _():
        lo, hi = _unpack_bf16_pair(xs_ref[...])
        half = lo.shape[1]
        gate = (jnp.dot(lo, wgb_ref[:half], preferred_element_type=F32)
                + jnp.dot(hi, wgb_ref[half:], preferred_element_type=F32))
        up = (jnp.dot(lo, wub_ref[:half], preferred_element_type=F32)
              + jnp.dot(hi, wub_ref[half:], preferred_element_type=F32))
        hid = (_silu(gate) * up).astype(BF16)
        o_ref[...] = jnp.dot(hid, wdb_ref[...], preferred_element_type=F32)

    @pl.when(jnp.logical_not(used))
    def _():
        o_ref[...] = jnp.zeros(o_ref.shape, o_ref.dtype)


def expert_ffn(tiles, xs, w_gate, w_up, w_down, tme):
    p, half = xs.shape
    e, d, f = w_gate.shape
    hbm = pl.BlockSpec(memory_space=pl.ANY)
    grid_spec = pltpu.PrefetchScalarGridSpec(
        num_scalar_prefetch=5,
        grid=(p // tme,),
        in_specs=[pl.BlockSpec((tme, half), lambda i, te, nu, *_: (jnp.minimum(i, nu[0] - 1), 0)),
                  hbm, hbm, hbm],
        out_specs=pl.BlockSpec((tme, d), lambda i, *_: (i, 0)),
        scratch_shapes=[pltpu.VMEM((2, d, f), F32), pltpu.VMEM((2, d, f), F32), pltpu.VMEM((2, f, d), F32),
                        pltpu.VMEM((d, f), BF16), pltpu.VMEM((d, f), BF16), pltpu.VMEM((f, d), BF16),
                        pltpu.SemaphoreType.DMA((2, 3))],
    )
    return pl.pallas_call(
        _expert_kernel,
        grid_spec=grid_spec,
        out_shape=jax.ShapeDtypeStruct((p, d), F32),
        compiler_params=_params("arbitrary"),
        name="expert_ffn",
    )(tiles["expert"], tiles["n_used"], tiles["first"], tiles["next"], tiles["parity"], xs, w_gate, w_up, w_down)


def _combine_kernel(slot_ref, hu_ref, x1_ref, g2_ref, gk_ref, wg_ref, wu_ref, wd_ref, ys_ref,
                    o_ref, rows_ref, sem, *, tmc):
    def row_copy(j, k):
        return pltpu.make_async_copy(ys_ref.at[pl.ds(slot_ref[j * N_ACTIVE + k], 1)],
                                     rows_ref.at[k, pl.ds(j, 1)], sem)

    def start(jj, carry):
        base = pl.multiple_of(jj * SUBLANES, SUBLANES)
        for r in range(SUBLANES):
            for k in range(N_ACTIVE):
                row_copy(base + r, k).start(priority=k % 2)
        return carry

    lax.fori_loop(0, tmc // SUBLANES, start, 0)
    lo, hi = _unpack_bf16_pair(hu_ref[...])
    half = lo.shape[1]
    gate = (jnp.dot(lo, wg_ref[:half], preferred_element_type=F32)
            + jnp.dot(hi, wg_ref[half:], preferred_element_type=F32))
    up = (jnp.dot(lo, wu_ref[:half], preferred_element_type=F32)
          + jnp.dot(hi, wu_ref[half:], preferred_element_type=F32))
    y = jnp.dot((_silu(gate) * up).astype(BF16), wd_ref[...], preferred_element_type=F32)
    pltpu.make_async_copy(rows_ref, rows_ref, sem).wait()
    gk = gk_ref[...]
    for k in range(N_ACTIVE):
        y = y + gk[:, k:k + 1] * rows_ref[k]
    o_ref[...] = x1_ref[...] + _gate_rows(y, g2_ref[...])


def combine(slots_flat, hu, x1, gate2_g, gate_k, w_gs, w_us, w_ds, ys, tmc):
    t, d = x1.shape
    f = w_gs.shape[1]
    gm = tmc // MOD_GROUP
    row = lambda w: pl.BlockSpec((tmc, w), lambda i: (i, 0))
    const = lambda a, b: pl.BlockSpec((a, b), lambda i: (0, 0))
    return pl.pallas_call(
        functools.partial(_combine_kernel, tmc=tmc),
        grid=(t // tmc,),
        in_specs=[pl.BlockSpec((tmc * N_ACTIVE,), lambda i: (i,), memory_space=pltpu.SMEM),
                  pl.BlockSpec((tmc,) + hu.shape[1:], lambda i: (i, 0)),
                  row(d), pl.BlockSpec((gm, d), lambda i: (i, 0)), row(N_ACTIVE),
                  const(d, f), const(d, f), const(f, d),
                  pl.BlockSpec(memory_space=pl.ANY)],
        out_specs=row(d),
        out_shape=jax.ShapeDtypeStruct((t, d), F32),
        scratch_shapes=[pltpu.VMEM((N_ACTIVE, tmc, d), F32), pltpu.SemaphoreType.DMA(())],
        compiler_params=_params("arbitrary"),
        name="combine",
    )(slots_flat, hu, x1, gate2_g, gate_k, w_gs, w_us, w_ds, ys)


def _group_rows(v, n):
    b, d = v.shape
    return jnp.broadcast_to(v[:, None, :], (b, n // MOD_GROUP, d)).reshape(b * n // MOD_GROUP, d)


def _pad_keys(a, l_pad):
    return jnp.pad(a, ((0, 0), (0, l_pad - a.shape[1]), (0, 0)))


def _token_stage(x, mod, w, caches, count0):
    b, n, d = x.shape
    t = b * n
    shift1, scale1, gate1, shift2, scale2, gate2 = [_group_rows(m, n) for m in jnp.split(mod, 6, axis=-1)]
    xf = x.reshape(t, d)
    tm = _row_tile(t, TM_DENSE)
    (h, qa, ka, va, kab, vab, qi, ki, kib, wi) = in_proj_a(
        xf, w["norm_mix"], scale1, shift1, w["w_in_a"], w["q_norm_a"], w["k_norm_a"], tm)
    qb, kb, vb, kbb, vbb = in_proj_b(h, w["w_in_b"], tm)

    r3 = lambda a: a.reshape(b, n, a.shape[-1])
    if caches is None:
        ya = dsa_attention(r3(qa), r3(qi), r3(wi), r3(kab), r3(vab), r3(kib),
                           q_pos0=0, n_keys=n, tq=min(TQ_DSA, n), tc=min(TC_DSA, n))
        yb = stick_attention(r3(qb), r3(kbb), r3(vbb), q_pos0=0, tq=min(TQ_STICK, n), tc=min(TC_STICK, n))
    else:
        cka, cva, cki, ckb, cvb = caches
        past = cka.shape[1]
        tc_a, tc_b = min(TC_DSA, past), min(TC_STICK, past)
        new_chunk = lambda a: _pad_keys(r3(a), tc_a)
        ya = dsa_attention(r3(qa), r3(qi), r3(wi), new_chunk(kab), new_chunk(vab), new_chunk(kib),
                           q_pos0=past, n_keys=past + n, tq=n, tc=tc_a, caches=(cka, cva, cki))
        yb = stick_attention_cached(r3(qb), r3(kbb), r3(vbb), ckb, cvb, tc=tc_b)

    merged = merge_branches(h, ya.reshape(t, A_WIDTH), yb.reshape(t, B_WIDTH),
                            w["w_ga"], w["w_gb"], w["w_branch_a"], w["w_branch_b"], tm, _row_tile(d, TN_MERGE))
    x1, hu, top_e, rank_k, gate_k, counts = out_proj(
        merged, xf, gate1, scale2, shift2, w["norm_ffn"], w["w_out"], w["w_router_t"], w["b_router"], count0, tm)
    rows = (ka.reshape(b, n, A_KV_HEADS, HEAD_DIM), va.reshape(b, n, A_KV_HEADS, HEAD_DIM),
            ki.reshape(b, n, IDX_DIM), kb.reshape(b, n, B_HEADS, HEAD_DIM), vb.reshape(b, n, B_HEADS, HEAD_DIM))
    return dict(x1=x1, hu=hu, top_e=top_e, rank_k=rank_k, gate_k=gate_k, counts=counts, gate2=gate2, rows=rows)


def _routing(counts, top_e, rank_k, tme):
    e = counts.shape[0]
    t = top_e.shape[1]
    counts = counts.reshape(e).astype(jnp.int32)
    padded = (counts + tme - 1) // tme * tme
    ends = jnp.cumsum(padded)
    offsets = ends - padded
    eids = jnp.arange(e, dtype=jnp.int32)[:, None, None]
    offset_k = jnp.sum(jnp.where(top_e[None] == eids, offsets[:, None, None], 0), axis=0)
    slots = offset_k + rank_k - 1
    n_tiles = (t * N_ACTIVE + e * tme) // tme
    tile_start = jnp.arange(n_tiles, dtype=jnp.int32) * tme
    tile_expert = jnp.minimum(jnp.sum(ends[None, :] <= tile_start[:, None], axis=1), e - 1).astype(jnp.int32)
    n_used = (ends[-1] // tme).astype(jnp.int32).reshape(1)
    ids = jnp.arange(e, dtype=jnp.int32)
    nonempty = padded > 0
    later = jnp.where((ids[None, :] > ids[:, None]) & nonempty[None, :], ids[None, :], e)
    next_e = jnp.min(later, axis=1)
    next_e = jnp.where(next_e == e, -1, next_e).astype(jnp.int32)
    rank_e = jnp.cumsum(nonempty.astype(jnp.int32)) - 1
    tiles = dict(expert=tile_expert, n_used=n_used,
                 first=(tile_start == offsets[tile_expert]).astype(jnp.int32),
                 next=next_e[tile_expert], parity=(rank_e[tile_expert] % 2).astype(jnp.int32))
    return dict(slots=slots.T.reshape(-1), tiles=tiles, n_used=n_used, p_rows=n_tiles * tme,
                pad_start=offsets + counts, pad_count=padded - counts)


def kernel(x_prompt, x_sample, cache_a_k, cache_a_v, cache_a_idx_k, cache_b_k, cache_b_v, c_prompt, c_sample,
           norm_mix, norm_ffn, w_ada, b_ada, w_in, q_norm_a, k_norm_a, w_branch_a, w_branch_b, w_out,
           w_router, b_router, w_gate_e, w_up_e, w_down_e, w_gate_s, w_up_s, w_down_s):
    depth = norm_mix.shape[0]
    d = x_prompt.shape[-1]
    bp, bs = c_prompt.shape[0], c_sample.shape[0]
    c_all = jnp.concatenate([c_prompt, c_sample], axis=0)
    c_all = jnp.pad(c_all, ((0, -(bp + bs) % (2 * SUBLANES)), (0, 0)))
    tme = TM_EXPERT

    xp, xs = x_prompt, x_sample
    prompt_rows, sample_rows = [], []
    for layer in range(depth):
        wl = w_in[layer]
        o_idx = A_WIDTH + 2 * A_KV_WIDTH + IDX_Q_WIDTH
        o_b = o_idx + IDX_DIM + IDX_HEADS
        o_g = o_b + 3 * B_WIDTH
        w_in_a = jnp.concatenate(
            [wl[:, :o_b], jnp.zeros((d, LANES - IDX_DIM - IDX_HEADS), wl.dtype)], axis=1).astype(BF16)
        w = dict(
            norm_mix=norm_mix[layer].reshape(1, d), norm_ffn=norm_ffn[layer].reshape(1, d),
            q_norm_a=q_norm_a[layer].reshape(1, HEAD_DIM), k_norm_a=k_norm_a[layer].reshape(1, HEAD_DIM),
            w_in_a=w_in_a, w_in_b=wl[:, o_b:o_g].astype(BF16),
            w_ga=wl[:, o_g:o_g + d].astype(BF16), w_gb=wl[:, o_g + d:].astype(BF16),
            w_branch_a=w_branch_a[layer].astype(BF16), w_branch_b=w_branch_b[layer].astype(BF16),
            w_out=w_out[layer].astype(BF16), w_router_t=w_router[layer].T.astype(BF16),
            b_router=b_router[layer].reshape(N_EXPERTS, 1))
        mod = ada_mod(c_all, w_ada[layer], b_ada[layer])
        caches = (cache_a_k[layer], cache_a_v[layer], cache_a_idx_k[layer], cache_b_k[layer], cache_b_v[layer])
        sp = _token_stage(xp, mod[:bp], w, None, jnp.zeros((N_EXPERTS, 1), F32))
        ss = _token_stage(xs, mod[bp:bp + bs], w, caches, sp["counts"])
        prompt_rows.append(sp["rows"])
        sample_rows.append(ss["rows"])

        tp, ts = sp["x1"].shape[0], ss["x1"].shape[0]
        both = lambda name: jnp.concatenate([sp[name], ss[name]], axis=1)
        rt = _routing(ss["counts"], both("top_e"), both("rank_k"), tme)
        slots, gate_k = rt["slots"], both("gate_k").T
        xsorted = dispatch(rt, slots, sp["hu"], ss["hu"], tme)
        ysorted = expert_ffn(rt["tiles"], xsorted, w_gate_e[layer], w_up_e[layer], w_down_e[layer], tme)
        shared = (w_gate_s[layer].astype(BF16), w_up_s[layer].astype(BF16), w_down_s[layer].astype(BF16))
        yp = combine(slots[:tp * N_ACTIVE], sp["hu"], sp["x1"], sp["gate2"], gate_k[:tp], *shared, ysorted,
                     _row_tile(tp, TM_ROWS))
        ys = combine(slots[tp * N_ACTIVE:], ss["hu"], ss["x1"], ss["gate2"], gate_k[tp:], *shared, ysorted,
                     _row_tile(ts, TM_ROWS))
        xp = yp.reshape(xp.shape)
        xs = ys.reshape(xs.shape)

    stack = lambda rows: tuple(jnp.stack(r) for r in zip(*rows))
    return (xp, xs) + stack(prompt_rows) + stack(sample_rows)
```
